```python
import jax, jax.numpy as jnp
from jax import lax
import numpy as np

D_MODEL = 1024
BATCH = 2
SEQ = 8192
DEPTH = 1
DEC_BATCH = 16
DEC_SEQ = 64
PAST_LEN = 1024

CHUNK = 64
M_HEADS = 4
M_HEAD_DIM = D_MODEL // M_HEADS
M_WIDTH = M_HEADS * M_HEAD_DIM
CONV_W = 4
A_HEADS = 8
A_HEAD_DIM = D_MODEL // A_HEADS
A_WIDTH = A_HEADS * A_HEAD_DIM
IDX_HEADS = 8
IDX_DIM = 64
TOPK_MAX = 256
Q_BLOCK = 128
D_FF = ((8 * D_MODEL + 3 * 256 - 1) // (3 * 256)) * 256
EPS = 1e-6
IN_SPLITS = (M_WIDTH, M_WIDTH, M_WIDTH, M_WIDTH, M_HEADS, M_HEADS,
             A_WIDTH, A_WIDTH, A_WIDTH,
             IDX_HEADS * IDX_DIM, IDX_DIM, IDX_HEADS,
             D_MODEL, D_MODEL)
D_IN = sum(IN_SPLITS)

kernel_name = 'hybrid_mlstm_dsa_stream_step'


def rms_norm(x, g):
    x32 = x.astype(jnp.float32)
    y = x32 * lax.rsqrt(jnp.mean(x32 * x32, axis=-1, keepdims=True) + EPS)
    return (y * g.astype(jnp.float32)).astype(x.dtype)


def split_cols(z):
    out, start = [], 0
    for w in IN_SPLITS:
        out.append(z[..., start:start + w])
        start += w
    return out


def causal_conv(u, buf, w, b):
    T = u.shape[1]
    full = jnp.concatenate([buf.astype(u.dtype), u], axis=1)
    out = b.astype(u.dtype) + full[:, 0:T] * w[0]
    for j in range(1, CONV_W):
        out = out + full[:, j:j + T] * w[j]
    return out, full[:, -(CONV_W - 1):]


def mlstm_chunkwise(q, k, v, li, lf, C0, n0, m0):
    B, T, H, D = q.shape
    L = min(CHUNK, T)
    nc = T // L
    f32 = jnp.float32

    def chunks(a):
        a = a.astype(f32)
        return jnp.moveaxis(a.reshape((B, nc, L) + a.shape[2:]), 1, 0)

    causal = jnp.tril(jnp.ones((L, L), dtype=bool))

    def step(carry, xs):
        C, n, m = carry
        qc, kc, vc, lic, lfc = xs
        b = jnp.swapaxes(jnp.cumsum(lfc, axis=1), 1, 2)
        ig = jnp.swapaxes(lic, 1, 2)
        dmat = jnp.where(causal, b[..., :, None] - b[..., None, :] + ig[..., None, :], -jnp.inf)
        inter = b + m[..., None]
        m_t = jnp.maximum(inter, jnp.max(dmat, axis=-1))
        w_intra = jnp.exp(dmat - m_t[..., None])
        w_inter = jnp.exp(inter - m_t)
        s = jnp.einsum('blhd,bshd->bhls', qc, kc) * w_intra
        num = (jnp.einsum('bhls,bshe->blhe', s, vc)
               + jnp.einsum('bhl,bhed,blhd->blhe', w_inter, C, qc))
        den = jnp.sum(s, axis=-1) + w_inter * jnp.einsum('bhd,blhd->bhl', n, qc)
        denom = jnp.maximum(jnp.abs(den), jnp.exp(-m_t))
        h = num / jnp.swapaxes(denom, 1, 2)[..., None]
        m_new = m_t[..., -1]
        g = jnp.exp(b[..., -1:] - b + ig - m_new[..., None])
        decay = jnp.exp(b[..., -1] + m - m_new)
        C_new = decay[..., None, None] * C + jnp.einsum('bhs,bshe,bshd->bhed', g, vc, kc)
        n_new = decay[..., None] * n + jnp.einsum('bhs,bshd->bhd', g, kc)
        return (C_new, n_new, m_new), h

    (C, n, m), hs = lax.scan(step, (C0.astype(f32), n0.astype(f32), m0.astype(f32)),
                             (chunks(q), chunks(k), chunks(v), chunks(li), chunks(lf)))
    h = jnp.moveaxis(hs, 0, 1).reshape(B, T, H, D)
    return h, C, n, m


def mlstm_branch(mq, mk, mv, mo, mi, mf, conv_buf, w_conv, b_conv, b_if, g_mnorm, C0, n0, m0):
    B, T, _ = mq.shape
    f32 = jnp.float32
    qk, conv_new = causal_conv(jnp.concatenate([mq, mk], axis=-1), conv_buf, w_conv, b_conv)
    qk = jax.nn.silu(qk)
    q = qk[..., :M_WIDTH].reshape(B, T, M_HEADS, M_HEAD_DIM)
    k = qk[..., M_WIDTH:].reshape(B, T, M_HEADS, M_HEAD_DIM) * (M_HEAD_DIM ** -0.5)
    v = mv.reshape(B, T, M_HEADS, M_HEAD_DIM)
    b32 = b_if.astype(f32)
    li = mi.astype(f32) + b32[:M_HEADS]
    lf = jax.nn.log_sigmoid(mf.astype(f32) + b32[M_HEADS:])
    h, C, n, m = mlstm_chunkwise(q, k, v, li, lf, C0, n0, m0)
    h = rms_norm(h, g_mnorm.reshape(M_HEADS, M_HEAD_DIM)).reshape(B, T, M_WIDTH).astype(mq.dtype)
    return h * jax.nn.sigmoid(mo), conv_new, C, n, m


def dsa_attention(q, k, v, qi, ki, wi, q_pos0):
    B, Tq, H, Dh = q.shape
    S = k.shape[1]
    f32 = jnp.float32
    topk = min(TOPK_MAX, S // 4)
    blk = Q_BLOCK if Tq % Q_BLOCK == 0 else Tq
    nb = Tq // blk
    k_chunk = jnp.arange(S) // CHUNK
    q_pos = (q_pos0 + jnp.arange(Tq)).reshape(nb, blk)
    w_scale = (IDX_HEADS * IDX_DIM) ** -0.5
    ki32 = ki.astype(f32)

    def blocks(a):
        return jnp.moveaxis(a.reshape((B, nb, blk) + a.shape[2:]), 1, 0)

    def one_block(args):
        qb, qib, wib, pos = args
        rel = jax.nn.relu(jnp.einsum('bqhd,bsd->bqhs', qib.astype(f32), ki32))
        score = jnp.einsum('bqh,bqhs->bqs', wib.astype(f32) * w_scale, rel)
        q_chunk = pos // CHUNK
        adm = k_chunk[None, :] <= q_chunk[:, None]
        score = jnp.where(adm[None], score, -jnp.inf)
        _, idx = lax.top_k(score, topk)
        valid = k_chunk[idx] <= q_chunk[None, :, None]
        kg = jax.vmap(lambda kb, ib: kb[ib])(k, idx)
        vg = jax.vmap(lambda vb, ib: vb[ib])(v, idx)
        logits = jnp.einsum('bqhd,bqkhd->bqhk', qb.astype(f32), kg.astype(f32)) * (Dh ** -0.5)
        logits = jnp.where(valid[:, :, None, :], logits, -jnp.inf)
        p = jax.nn.softmax(logits, axis=-1)
        return jnp.einsum('bqhk,bqkhd->bqhd', p, vg.astype(f32)).astype(q.dtype)

    out = lax.map(one_block, (blocks(q), blocks(qi), blocks(wi), q_pos))
    return jnp.moveaxis(out, 0, 1).reshape(B, Tq, H * Dh)


def layer(x, past, p):
    k_past, v_past, ki_past, C0, n0, m0, conv_buf = past
    (g_norm1, w_in, b_if, w_conv, b_conv, g_mnorm, g_q, g_k,
     w_a_out, w_b_out, w_o, g_norm2, w_ffn_in, w_ffn_out) = p
    B, T, _ = x.shape
    h = rms_norm(x, g_norm1)
    (mq, mk, mv, mo, mi, mf, aq, ak, av, iq, ik, iw, ga, gb) = split_cols(h @ w_in)
    y_a, conv_new, C, n, m = mlstm_branch(mq, mk, mv, mo, mi, mf, conv_buf, w_conv, b_conv,
                                          b_if, g_mnorm, C0, n0, m0)
    q = rms_norm(aq.reshape(B, T, A_HEADS, A_HEAD_DIM), g_q)
    k = rms_norm(ak.reshape(B, T, A_HEADS, A_HEAD_DIM), g_k)
    v = av.reshape(B, T, A_HEADS, A_HEAD_DIM)
    k_all = jnp.concatenate([k_past.astype(k.dtype), k], axis=1)
    v_all = jnp.concatenate([v_past.astype(v.dtype), v], axis=1)
    ki_all = jnp.concatenate([ki_past.astype(ik.dtype), ik], axis=1)
    y_b = dsa_attention(q, k_all, v_all, iq.reshape(B, T, IDX_HEADS, IDX_DIM), ki_all, iw,
                        k_past.shape[1])
    mix = jax.nn.sigmoid(ga) * (y_a @ w_a_out) + jax.nn.sigmoid(gb) * (y_b @ w_b_out)
    x = x + mix @ w_o
    h2 = rms_norm(x, g_norm2)
    gate, up = jnp.split(h2 @ w_ffn_in, 2, axis=-1)
    x = x + (jax.nn.silu(gate) * up) @ w_ffn_out
    return x, (k, v, ik, C, n, m, conv_new)


def setup_inputs(seed: int = 0) -> dict:
    key = jax.random.key(seed)
    ks = jax.random.split(key, 26)
    f32 = jnp.float32

    def nrm(k, shape, s=1.0):
        return s * jax.random.normal(k, shape, f32)

    def gain(k, shape):
        return 1.0 + 0.1 * jax.random.normal(k, shape, f32)

    b_if = jnp.concatenate([-1.0 + nrm(ks[0], (DEPTH, M_HEADS), 0.1),
                            3.0 + nrm(ks[1], (DEPTH, M_HEADS), 0.5)], axis=-1)
    return {
        'x_prompt': nrm(ks[2], (BATCH, SEQ, D_MODEL)),
        'x_sample': nrm(ks[3], (DEC_BATCH, DEC_SEQ, D_MODEL)),
        'cache_k': nrm(ks[4], (DEPTH, DEC_BATCH, PAST_LEN, A_HEADS, A_HEAD_DIM)),
        'cache_v': nrm(ks[5], (DEPTH, DEC_BATCH, PAST_LEN, A_HEADS, A_HEAD_DIM)),
        'cache_kidx': nrm(ks[6], (DEPTH, DEC_BATCH, PAST_LEN, IDX_DIM)),
        'state_C': nrm(ks[7], (DEPTH, DEC_BATCH, M_HEADS, M_HEAD_DIM, M_HEAD_DIM), 0.1),
        'state_n': nrm(ks[8], (DEPTH, DEC_BATCH, M_HEADS, M_HEAD_DIM), 0.1),
        'state_m': nrm(ks[9], (DEPTH, DEC_BATCH, M_HEADS)),
        'state_conv': nrm(ks[10], (DEPTH, DEC_BATCH, CONV_W - 1, 2 * M_WIDTH)),
        'g_norm1': gain(ks[11], (DEPTH, D_MODEL)),
        'w_in': nrm(ks[12], (DEPTH, D_MODEL, D_IN), D_MODEL ** -0.5),
        'b_if': b_if,
        'w_conv': nrm(ks[13], (DEPTH, CONV_W, 2 * M_WIDTH), CONV_W ** -0.5),
        'b_conv': nrm(ks[14], (DEPTH, 2 * M_WIDTH), 0.01),
        'g_mnorm': gain(ks[15], (DEPTH, M_WIDTH)),
        'g_q': gain(ks[16], (DEPTH, A_HEAD_DIM)),
        'g_k': gain(ks[17], (DEPTH, A_HEAD_DIM)),
        'w_a_out': nrm(ks[18], (DEPTH, M_WIDTH, D_MODEL), M_WIDTH ** -0.5),
        'w_b_out': nrm(ks[19], (DEPTH, A_WIDTH, D_MODEL), A_WIDTH ** -0.5),
        'w_o': nrm(ks[20], (DEPTH, D_MODEL, D_MODEL), D_MODEL ** -0.5),
        'g_norm2': gain(ks[21], (DEPTH, D_MODEL)),
        'w_ffn_in': nrm(ks[22], (DEPTH, D_MODEL, 2 * D_FF), D_MODEL ** -0.5),
        'w_ffn_out': nrm(ks[23], (DEPTH, D_FF, D_MODEL), D_FF ** -0.5),
    }


def reference(x_prompt, x_sample, cache_k, cache_v, cache_kidx, state_C, state_n, state_m, state_conv,
              g_norm1, w_in, b_if, w_conv, b_conv, g_mnorm, g_q, g_k, w_a_out, w_b_out, w_o,
              g_norm2, w_ffn_in, w_ffn_out):
    B = x_prompt.shape[0]
    dt = x_prompt.dtype
    f32 = jnp.float32
    empty_past = (jnp.zeros((B, 0, A_HEADS, A_HEAD_DIM), dt),
                  jnp.zeros((B, 0, A_HEADS, A_HEAD_DIM), dt),
                  jnp.zeros((B, 0, IDX_DIM), dt),
                  jnp.zeros((B, M_HEADS, M_HEAD_DIM, M_HEAD_DIM), f32),
                  jnp.zeros((B, M_HEADS, M_HEAD_DIM), f32),
                  jnp.zeros((B, M_HEADS), f32),
                  jnp.zeros((B, CONV_W - 1, 2 * M_WIDTH), dt))
    yp, ys = x_prompt, x_sample
    new_p, new_s = [], []
    for l in range(DEPTH):
        p = (g_norm1[l], w_in[l], b_if[l], w_conv[l], b_conv[l], g_mnorm[l], g_q[l], g_k[l],
             w_a_out[l], w_b_out[l], w_o[l], g_norm2[l], w_ffn_in[l], w_ffn_out[l])
        yp, sp = layer(yp, empty_past, p)
        ys, ss = layer(ys, (cache_k[l], cache_v[l], cache_kidx[l], state_C[l], state_n[l],
                            state_m[l], state_conv[l]), p)
        new_p.append(sp)
        new_s.append(ss)

    def stk(lst, i):
        return jnp.stack([s[i] for s in lst])

    return (yp, ys,
            stk(new_p, 0), stk(new_p, 1), stk(new_p, 2), stk(new_p, 3), stk(new_p, 4), stk(new_p, 5), stk(new_p, 6),
            stk(new_s, 0), stk(new_s, 1), stk(new_s, 2), stk(new_s, 3), stk(new_s, 4), stk(new_s, 5), stk(new_s, 6))
```

```python
import functools
import math

import jax
import jax.numpy as jnp
from jax import lax
from jax.experimental import pallas as pl
from jax.experimental.pallas import tpu as pltpu

F32 = jnp.float32
BF16 = jnp.bfloat16
HIGHEST = lax.Precision.HIGHEST

EPS = 1e-6
CHUNK = 64
CHUNK_SHIFT = 6
M_HEADS = 4
A_HEADS = 8
IDX_HEADS = 8
IDX_DIM = 64
CONV_W = 4
TOPK_MAX = 256
LANES = 128
MASK_NEG = -1e30
F32_MAX = float(jnp.finfo(jnp.float32).max)
VMEM_LIMIT = 48 * 1024 * 1024
SELECT_MAX_ITERS = 40

D = 1024
COL_MQ, COL_MK, COL_MV, COL_MO = 0, 1024, 2048, 3072
COL_AQ, COL_AK, COL_AV = 4096, 5120, 6144
COL_GA, COL_GB = 7168, 8192
COL_IQ = 9216
COL_SMALL = 9728
D_Z = 9856
SM_IW, SM_MI, SM_MF = 64, 72, 76


def _cparams(sem):
    return pltpu.CompilerParams(dimension_semantics=sem, vmem_limit_bytes=VMEM_LIMIT)


def _norm_matmul_kernel(x_ref, g_ref, w_ref, o_ref, xn_ref):
    @pl.when(pl.program_id(1) == 0)
    def _():
        x = x_ref[...]
        ms = jnp.mean(x * x, axis=-1, keepdims=True)
        xn_ref[...] = ((x * lax.rsqrt(ms + EPS)) * g_ref[...]).astype(BF16)

    o_ref[...] = jnp.dot(xn_ref[...], w_ref[...], preferred_element_type=F32)


def _norm_matmul(x, g, w, tm, tn):
    n, d = x.shape
    nout = w.shape[1]
    return pl.pallas_call(
        _norm_matmul_kernel,
        grid=(n // tm, nout // tn),
        in_specs=[pl.BlockSpec((tm, d), lambda i, j: (i, 0)),
                  pl.BlockSpec((1, d), lambda i, j: (0, 0)),
                  pl.BlockSpec((d, tn), lambda i, j: (0, j))],
        out_specs=pl.BlockSpec((tm, tn), lambda i, j: (i, j)),
        out_shape=jax.ShapeDtypeStruct((n, nout), F32),
        scratch_shapes=[pltpu.VMEM((tm, d), BF16)],
        compiler_params=_cparams(("parallel", "arbitrary")),
        name="norm_in_proj",
    )(x, g, w)


def _mlstm_kernel(mq_ref, mk_ref, mv_ref, mo_ref, sm_ref, bias_ref, wconv_ref, bconv_ref,
                  gm_ref, c0_ref, n0_ref, m0_ref, conv0_ref,
                  y_ref, c_ref, n_ref, m_ref, conv_ref, cbuf_ref, *, L):
    c = pl.program_id(1)
    hd = D // M_HEADS

    @pl.when(c == 0)
    def _():
        c_ref[...] = c0_ref[...]
        n_ref[...] = n0_ref[...]
        m_ref[...] = m0_ref[...]
        cbuf_ref[8 - (CONV_W - 1):8, :] = conv0_ref[0]

    cbuf_ref[8:8 + L, 0:D] = mq_ref[0]
    cbuf_ref[8:8 + L, D:2 * D] = mk_ref[0]
    wc = wconv_ref[...]
    qk = bconv_ref[...] + cbuf_ref[5:5 + L, :] * wc[0:1, :]
    for j in range(1, CONV_W):
        qk = qk + cbuf_ref[5 + j:5 + j + L, :] * wc[j:j + 1, :]
    tail = cbuf_ref[5 + L:8 + L, :]
    cbuf_ref[5:8, :] = tail
    conv_ref[0] = tail
    qk = qk * jax.nn.sigmoid(qk)

    g_all = sm_ref[0] + bias_ref[...]
    lf_all = jnp.minimum(g_all, 0.0) - jnp.log1p(jnp.exp(-jnp.abs(g_all)))
    lane = lax.broadcasted_iota(jnp.int32, (L, LANES), 1)
    gates = jnp.where(lane >= SM_MF, lf_all, g_all)
    r_i = lax.broadcasted_iota(jnp.int32, (L, L), 0)
    c_i = lax.broadcasted_iota(jnp.int32, (L, L), 1)
    tril = (c_i <= r_i).astype(F32)
    triu = (r_i <= c_i).astype(F32)
    b_col_all = jnp.dot(tril, lf_all, precision=HIGHEST, preferred_element_type=F32)
    sel = (lax.broadcasted_iota(jnp.int32, (8, LANES), 1)
           == lax.broadcasted_iota(jnp.int32, (8, LANES), 0) + SM_MI).astype(F32)
    rows = lax.dot_general(sel, gates, (((1,), (1,)), ((), ())), precision=HIGHEST,
                           preferred_element_type=F32)
    b_row_all = jnp.dot(rows, triu, precision=HIGHEST, preferred_element_type=F32)
    causal = c_i <= r_i

    mo = mo_ref[0]
    mv = mv_ref[0]
    for h in range(M_HEADS):
        hs = slice(h * hd, (h + 1) * hd)
        qh = qk[:, hs]
        kh = qk[:, D + h * hd:D + (h + 1) * hd] * (hd ** -0.5)
        vh = mv[:, hs]
        qb, kb, vb = qh.astype(BF16), kh.astype(BF16), vh.astype(BF16)
        b_col = b_col_all[:, SM_MF + h:SM_MF + h + 1]
        i_col = gates[:, SM_MI + h:SM_MI + h + 1]
        b_row = b_row_all[M_HEADS + h:M_HEADS + h + 1, :]
        i_row = rows[h:h + 1, :]
        m_prev = m_ref[0, :, h:h + 1]
        c_prev = c_ref[0, h]
        n_prev = n_ref[0, h:h + 1, :]

        dmat = jnp.where(causal, b_col - b_row + i_row, -jnp.inf)
        inter = b_col + m_prev
        m_t = jnp.maximum(inter, jnp.max(dmat, axis=-1, keepdims=True))
        w_intra = jnp.exp(dmat - m_t)
        w_inter = jnp.exp(inter - m_t)
        s = lax.dot_general(qb, kb, (((1,), (1,)), ((), ())), preferred_element_type=F32) * w_intra
        qc = lax.dot_general(qb, c_prev.astype(BF16), (((1,), (1,)), ((), ())),
                             preferred_element_type=F32)
        num = jnp.dot(s.astype(BF16), vb, preferred_element_type=F32) + w_inter * qc
        den = jnp.sum(s, axis=-1, keepdims=True) + w_inter * jnp.sum(qh * n_prev, axis=-1, keepdims=True)
        denom = jnp.maximum(jnp.abs(den), jnp.exp(-m_t))
        hh = num / denom

        m_new = m_t[L - 1:L, :]
        b_last = b_col[L - 1:L, :]
        g_col = jnp.exp(b_last - b_col + i_col - m_new)
        decay = jnp.exp(b_last + m_prev - m_new)
        gv = (g_col * vh).astype(BF16)
        c_ref[0, h] = decay * c_prev + lax.dot_general(
            gv, kb, (((0,), (0,)), ((), ())), preferred_element_type=F32)
        n_ref[0, h:h + 1, :] = decay * n_prev + jnp.sum(g_col * kh, axis=0, keepdims=True)
        m_ref[0, :, h:h + 1] = m_new

        hn = hh * lax.rsqrt(jnp.mean(hh * hh, axis=-1, keepdims=True) + EPS) * gm_ref[:, hs]
        y_ref[0, :, hs] = (hn * jax.nn.sigmoid(mo[:, hs])).astype(BF16)


def _mlstm(z3, bias_row, w_conv, b_conv, g_mnorm, c0, n0, m0, conv0, L):
    b, t, _ = z3.shape
    nc = t // L
    hd = D // M_HEADS

    def zspec(col, width):
        return pl.BlockSpec((1, L, width), lambda i, c: (i, c, col // width))

    def per_batch(shape):
        nd = len(shape)
        return pl.BlockSpec((1,) + shape, lambda i, c: (i,) + (0,) * nd)

    def const(shape):
        nd = len(shape)
        return pl.BlockSpec(shape, lambda i, c: (0,) * nd)

    return pl.pallas_call(
        functools.partial(_mlstm_kernel, L=L),
        grid=(b, nc),
        in_specs=[zspec(COL_MQ, D), zspec(COL_MK, D), zspec(COL_MV, D), zspec(COL_MO, D),
                  zspec(COL_SMALL, LANES), const((1, LANES)), const((CONV_W, 2 * D)),
                  const((1, 2 * D)), const((1, D)),
                  per_batch((M_HEADS, hd, hd)), per_batch((M_HEADS, hd)),
                  per_batch((1, M_HEADS)), per_batch((CONV_W - 1, 2 * D))],
        out_specs=[pl.BlockSpec((1, L, D), lambda i, c: (i, c, 0)),
                   per_batch((M_HEADS, hd, hd)), per_batch((M_HEADS, hd)),
                   per_batch((1, M_HEADS)), per_batch((CONV_W - 1, 2 * D))],
        out_shape=[jax.ShapeDtypeStruct((b, t, D), BF16),
                   jax.ShapeDtypeStruct((b, M_HEADS, hd, hd), F32),
                   jax.ShapeDtypeStruct((b, M_HEADS, hd), F32),
                   jax.ShapeDtypeStruct((b, 1, M_HEADS), F32),
                   jax.ShapeDtypeStruct((b, CONV_W - 1, 2 * D), F32)],
        scratch_shapes=[pltpu.VMEM((8 + L, 2 * D), F32)],
        compiler_params=_cparams(("parallel", "arbitrary")),
        name="mlstm",
    )(z3, z3, z3, z3, z3, bias_row, w_conv, b_conv, g_mnorm, c0, n0, m0, conv0)


def _qkv_kernel(aq_ref, ak_ref, av_ref, gq_ref, gk_ref, k32_ref, v32_ref, qb_ref, kb_ref, vb_ref):
    hd = D // A_HEADS
    for h in range(A_HEADS):
        hs = slice(h * hd, (h + 1) * hd)
        q = aq_ref[:, hs]
        k = ak_ref[:, hs]
        qn = (q * lax.rsqrt(jnp.mean(q * q, axis=-1, keepdims=True) + EPS)) * gq_ref[...]
        kn = (k * lax.rsqrt(jnp.mean(k * k, axis=-1, keepdims=True) + EPS)) * gk_ref[...]
        qb_ref[:, hs] = qn.astype(BF16)
        k32_ref[:, hs] = kn
        kb_ref[:, hs] = kn.astype(BF16)
    v = av_ref[...]
    v32_ref[...] = v
    vb_ref[...] = v.astype(BF16)


def _qkv(z, g_q, g_k, tm):
    n = z.shape[0]
    hd = D // A_HEADS

    def zspec(col):
        return pl.BlockSpec((tm, D), lambda i: (i, col // D))

    row = pl.BlockSpec((tm, D), lambda i: (i, 0))
    gspec = pl.BlockSpec((1, hd), lambda i: (0, 0))
    return pl.pallas_call(
        _qkv_kernel,
        grid=(n // tm,),
        in_specs=[zspec(COL_AQ), zspec(COL_AK), zspec(COL_AV), gspec, gspec],
        out_specs=[row, row, row, row, row],
        out_shape=[jax.ShapeDtypeStruct((n, D), F32), jax.ShapeDtypeStruct((n, D), F32),
                   jax.ShapeDtypeStruct((n, D), BF16), jax.ShapeDtypeStruct((n, D), BF16),
                   jax.ShapeDtypeStruct((n, D), BF16)],
        compiler_params=_cparams(("parallel",)),
        name="qkv_norm",
    )(z, z, z, g_q, g_k)


def _num_key_tiles(qi, tq, tile, q_pos0, s_real):
    last_chunk = (q_pos0 + (qi + 1) * tq - 1) // CHUNK
    kend = jnp.minimum((last_chunk + 1) * CHUNK, s_real)
    return (kend + tile - 1) // tile


def _select_kernel(iq_ref, sm_ref, ki_ref, mask_ref, score_ref, *, tq, kb, s_pad, s_real, q_pos0, topk):
    qi = pl.program_id(1)
    nkt = _num_key_tiles(qi, tq, kb, q_pos0, s_real)
    nt = nkt * (kb // LANES)
    kf = float(topk)

    w = sm_ref[0][:, SM_IW:SM_IW + IDX_HEADS] * ((IDX_HEADS * IDX_DIM) ** -0.5)
    row_pos = q_pos0 + qi * tq + lax.broadcasted_iota(jnp.int32, (tq, 1), 0)
    row_chunk = row_pos >> CHUNK_SHIFT

    def score_tile(j, carry):
        rmin, rmax = carry
        k0 = pl.multiple_of(j * kb, kb)
        rel = jnp.dot(iq_ref[0, 0], ki_ref[0, :, pl.ds(k0, kb)],
                      preferred_element_type=F32)
        acc = w[:, 0:1] * jnp.maximum(rel[0:tq], 0.0)
        for h in range(1, IDX_HEADS):
            acc = acc + w[:, h:h + 1] * jnp.maximum(rel[h * tq:(h + 1) * tq], 0.0)
        col = k0 + lax.broadcasted_iota(jnp.int32, (tq, kb), 1)
        adm = jnp.logical_and((col >> CHUNK_SHIFT) <= row_chunk, col < s_real)
        score_ref[:, pl.ds(k0, kb)] = jnp.where(adm, acc, -jnp.inf)
        rmin = jnp.minimum(rmin, jnp.min(jnp.where(adm, acc, jnp.inf), axis=-1, keepdims=True))
        rmax = jnp.maximum(rmax, jnp.max(jnp.where(adm, acc, -jnp.inf), axis=-1, keepdims=True))
        return rmin, rmax

    rmin, rmax = lax.fori_loop(0, nkt, score_tile,
                               (jnp.full((tq, 1), jnp.inf, F32), jnp.full((tq, 1), -jnp.inf, F32)))

    def count(pred):
        def body(j, acc):
            k0 = pl.multiple_of(j * LANES, LANES)
            return acc + jnp.where(pred(score_ref[:, pl.ds(k0, LANES)], k0), 1.0, 0.0)
        acc = lax.fori_loop(0, nt, body, jnp.zeros((tq, LANES), F32))
        return jnp.sum(acc, axis=-1, keepdims=True)

    n_adm = jnp.minimum((row_chunk + 1) * CHUNK, s_real).astype(F32)
    lo0 = jnp.full((tq, 1), -F32_MAX, F32)
    hi0 = jnp.full((tq, 1), F32_MAX, F32)

    def open_rows(cnt_lo):
        return cnt_lo > kf

    def cond(st):
        it, n_open = st[0], st[1]
        return jnp.logical_and(it < SELECT_MAX_ITERS, n_open > 0.0)

    def body(st):
        it, _, lo, hi, cnt_lo, cnt_hi = st
        active = open_rows(cnt_lo)
        mid = 0.5 * jnp.maximum(lo, rmin) + 0.5 * jnp.minimum(hi, rmax)
        midb = jnp.broadcast_to(mid, (tq, LANES))
        cm = count(lambda s, k0: s >= midb)
        up = jnp.logical_and(active, cm >= kf)
        dn = jnp.logical_and(active, cm < kf)
        lo = jnp.where(up, mid, lo)
        cnt_lo = jnp.where(up, cm, cnt_lo)
        hi = jnp.where(dn, mid, hi)
        cnt_hi = jnp.where(dn, cm, cnt_hi)
        n_open = jnp.max(jnp.where(open_rows(cnt_lo), 1.0, 0.0))
        return it + 1, n_open, lo, hi, cnt_lo, cnt_hi

    n_open0 = jnp.max(jnp.where(open_rows(n_adm), 1.0, 0.0))
    _, n_open, lo, hi, cnt_lo, cnt_hi = lax.while_loop(
        cond, body, (jnp.int32(0), n_open0, lo0, hi0, n_adm, jnp.zeros((tq, 1), F32)))

    lob = jnp.broadcast_to(lo, (tq, LANES))
    hib = jnp.broadcast_to(hi, (tq, LANES))
    lane = lax.broadcasted_iota(jnp.int32, (tq, LANES), 1)

    def tie_search(_):
        need = kf - cnt_hi

        def step(_, st):
            jlo, jhi = st
            jmid = (jlo + jhi) >> 1
            jmb = jnp.broadcast_to(jmid, (tq, LANES))
            cm = count(lambda s, k0: jnp.logical_and(
                jnp.logical_and(s >= lob, s < hib), k0 + lane < jmb))
            ok = cm >= need
            return jnp.where(ok, jlo, jmid), jnp.where(ok, jmid, jhi)

        steps = int(math.ceil(math.log2(s_pad))) + 1
        _, jhi = lax.fori_loop(0, steps, step, (jnp.zeros((tq, 1), jnp.int32),
                                                jnp.full((tq, 1), s_pad, jnp.int32)))
        return jnp.where(open_rows(cnt_lo), jhi, s_pad)

    jcut = lax.cond(n_open > 0.0, tie_search, lambda _: jnp.full((tq, 1), s_pad, jnp.int32), 0)
    jcb = jnp.broadcast_to(jcut, (tq, LANES))

    mask_ref[...] = jnp.zeros(mask_ref.shape, mask_ref.dtype)

    def write_tile(j, carry):
        k0 = pl.multiple_of(j * LANES, LANES)
        s = score_ref[:, pl.ds(k0, LANES)]
        keep = jnp.logical_or(s >= hib, jnp.logical_and(s >= lob, k0 + lane < jcb))
        mask_ref[0, :, pl.ds(k0, LANES)] = jnp.where(keep, 1, 0).astype(jnp.int8)
        return carry

    lax.fori_loop(0, nt, write_tile, 0)


def _select(iq3, z3, ki3, tq, kb, s_real, q_pos0, topk):
    b, nq = iq3.shape[0], iq3.shape[1]
    s_pad = ki3.shape[2]
    tq_all = nq * tq
    kern = functools.partial(_select_kernel, tq=tq, kb=kb, s_pad=s_pad, s_real=s_real,
                             q_pos0=q_pos0, topk=topk)
    return pl.pallas_call(
        kern,
        grid=(b, nq),
        in_specs=[pl.BlockSpec((1, 1, IDX_HEADS * tq, 3 * IDX_DIM), lambda i, j: (i, j, 0, 0)),
                  pl.BlockSpec((1, tq, LANES), lambda i, j: (i, j, COL_SMALL // LANES)),
                  pl.BlockSpec((1, 3 * IDX_DIM, s_pad), lambda i, j: (i, 0, 0))],
        out_specs=pl.BlockSpec((1, tq, s_pad), lambda i, j: (i, j, 0)),
        out_shape=jax.ShapeDtypeStruct((b, tq_all, s_pad), jnp.int8),
        scratch_shapes=[pltpu.VMEM((tq, s_pad), F32)],
        compiler_params=_cparams(("parallel", "arbitrary")),
        name="index_select",
    )(iq3, z3, ki3)


def _attn_kernel(q_ref, k_ref, v_ref, mask_ref, o_ref, acc_ref, m_ref, l_ref, *, tq, tk, s_real, q_pos0):
    i = pl.program_id(1)
    j = pl.program_id(2)
    hd = D // A_HEADS
    scale = hd ** -0.5

    @pl.when(j == 0)
    def _():
        acc_ref[...] = jnp.zeros(acc_ref.shape, F32)
        m_ref[...] = jnp.full(m_ref.shape, MASK_NEG, F32)
        l_ref[...] = jnp.zeros(l_ref.shape, F32)

    @pl.when(j < _num_key_tiles(i, tq, tk, q_pos0, s_real))
    def _():
        bias = jnp.where(mask_ref[0].astype(jnp.int32) != 0, 0.0, MASK_NEG)
        for h in range(A_HEADS):
            hs = slice(h * hd, (h + 1) * hd)
            s = lax.dot_general(q_ref[0, :, hs], k_ref[0, :, hs], (((1,), (1,)), ((), ())),
                                preferred_element_type=F32) * scale + bias
            m_prev = m_ref[h]
            m_new = jnp.maximum(m_prev, jnp.max(s, axis=-1, keepdims=True))
            alpha = jnp.exp(m_prev - m_new)
            p = jnp.exp(s - m_new[:, 0:1])
            l_ref[h] = alpha * l_ref[h] + jnp.sum(p, axis=-1, keepdims=True)
            acc_ref[:, hs] = alpha * acc_ref[:, hs] + jnp.dot(
                p.astype(BF16), v_ref[0, :, hs], preferred_element_type=F32)
            m_ref[h] = m_new

    @pl.when(j == pl.num_programs(2) - 1)
    def _():
        for h in range(A_HEADS):
            hs = slice(h * hd, (h + 1) * hd)
            o_ref[0, :, hs] = (acc_ref[:, hs] / l_ref[h]).astype(BF16)


def _attention(qb, kb, vb, mask, tq, tk, s_real, q_pos0):
    b, t, _ = qb.shape
    s_pad = kb.shape[1]
    nq, nk = t // tq, s_pad // tk
    hd = D // A_HEADS

    def kidx(i, q, j):
        return (i, jnp.minimum(j, _num_key_tiles(q, tq, tk, q_pos0, s_real) - 1), 0)

    def midx(i, q, j):
        return (i, q, jnp.minimum(j, _num_key_tiles(q, tq, tk, q_pos0, s_real) - 1))

    kern = functools.partial(_attn_kernel, tq=tq, tk=tk, s_real=s_real, q_pos0=q_pos0)
    return pl.pallas_call(
        kern,
        grid=(b, nq, nk),
        in_specs=[pl.BlockSpec((1, tq, D), lambda i, q, j: (i, q, 0)),
                  pl.BlockSpec((1, tk, D), kidx),
                  pl.BlockSpec((1, tk, D), kidx),
                  pl.BlockSpec((1, tq, tk), midx)],
        out_specs=pl.BlockSpec((1, tq, D), lambda i, q, j: (i, q, 0)),
        out_shape=jax.ShapeDtypeStruct((b, t, D), BF16),
        scratch_shapes=[pltpu.VMEM((tq, D), F32),
                        pltpu.VMEM((A_HEADS, tq, hd), F32),
                        pltpu.VMEM((A_HEADS, tq, hd), F32)],
        compiler_params=_cparams(("parallel", "parallel", "arbitrary")),
        name="masked_attention",
    )(qb, kb, vb, mask)


def _merge_kernel(x_ref, ya_ref, yb_ref, ga_ref, gb_ref, wa_ref, wb_ref, wo_ref, o_ref):
    a = jnp.dot(ya_ref[...], wa_ref[...], preferred_element_type=F32)
    bb = jnp.dot(yb_ref[...], wb_ref[...], preferred_element_type=F32)
    mix = jax.nn.sigmoid(ga_ref[...]) * a + jax.nn.sigmoid(gb_ref[...]) * bb
    o_ref[...] = x_ref[...] + jnp.dot(mix.astype(BF16), wo_ref[...], preferred_element_type=F32)


def _merge(x, ya, yb, z, wa, wb, wo, tm):
    n = x.shape[0]
    row = pl.BlockSpec((tm, D), lambda i: (i, 0))
    wspec = pl.BlockSpec((D, D), lambda i: (0, 0))
    return pl.pallas_call(
        _merge_kernel,
        grid=(n // tm,),
        in_specs=[row, row, row,
                  pl.BlockSpec((tm, D), lambda i: (i, COL_GA // D)),
                  pl.BlockSpec((tm, D), lambda i: (i, COL_GB // D)),
                  wspec, wspec, wspec],
        out_specs=row,
        out_shape=jax.ShapeDtypeStruct((n, D), F32),
        compiler_params=_cparams(("parallel",)),
        name="merge_out_proj",
    )(x, ya, yb, z, z, wa, wb, wo)


def _ffn_kernel(x_ref, g_ref, wg_ref, wu_ref, wd_ref, o_ref, xn_ref, acc_ref):
    c = pl.program_id(1)

    @pl.when(c == 0)
    def _():
        x = x_ref[...]
        ms = jnp.mean(x * x, axis=-1, keepdims=True)
        xn_ref[...] = ((x * lax.rsqrt(ms + EPS)) * g_ref[...]).astype(BF16)
        acc_ref[...] = x

    xn = xn_ref[...]
    gate = jnp.dot(xn, wg_ref[...], preferred_element_type=F32)
    up = jnp.dot(xn, wu_ref[...], preferred_element_type=F32)
    act = (gate * jax.nn.sigmoid(gate)) * up
    acc_ref[...] += jnp.dot(act.astype(BF16), wd_ref[...], preferred_element_type=F32)

    @pl.when(c == pl.num_programs(1) - 1)
    def _():
        o_ref[...] = acc_ref[...]


def _ffn(x, g, w_in, w_out, tm, tf):
    n = x.shape[0]
    dff = w_out.shape[0]
    nf = dff // tf
    row = pl.BlockSpec((tm, D), lambda i, c: (i, 0))
    return pl.pallas_call(
        _ffn_kernel,
        grid=(n // tm, nf),
        in_specs=[row, pl.BlockSpec((1, D), lambda i, c: (0, 0)),
                  pl.BlockSpec((D, tf), lambda i, c: (0, c)),
                  pl.BlockSpec((D, tf), lambda i, c: (0, nf + c)),
                  pl.BlockSpec((tf, D), lambda i, c: (c, 0))],
        out_specs=row,
        out_shape=jax.ShapeDtypeStruct((n, D), F32),
        scratch_shapes=[pltpu.VMEM((tm, D), BF16), pltpu.VMEM((tm, D), F32)],
        compiler_params=_cparams(("parallel", "arbitrary")),
        name="swiglu_ffn",
    )(x, g, w_in, w_in, w_out)


def _split_hi_lo(a):
    hi = a.astype(BF16)
    lo = (a - hi.astype(F32)).astype(BF16)
    return hi, lo


def _layer(x, past, p, cfg):
    k_past, v_past, ki_past, c0, n0, m0, conv0 = past
    b, t, _ = x.shape
    n_tok = b * t
    past_len = k_past.shape[1]
    s_real = past_len + t
    topk = min(TOPK_MAX, s_real // 4)
    tq_sel, kb_sel, tq_att, tk_att = cfg["tq_sel"], cfg["kb_sel"], cfg["tq_att"], cfg["tk_att"]
    s_pad = -(-s_real // tk_att) * tk_att

    x2 = x.reshape(n_tok, D)
    z = _norm_matmul(x2, p["g_norm1"], p["w_in"], cfg["tm_proj"], cfg["tn_proj"])
    z3 = z.reshape(b, t, D_Z)

    y_a, c_new, n_new, m_new, conv_new = _mlstm(
        z3, p["gate_bias"], p["w_conv"], p["b_conv"], p["g_mnorm"], c0, n0,
        m0.reshape(b, 1, M_HEADS), conv0, cfg["mlstm_chunk"])

    k32, v32, qb, kb, vb = _qkv(z, p["g_q"], p["g_k"], cfg["tm_rows"])
    ik = z[:, COL_SMALL:COL_SMALL + IDX_DIM].reshape(b, t, IDX_DIM)
    nq = t // tq_sel
    q_hi, q_lo = _split_hi_lo(z[:, COL_IQ:COL_IQ + IDX_HEADS * IDX_DIM])

    def heads_major(a):
        return a.reshape(b, nq, tq_sel, IDX_HEADS, IDX_DIM).transpose(0, 1, 3, 2, 4)

    iq3 = jnp.concatenate([heads_major(q_hi), heads_major(q_hi), heads_major(q_lo)], axis=-1)
    iq3 = iq3.reshape(b, nq, IDX_HEADS * tq_sel, 3 * IDX_DIM)
    ki_all = jnp.concatenate([ki_past.astype(F32), ik], axis=1)
    k_hi, k_lo = _split_hi_lo(ki_all)
    ki3 = jnp.concatenate([k_hi, k_lo, k_hi], axis=-1).transpose(0, 2, 1)
    ki3 = jnp.pad(ki3, ((0, 0), (0, 0), (0, s_pad - s_real)))
    mask = _select(iq3, z3, ki3, tq_sel, kb_sel, s_real, past_len, topk)

    def with_past(past_rows, new_rows):
        parts = [past_rows.reshape(b, past_len, D).astype(BF16), new_rows.reshape(b, t, D)]
        if s_pad > s_real:
            parts.append(jnp.zeros((b, s_pad - s_real, D), BF16))
        return jnp.concatenate(parts, axis=1)

    y_b = _attention(qb.reshape(b, t, D), with_past(k_past, kb), with_past(v_past, vb), mask,
                     tq_att, tk_att, s_real, past_len)

    x1 = _merge(x2, y_a.reshape(n_tok, D), y_b.reshape(n_tok, D), z,
                p["w_a_out"], p["w_b_out"], p["w_o"], cfg["tm_rows"])
    y = _ffn(x1, p["g_norm2"], p["w_ffn_in"], p["w_ffn_out"], cfg["tm_ffn"], cfg["tf_ffn"])

    hd = D // A_HEADS
    return y.reshape(b, t, D), (k32.reshape(b, t, A_HEADS, hd), v32.reshape(b, t, A_HEADS, hd), ik,
                                c_new, n_new, m_new.reshape(b, M_HEADS), conv_new)


def _prep_params(g_norm1, w_in, b_if, w_conv, b_conv, g_mnorm, g_q, g_k, w_a_out, w_b_out, w_o,
                 g_norm2, w_ffn_in, w_ffn_out):
    o_mi = 4 * D
    o_aq = o_mi + 2 * M_HEADS
    o_iq = o_aq + 3 * D
    o_ik = o_iq + IDX_HEADS * IDX_DIM
    o_iw = o_ik + IDX_DIM
    o_ga = o_iw + IDX_HEADS
    w_perm = jnp.concatenate([
        w_in[:, 0:o_mi], w_in[:, o_aq:o_iq], w_in[:, o_ga:o_ga + 2 * D], w_in[:, o_iq:o_ik],
        w_in[:, o_ik:o_ga], w_in[:, o_mi:o_aq],
        jnp.zeros((D, D_Z - COL_SMALL - (SM_MF + M_HEADS)), w_in.dtype)], axis=1).astype(BF16)
    gate_bias = jnp.zeros((1, LANES), F32).at[0, SM_MI:SM_MI + 2 * M_HEADS].set(b_if.astype(F32))
    return {
        "g_norm1": g_norm1.reshape(1, D), "w_in": w_perm, "gate_bias": gate_bias,
        "w_conv": w_conv, "b_conv": b_conv.reshape(1, 2 * D), "g_mnorm": g_mnorm.reshape(1, D),
        "g_q": g_q.reshape(1, -1), "g_k": g_k.reshape(1, -1),
        "w_a_out": w_a_out.astype(BF16), "w_b_out": w_b_out.astype(BF16), "w_o": w_o.astype(BF16),
        "g_norm2": g_norm2.reshape(1, D), "w_ffn_in": w_ffn_in.astype(BF16),
        "w_ffn_out": w_ffn_out.astype(BF16),
    }


def _config(b, t, past_len):
    n_tok = b * t
    s_real = past_len + t
    tm = min(1024, n_tok)
    tq_sel = min(128, t)
    tq_att = min(256, t)
    if s_real % 512 == 0:
        kb_sel = tk_att = 512
    else:
        kb_sel = tk_att = -(-s_real // LANES) * LANES
    return {"tm_proj": tm, "tn_proj": 896, "mlstm_chunk": min(128, t), "tm_rows": min(512, n_tok),
            "tq_sel": tq_sel, "kb_sel": kb_sel, "tq_att": tq_att, "tk_att": tk_att,
            "tm_ffn": min(512, n_tok), "tf_ffn": 1408}


def kernel(x_prompt, x_sample, cache_k, cache_v, cache_kidx, state_C, state_n, state_m, state_conv,
           g_norm1, w_in, b_if, w_conv, b_conv, g_mnorm, g_q, g_k, w_a_out, w_b_out, w_o,
           g_norm2, w_ffn_in, w_ffn_out):
    depth = w_in.shape[0]
    bp = x_prompt.shape[0]
    hd_a = D // A_HEADS
    hd_m = D // M_HEADS
    yp, ys = x_prompt, x_sample
    new_p, new_s = [], []
    for l in range(depth):
        p = _prep_params(g_norm1[l], w_in[l], b_if[l], w_conv[l], b_conv[l], g_mnorm[l], g_q[l],
                         g_k[l], w_a_out[l], w_b_out[l], w_o[l], g_norm2[l], w_ffn_in[l],
                         w_ffn_out[l])
        empty = (jnp.zeros((bp, 0, A_HEADS, hd_a), F32), jnp.zeros((bp, 0, A_HEADS, hd_a), F32),
                 jnp.zeros((bp, 0, IDX_DIM), F32), jnp.zeros((bp, M_HEADS, hd_m, hd_m), F32),
                 jnp.zeros((bp, M_HEADS, hd_m), F32), jnp.zeros((bp, M_HEADS), F32),
                 jnp.zeros((bp, CONV_W - 1, 2 * D), F32))
        yp, sp = _layer(yp, empty, p, _config(bp, yp.shape[1], 0))
        ys, ss = _layer(ys, (cache_k[l], cache_v[l], cache_kidx[l], state_C[l], state_n[l],
                             state_m[l], state_conv[l]), p,
                        _config(ys.shape[0], ys.shape[1], cache_k.shape[2]))
        new_p.append(sp)
        new_s.append(ss)

    def stk(lst, i):
        return jnp.stack([s[i] for s in lst])

    return (yp, ys) + tuple(stk(new_p, i) for i in range(7)) + tuple(stk(new_s, i) for i in range(7))
```

```python
import functools
import math

import jax
import jax.numpy as jnp
from jax import lax
from jax.experimental import pallas as pl
from jax.experimental.pallas import tpu as pltpu

F32 = jnp.float32
BF16 = jnp.bfloat16
HIGHEST = lax.Precision.HIGHEST

EPS = 1e-6
CHUNK = 64
CHUNK_SHIFT = 6
M_HEADS = 4
A_HEADS = 8
IDX_HEADS = 8
IDX_DIM = 64
CONV_W = 4
TOPK_MAX = 256
LANES = 128
SUBLANES = 8
MASK_NEG = -1e30
F32_MAX = float(jnp.finfo(jnp.float32).max)
VMEM_LIMIT = 48 * 1024 * 1024
SELECT_MAX_ITERS = 40
COUNT_PARTS = 4

D = 1024
COL_MQ, COL_MK, COL_MV, COL_MO = 0, 1024, 2048, 3072
COL_AQ, COL_AK, COL_AV = 4096, 5120, 6144
COL_GA, COL_GB = 7168, 8192
COL_IQ = 9216
COL_SMALL = 9728
D_Z = 9856
SM_IW, SM_MI, SM_MF = 64, 72, 76


def _cparams(sem):
    return pltpu.CompilerParams(dimension_semantics=sem, vmem_limit_bytes=VMEM_LIMIT)


def _norm_matmul_kernel(x_ref, g_ref, w_ref, o_ref, xn_ref):
    @pl.when(pl.program_id(1) == 0)
    def _():
        x = x_ref[...]
        ms = jnp.mean(x * x, axis=-1, keepdims=True)
        xn_ref[...] = ((x * lax.rsqrt(ms + EPS)) * g_ref[...]).astype(BF16)

    o_ref[...] = jnp.dot(xn_ref[...], w_ref[...], preferred_element_type=F32)


def _norm_matmul(x, g, w, tm, tn):
    n, d = x.shape
    nout = w.shape[1]
    return pl.pallas_call(
        _norm_matmul_kernel,
        grid=(n // tm, nout // tn),
        in_specs=[pl.BlockSpec((tm, d), lambda i, j: (i, 0)),
                  pl.BlockSpec((1, d), lambda i, j: (0, 0)),
                  pl.BlockSpec((d, tn), lambda i, j: (0, j))],
        out_specs=pl.BlockSpec((tm, tn), lambda i, j: (i, j)),
        out_shape=jax.ShapeDtypeStruct((n, nout), F32),
        scratch_shapes=[pltpu.VMEM((tm, d), BF16)],
        compiler_params=_cparams(("parallel", "arbitrary")),
        name="norm_in_proj",
    )(x, g, w)


def _mlstm_kernel(mq_ref, mk_ref, mv_ref, mo_ref, sm_ref, bias_ref, wconv_ref, bconv_ref,
                  gm_ref, c0_ref, n0_ref, m0_ref, conv0_ref,
                  y_ref, c_ref, n_ref, m_ref, conv_ref, cbuf_ref, *, L):
    c = pl.program_id(1)
    hd = D // M_HEADS

    @pl.when(c == 0)
    def _():
        c_ref[...] = c0_ref[...]
        n_ref[...] = n0_ref[...]
        m_ref[...] = m0_ref[...]
        cbuf_ref[8 - (CONV_W - 1):8, :] = conv0_ref[0]

    cbuf_ref[8:8 + L, 0:D] = mq_ref[0]
    cbuf_ref[8:8 + L, D:2 * D] = mk_ref[0]
    wc = wconv_ref[...]
    qk = bconv_ref[...] + cbuf_ref[5:5 + L, :] * wc[0:1, :]
    for j in range(1, CONV_W):
        qk = qk + cbuf_ref[5 + j:5 + j + L, :] * wc[j:j + 1, :]
    tail = cbuf_ref[5 + L:8 + L, :]
    cbuf_ref[5:8, :] = tail
    conv_ref[0] = tail
    qk = qk * jax.nn.sigmoid(qk)

    g_all = sm_ref[0] + bias_ref[...]
    lf_all = jnp.minimum(g_all, 0.0) - jnp.log1p(jnp.exp(-jnp.abs(g_all)))
    lane = lax.broadcasted_iota(jnp.int32, (L, LANES), 1)
    gates = jnp.where(lane >= SM_MF, lf_all, g_all)
    r_i = lax.broadcasted_iota(jnp.int32, (L, L), 0)
    c_i = lax.broadcasted_iota(jnp.int32, (L, L), 1)
    tril = (c_i <= r_i).astype(F32)
    triu = (r_i <= c_i).astype(F32)
    b_col_all = jnp.dot(tril, lf_all, precision=HIGHEST, preferred_element_type=F32)
    sel = (lax.broadcasted_iota(jnp.int32, (8, LANES), 1)
           == lax.broadcasted_iota(jnp.int32, (8, LANES), 0) + SM_MI).astype(F32)
    rows = lax.dot_general(sel, gates, (((1,), (1,)), ((), ())), precision=HIGHEST,
                           preferred_element_type=F32)
    b_row_all = jnp.dot(rows, triu, precision=HIGHEST, preferred_element_type=F32)
    causal = c_i <= r_i

    mo = mo_ref[0]
    mv = mv_ref[0]
    for h in range(M_HEADS):
        hs = slice(h * hd, (h + 1) * hd)
        qh = qk[:, hs]
        kh = qk[:, D + h * hd:D + (h + 1) * hd] * (hd ** -0.5)
        vh = mv[:, hs]
        qb, kb, vb = qh.astype(BF16), kh.astype(BF16), vh.astype(BF16)
        b_col = b_col_all[:, SM_MF + h:SM_MF + h + 1]
        i_col = gates[:, SM_MI + h:SM_MI + h + 1]
        b_row = b_row_all[M_HEADS + h:M_HEADS + h + 1, :]
        i_row = rows[h:h + 1, :]
        m_prev = m_ref[0, :, h:h + 1]
        c_prev = c_ref[0, h]
        n_prev = n_ref[0, h:h + 1, :]

        dmat = jnp.where(causal, b_col - b_row + i_row, -jnp.inf)
        inter = b_col + m_prev
        m_t = jnp.maximum(inter, jnp.max(dmat, axis=-1, keepdims=True))
        w_intra = jnp.exp(dmat - m_t)
        w_inter = jnp.exp(inter - m_t)
        s = lax.dot_general(qb, kb, (((1,), (1,)), ((), ())), preferred_element_type=F32) * w_intra
        qc = lax.dot_general(qb, c_prev.astype(BF16), (((1,), (1,)), ((), ())),
                             preferred_element_type=F32)
        num = jnp.dot(s.astype(BF16), vb, preferred_element_type=F32) + w_inter * qc
        den = jnp.sum(s, axis=-1, keepdims=True) + w_inter * jnp.sum(qh * n_prev, axis=-1, keepdims=True)
        denom = jnp.maximum(jnp.abs(den), jnp.exp(-m_t))
        hh = num / denom

        m_new = m_t[L - 1:L, :]
        b_last = b_col[L - 1:L, :]
        g_col = jnp.exp(b_last - b_col + i_col - m_new)
        decay = jnp.exp(b_last + m_prev - m_new)
        gv = (g_col * vh).astype(BF16)
        c_ref[0, h] = decay * c_prev + lax.dot_general(
            gv, kb, (((0,), (0,)), ((), ())), preferred_element_type=F32)
        n_ref[0, h:h + 1, :] = decay * n_prev + jnp.sum(g_col * kh, axis=0, keepdims=True)
        m_ref[0, :, h:h + 1] = m_new

        hn = hh * lax.rsqrt(jnp.mean(hh * hh, axis=-1, keepdims=True) + EPS) * gm_ref[:, hs]
        y_ref[0, :, hs] = (hn * jax.nn.sigmoid(mo[:, hs])).astype(BF16)


def _mlstm(z3, bias_row, w_conv, b_conv, g_mnorm, c0, n0, m0, conv0, L):
    b, t, _ = z3.shape
    nc = t // L
    hd = D // M_HEADS

    def zspec(col, width):
        return pl.BlockSpec((1, L, width), lambda i, c: (i, c, col // width))

    def per_batch(shape):
        nd = len(shape)
        return pl.BlockSpec((1,) + shape, lambda i, c: (i,) + (0,) * nd)

    def const(shape):
        nd = len(shape)
        return pl.BlockSpec(shape, lambda i, c: (0,) * nd)

    return pl.pallas_call(
        functools.partial(_mlstm_kernel, L=L),
        grid=(b, nc),
        in_specs=[zspec(COL_MQ, D), zspec(COL_MK, D), zspec(COL_MV, D), zspec(COL_MO, D),
                  zspec(COL_SMALL, LANES), const((1, LANES)), const((CONV_W, 2 * D)),
                  const((1, 2 * D)), const((1, D)),
                  per_batch((M_HEADS, hd, hd)), per_batch((M_HEADS, hd)),
                  per_batch((1, M_HEADS)), per_batch((CONV_W - 1, 2 * D))],
        out_specs=[pl.BlockSpec((1, L, D), lambda i, c: (i, c, 0)),
                   per_batch((M_HEADS, hd, hd)), per_batch((M_HEADS, hd)),
                   per_batch((1, M_HEADS)), per_batch((CONV_W - 1, 2 * D))],
        out_shape=[jax.ShapeDtypeStruct((b, t, D), BF16),
                   jax.ShapeDtypeStruct((b, M_HEADS, hd, hd), F32),
                   jax.ShapeDtypeStruct((b, M_HEADS, hd), F32),
                   jax.ShapeDtypeStruct((b, 1, M_HEADS), F32),
                   jax.ShapeDtypeStruct((b, CONV_W - 1, 2 * D), F32)],
        scratch_shapes=[pltpu.VMEM((8 + L, 2 * D), F32)],
        compiler_params=_cparams(("parallel", "arbitrary")),
        name="mlstm",
    )(z3, z3, z3, z3, z3, bias_row, w_conv, b_conv, g_mnorm, c0, n0, m0, conv0)


def _eye_bf16(n):
    return (lax.broadcasted_iota(jnp.int32, (n, n), 0)
            == lax.broadcasted_iota(jnp.int32, (n, n), 1)).astype(BF16)


def _transpose_bf16(eye, a):
    return lax.dot_general(eye, a, (((1,), (1,)), ((), ())), preferred_element_type=F32)


def _qkv_kernel(aq_ref, ak_ref, av_ref, gq_ref, gk_ref, k32_ref, v32_ref, kb_ref, qt_ref, vt_ref):
    hd = D // A_HEADS
    eye = _eye_bf16(hd)
    for h in range(A_HEADS):
        hs = slice(h * hd, (h + 1) * hd)
        q = aq_ref[0, :, hs]
        k = ak_ref[0, :, hs]
        v = av_ref[0, :, hs]
        qn = (q * lax.rsqrt(jnp.mean(q * q, axis=-1, keepdims=True) + EPS)) * gq_ref[...]
        kn = (k * lax.rsqrt(jnp.mean(k * k, axis=-1, keepdims=True) + EPS)) * gk_ref[...]
        k32_ref[0, :, hs] = kn
        kb_ref[0, :, hs] = kn.astype(BF16)
        v32_ref[0, :, hs] = v
        qt_ref[0, hs, :] = _transpose_bf16(eye, qn.astype(BF16)).astype(BF16)
        vt_ref[0, hs, :] = _transpose_bf16(eye, v.astype(BF16)).astype(BF16)


def _qkv(z3, g_q, g_k, tm):
    b, t, _ = z3.shape
    hd = D // A_HEADS

    def zspec(col):
        return pl.BlockSpec((1, tm, D), lambda i, j: (i, j, col // D))

    row = pl.BlockSpec((1, tm, D), lambda i, j: (i, j, 0))
    colb = pl.BlockSpec((1, D, tm), lambda i, j: (i, 0, j))
    gspec = pl.BlockSpec((1, hd), lambda i, j: (0, 0))
    return pl.pallas_call(
        _qkv_kernel,
        grid=(b, t // tm),
        in_specs=[zspec(COL_AQ), zspec(COL_AK), zspec(COL_AV), gspec, gspec],
        out_specs=[row, row, row, colb, colb],
        out_shape=[jax.ShapeDtypeStruct((b, t, D), F32), jax.ShapeDtypeStruct((b, t, D), F32),
                   jax.ShapeDtypeStruct((b, t, D), BF16), jax.ShapeDtypeStruct((b, D, t), BF16),
                   jax.ShapeDtypeStruct((b, D, t), BF16)],
        compiler_params=_cparams(("parallel", "parallel")),
        name="qkv_norm",
    )(z3, z3, z3, g_q, g_k)


def _num_key_tiles(qi, tq, tile, q_pos0, s_real):
    last_chunk = (q_pos0 + (qi + 1) * tq - 1) // CHUNK
    kend = jnp.minimum((last_chunk + 1) * CHUNK, s_real)
    return (kend + tile - 1) // tile


def _select_kernel(iq_ref, w_ref, ki_ref, mask_ref, score_ref, *, tq, kb, s_pad, s_real, q_pos0, topk):
    qi = pl.program_id(1)
    nkt = _num_key_tiles(qi, tq, kb, q_pos0, s_real)
    kf = float(topk)
    groups = kb // SUBLANES

    w = w_ref[0] * ((IDX_HEADS * IDX_DIM) ** -0.5)
    q_pos = q_pos0 + qi * tq + lax.broadcasted_iota(jnp.int32, (1, tq), 1)
    q_chunk = q_pos >> CHUNK_SHIFT
    first_key = jnp.minimum(((q_pos0 + qi * tq) >> CHUNK_SHIFT) << CHUNK_SHIFT, s_real)
    n_full = first_key // kb

    def score_tile(j, carry, masked):
        rmin, rmax = carry
        k0 = pl.multiple_of(j * kb, kb)
        kt = ki_ref[0, pl.ds(k0, kb), :]
        acc = None
        for p in range(IDX_HEADS // 2):
            rel = jnp.dot(kt, iq_ref[0, 0, p], preferred_element_type=F32)
            part = (w[2 * p:2 * p + 1] * jnp.maximum(rel[:, 0:tq], 0.0)
                    + w[2 * p + 1:2 * p + 2] * jnp.maximum(rel[:, tq:2 * tq], 0.0))
            acc = part if acc is None else acc + part
        if masked:
            key = k0 + lax.broadcasted_iota(jnp.int32, (kb, tq), 0)
            adm = jnp.logical_and((key >> CHUNK_SHIFT) <= q_chunk, key < s_real)
            low = jnp.where(adm, acc, jnp.inf)
            acc = jnp.where(adm, acc, -jnp.inf)
        else:
            low = acc
        score_ref[pl.ds(k0, kb), :] = acc
        rmin = jnp.minimum(rmin, jnp.min(low.reshape(groups, SUBLANES, tq), axis=0))
        rmax = jnp.maximum(rmax, jnp.max(acc.reshape(groups, SUBLANES, tq), axis=0))
        return rmin, rmax

    stats = (jnp.full((SUBLANES, tq), jnp.inf, F32), jnp.full((SUBLANES, tq), -jnp.inf, F32))
    stats = lax.fori_loop(0, n_full, functools.partial(score_tile, masked=False), stats)
    rmin8, rmax8 = lax.fori_loop(n_full, nkt, functools.partial(score_tile, masked=True), stats)
    rmin = jnp.min(rmin8, axis=0, keepdims=True)
    rmax = jnp.max(rmax8, axis=0, keepdims=True)

    def count(pred):
        def body(j, acc):
            k0 = pl.multiple_of(j * kb, kb)
            hit = jnp.where(pred(score_ref[pl.ds(k0, kb), :], k0), 1.0, 0.0)
            return acc + jnp.sum(hit.reshape(groups // COUNT_PARTS, COUNT_PARTS, SUBLANES, tq), axis=0)
        acc = lax.fori_loop(0, nkt, body, jnp.zeros((COUNT_PARTS, SUBLANES, tq), F32))
        return jnp.sum(jnp.sum(acc, axis=0), axis=0, keepdims=True)

    n_adm = jnp.minimum((q_chunk + 1) << CHUNK_SHIFT, s_real).astype(F32)
    lo0 = jnp.full((1, tq), -F32_MAX, F32)
    hi0 = jnp.full((1, tq), F32_MAX, F32)

    def open_rows(cnt_lo):
        return cnt_lo > kf

    def any_open(cnt_lo):
        return jnp.max(jnp.where(open_rows(cnt_lo), 1.0, 0.0))

    def cond(st):
        return jnp.logical_and(st[0] < SELECT_MAX_ITERS, st[1] > 0.0)

    def body(st):
        it, _, lo, hi, cnt_lo, cnt_hi = st
        active = open_rows(cnt_lo)
        mid = 0.5 * jnp.maximum(lo, rmin) + 0.5 * jnp.minimum(hi, rmax)
        cm = count(lambda s, k0: s >= mid)
        up = jnp.logical_and(active, cm >= kf)
        dn = jnp.logical_and(active, cm < kf)
        lo = jnp.where(up, mid, lo)
        cnt_lo = jnp.where(up, cm, cnt_lo)
        hi = jnp.where(dn, mid, hi)
        cnt_hi = jnp.where(dn, cm, cnt_hi)
        return it + 1, any_open(cnt_lo), lo, hi, cnt_lo, cnt_hi

    _, n_open, lo, hi, cnt_lo, cnt_hi = lax.while_loop(
        cond, body, (jnp.int32(0), any_open(n_adm), lo0, hi0, n_adm, jnp.zeros((1, tq), F32)))

    def clear_tile(j, carry):
        k0 = pl.multiple_of(j * kb, kb)
        mask_ref[0, pl.ds(k0, kb), :] = jnp.zeros((kb, tq), jnp.int8)
        return carry

    lax.fori_loop(nkt, s_pad // kb, clear_tile, 0)

    @pl.when(n_open == 0.0)
    def _():
        def write_tile(j, carry):
            k0 = pl.multiple_of(j * kb, kb)
            keep = score_ref[pl.ds(k0, kb), :] >= lo
            mask_ref[0, pl.ds(k0, kb), :] = jnp.where(keep, 1, 0).astype(jnp.int8)
            return carry

        lax.fori_loop(0, nkt, write_tile, 0)

    @pl.when(n_open > 0.0)
    def _():
        need = kf - cnt_hi
        key_iota = lax.broadcasted_iota(jnp.int32, (kb, tq), 0)

        def step(_, st):
            jlo, jhi = st
            jmid = (jlo + jhi) >> 1
            cm = count(lambda s, k0: jnp.logical_and(
                jnp.logical_and(s >= lo, s < hi), k0 + key_iota < jmid))
            ok = cm >= need
            return jnp.where(ok, jlo, jmid), jnp.where(ok, jmid, jhi)

        steps = int(math.ceil(math.log2(s_pad))) + 1
        _, jhi = lax.fori_loop(0, steps, step, (jnp.zeros((1, tq), jnp.int32),
                                                jnp.full((1, tq), s_pad, jnp.int32)))
        jcut = jnp.where(open_rows(cnt_lo), jhi, s_pad)

        def write_tile(j, carry):
            k0 = pl.multiple_of(j * kb, kb)
            s = score_ref[pl.ds(k0, kb), :]
            keep = jnp.logical_or(s >= hi, jnp.logical_and(s >= lo, k0 + key_iota < jcut))
            mask_ref[0, pl.ds(k0, kb), :] = jnp.where(keep, 1, 0).astype(jnp.int8)
            return carry

        lax.fori_loop(0, nkt, write_tile, 0)


def _select(iqt, wt, ki3, tq, kb, s_real, q_pos0, topk):
    b, nq = iqt.shape[0], iqt.shape[1]
    s_pad = ki3.shape[1]
    kern = functools.partial(_select_kernel, tq=tq, kb=kb, s_pad=s_pad, s_real=s_real,
                             q_pos0=q_pos0, topk=topk)
    return pl.pallas_call(
        kern,
        grid=(b, nq),
        in_specs=[pl.BlockSpec((1, 1, IDX_HEADS // 2, 3 * IDX_DIM, 2 * tq), lambda i, j: (i, j, 0, 0, 0)),
                  pl.BlockSpec((1, IDX_HEADS, tq), lambda i, j: (i, 0, j)),
                  pl.BlockSpec((1, s_pad, 3 * IDX_DIM), lambda i, j: (i, 0, 0))],
        out_specs=pl.BlockSpec((1, s_pad, tq), lambda i, j: (i, 0, j)),
        out_shape=jax.ShapeDtypeStruct((b, s_pad, nq * tq), jnp.int8),
        scratch_shapes=[pltpu.VMEM((s_pad, tq), F32)],
        compiler_params=_cparams(("parallel", "arbitrary")),
        name="index_select",
    )(iqt, wt, ki3)


def _attn_kernel(qt_ref, k_ref, vt_ref, mask_ref, o_ref, acc_ref, m_ref, l_ref, *, tq, tk, s_real, q_pos0):
    i = pl.program_id(1)
    j = pl.program_id(2)
    hd = D // A_HEADS
    scale = hd ** -0.5

    @pl.when(j == 0)
    def _():
        acc_ref[...] = jnp.zeros(acc_ref.shape, F32)
        m_ref[...] = jnp.full(m_ref.shape, MASK_NEG, F32)
        l_ref[...] = jnp.zeros(l_ref.shape, F32)

    @pl.when(j < _num_key_tiles(i, tq, tk, q_pos0, s_real))
    def _():
        bias = jnp.where(mask_ref[0].astype(jnp.int32) != 0, 0.0, MASK_NEG)
        for h in range(A_HEADS):
            hs = slice(h * hd, (h + 1) * hd)
            s = jnp.dot(k_ref[0, :, hs], qt_ref[0, hs, :], preferred_element_type=F32) * scale + bias
            m_prev = m_ref[h:h + 1, :]
            m_new = jnp.maximum(m_prev, jnp.max(s, axis=0, keepdims=True))
            alpha = jnp.exp(m_prev - m_new)
            p = jnp.exp(s - m_new)
            l_ref[h:h + 1, :] = alpha * l_ref[h:h + 1, :] + jnp.sum(p, axis=0, keepdims=True)
            acc_ref[hs, :] = alpha * acc_ref[hs, :] + jnp.dot(
                vt_ref[0, hs, :], p.astype(BF16), preferred_element_type=F32)
            m_ref[h:h + 1, :] = m_new

    @pl.when(j == pl.num_programs(2) - 1)
    def _():
        eye = _eye_bf16(tq)
        for h in range(A_HEADS):
            hs = slice(h * hd, (h + 1) * hd)
            out_t = (acc_ref[hs, :] / l_ref[h:h + 1, :]).astype(BF16)
            o_ref[0, :, hs] = _transpose_bf16(eye, out_t).astype(BF16)


def _attention(qt, kb, vt, mask_t, tq, tk, s_real, q_pos0):
    b, _, t = qt.shape
    s_pad = kb.shape[1]
    nq, nk = t // tq, s_pad // tk

    def last(q):
        return _num_key_tiles(q, tq, tk, q_pos0, s_real) - 1

    kern = functools.partial(_attn_kernel, tq=tq, tk=tk, s_real=s_real, q_pos0=q_pos0)
    return pl.pallas_call(
        kern,
        grid=(b, nq, nk),
        in_specs=[pl.BlockSpec((1, D, tq), lambda i, q, j: (i, 0, q)),
                  pl.BlockSpec((1, tk, D), lambda i, q, j: (i, jnp.minimum(j, last(q)), 0)),
                  pl.BlockSpec((1, D, tk), lambda i, q, j: (i, 0, jnp.minimum(j, last(q)))),
                  pl.BlockSpec((1, tk, tq), lambda i, q, j: (i, jnp.minimum(j, last(q)), q))],
        out_specs=pl.BlockSpec((1, tq, D), lambda i, q, j: (i, q, 0)),
        out_shape=jax.ShapeDtypeStruct((b, t, D), BF16),
        scratch_shapes=[pltpu.VMEM((D, tq), F32),
                        pltpu.VMEM((A_HEADS, tq), F32),
                        pltpu.VMEM((A_HEADS, tq), F32)],
        compiler_params=_cparams(("parallel", "parallel", "arbitrary")),
        name="masked_attention",
    )(qt, kb, vt, mask_t)


def _merge_kernel(x_ref, ya_ref, yb_ref, ga_ref, gb_ref, wa_ref, wb_ref, wo_ref, o_ref):
    a = jnp.dot(ya_ref[...], wa_ref[...], preferred_element_type=F32)
    bb = jnp.dot(yb_ref[...], wb_ref[...], preferred_element_type=F32)
    mix = jax.nn.sigmoid(ga_ref[...]) * a + jax.nn.sigmoid(gb_ref[...]) * bb
    o_ref[...] = x_ref[...] + jnp.dot(mix.astype(BF16), wo_ref[...], preferred_element_type=F32)


def _merge(x, ya, yb, z, wa, wb, wo, tm):
    n = x.shape[0]
    row = pl.BlockSpec((tm, D), lambda i: (i, 0))
    wspec = pl.BlockSpec((D, D), lambda i: (0, 0))
    return pl.pallas_call(
        _merge_kernel,
        grid=(n // tm,),
        in_specs=[row, row, row,
                  pl.BlockSpec((tm, D), lambda i: (i, COL_GA // D)),
                  pl.BlockSpec((tm, D), lambda i: (i, COL_GB // D)),
                  wspec, wspec, wspec],
        out_specs=row,
        out_shape=jax.ShapeDtypeStruct((n, D), F32),
        compiler_params=_cparams(("parallel",)),
        name="merge_out_proj",
    )(x, ya, yb, z, z, wa, wb, wo)


def _ffn_kernel(x_ref, g_ref, wg_ref, wu_ref, wd_ref, o_ref, xn_ref, acc_ref):
    c = pl.program_id(1)

    @pl.when(c == 0)
    def _():
        x = x_ref[...]
        ms = jnp.mean(x * x, axis=-1, keepdims=True)
        xn_ref[...] = ((x * lax.rsqrt(ms + EPS)) * g_ref[...]).astype(BF16)
        acc_ref[...] = x

    xn = xn_ref[...]
    gate = jnp.dot(xn, wg_ref[...], preferred_element_type=F32)
    up = jnp.dot(xn, wu_ref[...], preferred_element_type=F32)
    act = (gate * jax.nn.sigmoid(gate)) * up
    acc_ref[...] += jnp.dot(act.astype(BF16), wd_ref[...], preferred_element_type=F32)

    @pl.when(c == pl.num_programs(1) - 1)
    def _():
        o_ref[...] = acc_ref[...]


def _ffn(x, g, w_in, w_out, tm, tf):
    n = x.shape[0]
    dff = w_out.shape[0]
    nf = dff // tf
    row = pl.BlockSpec((tm, D), lambda i, c: (i, 0))
    return pl.pallas_call(
        _ffn_kernel,
        grid=(n // tm, nf),
        in_specs=[row, pl.BlockSpec((1, D), lambda i, c: (0, 0)),
                  pl.BlockSpec((D, tf), lambda i, c: (0, c)),
                  pl.BlockSpec((D, tf), lambda i, c: (0, nf + c)),
                  pl.BlockSpec((tf, D), lambda i, c: (c, 0))],
        out_specs=row,
        out_shape=jax.ShapeDtypeStruct((n, D), F32),
        scratch_shapes=[pltpu.VMEM((tm, D), BF16), pltpu.VMEM((tm, D), F32)],
        compiler_params=_cparams(("parallel", "arbitrary")),
        name="swiglu_ffn",
    )(x, g, w_in, w_in, w_out)


def _split_hi_lo(a):
    hi = a.astype(BF16)
    lo = (a - hi.astype(F32)).astype(BF16)
    return hi, lo


def _layer(x, past, p, cfg):
    k_past, v_past, ki_past, c0, n0, m0, conv0 = past
    b, t, _ = x.shape
    n_tok = b * t
    past_len = k_past.shape[1]
    s_real = past_len + t
    topk = min(TOPK_MAX, s_real // 4)
    tq_sel, kb_sel, tq_att, tk_att = cfg["tq_sel"], cfg["kb_sel"], cfg["tq_att"], cfg["tk_att"]
    s_pad = -(-s_real // tk_att) * tk_att

    x2 = x.reshape(n_tok, D)
    z = _norm_matmul(x2, p["g_norm1"], p["w_in"], cfg["tm_proj"], cfg["tn_proj"])
    z3 = z.reshape(b, t, D_Z)

    y_a, c_new, n_new, m_new, conv_new = _mlstm(
        z3, p["gate_bias"], p["w_conv"], p["b_conv"], p["g_mnorm"], c0, n0,
        m0.reshape(b, 1, M_HEADS), conv0, cfg["mlstm_chunk"])

    k32, v32, kb, qt, vt = _qkv(z3, p["g_q"], p["g_k"], cfg["tm_qkv"])
    ik = z3[:, :, COL_SMALL:COL_SMALL + IDX_DIM]
    nq = t // tq_sel
    q_hi, q_lo = _split_hi_lo(z[:, COL_IQ:COL_IQ + IDX_HEADS * IDX_DIM])

    def pairs_t(a):
        a = a.reshape(b, nq, tq_sel, IDX_HEADS // 2, 2, IDX_DIM).transpose(0, 1, 3, 5, 4, 2)
        return a.reshape(b, nq, IDX_HEADS // 2, IDX_DIM, 2 * tq_sel)

    iqt = jnp.concatenate([pairs_t(q_hi), pairs_t(q_hi), pairs_t(q_lo)], axis=3)
    wt = z3[:, :, COL_SMALL + SM_IW:COL_SMALL + SM_IW + IDX_HEADS].transpose(0, 2, 1)
    k_hi, k_lo = _split_hi_lo(jnp.concatenate([ki_past.astype(F32), ik], axis=1))
    ki3 = jnp.pad(jnp.concatenate([k_hi, k_lo, k_hi], axis=-1), ((0, 0), (0, s_pad - s_real), (0, 0)))
    mask_t = _select(iqt, wt, ki3, tq_sel, kb_sel, s_real, past_len, topk)

    k_all = jnp.concatenate([k_past.reshape(b, past_len, D).astype(BF16), kb,
                             jnp.zeros((b, s_pad - s_real, D), BF16)], axis=1)
    vt_all = jnp.concatenate([v_past.reshape(b, past_len, D).astype(BF16).transpose(0, 2, 1), vt,
                              jnp.zeros((b, D, s_pad - s_real), BF16)], axis=2)
    y_b = _attention(qt, k_all, vt_all, mask_t, tq_att, tk_att, s_real, past_len)

    x1 = _merge(x2, y_a.reshape(n_tok, D), y_b.reshape(n_tok, D), z,
                p["w_a_out"], p["w_b_out"], p["w_o"], cfg["tm_rows"])
    y = _ffn(x1, p["g_norm2"], p["w_ffn_in"], p["w_ffn_out"], cfg["tm_ffn"], cfg["tf_ffn"])

    hd = D // A_HEADS
    return y.reshape(b, t, D), (k32.reshape(b, t, A_HEADS, hd), v32.reshape(b, t, A_HEADS, hd), ik,
                                c_new, n_new, m_new.reshape(b, M_HEADS), conv_new)


def _prep_params(g_norm1, w_in, b_if, w_conv, b_conv, g_mnorm, g_q, g_k, w_a_out, w_b_out, w_o,
                 g_norm2, w_ffn_in, w_ffn_out):
    o_mi = 4 * D
    o_aq = o_mi + 2 * M_HEADS
    o_iq = o_aq + 3 * D
    o_ik = o_iq + IDX_HEADS * IDX_DIM
    o_iw = o_ik + IDX_DIM
    o_ga = o_iw + IDX_HEADS
    w_perm = jnp.concatenate([
        w_in[:, 0:o_mi], w_in[:, o_aq:o_iq], w_in[:, o_ga:o_ga + 2 * D], w_in[:, o_iq:o_ik],
        w_in[:, o_ik:o_ga], w_in[:, o_mi:o_aq],
        jnp.zeros((D, D_Z - COL_SMALL - (SM_MF + M_HEADS)), w_in.dtype)], axis=1).astype(BF16)
    gate_bias = jnp.zeros((1, LANES), F32).at[0, SM_MI:SM_MI + 2 * M_HEADS].set(b_if.astype(F32))
    return {
        "g_norm1": g_norm1.reshape(1, D), "w_in": w_perm, "gate_bias": gate_bias,
        "w_conv": w_conv, "b_conv": b_conv.reshape(1, 2 * D), "g_mnorm": g_mnorm.reshape(1, D),
        "g_q": g_q.reshape(1, -1), "g_k": g_k.reshape(1, -1),
        "w_a_out": w_a_out.astype(BF16), "w_b_out": w_b_out.astype(BF16), "w_o": w_o.astype(BF16),
        "g_norm2": g_norm2.reshape(1, D), "w_ffn_in": w_ffn_in.astype(BF16),
        "w_ffn_out": w_ffn_out.astype(BF16),
    }


def _config(b, t, past_len):
    n_tok = b * t
    s_real = past_len + t
    tm = min(1024, n_tok)
    tq_sel = min(128, t)
    tq_att = min(256, t)
    if s_real % 512 == 0:
        kb_sel = tk_att = 512
    else:
        kb_sel = tk_att = -(-s_real // LANES) * LANES
    return {"tm_proj": tm, "tn_proj": 896, "mlstm_chunk": min(128, t), "tm_rows": min(512, n_tok),
            "tm_qkv": min(512, t), "tq_sel": tq_sel, "kb_sel": kb_sel, "tq_att": tq_att,
            "tk_att": tk_att, "tm_ffn": min(512, n_tok), "tf_ffn": 1408}


def kernel(x_prompt, x_sample, cache_k, cache_v, cache_kidx, state_C, state_n, state_m, state_conv,
           g_norm1, w_in, b_if, w_conv, b_conv, g_mnorm, g_q, g_k, w_a_out, w_b_out, w_o,
           g_norm2, w_ffn_in, w_ffn_out):
    depth = w_in.shape[0]
    bp = x_prompt.shape[0]
    hd_a = D // A_HEADS
    hd_m = D // M_HEADS
    yp, ys = x_prompt, x_sample
    new_p, new_s = [], []
    for l in range(depth):
        p = _prep_params(g_norm1[l], w_in[l], b_if[l], w_conv[l], b_conv[l], g_mnorm[l], g_q[l],
                         g_k[l], w_a_out[l], w_b_out[l], w_o[l], g_norm2[l], w_ffn_in[l],
                         w_ffn_out[l])
        empty = (jnp.zeros((bp, 0, A_HEADS, hd_a), F32), jnp.zeros((bp, 0, A_HEADS, hd_a), F32),
                 jnp.zeros((bp, 0, IDX_DIM), F32), jnp.zeros((bp, M_HEADS, hd_m, hd_m), F32),
                 jnp.zeros((bp, M_HEADS, hd_m), F32), jnp.zeros((bp, M_HEADS), F32),
                 jnp.zeros((bp, CONV_W - 1, 2 * D), F32))
        yp, sp = _layer(yp, empty, p, _config(bp, yp.shape[1], 0))
        ys, ss = _layer(ys, (cache_k[l], cache_v[l], cache_kidx[l], state_C[l], state_n[l],
                             state_m[l], state_conv[l]), p,
                        _config(ys.shape[0], ys.shape[1], cache_k.shape[2]))
        new_p.append(sp)
        new_s.append(ss)

    def stk(lst, i):
        return jnp.stack([s[i] for s in lst])

    return (yp, ys) + tuple(stk(new_p, i) for i in range(7)) + tuple(stk(new_s, i) for i in range(7))
```

```python
import functools
import math

import jax
import jax.numpy as jnp
from jax import lax
from jax.experimental import pallas as pl
from jax.experimental.pallas import tpu as pltpu

F32 = jnp.float32
BF16 = jnp.bfloat16
HIGHEST = lax.Precision.HIGHEST

EPS = 1e-6
CHUNK = 64
CHUNK_SHIFT = 6
M_HEADS = 4
A_HEADS = 8
IDX_HEADS = 8
IDX_DIM = 64
CONV_W = 4
TOPK_MAX = 256
LANES = 128
SUBLANES = 8
MASK_NEG = -1e30
F32_MAX = float(jnp.finfo(jnp.float32).max)
VMEM_LIMIT = 48 * 1024 * 1024
SELECT_MAX_ITERS = 40
SELECT_MIN_ITERS = 12
FINISH_MAX = 4
COUNT_PARTS = 4
COUNT_ROWS = 128

D = 1024
COL_MQ, COL_MK, COL_MV, COL_MO = 0, 1024, 2048, 3072
COL_AQ, COL_AK, COL_AV = 4096, 5120, 6144
COL_GA, COL_GB = 7168, 8192
COL_IQ = 9216
COL_SMALL = 9728
D_Z = 9856
SM_IW, SM_MI, SM_MF = 64, 72, 76
Q_LOGIT_SCALE = math.log2(math.e) * (D // A_HEADS) ** -0.5


def _cparams(sem):
    return pltpu.CompilerParams(dimension_semantics=sem, vmem_limit_bytes=VMEM_LIMIT)


def _norm_matmul_kernel(x_ref, g_ref, w_ref, o_ref, xn_ref):
    @pl.when(pl.program_id(1) == 0)
    def _():
        x = x_ref[...]
        ms = jnp.mean(x * x, axis=-1, keepdims=True)
        xn_ref[...] = ((x * lax.rsqrt(ms + EPS)) * g_ref[...]).astype(BF16)

    o_ref[...] = jnp.dot(xn_ref[...], w_ref[...], preferred_element_type=F32)


def _norm_matmul(x, g, w, tm, tn):
    n, d = x.shape
    nout = w.shape[1]
    return pl.pallas_call(
        _norm_matmul_kernel,
        grid=(n // tm, nout // tn),
        in_specs=[pl.BlockSpec((tm, d), lambda i, j: (i, 0)),
                  pl.BlockSpec((1, d), lambda i, j: (0, 0)),
                  pl.BlockSpec((d, tn), lambda i, j: (0, j))],
        out_specs=pl.BlockSpec((tm, tn), lambda i, j: (i, j)),
        out_shape=jax.ShapeDtypeStruct((n, nout), F32),
        scratch_shapes=[pltpu.VMEM((tm, d), BF16)],
        compiler_params=_cparams(("parallel", "arbitrary")),
        name="norm_in_proj",
    )(x, g, w)


def _mlstm_kernel(mq_ref, mk_ref, mv_ref, mo_ref, sm_ref, bias_ref, wconv_ref, bconv_ref,
                  gm_ref, c0_ref, n0_ref, m0_ref, conv0_ref,
                  y_ref, c_ref, n_ref, m_ref, conv_ref, cbuf_ref, *, L):
    c = pl.program_id(1)
    hd = D // M_HEADS

    @pl.when(c == 0)
    def _():
        c_ref[...] = c0_ref[...]
        n_ref[...] = n0_ref[...]
        m_ref[...] = m0_ref[...]
        cbuf_ref[8 - (CONV_W - 1):8, :] = conv0_ref[0]

    cbuf_ref[8:8 + L, 0:D] = mq_ref[0]
    cbuf_ref[8:8 + L, D:2 * D] = mk_ref[0]
    wc = wconv_ref[...]
    qk = bconv_ref[...] + cbuf_ref[5:5 + L, :] * wc[0:1, :]
    for j in range(1, CONV_W):
        qk = qk + cbuf_ref[5 + j:5 + j + L, :] * wc[j:j + 1, :]
    tail = cbuf_ref[5 + L:8 + L, :]
    cbuf_ref[5:8, :] = tail
    conv_ref[0] = tail
    qk = qk * jax.nn.sigmoid(qk)

    g_all = sm_ref[0] + bias_ref[...]
    lf_all = jnp.minimum(g_all, 0.0) - jnp.log1p(jnp.exp(-jnp.abs(g_all)))
    lane = lax.broadcasted_iota(jnp.int32, (L, LANES), 1)
    gates = jnp.where(lane >= SM_MF, lf_all, g_all)
    r_i = lax.broadcasted_iota(jnp.int32, (L, L), 0)
    c_i = lax.broadcasted_iota(jnp.int32, (L, L), 1)
    tril = (c_i <= r_i).astype(F32)
    triu = (r_i <= c_i).astype(F32)
    b_col_all = jnp.dot(tril, lf_all, precision=HIGHEST, preferred_element_type=F32)
    sel = (lax.broadcasted_iota(jnp.int32, (8, LANES), 1)
           == lax.broadcasted_iota(jnp.int32, (8, LANES), 0) + SM_MI).astype(F32)
    rows = lax.dot_general(sel, gates, (((1,), (1,)), ((), ())), precision=HIGHEST,
                           preferred_element_type=F32)
    b_row_all = jnp.dot(rows, triu, precision=HIGHEST, preferred_element_type=F32)
    causal = c_i <= r_i

    mo = mo_ref[0]
    mv = mv_ref[0]
    for h in range(M_HEADS):
        hs = slice(h * hd, (h + 1) * hd)
        qh = qk[:, hs]
        kh = qk[:, D + h * hd:D + (h + 1) * hd] * (hd ** -0.5)
        vh = mv[:, hs]
        qb, kb, vb = qh.astype(BF16), kh.astype(BF16), vh.astype(BF16)
        b_col = b_col_all[:, SM_MF + h:SM_MF + h + 1]
        i_col = gates[:, SM_MI + h:SM_MI + h + 1]
        b_row = b_row_all[M_HEADS + h:M_HEADS + h + 1, :]
        i_row = rows[h:h + 1, :]
        m_prev = m_ref[0, :, h:h + 1]
        c_prev = c_ref[0, h]
        n_prev = n_ref[0, h:h + 1, :]

        dmat = jnp.where(causal, b_col - b_row + i_row, -jnp.inf)
        inter = b_col + m_prev
        m_t = jnp.maximum(inter, jnp.max(dmat, axis=-1, keepdims=True))
        w_intra = jnp.exp(dmat - m_t)
        w_inter = jnp.exp(inter - m_t)
        s = lax.dot_general(qb, kb, (((1,), (1,)), ((), ())), preferred_element_type=F32) * w_intra
        qc = lax.dot_general(qb, c_prev.astype(BF16), (((1,), (1,)), ((), ())),
                             preferred_element_type=F32)
        num = jnp.dot(s.astype(BF16), vb, preferred_element_type=F32) + w_inter * qc
        den = jnp.sum(s, axis=-1, keepdims=True) + w_inter * jnp.sum(qh * n_prev, axis=-1, keepdims=True)
        denom = jnp.maximum(jnp.abs(den), jnp.exp(-m_t))
        hh = num / denom

        m_new = m_t[L - 1:L, :]
        b_last = b_col[L - 1:L, :]
        g_col = jnp.exp(b_last - b_col + i_col - m_new)
        decay = jnp.exp(b_last + m_prev - m_new)
        gv = (g_col * vh).astype(BF16)
        c_ref[0, h] = decay * c_prev + lax.dot_general(
            gv, kb, (((0,), (0,)), ((), ())), preferred_element_type=F32)
        n_ref[0, h:h + 1, :] = decay * n_prev + jnp.sum(g_col * kh, axis=0, keepdims=True)
        m_ref[0, :, h:h + 1] = m_new

        hn = hh * lax.rsqrt(jnp.mean(hh * hh, axis=-1, keepdims=True) + EPS) * gm_ref[:, hs]
        y_ref[0, :, hs] = (hn * jax.nn.sigmoid(mo[:, hs])).astype(BF16)


def _mlstm(z3, bias_row, w_conv, b_conv, g_mnorm, c0, n0, m0, conv0, L):
    b, t, _ = z3.shape
    nc = t // L
    hd = D // M_HEADS

    def zspec(col, width):
        return pl.BlockSpec((1, L, width), lambda i, c: (i, c, col // width))

    def per_batch(shape):
        nd = len(shape)
        return pl.BlockSpec((1,) + shape, lambda i, c: (i,) + (0,) * nd)

    def const(shape):
        nd = len(shape)
        return pl.BlockSpec(shape, lambda i, c: (0,) * nd)

    return pl.pallas_call(
        functools.partial(_mlstm_kernel, L=L),
        grid=(b, nc),
        in_specs=[zspec(COL_MQ, D), zspec(COL_MK, D), zspec(COL_MV, D), zspec(COL_MO, D),
                  zspec(COL_SMALL, LANES), const((1, LANES)), const((CONV_W, 2 * D)),
                  const((1, 2 * D)), const((1, D)),
                  per_batch((M_HEADS, hd, hd)), per_batch((M_HEADS, hd)),
                  per_batch((1, M_HEADS)), per_batch((CONV_W - 1, 2 * D))],
        out_specs=[pl.BlockSpec((1, L, D), lambda i, c: (i, c, 0)),
                   per_batch((M_HEADS, hd, hd)), per_batch((M_HEADS, hd)),
                   per_batch((1, M_HEADS)), per_batch((CONV_W - 1, 2 * D))],
        out_shape=[jax.ShapeDtypeStruct((b, t, D), BF16),
                   jax.ShapeDtypeStruct((b, M_HEADS, hd, hd), F32),
                   jax.ShapeDtypeStruct((b, M_HEADS, hd), F32),
                   jax.ShapeDtypeStruct((b, 1, M_HEADS), F32),
                   jax.ShapeDtypeStruct((b, CONV_W - 1, 2 * D), F32)],
        scratch_shapes=[pltpu.VMEM((8 + L, 2 * D), F32)],
        compiler_params=_cparams(("parallel", "arbitrary")),
        name="mlstm",
    )(z3, z3, z3, z3, z3, bias_row, w_conv, b_conv, g_mnorm, c0, n0, m0, conv0)


def _eye_bf16(n):
    return (lax.broadcasted_iota(jnp.int32, (n, n), 0)
            == lax.broadcasted_iota(jnp.int32, (n, n), 1)).astype(BF16)


def _transpose_bf16(eye, a):
    return lax.dot_general(eye, a, (((1,), (1,)), ((), ())), preferred_element_type=F32)


def _qkv_kernel(aq_ref, ak_ref, av_ref, gq_ref, gk_ref, k32_ref, v32_ref, kb_ref, qt_ref, vt_ref):
    hd = D // A_HEADS
    eye = _eye_bf16(hd)
    for h in range(A_HEADS):
        hs = slice(h * hd, (h + 1) * hd)
        q = aq_ref[0, :, hs]
        k = ak_ref[0, :, hs]
        v = av_ref[0, :, hs]
        qn = (q * lax.rsqrt(jnp.mean(q * q, axis=-1, keepdims=True) + EPS)) * gq_ref[...]
        kn = (k * lax.rsqrt(jnp.mean(k * k, axis=-1, keepdims=True) + EPS)) * gk_ref[...]
        k32_ref[0, :, hs] = kn
        kb_ref[0, :, hs] = kn.astype(BF16)
        v32_ref[0, :, hs] = v
        qt_ref[0, hs, :] = _transpose_bf16(eye, (qn * Q_LOGIT_SCALE).astype(BF16)).astype(BF16)
        vt_ref[0, hs, :] = _transpose_bf16(eye, v.astype(BF16)).astype(BF16)


def _qkv(z3, g_q, g_k, tm):
    b, t, _ = z3.shape
    hd = D // A_HEADS

    def zspec(col):
        return pl.BlockSpec((1, tm, D), lambda i, j: (i, j, col // D))

    row = pl.BlockSpec((1, tm, D), lambda i, j: (i, j, 0))
    colb = pl.BlockSpec((1, D, tm), lambda i, j: (i, 0, j))
    gspec = pl.BlockSpec((1, hd), lambda i, j: (0, 0))
    return pl.pallas_call(
        _qkv_kernel,
        grid=(b, t // tm),
        in_specs=[zspec(COL_AQ), zspec(COL_AK), zspec(COL_AV), gspec, gspec],
        out_specs=[row, row, row, colb, colb],
        out_shape=[jax.ShapeDtypeStruct((b, t, D), F32), jax.ShapeDtypeStruct((b, t, D), F32),
                   jax.ShapeDtypeStruct((b, t, D), BF16), jax.ShapeDtypeStruct((b, D, t), BF16),
                   jax.ShapeDtypeStruct((b, D, t), BF16)],
        compiler_params=_cparams(("parallel", "parallel")),
        name="qkv_norm",
    )(z3, z3, z3, g_q, g_k)


def _num_key_tiles(qi, tq, tile, q_pos0, s_real):
    last_chunk = (q_pos0 + (qi + 1) * tq - 1) // CHUNK
    kend = jnp.minimum((last_chunk + 1) * CHUNK, s_real)
    return (kend + tile - 1) // tile


def _num_key_tiles_static(qi, tq, tile, q_pos0, s_real):
    last_chunk = (q_pos0 + (qi + 1) * tq - 1) // CHUNK
    return -(-min((last_chunk + 1) * CHUNK, s_real) // tile)


def _select_kernel(iq_ref, sm_ref, ki_ref, mask_ref, score_ref, iqt_ref, *,
                   tq, kb, s_pad, s_real, q_pos0, topk):
    qi = pl.program_id(1)
    nkt = _num_key_tiles(qi, tq, kb, q_pos0, s_real)
    kf = float(topk)
    groups = kb // SUBLANES

    pick = (lax.broadcasted_iota(jnp.int32, (IDX_HEADS, LANES), 1)
            == lax.broadcasted_iota(jnp.int32, (IDX_HEADS, LANES), 0) + SM_IW).astype(F32)
    w = lax.dot_general(pick, sm_ref[0], (((1,), (1,)), ((), ())), precision=HIGHEST,
                        preferred_element_type=F32) * ((IDX_HEADS * IDX_DIM) ** -0.5)
    eye = _eye_bf16(IDX_DIM)
    for h in range(IDX_HEADS):
        a = iq_ref[0, :, h * IDX_DIM:(h + 1) * IDX_DIM]
        hi = a.astype(BF16)
        lo = (a - hi.astype(F32)).astype(BF16)
        hi_t = _transpose_bf16(eye, hi).astype(BF16)
        lo_t = _transpose_bf16(eye, lo).astype(BF16)
        cols = slice((h % 2) * tq, (h % 2 + 1) * tq)
        iqt_ref[h // 2, 0:IDX_DIM, cols] = hi_t
        iqt_ref[h // 2, IDX_DIM:2 * IDX_DIM, cols] = hi_t
        iqt_ref[h // 2, 2 * IDX_DIM:3 * IDX_DIM, cols] = lo_t

    q_pos = q_pos0 + qi * tq + lax.broadcasted_iota(jnp.int32, (1, tq), 1)
    q_chunk = q_pos >> CHUNK_SHIFT
    first_key = jnp.minimum(((q_pos0 + qi * tq) >> CHUNK_SHIFT) << CHUNK_SHIFT, s_real)
    n_full = first_key // kb

    def score_tile(j, carry, masked):
        rmin, rmax = carry
        k0 = pl.multiple_of(j * kb, kb)
        kt = ki_ref[0, pl.ds(k0, kb), :]
        acc = None
        for p in range(IDX_HEADS // 2):
            rel = jnp.dot(kt, iqt_ref[p], preferred_element_type=F32)
            part = (w[2 * p:2 * p + 1] * jnp.maximum(rel[:, 0:tq], 0.0)
                    + w[2 * p + 1:2 * p + 2] * jnp.maximum(rel[:, tq:2 * tq], 0.0))
            acc = part if acc is None else acc + part
        if masked:
            key = k0 + lax.broadcasted_iota(jnp.int32, (kb, tq), 0)
            adm = jnp.logical_and((key >> CHUNK_SHIFT) <= q_chunk, key < s_real)
            low = jnp.where(adm, acc, jnp.inf)
            acc = jnp.where(adm, acc, -jnp.inf)
        else:
            low = acc
        score_ref[pl.ds(k0, kb), :] = acc
        rmin = jnp.minimum(rmin, jnp.min(low.reshape(groups, SUBLANES, tq), axis=0))
        rmax = jnp.maximum(rmax, jnp.max(acc.reshape(groups, SUBLANES, tq), axis=0))
        return rmin, rmax

    stats = (jnp.full((SUBLANES, tq), jnp.inf, F32), jnp.full((SUBLANES, tq), -jnp.inf, F32))
    stats = lax.fori_loop(0, n_full, functools.partial(score_tile, masked=False), stats)
    rmin8, rmax8 = lax.fori_loop(n_full, nkt, functools.partial(score_tile, masked=True), stats)
    rmin = jnp.min(rmin8, axis=0, keepdims=True)
    rmax = jnp.max(rmax8, axis=0, keepdims=True)

    def count(pred):
        def body(j, acc):
            for c in range(kb // COUNT_ROWS):
                k0 = pl.multiple_of(j * kb + c * COUNT_ROWS, COUNT_ROWS)
                hit = jnp.where(pred(score_ref[pl.ds(k0, COUNT_ROWS), :], k0), 1.0, 0.0)
                acc = acc + jnp.sum(hit.reshape(COUNT_ROWS // (COUNT_PARTS * SUBLANES), COUNT_PARTS,
                                                SUBLANES, tq), axis=0)
            return acc
        acc = lax.fori_loop(0, nkt, body, jnp.zeros((COUNT_PARTS, SUBLANES, tq), F32))
        return jnp.sum(jnp.sum(acc, axis=0), axis=0, keepdims=True)

    def max_below(cur):
        def body(j, acc):
            for c in range(kb // COUNT_ROWS):
                k0 = pl.multiple_of(j * kb + c * COUNT_ROWS, COUNT_ROWS)
                s = score_ref[pl.ds(k0, COUNT_ROWS), :]
                low = jnp.where(s < cur, s, -jnp.inf)
                acc = jnp.maximum(acc, jnp.max(low.reshape(COUNT_ROWS // (COUNT_PARTS * SUBLANES),
                                                           COUNT_PARTS, SUBLANES, tq), axis=0))
            return acc
        acc = lax.fori_loop(0, nkt, body, jnp.full((COUNT_PARTS, SUBLANES, tq), -jnp.inf, F32))
        return jnp.max(jnp.max(acc, axis=0), axis=0, keepdims=True)

    n_adm = jnp.minimum((q_chunk + 1) << CHUNK_SHIFT, s_real).astype(F32)

    def open_rows(cnt_lo):
        return cnt_lo > kf

    def wide_rows(st):
        return jnp.logical_and(open_rows(st[2]), st[2] - st[3] > FINISH_MAX)

    def flag(pred):
        return jnp.max(jnp.where(pred, 1.0, 0.0))

    def bisect(st):
        lo, hi, cnt_lo, cnt_hi = st
        active = open_rows(cnt_lo)
        mid = 0.5 * jnp.maximum(lo, rmin) + 0.5 * jnp.minimum(hi, rmax)
        cm = count(lambda s, k0: s >= mid)
        up = jnp.logical_and(active, cm >= kf)
        dn = jnp.logical_and(active, cm < kf)
        return (jnp.where(up, mid, lo), jnp.where(dn, mid, hi),
                jnp.where(up, cm, cnt_lo), jnp.where(dn, cm, cnt_hi))

    def bisect_while(still, it, st):
        def cond(c):
            return jnp.logical_and(c[0] < SELECT_MAX_ITERS, c[1] > 0.0)

        def body(c):
            nxt = bisect(c[2])
            return c[0] + 1, flag(still(nxt)), nxt

        return lax.while_loop(cond, body, (it, flag(still(st)), st))

    st = (jnp.full((1, tq), -F32_MAX, F32), jnp.full((1, tq), F32_MAX, F32), n_adm,
          jnp.zeros((1, tq), F32))
    st = lax.fori_loop(0, SELECT_MIN_ITERS, lambda _, s: bisect(s), st)
    it, _, st = bisect_while(wide_rows, jnp.int32(SELECT_MIN_ITERS), st)

    lo, hi, cnt_lo, cnt_hi = st
    narrow = jnp.logical_and(open_rows(cnt_lo), cnt_lo - cnt_hi <= FINISH_MAX)
    rank = kf - cnt_hi
    cur = hi
    for r in range(1, FINISH_MAX):
        cur = jnp.where(jnp.logical_and(narrow, rank >= r), max_below(cur), cur)
    exact = jnp.logical_and(narrow, count(lambda s, k0: s >= cur) == kf)
    st = (jnp.where(exact, cur, lo), hi, jnp.where(exact, kf, cnt_lo), cnt_hi)
    _, n_open, (lo, hi, cnt_lo, cnt_hi) = bisect_while(lambda s: open_rows(s[2]), it, st)

    def clear_tile(j, carry):
        k0 = pl.multiple_of(j * kb, kb)
        mask_ref[0, pl.ds(k0, kb), :] = jnp.zeros((kb, tq), jnp.int8)
        return carry

    lax.fori_loop(nkt, s_pad // kb, clear_tile, 0)

    @pl.when(n_open == 0.0)
    def _():
        def write_tile(j, carry):
            k0 = pl.multiple_of(j * kb, kb)
            keep = score_ref[pl.ds(k0, kb), :] >= lo
            mask_ref[0, pl.ds(k0, kb), :] = jnp.where(keep, 1, 0).astype(jnp.int8)
            return carry

        lax.fori_loop(0, nkt, write_tile, 0)

    @pl.when(n_open > 0.0)
    def _():
        need = kf - cnt_hi
        key_iota = lax.broadcasted_iota(jnp.int32, (kb, tq), 0)
        strip_iota = lax.broadcasted_iota(jnp.int32, (COUNT_ROWS, tq), 0)

        def step(_, st):
            jlo, jhi = st
            jmid = (jlo + jhi) >> 1
            cm = count(lambda s, k0: jnp.logical_and(
                jnp.logical_and(s >= lo, s < hi), k0 + strip_iota < jmid))
            ok = cm >= need
            return jnp.where(ok, jlo, jmid), jnp.where(ok, jmid, jhi)

        steps = int(math.ceil(math.log2(s_pad))) + 1
        _, jhi = lax.fori_loop(0, steps, step, (jnp.zeros((1, tq), jnp.int32),
                                                jnp.full((1, tq), s_pad, jnp.int32)))
        jcut = jnp.where(open_rows(cnt_lo), jhi, s_pad)

        def write_tile(j, carry):
            k0 = pl.multiple_of(j * kb, kb)
            s = score_ref[pl.ds(k0, kb), :]
            keep = jnp.logical_or(s >= hi, jnp.logical_and(s >= lo, k0 + key_iota < jcut))
            mask_ref[0, pl.ds(k0, kb), :] = jnp.where(keep, 1, 0).astype(jnp.int8)
            return carry

        lax.fori_loop(0, nkt, write_tile, 0)


def _select(z3, ki3, tq, kb, s_real, q_pos0, topk):
    b, t, _ = z3.shape
    s_pad = ki3.shape[1]
    iq_w = IDX_HEADS * IDX_DIM
    kern = functools.partial(_select_kernel, tq=tq, kb=kb, s_pad=s_pad, s_real=s_real,
                             q_pos0=q_pos0, topk=topk)
    return pl.pallas_call(
        kern,
        grid=(b, t // tq),
        in_specs=[pl.BlockSpec((1, tq, iq_w), lambda i, j: (i, j, COL_IQ // iq_w)),
                  pl.BlockSpec((1, tq, LANES), lambda i, j: (i, j, COL_SMALL // LANES)),
                  pl.BlockSpec((1, s_pad, 3 * IDX_DIM), lambda i, j: (i, 0, 0))],
        out_specs=pl.BlockSpec((1, s_pad, tq), lambda i, j: (i, 0, j)),
        out_shape=jax.ShapeDtypeStruct((b, s_pad, t), jnp.int8),
        scratch_shapes=[pltpu.VMEM((s_pad, tq), F32),
                        pltpu.VMEM((IDX_HEADS // 2, 3 * IDX_DIM, 2 * tq), BF16)],
        compiler_params=_cparams(("parallel", "arbitrary")),
        name="index_select",
    )(z3, z3, ki3)


def _attn_kernel(q_of_ref, k_of_ref, qt_ref, k_ref, vt_ref, mask_ref, o_ref, acc_ref, m_ref, l_ref, s_ref):
    step = pl.program_id(1)
    hd = D // A_HEADS
    tq = qt_ref.shape[2]
    first = k_of_ref[step] == 0
    nxt = jnp.minimum(step + 1, pl.num_programs(1) - 1)
    last = jnp.logical_or(step == pl.num_programs(1) - 1, q_of_ref[nxt] != q_of_ref[step])

    @pl.when(first)
    def _():
        acc_ref[...] = jnp.zeros(acc_ref.shape, F32)
        m_ref[...] = jnp.full(m_ref.shape, MASK_NEG, F32)
        l_ref[...] = jnp.zeros(l_ref.shape, F32)

    bias = jnp.where(mask_ref[0].astype(jnp.int32) != 0, 0.0, MASK_NEG)
    tile_max = []
    for h in range(A_HEADS):
        hs = slice(h * hd, (h + 1) * hd)
        s = jnp.dot(k_ref[0, :, hs], qt_ref[0, hs, :], preferred_element_type=F32) + bias
        s_ref[h] = s
        tile_max.append(jnp.max(s, axis=0, keepdims=True))
    for h in range(A_HEADS):
        hs = slice(h * hd, (h + 1) * hd)
        m_prev = m_ref[h:h + 1, :]
        m_new = jnp.maximum(m_prev, tile_max[h])
        alpha = jnp.exp2(m_prev - m_new)
        p = jnp.exp2(s_ref[h] - m_new)
        l_ref[h:h + 1, :] = alpha * l_ref[h:h + 1, :] + jnp.sum(p, axis=0, keepdims=True)
        acc_ref[hs, :] = alpha * acc_ref[hs, :] + jnp.dot(
            vt_ref[0, hs, :], p.astype(BF16), preferred_element_type=F32)
        m_ref[h:h + 1, :] = m_new

    @pl.when(last)
    def _():
        eye = _eye_bf16(tq)
        for h in range(A_HEADS):
            hs = slice(h * hd, (h + 1) * hd)
            out_t = (acc_ref[hs, :] / l_ref[h:h + 1, :]).astype(BF16)
            o_ref[0, :, hs] = _transpose_bf16(eye, out_t).astype(BF16)


def _attention(qt, kb, vt, mask_t, tq, tk, s_real, q_pos0):
    b, _, t = qt.shape
    steps = [(q, j) for q in range(t // tq) for j in range(_num_key_tiles_static(q, tq, tk, q_pos0, s_real))]
    q_of = jnp.array([q for q, _ in steps], jnp.int32)
    k_of = jnp.array([j for _, j in steps], jnp.int32)
    return pl.pallas_call(
        _attn_kernel,
        grid_spec=pltpu.PrefetchScalarGridSpec(
            num_scalar_prefetch=2,
            grid=(b, len(steps)),
            in_specs=[pl.BlockSpec((1, D, tq), lambda i, s, q_of, k_of: (i, 0, q_of[s])),
                      pl.BlockSpec((1, tk, D), lambda i, s, q_of, k_of: (i, k_of[s], 0)),
                      pl.BlockSpec((1, D, tk), lambda i, s, q_of, k_of: (i, 0, k_of[s])),
                      pl.BlockSpec((1, tk, tq), lambda i, s, q_of, k_of: (i, k_of[s], q_of[s]))],
            out_specs=pl.BlockSpec((1, tq, D), lambda i, s, q_of, k_of: (i, q_of[s], 0)),
            scratch_shapes=[pltpu.VMEM((D, tq), F32),
                            pltpu.VMEM((A_HEADS, tq), F32),
                            pltpu.VMEM((A_HEADS, tq), F32),
                            pltpu.VMEM((A_HEADS, tk, tq), F32)]),
        out_shape=jax.ShapeDtypeStruct((b, t, D), BF16),
        compiler_params=_cparams(("parallel", "arbitrary")),
        name="masked_attention",
    )(q_of, k_of, qt, kb, vt, mask_t)


def _merge_kernel(x_ref, ya_ref, yb_ref, ga_ref, gb_ref, wa_ref, wb_ref, wo_ref, o_ref):
    a = jnp.dot(ya_ref[...], wa_ref[...], preferred_element_type=F32)
    bb = jnp.dot(yb_ref[...], wb_ref[...], preferred_element_type=F32)
    mix = jax.nn.sigmoid(ga_ref[...]) * a + jax.nn.sigmoid(gb_ref[...]) * bb
    o_ref[...] = x_ref[...] + jnp.dot(mix.astype(BF16), wo_ref[...], preferred_element_type=F32)


def _merge(x, ya, yb, z, wa, wb, wo, tm):
    n = x.shape[0]
    row = pl.BlockSpec((tm, D), lambda i: (i, 0))
    wspec = pl.BlockSpec((D, D), lambda i: (0, 0))
    return pl.pallas_call(
        _merge_kernel,
        grid=(n // tm,),
        in_specs=[row, row, row,
                  pl.BlockSpec((tm, D), lambda i: (i, COL_GA // D)),
                  pl.BlockSpec((tm, D), lambda i: (i, COL_GB // D)),
                  wspec, wspec, wspec],
        out_specs=row,
        out_shape=jax.ShapeDtypeStruct((n, D), F32),
        compiler_params=_cparams(("parallel",)),
        name="merge_out_proj",
    )(x, ya, yb, z, z, wa, wb, wo)


def _ffn_kernel(x_ref, g_ref, wg_ref, wu_ref, wd_ref, o_ref, xn_ref, acc_ref):
    c = pl.program_id(1)

    @pl.when(c == 0)
    def _():
        x = x_ref[...]
        ms = jnp.mean(x * x, axis=-1, keepdims=True)
        xn_ref[...] = ((x * lax.rsqrt(ms + EPS)) * g_ref[...]).astype(BF16)
        acc_ref[...] = x

    xn = xn_ref[...]
    gate = jnp.dot(xn, wg_ref[...], preferred_element_type=F32)
    up = jnp.dot(xn, wu_ref[...], preferred_element_type=F32)
    act = (gate * jax.nn.sigmoid(gate)) * up
    acc_ref[...] += jnp.dot(act.astype(BF16), wd_ref[...], preferred_element_type=F32)

    @pl.when(c == pl.num_programs(1) - 1)
    def _():
        o_ref[...] = acc_ref[...]


def _ffn(x, g, w_in, w_out, tm, tf):
    n = x.shape[0]
    dff = w_out.shape[0]
    nf = dff // tf
    row = pl.BlockSpec((tm, D), lambda i, c: (i, 0))
    return pl.pallas_call(
        _ffn_kernel,
        grid=(n // tm, nf),
        in_specs=[row, pl.BlockSpec((1, D), lambda i, c: (0, 0)),
                  pl.BlockSpec((D, tf), lambda i, c: (0, c)),
                  pl.BlockSpec((D, tf), lambda i, c: (0, nf + c)),
                  pl.BlockSpec((tf, D), lambda i, c: (c, 0))],
        out_specs=row,
        out_shape=jax.ShapeDtypeStruct((n, D), F32),
        scratch_shapes=[pltpu.VMEM((tm, D), BF16), pltpu.VMEM((tm, D), F32)],
        compiler_params=_cparams(("parallel", "arbitrary")),
        name="swiglu_ffn",
    )(x, g, w_in, w_in, w_out)


def _split_hi_lo(a):
    hi = a.astype(BF16)
    lo = (a - hi.astype(F32)).astype(BF16)
    return hi, lo


def _layer(x, past, p, cfg):
    k_past, v_past, ki_past, c0, n0, m0, conv0 = past
    b, t, _ = x.shape
    n_tok = b * t
    past_len = k_past.shape[1]
    s_real = past_len + t
    topk = min(TOPK_MAX, s_real // 4)
    tq_sel, kb_sel, tq_att, tk_att = cfg["tq_sel"], cfg["kb_sel"], cfg["tq_att"], cfg["tk_att"]
    s_pad = -(-s_real // tk_att) * tk_att

    x2 = x.reshape(n_tok, D)
    z = _norm_matmul(x2, p["g_norm1"], p["w_in"], cfg["tm_proj"], cfg["tn_proj"])
    z3 = z.reshape(b, t, D_Z)

    y_a, c_new, n_new, m_new, conv_new = _mlstm(
        z3, p["gate_bias"], p["w_conv"], p["b_conv"], p["g_mnorm"], c0, n0,
        m0.reshape(b, 1, M_HEADS), conv0, cfg["mlstm_chunk"])

    k32, v32, kb, qt, vt = _qkv(z3, p["g_q"], p["g_k"], cfg["tm_qkv"])
    ik = z3[:, :, COL_SMALL:COL_SMALL + IDX_DIM]
    ki_all = ik if past_len == 0 else jnp.concatenate([ki_past.astype(F32), ik], axis=1)
    k_hi, k_lo = _split_hi_lo(ki_all)
    ki3 = jnp.concatenate([k_hi, k_lo, k_hi], axis=-1)
    k_all, vt_all = kb, vt
    if past_len > 0 or s_pad > s_real:
        ki3 = jnp.pad(ki3, ((0, 0), (0, s_pad - s_real), (0, 0)))
        k_all = jnp.concatenate([k_past.reshape(b, past_len, D).astype(BF16), kb,
                                 jnp.zeros((b, s_pad - s_real, D), BF16)], axis=1)
        vt_all = jnp.concatenate([v_past.reshape(b, past_len, D).astype(BF16).transpose(0, 2, 1), vt,
                                  jnp.zeros((b, D, s_pad - s_real), BF16)], axis=2)
    mask_t = _select(z3, ki3, tq_sel, kb_sel, s_real, past_len, topk)
    y_b = _attention(qt, k_all, vt_all, mask_t, tq_att, tk_att, s_real, past_len)

    x1 = _merge(x2, y_a.reshape(n_tok, D), y_b.reshape(n_tok, D), z,
                p["w_a_out"], p["w_b_out"], p["w_o"], cfg["tm_rows"])
    y = _ffn(x1, p["g_norm2"], p["w_ffn_in"], p["w_ffn_out"], cfg["tm_ffn"], cfg["tf_ffn"])

    hd = D // A_HEADS
    return y.reshape(b, t, D), (k32.reshape(b, t, A_HEADS, hd), v32.reshape(b, t, A_HEADS, hd), ik,
                                c_new, n_new, m_new.reshape(b, M_HEADS), conv_new)


def _prep_params(g_norm1, w_in, b_if, w_conv, b_conv, g_mnorm, g_q, g_k, w_a_out, w_b_out, w_o,
                 g_norm2, w_ffn_in, w_ffn_out):
    o_mi = 4 * D
    o_aq = o_mi + 2 * M_HEADS
    o_iq = o_aq + 3 * D
    o_ik = o_iq + IDX_HEADS * IDX_DIM
    o_iw = o_ik + IDX_DIM
    o_ga = o_iw + IDX_HEADS
    w_perm = jnp.concatenate([
        w_in[:, 0:o_mi], w_in[:, o_aq:o_iq], w_in[:, o_ga:o_ga + 2 * D], w_in[:, o_iq:o_ik],
        w_in[:, o_ik:o_ga], w_in[:, o_mi:o_aq],
        jnp.zeros((D, D_Z - COL_SMALL - (SM_MF + M_HEADS)), w_in.dtype)], axis=1).astype(BF16)
    gate_bias = jnp.zeros((1, LANES), F32).at[0, SM_MI:SM_MI + 2 * M_HEADS].set(b_if.astype(F32))
    return {
        "g_norm1": g_norm1.reshape(1, D), "w_in": w_perm, "gate_bias": gate_bias,
        "w_conv": w_conv, "b_conv": b_conv.reshape(1, 2 * D), "g_mnorm": g_mnorm.reshape(1, D),
        "g_q": g_q.reshape(1, -1), "g_k": g_k.reshape(1, -1),
        "w_a_out": w_a_out.astype(BF16), "w_b_out": w_b_out.astype(BF16), "w_o": w_o.astype(BF16),
        "g_norm2": g_norm2.reshape(1, D), "w_ffn_in": w_ffn_in.astype(BF16),
        "w_ffn_out": w_ffn_out.astype(BF16),
    }


def _config(b, t, past_len):
    n_tok = b * t
    s_real = past_len + t
    tm = min(1024, n_tok)
    tq_sel = tq_att = min(256, t)
    if s_real % 512 == 0:
        kb_sel = tk_att = 512
    else:
        kb_sel = tk_att = -(-s_real // LANES) * LANES
    return {"tm_proj": tm, "tn_proj": 896, "mlstm_chunk": min(128, t), "tm_rows": min(512, n_tok),
            "tm_qkv": min(512, t), "tq_sel": tq_sel, "kb_sel": kb_sel, "tq_att": tq_att,
            "tk_att": tk_att, "tm_ffn": min(512, n_tok), "tf_ffn": 1408}


def kernel(x_prompt, x_sample, cache_k, cache_v, cache_kidx, state_C, state_n, state_m, state_conv,
           g_norm1, w_in, b_if, w_conv, b_conv, g_mnorm, g_q, g_k, w_a_out, w_b_out, w_o,
           g_norm2, w_ffn_in, w_ffn_out):
    depth = w_in.shape[0]
    bp = x_prompt.shape[0]
    hd_a = D // A_HEADS
    hd_m = D // M_HEADS
    yp, ys = x_prompt, x_sample
    new_p, new_s = [], []
    for l in range(depth):
        p = _prep_params(g_norm1[l], w_in[l], b_if[l], w_conv[l], b_conv[l], g_mnorm[l], g_q[l],
                         g_k[l], w_a_out[l], w_b_out[l], w_o[l], g_norm2[l], w_ffn_in[l],
                         w_ffn_out[l])
        empty = (jnp.zeros((bp, 0, A_HEADS, hd_a), F32), jnp.zeros((bp, 0, A_HEADS, hd_a), F32),
                 jnp.zeros((bp, 0, IDX_DIM), F32), jnp.zeros((bp, M_HEADS, hd_m, hd_m), F32),
                 jnp.zeros((bp, M_HEADS, hd_m), F32), jnp.zeros((bp, M_HEADS), F32),
                 jnp.zeros((bp, CONV_W - 1, 2 * D), F32))
        yp, sp = _layer(yp, empty, p, _config(bp, yp.shape[1], 0))
        ys, ss = _layer(ys, (cache_k[l], cache_v[l], cache_kidx[l], state_C[l], state_n[l],
                             state_m[l], state_conv[l]), p,
                        _config(ys.shape[0], ys.shape[1], cache_k.shape[2]))
        new_p.append(sp)
        new_s.append(ss)

    def stk(lst, i):
        return jnp.stack([s[i] for s in lst])

    return (yp, ys) + tuple(stk(new_p, i) for i in range(7)) + tuple(stk(new_s, i) for i in range(7))
```

```python
import functools
import math

import jax
import jax.numpy as jnp
from jax import lax
from jax.experimental import pallas as pl
from jax.experimental.pallas import tpu as pltpu

F32 = jnp.float32
BF16 = jnp.bfloat16
HIGHEST = lax.Precision.HIGHEST

EPS = 1e-6
CHUNK = 64
CHUNK_SHIFT = 6
M_HEADS = 4
A_HEADS = 8
IDX_HEADS = 8
IDX_DIM = 64
CONV_W = 4
TOPK_MAX = 256
LANES = 128
SUBLANES = 8
MASK_NEG = -1e30
F32_MAX = float(jnp.finfo(jnp.float32).max)
VMEM_LIMIT = 48 * 1024 * 1024
SELECT_MIN_ITERS = 12
SELECT_MAX_ITERS = 20
FINISH_MAX = 4
COUNT_PARTS = 4
COUNT_ROWS = 128

D = 1024
COL_MQ, COL_MK, COL_MV, COL_MO = 0, 1024, 2048, 3072
COL_AQ, COL_AK, COL_AV = 4096, 5120, 6144
COL_GA, COL_GB = 7168, 8192
COL_IQ = 9216
COL_SMALL = 9728
D_Z = 9856
SM_IW, SM_MI, SM_MF = 64, 72, 76
Q_LOGIT_SCALE = math.log2(math.e) * (D // A_HEADS) ** -0.5


def _cparams(sem):
    return pltpu.CompilerParams(dimension_semantics=sem, vmem_limit_bytes=VMEM_LIMIT)


def _norm_matmul_kernel(x_ref, g_ref, w_ref, o_ref, xn_ref):
    @pl.when(pl.program_id(1) == 0)
    def _():
        x = x_ref[...]
        ms = jnp.mean(x * x, axis=-1, keepdims=True)
        xn_ref[...] = ((x * lax.rsqrt(ms + EPS)) * g_ref[...]).astype(BF16)

    o_ref[...] = jnp.dot(xn_ref[...], w_ref[...], preferred_element_type=F32)


def _norm_matmul(x, g, w, tm, tn):
    n, d = x.shape
    nout = w.shape[1]
    return pl.pallas_call(
        _norm_matmul_kernel,
        grid=(n // tm, nout // tn),
        in_specs=[pl.BlockSpec((tm, d), lambda i, j: (i, 0)),
                  pl.BlockSpec((1, d), lambda i, j: (0, 0)),
                  pl.BlockSpec((d, tn), lambda i, j: (0, j))],
        out_specs=pl.BlockSpec((tm, tn), lambda i, j: (i, j)),
        out_shape=jax.ShapeDtypeStruct((n, nout), F32),
        scratch_shapes=[pltpu.VMEM((tm, d), BF16)],
        compiler_params=_cparams(("parallel", "arbitrary")),
        name="norm_in_proj",
    )(x, g, w)


def _mlstm_kernel(mq_ref, mk_ref, mv_ref, mo_ref, sm_ref, bias_ref, wconv_ref, bconv_ref,
                  gm_ref, c0_ref, n0_ref, m0_ref, conv0_ref,
                  y_ref, c_ref, n_ref, m_ref, conv_ref, cbuf_ref, *, L):
    c = pl.program_id(1)
    hd = D // M_HEADS

    @pl.when(c == 0)
    def _():
        c_ref[...] = c0_ref[...]
        n_ref[...] = n0_ref[...]
        m_ref[...] = m0_ref[...]
        cbuf_ref[8 - (CONV_W - 1):8, :] = conv0_ref[0]

    cbuf_ref[8:8 + L, 0:D] = mq_ref[0]
    cbuf_ref[8:8 + L, D:2 * D] = mk_ref[0]
    wc = wconv_ref[...]
    qk = bconv_ref[...] + cbuf_ref[5:5 + L, :] * wc[0:1, :]
    for j in range(1, CONV_W):
        qk = qk + cbuf_ref[5 + j:5 + j + L, :] * wc[j:j + 1, :]
    tail = cbuf_ref[5 + L:8 + L, :]
    cbuf_ref[5:8, :] = tail
    conv_ref[0] = tail
    qk = qk * jax.nn.sigmoid(qk)

    g_all = sm_ref[0] + bias_ref[...]
    lf_all = jnp.minimum(g_all, 0.0) - jnp.log1p(jnp.exp(-jnp.abs(g_all)))
    lane = lax.broadcasted_iota(jnp.int32, (L, LANES), 1)
    gates = jnp.where(lane >= SM_MF, lf_all, g_all)
    r_i = lax.broadcasted_iota(jnp.int32, (L, L), 0)
    c_i = lax.broadcasted_iota(jnp.int32, (L, L), 1)
    tril = (c_i <= r_i).astype(F32)
    triu = (r_i <= c_i).astype(F32)
    b_col_all = jnp.dot(tril, lf_all, precision=HIGHEST, preferred_element_type=F32)
    sel = (lax.broadcasted_iota(jnp.int32, (8, LANES), 1)
           == lax.broadcasted_iota(jnp.int32, (8, LANES), 0) + SM_MI).astype(F32)
    rows = lax.dot_general(sel, gates, (((1,), (1,)), ((), ())), precision=HIGHEST,
                           preferred_element_type=F32)
    b_row_all = jnp.dot(rows, triu, precision=HIGHEST, preferred_element_type=F32)
    causal = c_i <= r_i

    mo = mo_ref[0]
    mv = mv_ref[0]
    for h in range(M_HEADS):
        hs = slice(h * hd, (h + 1) * hd)
        qh = qk[:, hs]
        kh = qk[:, D + h * hd:D + (h + 1) * hd] * (hd ** -0.5)
        vh = mv[:, hs]
        qb, kb, vb = qh.astype(BF16), kh.astype(BF16), vh.astype(BF16)
        b_col = b_col_all[:, SM_MF + h:SM_MF + h + 1]
        i_col = gates[:, SM_MI + h:SM_MI + h + 1]
        b_row = b_row_all[M_HEADS + h:M_HEADS + h + 1, :]
        i_row = rows[h:h + 1, :]
        m_prev = m_ref[0, :, h:h + 1]
        c_prev = c_ref[0, h]
        n_prev = n_ref[0, h:h + 1, :]

        dmat = jnp.where(causal, b_col - b_row + i_row, -jnp.inf)
        inter = b_col + m_prev
        m_t = jnp.maximum(inter, jnp.max(dmat, axis=-1, keepdims=True))
        w_intra = jnp.exp(dmat - m_t)
        w_inter = jnp.exp(inter - m_t)
        s = lax.dot_general(qb, kb, (((1,), (1,)), ((), ())), preferred_element_type=F32) * w_intra
        qc = lax.dot_general(qb, c_prev.astype(BF16), (((1,), (1,)), ((), ())),
                             preferred_element_type=F32)
        num = jnp.dot(s.astype(BF16), vb, preferred_element_type=F32) + w_inter * qc
        den = jnp.sum(s, axis=-1, keepdims=True) + w_inter * jnp.sum(qh * n_prev, axis=-1, keepdims=True)
        denom = jnp.maximum(jnp.abs(den), jnp.exp(-m_t))
        hh = num / denom

        m_new = m_t[L - 1:L, :]
        b_last = b_col[L - 1:L, :]
        g_col = jnp.exp(b_last - b_col + i_col - m_new)
        decay = jnp.exp(b_last + m_prev - m_new)
        gv = (g_col * vh).astype(BF16)
        c_ref[0, h] = decay * c_prev + lax.dot_general(
            gv, kb, (((0,), (0,)), ((), ())), preferred_element_type=F32)
        n_ref[0, h:h + 1, :] = decay * n_prev + jnp.sum(g_col * kh, axis=0, keepdims=True)
        m_ref[0, :, h:h + 1] = m_new

        hn = hh * lax.rsqrt(jnp.mean(hh * hh, axis=-1, keepdims=True) + EPS) * gm_ref[:, hs]
        y_ref[0, :, hs] = (hn * jax.nn.sigmoid(mo[:, hs])).astype(BF16)


def _mlstm(z3, bias_row, w_conv, b_conv, g_mnorm, c0, n0, m0, conv0, L):
    b, t, _ = z3.shape
    nc = t // L
    hd = D // M_HEADS

    def zspec(col, width):
        return pl.BlockSpec((1, L, width), lambda i, c: (i, c, col // width))

    def per_batch(shape):
        nd = len(shape)
        return pl.BlockSpec((1,) + shape, lambda i, c: (i,) + (0,) * nd)

    def const(shape):
        nd = len(shape)
        return pl.BlockSpec(shape, lambda i, c: (0,) * nd)

    return pl.pallas_call(
        functools.partial(_mlstm_kernel, L=L),
        grid=(b, nc),
        in_specs=[zspec(COL_MQ, D), zspec(COL_MK, D), zspec(COL_MV, D), zspec(COL_MO, D),
                  zspec(COL_SMALL, LANES), const((1, LANES)), const((CONV_W, 2 * D)),
                  const((1, 2 * D)), const((1, D)),
                  per_batch((M_HEADS, hd, hd)), per_batch((M_HEADS, hd)),
                  per_batch((1, M_HEADS)), per_batch((CONV_W - 1, 2 * D))],
        out_specs=[pl.BlockSpec((1, L, D), lambda i, c: (i, c, 0)),
                   per_batch((M_HEADS, hd, hd)), per_batch((M_HEADS, hd)),
                   per_batch((1, M_HEADS)), per_batch((CONV_W - 1, 2 * D))],
        out_shape=[jax.ShapeDtypeStruct((b, t, D), BF16),
                   jax.ShapeDtypeStruct((b, M_HEADS, hd, hd), F32),
                   jax.ShapeDtypeStruct((b, M_HEADS, hd), F32),
                   jax.ShapeDtypeStruct((b, 1, M_HEADS), F32),
                   jax.ShapeDtypeStruct((b, CONV_W - 1, 2 * D), F32)],
        scratch_shapes=[pltpu.VMEM((8 + L, 2 * D), F32)],
        compiler_params=_cparams(("parallel", "arbitrary")),
        name="mlstm",
    )(z3, z3, z3, z3, z3, bias_row, w_conv, b_conv, g_mnorm, c0, n0, m0, conv0)


def _eye_bf16(n):
    return (lax.broadcasted_iota(jnp.int32, (n, n), 0)
            == lax.broadcasted_iota(jnp.int32, (n, n), 1)).astype(BF16)


def _transpose_bf16(eye, a):
    return lax.dot_general(eye, a, (((1,), (1,)), ((), ())), preferred_element_type=F32)


def _qkv_kernel(aq_ref, ak_ref, av_ref, gq_ref, gk_ref, k32_ref, v32_ref, kb_ref, qt_ref, vt_ref):
    hd = D // A_HEADS
    eye = _eye_bf16(hd)
    for h in range(A_HEADS):
        hs = slice(h * hd, (h + 1) * hd)
        q = aq_ref[0, :, hs]
        k = ak_ref[0, :, hs]
        v = av_ref[0, :, hs]
        qn = (q * lax.rsqrt(jnp.mean(q * q, axis=-1, keepdims=True) + EPS)) * gq_ref[...]
        kn = (k * lax.rsqrt(jnp.mean(k * k, axis=-1, keepdims=True) + EPS)) * gk_ref[...]
        k32_ref[0, :, hs] = kn
        kb_ref[0, :, hs] = kn.astype(BF16)
        v32_ref[0, :, hs] = v
        qt_ref[0, hs, :] = _transpose_bf16(eye, (qn * Q_LOGIT_SCALE).astype(BF16)).astype(BF16)
        vt_ref[0, hs, :] = _transpose_bf16(eye, v.astype(BF16)).astype(BF16)


def _qkv(z3, g_q, g_k, tm):
    b, t, _ = z3.shape
    hd = D // A_HEADS

    def zspec(col):
        return pl.BlockSpec((1, tm, D), lambda i, j: (i, j, col // D))

    row = pl.BlockSpec((1, tm, D), lambda i, j: (i, j, 0))
    colb = pl.BlockSpec((1, D, tm), lambda i, j: (i, 0, j))
    gspec = pl.BlockSpec((1, hd), lambda i, j: (0, 0))
    return pl.pallas_call(
        _qkv_kernel,
        grid=(b, t // tm),
        in_specs=[zspec(COL_AQ), zspec(COL_AK), zspec(COL_AV), gspec, gspec],
        out_specs=[row, row, row, colb, colb],
        out_shape=[jax.ShapeDtypeStruct((b, t, D), F32), jax.ShapeDtypeStruct((b, t, D), F32),
                   jax.ShapeDtypeStruct((b, t, D), BF16), jax.ShapeDtypeStruct((b, D, t), BF16),
                   jax.ShapeDtypeStruct((b, D, t), BF16)],
        compiler_params=_cparams(("parallel", "parallel")),
        name="qkv_norm",
    )(z3, z3, z3, g_q, g_k)


def _num_key_tiles(qi, tq, tile, q_pos0, s_real):
    last_chunk = (q_pos0 + (qi + 1) * tq - 1) // CHUNK
    kend = jnp.minimum((last_chunk + 1) * CHUNK, s_real)
    return (kend + tile - 1) // tile


def _num_key_tiles_static(qi, tq, tile, q_pos0, s_real):
    last_chunk = (q_pos0 + (qi + 1) * tq - 1) // CHUNK
    return -(-min((last_chunk + 1) * CHUNK, s_real) // tile)


def _select_kernel(iq_ref, sm_ref, ki_ref, mask_ref, score_ref, iqt_ref, *,
                   tq, kb, s_pad, s_real, q_pos0, topk):
    qi = pl.program_id(1)
    nkt = _num_key_tiles(qi, tq, kb, q_pos0, s_real)
    kf = float(topk)
    groups = kb // SUBLANES

    pick = (lax.broadcasted_iota(jnp.int32, (IDX_HEADS, LANES), 1)
            == lax.broadcasted_iota(jnp.int32, (IDX_HEADS, LANES), 0) + SM_IW).astype(F32)
    w = lax.dot_general(pick, sm_ref[0], (((1,), (1,)), ((), ())), precision=HIGHEST,
                        preferred_element_type=F32) * ((IDX_HEADS * IDX_DIM) ** -0.5)
    eye = _eye_bf16(IDX_DIM)
    for h in range(IDX_HEADS):
        a = iq_ref[0, :, h * IDX_DIM:(h + 1) * IDX_DIM]
        hi = a.astype(BF16)
        lo = (a - hi.astype(F32)).astype(BF16)
        hi_t = _transpose_bf16(eye, hi).astype(BF16)
        lo_t = _transpose_bf16(eye, lo).astype(BF16)
        cols = slice((h % 2) * tq, (h % 2 + 1) * tq)
        iqt_ref[h // 2, 0:IDX_DIM, cols] = hi_t
        iqt_ref[h // 2, IDX_DIM:2 * IDX_DIM, cols] = hi_t
        iqt_ref[h // 2, 2 * IDX_DIM:3 * IDX_DIM, cols] = lo_t

    q_pos = q_pos0 + qi * tq + lax.broadcasted_iota(jnp.int32, (1, tq), 1)
    q_chunk = q_pos >> CHUNK_SHIFT
    first_key = jnp.minimum(((q_pos0 + qi * tq) >> CHUNK_SHIFT) << CHUNK_SHIFT, s_real)
    n_full = first_key // kb

    def score_tile(j, carry, masked):
        rmin, rmax = carry
        k0 = pl.multiple_of(j * kb, kb)
        kt = ki_ref[0, pl.ds(k0, kb), :]
        acc = None
        for p in range(IDX_HEADS // 2):
            rel = jnp.dot(kt, iqt_ref[p], preferred_element_type=F32)
            part = (w[2 * p:2 * p + 1] * jnp.maximum(rel[:, 0:tq], 0.0)
                    + w[2 * p + 1:2 * p + 2] * jnp.maximum(rel[:, tq:2 * tq], 0.0))
            acc = part if acc is None else acc + part
        if masked:
            key = k0 + lax.broadcasted_iota(jnp.int32, (kb, tq), 0)
            adm = jnp.logical_and((key >> CHUNK_SHIFT) <= q_chunk, key < s_real)
            low = jnp.where(adm, acc, jnp.inf)
            acc = jnp.where(adm, acc, -jnp.inf)
        else:
            low = acc
        score_ref[pl.ds(k0, kb), :] = acc
        rmin = jnp.minimum(rmin, jnp.min(low.reshape(groups, SUBLANES, tq), axis=0))
        rmax = jnp.maximum(rmax, jnp.max(acc.reshape(groups, SUBLANES, tq), axis=0))
        return rmin, rmax

    stats = (jnp.full((SUBLANES, tq), jnp.inf, F32), jnp.full((SUBLANES, tq), -jnp.inf, F32))
    stats = lax.fori_loop(0, n_full, functools.partial(score_tile, masked=False), stats)
    rmin8, rmax8 = lax.fori_loop(n_full, nkt, functools.partial(score_tile, masked=True), stats)
    rmin = jnp.min(rmin8, axis=0, keepdims=True)
    rmax = jnp.max(rmax8, axis=0, keepdims=True)

    def count(pred):
        def body(j, acc):
            for c in range(kb // COUNT_ROWS):
                k0 = pl.multiple_of(j * kb + c * COUNT_ROWS, COUNT_ROWS)
                hit = jnp.where(pred(score_ref[pl.ds(k0, COUNT_ROWS), :], k0), 1.0, 0.0)
                acc = acc + jnp.sum(hit.reshape(COUNT_ROWS // (COUNT_PARTS * SUBLANES), COUNT_PARTS,
                                                SUBLANES, tq), axis=0)
            return acc
        acc = lax.fori_loop(0, nkt, body, jnp.zeros((COUNT_PARTS, SUBLANES, tq), F32))
        return jnp.sum(jnp.sum(acc, axis=0), axis=0, keepdims=True)

    def below(cur):
        shape = (COUNT_ROWS // (COUNT_PARTS * SUBLANES), COUNT_PARTS, SUBLANES, tq)

        def body(j, carry):
            top, num = carry
            for c in range(kb // COUNT_ROWS):
                k0 = pl.multiple_of(j * kb + c * COUNT_ROWS, COUNT_ROWS)
                s = score_ref[pl.ds(k0, COUNT_ROWS), :]
                under = s < cur
                top = jnp.maximum(top, jnp.max(jnp.where(under, s, -jnp.inf).reshape(shape), axis=0))
                num = num + jnp.sum(jnp.where(under, 0.0, 1.0).reshape(shape), axis=0)
            return top, num

        top, num = lax.fori_loop(0, nkt, body, (jnp.full(shape[1:], -jnp.inf, F32),
                                                jnp.zeros(shape[1:], F32)))
        return (jnp.max(jnp.max(top, axis=0), axis=0, keepdims=True),
                jnp.sum(jnp.sum(num, axis=0), axis=0, keepdims=True))

    n_adm = jnp.minimum((q_chunk + 1) << CHUNK_SHIFT, s_real).astype(F32)

    def open_rows(cnt_lo):
        return cnt_lo > kf

    def wide_rows(st):
        return jnp.logical_and(open_rows(st[2]), st[2] - st[3] > FINISH_MAX)

    def flag(pred):
        return jnp.max(jnp.where(pred, 1.0, 0.0))

    def bisect(st):
        lo, hi, cnt_lo, cnt_hi = st
        active = open_rows(cnt_lo)
        mid = 0.5 * jnp.maximum(lo, rmin) + 0.5 * jnp.minimum(hi, rmax)
        cm = count(lambda s, k0: s >= mid)
        up = jnp.logical_and(active, cm >= kf)
        dn = jnp.logical_and(active, cm < kf)
        return (jnp.where(up, mid, lo), jnp.where(dn, mid, hi),
                jnp.where(up, cm, cnt_lo), jnp.where(dn, cm, cnt_hi))

    st = (jnp.full((1, tq), -F32_MAX, F32), jnp.full((1, tq), F32_MAX, F32), n_adm,
          jnp.zeros((1, tq), F32))
    st = lax.fori_loop(0, SELECT_MIN_ITERS, lambda _, s: bisect(s), st)

    def narrow(c):
        nxt = bisect(c[2])
        return c[0] + 1, flag(wide_rows(nxt)), nxt

    _, _, st = lax.while_loop(
        lambda c: jnp.logical_and(c[0] < SELECT_MAX_ITERS, c[1] > 0.0), narrow,
        (jnp.int32(SELECT_MIN_ITERS), flag(wide_rows(st)), st))

    lo, hi, cnt_lo, cnt_hi = st
    cand, _ = below(hi)

    def walk(c):
        _, walking, lo, hi, cnt_lo, cnt_hi, cand = c
        nxt, cge = below(cand)
        settle = jnp.logical_and(walking > 0.0, cge >= kf)
        move = jnp.logical_and(walking > 0.0, cge < kf)
        walking = jnp.where(move, 1.0, 0.0)
        return (jnp.max(walking), walking, jnp.where(settle, cand, lo), jnp.where(move, cand, hi),
                jnp.where(settle, cge, cnt_lo), jnp.where(move, cge, cnt_hi), jnp.where(move, nxt, cand))

    walking = jnp.where(open_rows(cnt_lo), 1.0, 0.0)
    _, _, lo, hi, cnt_lo, cnt_hi, _ = lax.while_loop(
        lambda c: c[0] > 0.0, walk, (jnp.max(walking), walking, lo, hi, cnt_lo, cnt_hi, cand))
    n_open = flag(open_rows(cnt_lo))

    def clear_tile(j, carry):
        k0 = pl.multiple_of(j * kb, kb)
        mask_ref[0, pl.ds(k0, kb), :] = jnp.zeros((kb, tq), jnp.int8)
        return carry

    lax.fori_loop(nkt, s_pad // kb, clear_tile, 0)

    @pl.when(n_open == 0.0)
    def _():
        def write_tile(j, carry):
            k0 = pl.multiple_of(j * kb, kb)
            keep = score_ref[pl.ds(k0, kb), :] >= lo
            mask_ref[0, pl.ds(k0, kb), :] = jnp.where(keep, 1, 0).astype(jnp.int8)
            return carry

        lax.fori_loop(0, nkt, write_tile, 0)

    @pl.when(n_open > 0.0)
    def _():
        need = kf - cnt_hi
        key_iota = lax.broadcasted_iota(jnp.int32, (kb, tq), 0)
        strip_iota = lax.broadcasted_iota(jnp.int32, (COUNT_ROWS, tq), 0)

        def step(_, st):
            jlo, jhi = st
            jmid = (jlo + jhi) >> 1
            cm = count(lambda s, k0: jnp.logical_and(
                jnp.logical_and(s >= lo, s < hi), k0 + strip_iota < jmid))
            ok = cm >= need
            return jnp.where(ok, jlo, jmid), jnp.where(ok, jmid, jhi)

        steps = int(math.ceil(math.log2(s_pad))) + 1
        _, jhi = lax.fori_loop(0, steps, step, (jnp.zeros((1, tq), jnp.int32),
                                                jnp.full((1, tq), s_pad, jnp.int32)))
        jcut = jnp.where(open_rows(cnt_lo), jhi, s_pad)

        def write_tile(j, carry):
            k0 = pl.multiple_of(j * kb, kb)
            s = score_ref[pl.ds(k0, kb), :]
            keep = jnp.logical_or(s >= hi, jnp.logical_and(s >= lo, k0 + key_iota < jcut))
            mask_ref[0, pl.ds(k0, kb), :] = jnp.where(keep, 1, 0).astype(jnp.int8)
            return carry

        lax.fori_loop(0, nkt, write_tile, 0)


def _select(z3, ki3, tq, kb, s_real, q_pos0, topk):
    b, t, _ = z3.shape
    s_pad = ki3.shape[1]
    iq_w = IDX_HEADS * IDX_DIM
    kern = functools.partial(_select_kernel, tq=tq, kb=kb, s_pad=s_pad, s_real=s_real,
                             q_pos0=q_pos0, topk=topk)
    return pl.pallas_call(
        kern,
        grid=(b, t // tq),
        in_specs=[pl.BlockSpec((1, tq, iq_w), lambda i, j: (i, j, COL_IQ // iq_w)),
                  pl.BlockSpec((1, tq, LANES), lambda i, j: (i, j, COL_SMALL // LANES)),
                  pl.BlockSpec((1, s_pad, 3 * IDX_DIM), lambda i, j: (i, 0, 0))],
        out_specs=pl.BlockSpec((1, s_pad, tq), lambda i, j: (i, 0, j)),
        out_shape=jax.ShapeDtypeStruct((b, s_pad, t), jnp.int8),
        scratch_shapes=[pltpu.VMEM((s_pad, tq), F32),
                        pltpu.VMEM((IDX_HEADS // 2, 3 * IDX_DIM, 2 * tq), BF16)],
        compiler_params=_cparams(("parallel", "arbitrary")),
        name="index_select",
    )(z3, z3, ki3)


def _attn_kernel(q_of_ref, k_of_ref, qt_ref, k_ref, vt_ref, mask_ref, o_ref, acc_ref, m_ref, l_ref, s_ref):
    step = pl.program_id(1)
    hd = D // A_HEADS
    tq = qt_ref.shape[2]
    first = k_of_ref[step] == 0
    nxt = jnp.minimum(step + 1, pl.num_programs(1) - 1)
    last = jnp.logical_or(step == pl.num_programs(1) - 1, q_of_ref[nxt] != q_of_ref[step])

    @pl.when(first)
    def _():
        acc_ref[...] = jnp.zeros(acc_ref.shape, F32)
        m_ref[...] = jnp.full(m_ref.shape, MASK_NEG, F32)
        l_ref[...] = jnp.zeros(l_ref.shape, F32)

    bias = jnp.where(mask_ref[0].astype(jnp.int32) != 0, 0.0, MASK_NEG)
    tile_max = []
    for h in range(A_HEADS):
        hs = slice(h * hd, (h + 1) * hd)
        s = jnp.dot(k_ref[0, :, hs], qt_ref[0, hs, :], preferred_element_type=F32) + bias
        s_ref[h] = s
        tile_max.append(jnp.max(s, axis=0, keepdims=True))
    for h in range(A_HEADS):
        hs = slice(h * hd, (h + 1) * hd)
        m_prev = m_ref[h:h + 1, :]
        m_new = jnp.maximum(m_prev, tile_max[h])
        alpha = jnp.exp2(m_prev - m_new)
        p = jnp.exp2(s_ref[h] - m_new)
        l_ref[h:h + 1, :] = alpha * l_ref[h:h + 1, :] + jnp.sum(p, axis=0, keepdims=True)
        acc_ref[hs, :] = alpha * acc_ref[hs, :] + jnp.dot(
            vt_ref[0, hs, :], p.astype(BF16), preferred_element_type=F32)
        m_ref[h:h + 1, :] = m_new

    @pl.when(last)
    def _():
        eye = _eye_bf16(tq)
        for h in range(A_HEADS):
            hs = slice(h * hd, (h + 1) * hd)
            out_t = (acc_ref[hs, :] / l_ref[h:h + 1, :]).astype(BF16)
            o_ref[0, :, hs] = _transpose_bf16(eye, out_t).astype(BF16)


def _attention(qt, kb, vt, mask_t, tq, tk, s_real, q_pos0):
    b, _, t = qt.shape
    steps = [(q, j) for q in range(t // tq) for j in range(_num_key_tiles_static(q, tq, tk, q_pos0, s_real))]
    q_of = jnp.array([q for q, _ in steps], jnp.int32)
    k_of = jnp.array([j for _, j in steps], jnp.int32)
    return pl.pallas_call(
        _attn_kernel,
        grid_spec=pltpu.PrefetchScalarGridSpec(
            num_scalar_prefetch=2,
            grid=(b, len(steps)),
            in_specs=[pl.BlockSpec((1, D, tq), lambda i, s, q_of, k_of: (i, 0, q_of[s])),
                      pl.BlockSpec((1, tk, D), lambda i, s, q_of, k_of: (i, k_of[s], 0)),
                      pl.BlockSpec((1, D, tk), lambda i, s, q_of, k_of: (i, 0, k_of[s])),
                      pl.BlockSpec((1, tk, tq), lambda i, s, q_of, k_of: (i, k_of[s], q_of[s]))],
            out_specs=pl.BlockSpec((1, tq, D), lambda i, s, q_of, k_of: (i, q_of[s], 0)),
            scratch_shapes=[pltpu.VMEM((D, tq), F32),
                            pltpu.VMEM((A_HEADS, tq), F32),
                            pltpu.VMEM((A_HEADS, tq), F32),
                            pltpu.VMEM((A_HEADS, tk, tq), F32)]),
        out_shape=jax.ShapeDtypeStruct((b, t, D), BF16),
        compiler_params=_cparams(("parallel", "arbitrary")),
        name="masked_attention",
    )(q_of, k_of, qt, kb, vt, mask_t)


def _merge_kernel(x_ref, ya_ref, yb_ref, ga_ref, gb_ref, wa_ref, wb_ref, wo_ref, o_ref):
    a = jnp.dot(ya_ref[...], wa_ref[...], preferred_element_type=F32)
    bb = jnp.dot(yb_ref[...], wb_ref[...], preferred_element_type=F32)
    mix = jax.nn.sigmoid(ga_ref[...]) * a + jax.nn.sigmoid(gb_ref[...]) * bb
    o_ref[...] = x_ref[...] + jnp.dot(mix.astype(BF16), wo_ref[...], preferred_element_type=F32)


def _merge(x, ya, yb, z, wa, wb, wo, tm):
    n = x.shape[0]
    row = pl.BlockSpec((tm, D), lambda i: (i, 0))
    wspec = pl.BlockSpec((D, D), lambda i: (0, 0))
    return pl.pallas_call(
        _merge_kernel,
        grid=(n // tm,),
        in_specs=[row, row, row,
                  pl.BlockSpec((tm, D), lambda i: (i, COL_GA // D)),
                  pl.BlockSpec((tm, D), lambda i: (i, COL_GB // D)),
                  wspec, wspec, wspec],
        out_specs=row,
        out_shape=jax.ShapeDtypeStruct((n, D), F32),
        compiler_params=_cparams(("parallel",)),
        name="merge_out_proj",
    )(x, ya, yb, z, z, wa, wb, wo)


def _ffn_kernel(x_ref, g_ref, wg_ref, wu_ref, wd_ref, o_ref, xn_ref, acc_ref):
    c = pl.program_id(1)

    @pl.when(c == 0)
    def _():
        x = x_ref[...]
        ms = jnp.mean(x * x, axis=-1, keepdims=True)
        xn_ref[...] = ((x * lax.rsqrt(ms + EPS)) * g_ref[...]).astype(BF16)
        acc_ref[...] = x

    xn = xn_ref[...]
    gate = jnp.dot(xn, wg_ref[...], preferred_element_type=F32)
    up = jnp.dot(xn, wu_ref[...], preferred_element_type=F32)
    act = (gate * jax.nn.sigmoid(gate)) * up
    acc_ref[...] += jnp.dot(act.astype(BF16), wd_ref[...], preferred_element_type=F32)

    @pl.when(c == pl.num_programs(1) - 1)
    def _():
        o_ref[...] = acc_ref[...]


def _ffn(x, g, w_in, w_out, tm, tf):
    n = x.shape[0]
    dff = w_out.shape[0]
    nf = dff // tf
    row = pl.BlockSpec((tm, D), lambda i, c: (i, 0))
    return pl.pallas_call(
        _ffn_kernel,
        grid=(n // tm, nf),
        in_specs=[row, pl.BlockSpec((1, D), lambda i, c: (0, 0)),
                  pl.BlockSpec((D, tf), lambda i, c: (0, c)),
                  pl.BlockSpec((D, tf), lambda i, c: (0, nf + c)),
                  pl.BlockSpec((tf, D), lambda i, c: (c, 0))],
        out_specs=row,
        out_shape=jax.ShapeDtypeStruct((n, D), F32),
        scratch_shapes=[pltpu.VMEM((tm, D), BF16), pltpu.VMEM((tm, D), F32)],
        compiler_params=_cparams(("parallel", "arbitrary")),
        name="swiglu_ffn",
    )(x, g, w_in, w_in, w_out)


def _split_hi_lo(a):
    hi = a.astype(BF16)
    lo = (a - hi.astype(F32)).astype(BF16)
    return hi, lo


def _layer(x, past, p, cfg):
    k_past, v_past, ki_past, c0, n0, m0, conv0 = past
    b, t, _ = x.shape
    n_tok = b * t
    past_len = k_past.shape[1]
    s_real = past_len + t
    topk = min(TOPK_MAX, s_real // 4)
    tq_sel, kb_sel, tq_att, tk_att = cfg["tq_sel"], cfg["kb_sel"], cfg["tq_att"], cfg["tk_att"]
    s_pad = -(-s_real // tk_att) * tk_att

    x2 = x.reshape(n_tok, D)
    z = _norm_matmul(x2, p["g_norm1"], p["w_in"], cfg["tm_proj"], cfg["tn_proj"])
    z3 = z.reshape(b, t, D_Z)

    y_a, c_new, n_new, m_new, conv_new = _mlstm(
        z3, p["gate_bias"], p["w_conv"], p["b_conv"], p["g_mnorm"], c0, n0,
        m0.reshape(b, 1, M_HEADS), conv0, cfg["mlstm_chunk"])

    k32, v32, kb, qt, vt = _qkv(z3, p["g_q"], p["g_k"], cfg["tm_qkv"])
    ik = z3[:, :, COL_SMALL:COL_SMALL + IDX_DIM]
    ki_all = ik if past_len == 0 else jnp.concatenate([ki_past.astype(F32), ik], axis=1)
    k_hi, k_lo = _split_hi_lo(ki_all)
    ki3 = jnp.concatenate([k_hi, k_lo, k_hi], axis=-1)
    k_all, vt_all = kb, vt
    if past_len > 0 or s_pad > s_real:
        ki3 = jnp.pad(ki3, ((0, 0), (0, s_pad - s_real), (0, 0)))
        k_all = jnp.concatenate([k_past.reshape(b, past_len, D).astype(BF16), kb,
                                 jnp.zeros((b, s_pad - s_real, D), BF16)], axis=1)
        vt_all = jnp.concatenate([v_past.reshape(b, past_len, D).astype(BF16).transpose(0, 2, 1), vt,
                                  jnp.zeros((b, D, s_pad - s_real), BF16)], axis=2)
    mask_t = _select(z3, ki3, tq_sel, kb_sel, s_real, past_len, topk)
    y_b = _attention(qt, k_all, vt_all, mask_t, tq_att, tk_att, s_real, past_len)

    x1 = _merge(x2, y_a.reshape(n_tok, D), y_b.reshape(n_tok, D), z,
                p["w_a_out"], p["w_b_out"], p["w_o"], cfg["tm_rows"])
    y = _ffn(x1, p["g_norm2"], p["w_ffn_in"], p["w_ffn_out"], cfg["tm_ffn"], cfg["tf_ffn"])

    hd = D // A_HEADS
    return y.reshape(b, t, D), (k32.reshape(b, t, A_HEADS, hd), v32.reshape(b, t, A_HEADS, hd), ik,
                                c_new, n_new, m_new.reshape(b, M_HEADS), conv_new)


def _prep_params(g_norm1, w_in, b_if, w_conv, b_conv, g_mnorm, g_q, g_k, w_a_out, w_b_out, w_o,
                 g_norm2, w_ffn_in, w_ffn_out):
    o_mi = 4 * D
    o_aq = o_mi + 2 * M_HEADS
    o_iq = o_aq + 3 * D
    o_ik = o_iq + IDX_HEADS * IDX_DIM
    o_iw = o_ik + IDX_DIM
    o_ga = o_iw + IDX_HEADS
    w_perm = jnp.concatenate([
        w_in[:, 0:o_mi], w_in[:, o_aq:o_iq], w_in[:, o_ga:o_ga + 2 * D], w_in[:, o_iq:o_ik],
        w_in[:, o_ik:o_ga], w_in[:, o_mi:o_aq],
        jnp.zeros((D, D_Z - COL_SMALL - (SM_MF + M_HEADS)), w_in.dtype)], axis=1).astype(BF16)
    gate_bias = jnp.zeros((1, LANES), F32).at[0, SM_MI:SM_MI + 2 * M_HEADS].set(b_if.astype(F32))
    return {
        "g_norm1": g_norm1.reshape(1, D), "w_in": w_perm, "gate_bias": gate_bias,
        "w_conv": w_conv, "b_conv": b_conv.reshape(1, 2 * D), "g_mnorm": g_mnorm.reshape(1, D),
        "g_q": g_q.reshape(1, -1), "g_k": g_k.reshape(1, -1),
        "w_a_out": w_a_out.astype(BF16), "w_b_out": w_b_out.astype(BF16), "w_o": w_o.astype(BF16),
        "g_norm2": g_norm2.reshape(1, D), "w_ffn_in": w_ffn_in.astype(BF16),
        "w_ffn_out": w_ffn_out.astype(BF16),
    }


def _config(b, t, past_len):
    n_tok = b * t
    s_real = past_len + t
    tm = min(1024, n_tok)
    tq_sel = tq_att = min(256, t)
    if s_real % 512 == 0:
        kb_sel = tk_att = 512
    else:
        kb_sel = tk_att = -(-s_real // LANES) * LANES
    return {"tm_proj": tm, "tn_proj": 896, "mlstm_chunk": min(128, t), "tm_rows": min(512, n_tok),
            "tm_qkv": min(512, t), "tq_sel": tq_sel, "kb_sel": kb_sel, "tq_att": tq_att,
            "tk_att": tk_att, "tm_ffn": min(512, n_tok), "tf_ffn": 1408}


def kernel(x_prompt, x_sample, cache_k, cache_v, cache_kidx, state_C, state_n, state_m, state_conv,
           g_norm1, w_in, b_if, w_conv, b_conv, g_mnorm, g_q, g_k, w_a_out, w_b_out, w_o,
           g_norm2, w_ffn_in, w_ffn_out):
    depth = w_in.shape[0]
    bp = x_prompt.shape[0]
    hd_a = D // A_HEADS
    hd_m = D // M_HEADS
    yp, ys = x_prompt, x_sample
    new_p, new_s = [], []
    for l in range(depth):
        p = _prep_params(g_norm1[l], w_in[l], b_if[l], w_conv[l], b_conv[l], g_mnorm[l], g_q[l],
                         g_k[l], w_a_out[l], w_b_out[l], w_o[l], g_norm2[l], w_ffn_in[l],
                         w_ffn_out[l])
        empty = (jnp.zeros((bp, 0, A_HEADS, hd_a), F32), jnp.zeros((bp, 0, A_HEADS, hd_a), F32),
                 jnp.zeros((bp, 0, IDX_DIM), F32), jnp.zeros((bp, M_HEADS, hd_m, hd_m), F32),
                 jnp.zeros((bp, M_HEADS, hd_m), F32), jnp.zeros((bp, M_HEADS), F32),
                 jnp.zeros((bp, CONV_W - 1, 2 * D), F32))
        yp, sp = _layer(yp, empty, p, _config(bp, yp.shape[1], 0))
        ys, ss = _layer(ys, (cache_k[l], cache_v[l], cache_kidx[l], state_C[l], state_n[l],
                             state_m[l], state_conv[l]), p,
                        _config(ys.shape[0], ys.shape[1], cache_k.shape[2]))
        new_p.append(sp)
        new_s.append(ss)

    def stk(lst, i):
        return jnp.stack([s[i] for s in lst])

    return (yp, ys) + tuple(stk(new_p, i) for i in range(7)) + tuple(stk(new_s, i) for i in range(7))
```

```python
import functools
import math

import jax
import jax.numpy as jnp
from jax import lax
from jax.experimental import pallas as pl
from jax.experimental.pallas import tpu as pltpu

F32 = jnp.float32
BF16 = jnp.bfloat16
HIGHEST = lax.Precision.HIGHEST

EPS = 1e-6
CHUNK = 64
CHUNK_SHIFT = 6
M_HEADS = 4
A_HEADS = 8
IDX_HEADS = 8
IDX_DIM = 64
CONV_W = 4
TOPK_MAX = 256
LANES = 128
SUBLANES = 8
MASK_NEG = -1e30
F32_MAX = float(jnp.finfo(jnp.float32).max)
VMEM_LIMIT = 48 * 1024 * 1024
SELECT_MIN_ITERS = 12
SELECT_MAX_ITERS = 15
FINISH_MAX = 4
QK_ROWS = 128
PV_ROWS = 256
HEAD_LAG = 8
TIE_WALK_MAX = 8
COUNT_PARTS = 4
COUNT_ROWS = 128

D = 1024
COL_MQ, COL_MK, COL_MV, COL_MO = 0, 1024, 2048, 3072
COL_AQ, COL_AK, COL_AV = 4096, 5120, 6144
COL_GA, COL_GB = 7168, 8192
COL_IQ = 9216
COL_SMALL = 9728
D_Z = 9856
SM_IW, SM_MI, SM_MF = 64, 72, 76
Q_LOGIT_SCALE = math.log2(math.e) * (D // A_HEADS) ** -0.5


def _cparams(sem):
    return pltpu.CompilerParams(dimension_semantics=sem, vmem_limit_bytes=VMEM_LIMIT)


def _norm_matmul_kernel(x_ref, g_ref, w_ref, o_ref, xn_ref):
    @pl.when(pl.program_id(1) == 0)
    def _():
        x = x_ref[...]
        ms = jnp.mean(x * x, axis=-1, keepdims=True)
        xn_ref[...] = ((x * lax.rsqrt(ms + EPS)) * g_ref[...]).astype(BF16)

    o_ref[...] = jnp.dot(xn_ref[...], w_ref[...], preferred_element_type=F32)


def _norm_matmul(x, g, w, tm, tn):
    n, d = x.shape
    nout = w.shape[1]
    return pl.pallas_call(
        _norm_matmul_kernel,
        grid=(n // tm, nout // tn),
        in_specs=[pl.BlockSpec((tm, d), lambda i, j: (i, 0)),
                  pl.BlockSpec((1, d), lambda i, j: (0, 0)),
                  pl.BlockSpec((d, tn), lambda i, j: (0, j))],
        out_specs=pl.BlockSpec((tm, tn), lambda i, j: (i, j)),
        out_shape=jax.ShapeDtypeStruct((n, nout), F32),
        scratch_shapes=[pltpu.VMEM((tm, d), BF16)],
        compiler_params=_cparams(("parallel", "arbitrary")),
        name="norm_in_proj",
    )(x, g, w)


def _mlstm_kernel(mq_ref, mk_ref, mv_ref, mo_ref, sm_ref, bias_ref, wconv_ref, bconv_ref,
                  gm_ref, c0_ref, n0_ref, m0_ref, conv0_ref,
                  y_ref, c_ref, n_ref, m_ref, conv_ref, cbuf_ref, *, L):
    c = pl.program_id(1)
    hd = D // M_HEADS

    @pl.when(c == 0)
    def _():
        c_ref[...] = c0_ref[...]
        n_ref[...] = n0_ref[...]
        m_ref[...] = m0_ref[...]
        cbuf_ref[8 - (CONV_W - 1):8, :] = conv0_ref[0]

    cbuf_ref[8:8 + L, 0:D] = mq_ref[0]
    cbuf_ref[8:8 + L, D:2 * D] = mk_ref[0]
    wc = wconv_ref[...]
    qk = bconv_ref[...] + cbuf_ref[5:5 + L, :] * wc[0:1, :]
    for j in range(1, CONV_W):
        qk = qk + cbuf_ref[5 + j:5 + j + L, :] * wc[j:j + 1, :]
    tail = cbuf_ref[5 + L:8 + L, :]
    cbuf_ref[5:8, :] = tail
    conv_ref[0] = tail
    qk = qk * jax.nn.sigmoid(qk)

    g_all = sm_ref[0] + bias_ref[...]
    lf_all = jnp.minimum(g_all, 0.0) - jnp.log1p(jnp.exp(-jnp.abs(g_all)))
    lane = lax.broadcasted_iota(jnp.int32, (L, LANES), 1)
    gates = jnp.where(lane >= SM_MF, lf_all, g_all)
    r_i = lax.broadcasted_iota(jnp.int32, (L, L), 0)
    c_i = lax.broadcasted_iota(jnp.int32, (L, L), 1)
    tril = (c_i <= r_i).astype(F32)
    triu = (r_i <= c_i).astype(F32)
    b_col_all = jnp.dot(tril, lf_all, precision=HIGHEST, preferred_element_type=F32)
    sel = (lax.broadcasted_iota(jnp.int32, (8, LANES), 1)
           == lax.broadcasted_iota(jnp.int32, (8, LANES), 0) + SM_MI).astype(F32)
    rows = lax.dot_general(sel, gates, (((1,), (1,)), ((), ())), precision=HIGHEST,
                           preferred_element_type=F32)
    b_row_all = jnp.dot(rows, triu, precision=HIGHEST, preferred_element_type=F32)
    causal = c_i <= r_i

    mo = mo_ref[0]
    mv = mv_ref[0]
    for h in range(M_HEADS):
        hs = slice(h * hd, (h + 1) * hd)
        qh = qk[:, hs]
        kh = qk[:, D + h * hd:D + (h + 1) * hd] * (hd ** -0.5)
        vh = mv[:, hs]
        qb, kb, vb = qh.astype(BF16), kh.astype(BF16), vh.astype(BF16)
        b_col = b_col_all[:, SM_MF + h:SM_MF + h + 1]
        i_col = gates[:, SM_MI + h:SM_MI + h + 1]
        b_row = b_row_all[M_HEADS + h:M_HEADS + h + 1, :]
        i_row = rows[h:h + 1, :]
        m_prev = m_ref[0, :, h:h + 1]
        c_prev = c_ref[0, h]
        n_prev = n_ref[0, h:h + 1, :]

        dmat = jnp.where(causal, b_col - b_row + i_row, -jnp.inf)
        inter = b_col + m_prev
        m_t = jnp.maximum(inter, jnp.max(dmat, axis=-1, keepdims=True))
        w_intra = jnp.exp(dmat - m_t)
        w_inter = jnp.exp(inter - m_t)
        s = lax.dot_general(qb, kb, (((1,), (1,)), ((), ())), preferred_element_type=F32) * w_intra
        qc = lax.dot_general(qb, c_prev.astype(BF16), (((1,), (1,)), ((), ())),
                             preferred_element_type=F32)
        num = jnp.dot(s.astype(BF16), vb, preferred_element_type=F32) + w_inter * qc
        den = jnp.sum(s, axis=-1, keepdims=True) + w_inter * jnp.sum(qh * n_prev, axis=-1, keepdims=True)
        denom = jnp.maximum(jnp.abs(den), jnp.exp(-m_t))
        hh = num / denom

        m_new = m_t[L - 1:L, :]
        b_last = b_col[L - 1:L, :]
        g_col = jnp.exp(b_last - b_col + i_col - m_new)
        decay = jnp.exp(b_last + m_prev - m_new)
        gv = (g_col * vh).astype(BF16)
        c_ref[0, h] = decay * c_prev + lax.dot_general(
            gv, kb, (((0,), (0,)), ((), ())), preferred_element_type=F32)
        n_ref[0, h:h + 1, :] = decay * n_prev + jnp.sum(g_col * kh, axis=0, keepdims=True)
        m_ref[0, :, h:h + 1] = m_new

        hn = hh * lax.rsqrt(jnp.mean(hh * hh, axis=-1, keepdims=True) + EPS) * gm_ref[:, hs]
        y_ref[0, :, hs] = (hn * jax.nn.sigmoid(mo[:, hs])).astype(BF16)


def _mlstm(z3, bias_row, w_conv, b_conv, g_mnorm, c0, n0, m0, conv0, L):
    b, t, _ = z3.shape
    nc = t // L
    hd = D // M_HEADS

    def zspec(col, width):
        return pl.BlockSpec((1, L, width), lambda i, c: (i, c, col // width))

    def per_batch(shape):
        nd = len(shape)
        return pl.BlockSpec((1,) + shape, lambda i, c: (i,) + (0,) * nd)

    def const(shape):
        nd = len(shape)
        return pl.BlockSpec(shape, lambda i, c: (0,) * nd)

    return pl.pallas_call(
        functools.partial(_mlstm_kernel, L=L),
        grid=(b, nc),
        in_specs=[zspec(COL_MQ, D), zspec(COL_MK, D), zspec(COL_MV, D), zspec(COL_MO, D),
                  zspec(COL_SMALL, LANES), const((1, LANES)), const((CONV_W, 2 * D)),
                  const((1, 2 * D)), const((1, D)),
                  per_batch((M_HEADS, hd, hd)), per_batch((M_HEADS, hd)),
                  per_batch((1, M_HEADS)), per_batch((CONV_W - 1, 2 * D))],
        out_specs=[pl.BlockSpec((1, L, D), lambda i, c: (i, c, 0)),
                   per_batch((M_HEADS, hd, hd)), per_batch((M_HEADS, hd)),
                   per_batch((1, M_HEADS)), per_batch((CONV_W - 1, 2 * D))],
        out_shape=[jax.ShapeDtypeStruct((b, t, D), BF16),
                   jax.ShapeDtypeStruct((b, M_HEADS, hd, hd), F32),
                   jax.ShapeDtypeStruct((b, M_HEADS, hd), F32),
                   jax.ShapeDtypeStruct((b, 1, M_HEADS), F32),
                   jax.ShapeDtypeStruct((b, CONV_W - 1, 2 * D), F32)],
        scratch_shapes=[pltpu.VMEM((8 + L, 2 * D), F32)],
        compiler_params=_cparams(("parallel", "arbitrary")),
        name="mlstm",
    )(z3, z3, z3, z3, z3, bias_row, w_conv, b_conv, g_mnorm, c0, n0, m0, conv0)


def _eye_bf16(n):
    return (lax.broadcasted_iota(jnp.int32, (n, n), 0)
            == lax.broadcasted_iota(jnp.int32, (n, n), 1)).astype(BF16)


def _transpose_bf16(eye, a):
    return lax.dot_general(eye, a, (((1,), (1,)), ((), ())), preferred_element_type=F32)


def _qkv_kernel(aq_ref, ak_ref, av_ref, gq_ref, gk_ref, k32_ref, v32_ref, kb_ref, qt_ref, vt_ref):
    hd = D // A_HEADS
    eye = _eye_bf16(hd)
    for h in range(A_HEADS):
        hs = slice(h * hd, (h + 1) * hd)
        q = aq_ref[0, :, hs]
        k = ak_ref[0, :, hs]
        v = av_ref[0, :, hs]
        qn = (q * lax.rsqrt(jnp.mean(q * q, axis=-1, keepdims=True) + EPS)) * gq_ref[...]
        kn = (k * lax.rsqrt(jnp.mean(k * k, axis=-1, keepdims=True) + EPS)) * gk_ref[...]
        k32_ref[0, :, hs] = kn
        kb_ref[0, :, hs] = kn.astype(BF16)
        v32_ref[0, :, hs] = v
        qt_ref[0, hs, :] = _transpose_bf16(eye, (qn * Q_LOGIT_SCALE).astype(BF16)).astype(BF16)
        vt_ref[0, hs, :] = _transpose_bf16(eye, v.astype(BF16)).astype(BF16)


def _qkv(z3, g_q, g_k, tm):
    b, t, _ = z3.shape
    hd = D // A_HEADS

    def zspec(col):
        return pl.BlockSpec((1, tm, D), lambda i, j: (i, j, col // D))

    row = pl.BlockSpec((1, tm, D), lambda i, j: (i, j, 0))
    colb = pl.BlockSpec((1, D, tm), lambda i, j: (i, 0, j))
    gspec = pl.BlockSpec((1, hd), lambda i, j: (0, 0))
    return pl.pallas_call(
        _qkv_kernel,
        grid=(b, t // tm),
        in_specs=[zspec(COL_AQ), zspec(COL_AK), zspec(COL_AV), gspec, gspec],
        out_specs=[row, row, row, colb, colb],
        out_shape=[jax.ShapeDtypeStruct((b, t, D), F32), jax.ShapeDtypeStruct((b, t, D), F32),
                   jax.ShapeDtypeStruct((b, t, D), BF16), jax.ShapeDtypeStruct((b, D, t), BF16),
                   jax.ShapeDtypeStruct((b, D, t), BF16)],
        compiler_params=_cparams(("parallel", "parallel")),
        name="qkv_norm",
    )(z3, z3, z3, g_q, g_k)


def _num_key_tiles(qi, tq, tile, q_pos0, s_real):
    last_chunk = (q_pos0 + (qi + 1) * tq - 1) // CHUNK
    kend = jnp.minimum((last_chunk + 1) * CHUNK, s_real)
    return (kend + tile - 1) // tile


def _num_key_tiles_static(qi, tq, tile, q_pos0, s_real):
    last_chunk = (q_pos0 + (qi + 1) * tq - 1) // CHUNK
    return -(-min((last_chunk + 1) * CHUNK, s_real) // tile)


def _select_kernel(iq_ref, sm_ref, ki_ref, mask_ref, score_ref, iqt_ref, *,
                   tq, kb, s_pad, s_real, q_pos0, topk):
    qi = pl.program_id(1)
    nkt = _num_key_tiles(qi, tq, kb, q_pos0, s_real)
    kf = float(topk)
    groups = kb // SUBLANES

    pick = (lax.broadcasted_iota(jnp.int32, (IDX_HEADS, LANES), 1)
            == lax.broadcasted_iota(jnp.int32, (IDX_HEADS, LANES), 0) + SM_IW).astype(F32)
    w = lax.dot_general(pick, sm_ref[0], (((1,), (1,)), ((), ())), precision=HIGHEST,
                        preferred_element_type=F32) * ((IDX_HEADS * IDX_DIM) ** -0.5)
    eye = _eye_bf16(IDX_DIM)
    for h in range(IDX_HEADS):
        a = iq_ref[0, :, h * IDX_DIM:(h + 1) * IDX_DIM]
        hi = a.astype(BF16)
        lo = (a - hi.astype(F32)).astype(BF16)
        hi_t = _transpose_bf16(eye, hi).astype(BF16)
        lo_t = _transpose_bf16(eye, lo).astype(BF16)
        cols = slice((h % 2) * tq, (h % 2 + 1) * tq)
        iqt_ref[h // 2, 0:IDX_DIM, cols] = hi_t
        iqt_ref[h // 2, IDX_DIM:2 * IDX_DIM, cols] = hi_t
        iqt_ref[h // 2, 2 * IDX_DIM:3 * IDX_DIM, cols] = lo_t

    q_pos = q_pos0 + qi * tq + lax.broadcasted_iota(jnp.int32, (1, tq), 1)
    q_chunk = q_pos >> CHUNK_SHIFT
    first_key = jnp.minimum(((q_pos0 + qi * tq) >> CHUNK_SHIFT) << CHUNK_SHIFT, s_real)
    n_full = first_key // kb

    def score_tile(j, carry, masked):
        rmin, rmax = carry
        k0 = pl.multiple_of(j * kb, kb)
        kt = ki_ref[0, pl.ds(k0, kb), :]
        acc = None
        for p in range(IDX_HEADS // 2):
            rel = jnp.dot(kt, iqt_ref[p], preferred_element_type=F32)
            part = (w[2 * p:2 * p + 1] * jnp.maximum(rel[:, 0:tq], 0.0)
                    + w[2 * p + 1:2 * p + 2] * jnp.maximum(rel[:, tq:2 * tq], 0.0))
            acc = part if acc is None else acc + part
        if masked:
            key = k0 + lax.broadcasted_iota(jnp.int32, (kb, tq), 0)
            adm = jnp.logical_and((key >> CHUNK_SHIFT) <= q_chunk, key < s_real)
            low = jnp.where(adm, acc, jnp.inf)
            acc = jnp.where(adm, acc, -jnp.inf)
        else:
            low = acc
        score_ref[pl.ds(k0, kb), :] = acc
        rmin = jnp.minimum(rmin, jnp.min(low.reshape(groups, SUBLANES, tq), axis=0))
        rmax = jnp.maximum(rmax, jnp.max(acc.reshape(groups, SUBLANES, tq), axis=0))
        return rmin, rmax

    stats = (jnp.full((SUBLANES, tq), jnp.inf, F32), jnp.full((SUBLANES, tq), -jnp.inf, F32))
    stats = lax.fori_loop(0, n_full, functools.partial(score_tile, masked=False), stats)
    rmin8, rmax8 = lax.fori_loop(n_full, nkt, functools.partial(score_tile, masked=True), stats)
    rmin = jnp.min(rmin8, axis=0, keepdims=True)
    rmax = jnp.max(rmax8, axis=0, keepdims=True)

    def count(pred):
        def body(j, acc):
            for c in range(kb // COUNT_ROWS):
                k0 = pl.multiple_of(j * kb + c * COUNT_ROWS, COUNT_ROWS)
                hit = jnp.where(pred(score_ref[pl.ds(k0, COUNT_ROWS), :], k0), 1.0, 0.0)
                acc = acc + jnp.sum(hit.reshape(COUNT_ROWS // (COUNT_PARTS * SUBLANES), COUNT_PARTS,
                                                SUBLANES, tq), axis=0)
            return acc
        acc = lax.fori_loop(0, nkt, body, jnp.zeros((COUNT_PARTS, SUBLANES, tq), F32))
        return jnp.sum(jnp.sum(acc, axis=0), axis=0, keepdims=True)

    def below(cur):
        shape = (COUNT_ROWS // (COUNT_PARTS * SUBLANES), COUNT_PARTS, SUBLANES, tq)

        def body(j, carry):
            top, num = carry
            for c in range(kb // COUNT_ROWS):
                k0 = pl.multiple_of(j * kb + c * COUNT_ROWS, COUNT_ROWS)
                s = score_ref[pl.ds(k0, COUNT_ROWS), :]
                under = s < cur
                top = jnp.maximum(top, jnp.max(jnp.where(under, s, -jnp.inf).reshape(shape), axis=0))
                num = num + jnp.sum(jnp.where(under, 0.0, 1.0).reshape(shape), axis=0)
            return top, num

        top, num = lax.fori_loop(0, nkt, body, (jnp.full(shape[1:], -jnp.inf, F32),
                                                jnp.zeros(shape[1:], F32)))
        return (jnp.max(jnp.max(top, axis=0), axis=0, keepdims=True),
                jnp.sum(jnp.sum(num, axis=0), axis=0, keepdims=True))

    n_adm = jnp.minimum((q_chunk + 1) << CHUNK_SHIFT, s_real).astype(F32)

    def open_rows(cnt_lo):
        return cnt_lo > kf

    def wide_rows(st):
        return jnp.logical_and(open_rows(st[2]), st[2] - st[3] > FINISH_MAX)

    def flag(pred):
        return jnp.max(jnp.where(pred, 1.0, 0.0))

    def bisect(st):
        lo, hi, cnt_lo, cnt_hi = st
        active = open_rows(cnt_lo)
        mid = 0.5 * jnp.maximum(lo, rmin) + 0.5 * jnp.minimum(hi, rmax)
        cm = count(lambda s, k0: s >= mid)
        up = jnp.logical_and(active, cm >= kf)
        dn = jnp.logical_and(active, cm < kf)
        return (jnp.where(up, mid, lo), jnp.where(dn, mid, hi),
                jnp.where(up, cm, cnt_lo), jnp.where(dn, cm, cnt_hi))

    st = (jnp.full((1, tq), -F32_MAX, F32), jnp.full((1, tq), F32_MAX, F32), n_adm,
          jnp.zeros((1, tq), F32))
    st = lax.fori_loop(0, SELECT_MIN_ITERS, lambda _, s: bisect(s), st)

    def narrow(c):
        nxt = bisect(c[2])
        return c[0] + 1, flag(wide_rows(nxt)), nxt

    _, _, st = lax.while_loop(
        lambda c: jnp.logical_and(c[0] < SELECT_MAX_ITERS, c[1] > 0.0), narrow,
        (jnp.int32(SELECT_MIN_ITERS), flag(wide_rows(st)), st))

    lo, hi, cnt_lo, cnt_hi = st
    cand, _ = below(hi)

    def walk(c):
        _, walking, lo, hi, cnt_lo, cnt_hi, cand = c
        nxt, cge = below(cand)
        settle = jnp.logical_and(walking > 0.0, cge >= kf)
        move = jnp.logical_and(walking > 0.0, cge < kf)
        walking = jnp.where(move, 1.0, 0.0)
        return (jnp.max(walking), walking, jnp.where(settle, cand, lo), jnp.where(move, cand, hi),
                jnp.where(settle, cge, cnt_lo), jnp.where(move, cge, cnt_hi), jnp.where(move, nxt, cand))

    walking = jnp.where(open_rows(cnt_lo), 1.0, 0.0)
    _, _, lo, hi, cnt_lo, cnt_hi, _ = lax.while_loop(
        lambda c: c[0] > 0.0, walk, (jnp.max(walking), walking, lo, hi, cnt_lo, cnt_hi, cand))
    n_open = flag(open_rows(cnt_lo))

    def clear_tile(j, carry):
        k0 = pl.multiple_of(j * kb, kb)
        mask_ref[0, pl.ds(k0, kb), :] = jnp.zeros((kb, tq), jnp.int8)
        return carry

    lax.fori_loop(nkt, s_pad // kb, clear_tile, 0)

    @pl.when(n_open == 0.0)
    def _():
        def write_tile(j, carry):
            k0 = pl.multiple_of(j * kb, kb)
            keep = score_ref[pl.ds(k0, kb), :] >= lo
            mask_ref[0, pl.ds(k0, kb), :] = jnp.where(keep, 1, 0).astype(jnp.int8)
            return carry

        lax.fori_loop(0, nkt, write_tile, 0)

    @pl.when(n_open > 0.0)
    def _():
        tied = open_rows(cnt_lo)
        need = jnp.where(tied, kf - cnt_hi, 0.0)
        key_iota = lax.broadcasted_iota(jnp.int32, (kb, tq), 0)
        strip_iota = lax.broadcasted_iota(jnp.int32, (COUNT_ROWS, tq), 0)
        shape = (COUNT_ROWS // (COUNT_PARTS * SUBLANES), COUNT_PARTS, SUBLANES, tq)

        def next_tied(prev):
            def body(j, acc):
                for c in range(kb // COUNT_ROWS):
                    k0 = pl.multiple_of(j * kb + c * COUNT_ROWS, COUNT_ROWS)
                    idx = k0 + strip_iota
                    hit = jnp.logical_and(score_ref[pl.ds(k0, COUNT_ROWS), :] == lo, idx > prev)
                    acc = jnp.minimum(acc, jnp.min(jnp.where(hit, idx, s_pad).reshape(shape), axis=0))
                return acc
            acc = lax.fori_loop(0, nkt, body, jnp.full(shape[1:], s_pad, jnp.int32))
            return jnp.min(jnp.min(acc, axis=0), axis=0, keepdims=True)

        def cut_by_walk(_):
            def step(c):
                _, left, last = c
                take = left > 0.0
                last = jnp.where(take, next_tied(last), last)
                left = jnp.where(take, left - 1.0, left)
                return jnp.max(left), left, last
            _, _, last = lax.while_loop(lambda c: c[0] > 0.0, step,
                                        (jnp.max(need), need, jnp.full((1, tq), -1, jnp.int32)))
            return last + 1

        def cut_by_bisection(_):
            def step(_, st):
                jlo, jhi = st
                jmid = (jlo + jhi) >> 1
                cm = count(lambda s, k0: jnp.logical_and(s == lo, k0 + strip_iota < jmid))
                ok = cm >= need
                return jnp.where(ok, jlo, jmid), jnp.where(ok, jmid, jhi)
            steps = int(math.ceil(math.log2(s_pad))) + 1
            return lax.fori_loop(0, steps, step, (jnp.zeros((1, tq), jnp.int32),
                                                  jnp.full((1, tq), s_pad, jnp.int32)))[1]

        jcut = lax.cond(jnp.max(need) <= TIE_WALK_MAX, cut_by_walk, cut_by_bisection, 0)
        jcut = jnp.where(tied, jcut, s_pad)

        def write_tile(j, carry):
            k0 = pl.multiple_of(j * kb, kb)
            s = score_ref[pl.ds(k0, kb), :]
            keep = jnp.logical_or(s >= hi, jnp.logical_and(s >= lo, k0 + key_iota < jcut))
            mask_ref[0, pl.ds(k0, kb), :] = jnp.where(keep, 1, 0).astype(jnp.int8)
            return carry

        lax.fori_loop(0, nkt, write_tile, 0)


def _select(z3, ki3, tq, kb, s_real, q_pos0, topk):
    b, t, _ = z3.shape
    s_pad = ki3.shape[1]
    iq_w = IDX_HEADS * IDX_DIM
    kern = functools.partial(_select_kernel, tq=tq, kb=kb, s_pad=s_pad, s_real=s_real,
                             q_pos0=q_pos0, topk=topk)
    return pl.pallas_call(
        kern,
        grid=(b, t // tq),
        in_specs=[pl.BlockSpec((1, tq, iq_w), lambda i, j: (i, j, COL_IQ // iq_w)),
                  pl.BlockSpec((1, tq, LANES), lambda i, j: (i, j, COL_SMALL // LANES)),
                  pl.BlockSpec((1, s_pad, 3 * IDX_DIM), lambda i, j: (i, 0, 0))],
        out_specs=pl.BlockSpec((1, s_pad, tq), lambda i, j: (i, 0, j)),
        out_shape=jax.ShapeDtypeStruct((b, s_pad, t), jnp.int8),
        scratch_shapes=[pltpu.VMEM((s_pad, tq), F32),
                        pltpu.VMEM((IDX_HEADS // 2, 3 * IDX_DIM, 2 * tq), BF16)],
        compiler_params=_cparams(("parallel", "arbitrary")),
        name="index_select",
    )(z3, z3, ki3)


def _attn_kernel(q_of_ref, k_of_ref, qt_ref, k_ref, vt_ref, mask_ref, o_ref,
                 acc_ref, m_ref, l_ref, s_ref, bias_ref):
    step = pl.program_id(1)
    hd = D // A_HEADS
    tq = qt_ref.shape[2]
    tk = k_ref.shape[1]
    pv_rows = math.gcd(tk, PV_ROWS)
    first = k_of_ref[step] == 0
    nxt = jnp.minimum(step + 1, pl.num_programs(1) - 1)
    last = jnp.logical_or(step == pl.num_programs(1) - 1, q_of_ref[nxt] != q_of_ref[step])

    @pl.when(first)
    def _():
        acc_ref[...] = jnp.zeros(acc_ref.shape, F32)
        m_ref[...] = jnp.full(m_ref.shape, MASK_NEG, F32)
        l_ref[...] = jnp.zeros(l_ref.shape, F32)

    bias_ref[...] = jnp.where(mask_ref[0].astype(jnp.int32) != 0, 0.0, MASK_NEG)
    def logits(h):
        hs = slice(h * hd, (h + 1) * hd)
        top = None
        for r in range(0, tk, QK_ROWS):
            s = jnp.dot(k_ref[0, r:r + QK_ROWS, hs], qt_ref[0, hs, :],
                        preferred_element_type=F32) + bias_ref[r:r + QK_ROWS, :]
            s_ref[h, r:r + QK_ROWS, :] = s
            part = jnp.max(s.reshape(QK_ROWS // SUBLANES, SUBLANES, tq), axis=0)
            top = part if top is None else jnp.maximum(top, part)
        return jnp.max(top, axis=0, keepdims=True)

    def weigh(h, tile_max):
        hs = slice(h * hd, (h + 1) * hd)
        m_prev = m_ref[h:h + 1, :]
        m_new = jnp.maximum(m_prev, tile_max)
        alpha = jnp.exp2(m_prev - m_new)
        psum = None
        pv = None
        for r in range(0, tk, pv_rows):
            p = jnp.exp2(s_ref[h, r:r + pv_rows, :] - m_new)
            part = jnp.sum(p.reshape(pv_rows // SUBLANES, SUBLANES, tq), axis=0)
            psum = part if psum is None else psum + part
            prod = jnp.dot(vt_ref[0, hs, r:r + pv_rows], p.astype(BF16),
                           preferred_element_type=F32)
            pv = prod if pv is None else pv + prod
        l_ref[h:h + 1, :] = alpha * l_ref[h:h + 1, :] + jnp.sum(psum, axis=0, keepdims=True)
        acc_ref[hs, :] = alpha * acc_ref[hs, :] + pv
        m_ref[h:h + 1, :] = m_new

    tile_max = [logits(h) for h in range(A_HEADS)]

    @pl.when(k_of_ref[step] >= 0)
    def _():
        for h in range(A_HEADS):
            weigh(h, tile_max[h])

    @pl.when(last)
    def _():
        eye = _eye_bf16(tq)
        for h in range(A_HEADS):
            hs = slice(h * hd, (h + 1) * hd)
            out_t = (acc_ref[hs, :] / l_ref[h:h + 1, :]).astype(BF16)
            o_ref[0, :, hs] = _transpose_bf16(eye, out_t).astype(BF16)


def _attention(qt, kb, vt, mask_t, tq, tk, s_real, q_pos0):
    b, _, t = qt.shape
    steps = [(q, j) for q in range(t // tq) for j in range(_num_key_tiles_static(q, tq, tk, q_pos0, s_real))]
    q_of = jnp.array([q for q, _ in steps], jnp.int32)
    k_of = jnp.array([j for _, j in steps], jnp.int32)
    return pl.pallas_call(
        _attn_kernel,
        grid_spec=pltpu.PrefetchScalarGridSpec(
            num_scalar_prefetch=2,
            grid=(b, len(steps)),
            in_specs=[pl.BlockSpec((1, D, tq), lambda i, s, q_of, k_of: (i, 0, q_of[s])),
                      pl.BlockSpec((1, tk, D), lambda i, s, q_of, k_of: (i, k_of[s], 0)),
                      pl.BlockSpec((1, D, tk), lambda i, s, q_of, k_of: (i, 0, k_of[s])),
                      pl.BlockSpec((1, tk, tq), lambda i, s, q_of, k_of: (i, k_of[s], q_of[s]))],
            out_specs=pl.BlockSpec((1, tq, D), lambda i, s, q_of, k_of: (i, q_of[s], 0)),
            scratch_shapes=[pltpu.VMEM((D, tq), F32),
                            pltpu.VMEM((A_HEADS, tq), F32),
                            pltpu.VMEM((A_HEADS, tq), F32),
                            pltpu.VMEM((A_HEADS, tk, tq), F32),
                            pltpu.VMEM((tk, tq), F32)]),
        out_shape=jax.ShapeDtypeStruct((b, t, D), BF16),
        compiler_params=_cparams(("parallel", "arbitrary")),
        name="masked_attention",
    )(q_of, k_of, qt, kb, vt, mask_t)


def _merge_kernel(x_ref, ya_ref, yb_ref, ga_ref, gb_ref, wa_ref, wb_ref, wo_ref, o_ref):
    a = jnp.dot(ya_ref[...], wa_ref[...], preferred_element_type=F32)
    bb = jnp.dot(yb_ref[...], wb_ref[...], preferred_element_type=F32)
    mix = jax.nn.sigmoid(ga_ref[...]) * a + jax.nn.sigmoid(gb_ref[...]) * bb
    o_ref[...] = x_ref[...] + jnp.dot(mix.astype(BF16), wo_ref[...], preferred_element_type=F32)


def _merge(x, ya, yb, z, wa, wb, wo, tm):
    n = x.shape[0]
    row = pl.BlockSpec((tm, D), lambda i: (i, 0))
    wspec = pl.BlockSpec((D, D), lambda i: (0, 0))
    return pl.pallas_call(
        _merge_kernel,
        grid=(n // tm,),
        in_specs=[row, row, row,
                  pl.BlockSpec((tm, D), lambda i: (i, COL_GA // D)),
                  pl.BlockSpec((tm, D), lambda i: (i, COL_GB // D)),
                  wspec, wspec, wspec],
        out_specs=row,
        out_shape=jax.ShapeDtypeStruct((n, D), F32),
        compiler_params=_cparams(("parallel",)),
        name="merge_out_proj",
    )(x, ya, yb, z, z, wa, wb, wo)


def _ffn_kernel(x_ref, g_ref, wg_ref, wu_ref, wd_ref, o_ref, xn_ref, acc_ref):
    c = pl.program_id(1)

    @pl.when(c == 0)
    def _():
        x = x_ref[...]
        ms = jnp.mean(x * x, axis=-1, keepdims=True)
        xn_ref[...] = ((x * lax.rsqrt(ms + EPS)) * g_ref[...]).astype(BF16)
        acc_ref[...] = x

    xn = xn_ref[...]
    gate = jnp.dot(xn, wg_ref[...], preferred_element_type=F32)
    up = jnp.dot(xn, wu_ref[...], preferred_element_type=F32)
    act = (gate * jax.nn.sigmoid(gate)) * up
    acc_ref[...] += jnp.dot(act.astype(BF16), wd_ref[...], preferred_element_type=F32)

    @pl.when(c == pl.num_programs(1) - 1)
    def _():
        o_ref[...] = acc_ref[...]


def _ffn(x, g, w_in, w_out, tm, tf):
    n = x.shape[0]
    dff = w_out.shape[0]
    nf = dff // tf
    row = pl.BlockSpec((tm, D), lambda i, c: (i, 0))
    return pl.pallas_call(
        _ffn_kernel,
        grid=(n // tm, nf),
        in_specs=[row, pl.BlockSpec((1, D), lambda i, c: (0, 0)),
                  pl.BlockSpec((D, tf), lambda i, c: (0, c)),
                  pl.BlockSpec((D, tf), lambda i, c: (0, nf + c)),
                  pl.BlockSpec((tf, D), lambda i, c: (c, 0))],
        out_specs=row,
        out_shape=jax.ShapeDtypeStruct((n, D), F32),
        scratch_shapes=[pltpu.VMEM((tm, D), BF16), pltpu.VMEM((tm, D), F32)],
        compiler_params=_cparams(("parallel", "arbitrary")),
        name="swiglu_ffn",
    )(x, g, w_in, w_in, w_out)


def _split_hi_lo(a):
    hi = a.astype(BF16)
    lo = (a - hi.astype(F32)).astype(BF16)
    return hi, lo


def _layer(x, past, p, cfg):
    k_past, v_past, ki_past, c0, n0, m0, conv0 = past
    b, t, _ = x.shape
    n_tok = b * t
    past_len = k_past.shape[1]
    s_real = past_len + t
    topk = min(TOPK_MAX, s_real // 4)
    tq_sel, kb_sel, tq_att, tk_att = cfg["tq_sel"], cfg["kb_sel"], cfg["tq_att"], cfg["tk_att"]
    s_pad = -(-s_real // tk_att) * tk_att

    x2 = x.reshape(n_tok, D)
    z = _norm_matmul(x2, p["g_norm1"], p["w_in"], cfg["tm_proj"], cfg["tn_proj"])
    z3 = z.reshape(b, t, D_Z)

    y_a, c_new, n_new, m_new, conv_new = _mlstm(
        z3, p["gate_bias"], p["w_conv"], p["b_conv"], p["g_mnorm"], c0, n0,
        m0.reshape(b, 1, M_HEADS), conv0, cfg["mlstm_chunk"])

    k32, v32, kb, qt, vt = _qkv(z3, p["g_q"], p["g_k"], cfg["tm_qkv"])
    ik = z3[:, :, COL_SMALL:COL_SMALL + IDX_DIM]
    ki_all = ik if past_len == 0 else jnp.concatenate([ki_past.astype(F32), ik], axis=1)
    k_hi, k_lo = _split_hi_lo(ki_all)
    ki3 = jnp.concatenate([k_hi, k_lo, k_hi], axis=-1)
    k_all, vt_all = kb, vt
    if past_len > 0 or s_pad > s_real:
        ki3 = jnp.pad(ki3, ((0, 0), (0, s_pad - s_real), (0, 0)))
        k_all = jnp.concatenate([k_past.reshape(b, past_len, D).astype(BF16), kb,
                                 jnp.zeros((b, s_pad - s_real, D), BF16)], axis=1)
        vt_all = jnp.concatenate([v_past.reshape(b, past_len, D).astype(BF16).transpose(0, 2, 1), vt,
                                  jnp.zeros((b, D, s_pad - s_real), BF16)], axis=2)
    mask_t = _select(z3, ki3, tq_sel, kb_sel, s_real, past_len, topk)
    y_b = _attention(qt, k_all, vt_all, mask_t, tq_att, tk_att, s_real, past_len)

    x1 = _merge(x2, y_a.reshape(n_tok, D), y_b.reshape(n_tok, D), z,
                p["w_a_out"], p["w_b_out"], p["w_o"], cfg["tm_rows"])
    y = _ffn(x1, p["g_norm2"], p["w_ffn_in"], p["w_ffn_out"], cfg["tm_ffn"], cfg["tf_ffn"])

    hd = D // A_HEADS
    return y.reshape(b, t, D), (k32.reshape(b, t, A_HEADS, hd), v32.reshape(b, t, A_HEADS, hd), ik,
                                c_new, n_new, m_new.reshape(b, M_HEADS), conv_new)


def _prep_params(g_norm1, w_in, b_if, w_conv, b_conv, g_mnorm, g_q, g_k, w_a_out, w_b_out, w_o,
                 g_norm2, w_ffn_in, w_ffn_out):
    o_mi = 4 * D
    o_aq = o_mi + 2 * M_HEADS
    o_iq = o_aq + 3 * D
    o_ik = o_iq + IDX_HEADS * IDX_DIM
    o_iw = o_ik + IDX_DIM
    o_ga = o_iw + IDX_HEADS
    w_perm = jnp.concatenate([
        w_in[:, 0:o_mi], w_in[:, o_aq:o_iq], w_in[:, o_ga:o_ga + 2 * D], w_in[:, o_iq:o_ik],
        w_in[:, o_ik:o_ga], w_in[:, o_mi:o_aq],
        jnp.zeros((D, D_Z - COL_SMALL - (SM_MF + M_HEADS)), w_in.dtype)], axis=1).astype(BF16)
    gate_bias = jnp.zeros((1, LANES), F32).at[0, SM_MI:SM_MI + 2 * M_HEADS].set(b_if.astype(F32))
    return {
        "g_norm1": g_norm1.reshape(1, D), "w_in": w_perm, "gate_bias": gate_bias,
        "w_conv": w_conv, "b_conv": b_conv.reshape(1, 2 * D), "g_mnorm": g_mnorm.reshape(1, D),
        "g_q": g_q.reshape(1, -1), "g_k": g_k.reshape(1, -1),
        "w_a_out": w_a_out.astype(BF16), "w_b_out": w_b_out.astype(BF16), "w_o": w_o.astype(BF16),
        "g_norm2": g_norm2.reshape(1, D), "w_ffn_in": w_ffn_in.astype(BF16),
        "w_ffn_out": w_ffn_out.astype(BF16),
    }


def _config(b, t, past_len):
    n_tok = b * t
    s_real = past_len + t
    tm = min(1024, n_tok)
    tq_sel = tq_att = min(256, t)
    if s_real % 512 == 0:
        kb_sel = tk_att = 512
    else:
        kb_sel = tk_att = -(-s_real // LANES) * LANES
    return {"tm_proj": tm, "tn_proj": 896, "mlstm_chunk": min(128, t), "tm_rows": min(512, n_tok),
            "tm_qkv": min(512, t), "tq_sel": tq_sel, "kb_sel": kb_sel, "tq_att": tq_att,
            "tk_att": tk_att, "tm_ffn": min(512, n_tok), "tf_ffn": 1408}


def kernel(x_prompt, x_sample, cache_k, cache_v, cache_kidx, state_C, state_n, state_m, state_conv,
           g_norm1, w_in, b_if, w_conv, b_conv, g_mnorm, g_q, g_k, w_a_out, w_b_out, w_o,
           g_norm2, w_ffn_in, w_ffn_out):
    depth = w_in.shape[0]
    bp = x_prompt.shape[0]
    hd_a = D // A_HEADS
    hd_m = D // M_HEADS
    yp, ys = x_prompt, x_sample
    new_p, new_s = [], []
    for l in range(depth):
        p = _prep_params(g_norm1[l], w_in[l], b_if[l], w_conv[l], b_conv[l], g_mnorm[l], g_q[l],
                         g_k[l], w_a_out[l], w_b_out[l], w_o[l], g_norm2[l], w_ffn_in[l],
                         w_ffn_out[l])
        empty = (jnp.zeros((bp, 0, A_HEADS, hd_a), F32), jnp.zeros((bp, 0, A_HEADS, hd_a), F32),
                 jnp.zeros((bp, 0, IDX_DIM), F32), jnp.zeros((bp, M_HEADS, hd_m, hd_m), F32),
                 jnp.zeros((bp, M_HEADS, hd_m), F32), jnp.zeros((bp, M_HEADS), F32),
                 jnp.zeros((bp, CONV_W - 1, 2 * D), F32))
        yp, sp = _layer(yp, empty, p, _config(bp, yp.shape[1], 0))
        ys, ss = _layer(ys, (cache_k[l], cache_v[l], cache_kidx[l], state_C[l], state_n[l],
                             state_m[l], state_conv[l]), p,
                        _config(ys.shape[0], ys.shape[1], cache_k.shape[2]))
        new_p.append(sp)
        new_s.append(ss)

    def stk(lst, i):
        return jnp.stack([s[i] for s in lst])

    return (yp, ys) + tuple(stk(new_p, i) for i in range(7)) + tuple(stk(new_s, i) for i in range(7))
```

```python
import functools
import math

import jax
import jax.numpy as jnp
from jax import lax
from jax.experimental import pallas as pl
from jax.experimental.pallas import tpu as pltpu

F32 = jnp.float32
BF16 = jnp.bfloat16
HIGHEST = lax.Precision.HIGHEST

EPS = 1e-6
CHUNK = 64
CHUNK_SHIFT = 6
M_HEADS = 4
A_HEADS = 8
IDX_HEADS = 8
IDX_DIM = 64
CONV_W = 4
TOPK_MAX = 256
LANES = 128
SUBLANES = 8
MASK_NEG = -1e30
F32_MAX = float(jnp.finfo(jnp.float32).max)
VMEM_LIMIT = 48 * 1024 * 1024
SELECT_MIN_ITERS = 12
SELECT_MAX_ITERS = 15
FINISH_MAX = 4
QK_ROWS = 128
PV_ROWS = 256
HEAD_LAG = 8
TIE_WALK_MAX = 8
COUNT_PARTS = 4
COUNT_ROWS = 128

D = 1024
COL_MQ, COL_MK, COL_MV, COL_MO = 0, 1024, 2048, 3072
COL_AQ, COL_AK, COL_AV = 4096, 5120, 6144
COL_GA, COL_GB = 7168, 8192
COL_IQ = 9216
COL_SMALL = 9728
D_Z = 9856
SM_IW, SM_MI, SM_MF = 64, 72, 76
Q_LOGIT_SCALE = math.log2(math.e) * (D // A_HEADS) ** -0.5


def _cparams(sem):
    return pltpu.CompilerParams(dimension_semantics=sem, vmem_limit_bytes=VMEM_LIMIT)


def _norm_matmul_kernel(x_ref, g_ref, w_ref, o_ref, xn_ref):
    @pl.when(pl.program_id(1) == 0)
    def _():
        x = x_ref[...]
        ms = jnp.mean(x * x, axis=-1, keepdims=True)
        xn_ref[...] = ((x * lax.rsqrt(ms + EPS)) * g_ref[...]).astype(BF16)

    o_ref[...] = jnp.dot(xn_ref[...], w_ref[...], preferred_element_type=F32)


def _norm_matmul(x, g, w, tm, tn):
    n, d = x.shape
    nout = w.shape[1]
    return pl.pallas_call(
        _norm_matmul_kernel,
        grid=(n // tm, nout // tn),
        in_specs=[pl.BlockSpec((tm, d), lambda i, j: (i, 0)),
                  pl.BlockSpec((1, d), lambda i, j: (0, 0)),
                  pl.BlockSpec((d, tn), lambda i, j: (0, j))],
        out_specs=pl.BlockSpec((tm, tn), lambda i, j: (i, j)),
        out_shape=jax.ShapeDtypeStruct((n, nout), F32),
        scratch_shapes=[pltpu.VMEM((tm, d), BF16)],
        compiler_params=_cparams(("parallel", "arbitrary")),
        name="norm_in_proj",
    )(x, g, w)


def _mlstm_kernel(mq_ref, mk_ref, mv_ref, mo_ref, sm_ref, bias_ref, wconv_ref, bconv_ref,
                  gm_ref, c0_ref, n0_ref, m0_ref, conv0_ref,
                  y_ref, c_ref, n_ref, m_ref, conv_ref, cbuf_ref, *, L):
    c = pl.program_id(1)
    hd = D // M_HEADS

    @pl.when(c == 0)
    def _():
        c_ref[...] = c0_ref[...]
        n_ref[...] = n0_ref[...]
        m_ref[...] = m0_ref[...]
        cbuf_ref[8 - (CONV_W - 1):8, :] = conv0_ref[0]

    cbuf_ref[8:8 + L, 0:D] = mq_ref[0]
    cbuf_ref[8:8 + L, D:2 * D] = mk_ref[0]
    wc = wconv_ref[...]
    qk = bconv_ref[...] + cbuf_ref[5:5 + L, :] * wc[0:1, :]
    for j in range(1, CONV_W):
        qk = qk + cbuf_ref[5 + j:5 + j + L, :] * wc[j:j + 1, :]
    tail = cbuf_ref[5 + L:8 + L, :]
    cbuf_ref[5:8, :] = tail
    conv_ref[0] = tail
    qk = qk * jax.nn.sigmoid(qk)

    g_all = sm_ref[0] + bias_ref[...]
    lf_all = jnp.minimum(g_all, 0.0) - jnp.log1p(jnp.exp(-jnp.abs(g_all)))
    lane = lax.broadcasted_iota(jnp.int32, (L, LANES), 1)
    gates = jnp.where(lane >= SM_MF, lf_all, g_all)
    r_i = lax.broadcasted_iota(jnp.int32, (L, L), 0)
    c_i = lax.broadcasted_iota(jnp.int32, (L, L), 1)
    tril = (c_i <= r_i).astype(F32)
    triu = (r_i <= c_i).astype(F32)
    b_col_all = jnp.dot(tril, lf_all, precision=HIGHEST, preferred_element_type=F32)
    sel = (lax.broadcasted_iota(jnp.int32, (8, LANES), 1)
           == lax.broadcasted_iota(jnp.int32, (8, LANES), 0) + SM_MI).astype(F32)
    rows = lax.dot_general(sel, gates, (((1,), (1,)), ((), ())), precision=HIGHEST,
                           preferred_element_type=F32)
    b_row_all = jnp.dot(rows, triu, precision=HIGHEST, preferred_element_type=F32)
    causal = c_i <= r_i

    mo = mo_ref[0]
    mv = mv_ref[0]
    for h in range(M_HEADS):
        hs = slice(h * hd, (h + 1) * hd)
        qh = qk[:, hs]
        kh = qk[:, D + h * hd:D + (h + 1) * hd] * (hd ** -0.5)
        vh = mv[:, hs]
        qb, kb, vb = qh.astype(BF16), kh.astype(BF16), vh.astype(BF16)
        b_col = b_col_all[:, SM_MF + h:SM_MF + h + 1]
        i_col = gates[:, SM_MI + h:SM_MI + h + 1]
        b_row = b_row_all[M_HEADS + h:M_HEADS + h + 1, :]
        i_row = rows[h:h + 1, :]
        m_prev = m_ref[0, :, h:h + 1]
        c_prev = c_ref[0, h]
        n_prev = n_ref[0, h:h + 1, :]

        dmat = jnp.where(causal, b_col - b_row + i_row, -jnp.inf)
        inter = b_col + m_prev
        m_t = jnp.maximum(inter, jnp.max(dmat, axis=-1, keepdims=True))
        w_intra = jnp.exp(dmat - m_t)
        w_inter = jnp.exp(inter - m_t)
        s = lax.dot_general(qb, kb, (((1,), (1,)), ((), ())), preferred_element_type=F32) * w_intra
        qc = lax.dot_general(qb, c_prev.astype(BF16), (((1,), (1,)), ((), ())),
                             preferred_element_type=F32)
        num = jnp.dot(s.astype(BF16), vb, preferred_element_type=F32) + w_inter * qc
        den = jnp.sum(s, axis=-1, keepdims=True) + w_inter * jnp.sum(qh * n_prev, axis=-1, keepdims=True)
        denom = jnp.maximum(jnp.abs(den), jnp.exp(-m_t))
        hh = num / denom

        m_new = m_t[L - 1:L, :]
        b_last = b_col[L - 1:L, :]
        g_col = jnp.exp(b_last - b_col + i_col - m_new)
        decay = jnp.exp(b_last + m_prev - m_new)
        gv = (g_col * vh).astype(BF16)
        c_ref[0, h] = decay * c_prev + lax.dot_general(
            gv, kb, (((0,), (0,)), ((), ())), preferred_element_type=F32)
        n_ref[0, h:h + 1, :] = decay * n_prev + jnp.sum(g_col * kh, axis=0, keepdims=True)
        m_ref[0, :, h:h + 1] = m_new

        hn = hh * lax.rsqrt(jnp.mean(hh * hh, axis=-1, keepdims=True) + EPS) * gm_ref[:, hs]
        y_ref[0, :, hs] = (hn * jax.nn.sigmoid(mo[:, hs])).astype(BF16)


def _mlstm(z3, bias_row, w_conv, b_conv, g_mnorm, c0, n0, m0, conv0, L):
    b, t, _ = z3.shape
    nc = t // L
    hd = D // M_HEADS

    def zspec(col, width):
        return pl.BlockSpec((1, L, width), lambda i, c: (i, c, col // width))

    def per_batch(shape):
        nd = len(shape)
        return pl.BlockSpec((1,) + shape, lambda i, c: (i,) + (0,) * nd)

    def const(shape):
        nd = len(shape)
        return pl.BlockSpec(shape, lambda i, c: (0,) * nd)

    return pl.pallas_call(
        functools.partial(_mlstm_kernel, L=L),
        grid=(b, nc),
        in_specs=[zspec(COL_MQ, D), zspec(COL_MK, D), zspec(COL_MV, D), zspec(COL_MO, D),
                  zspec(COL_SMALL, LANES), const((1, LANES)), const((CONV_W, 2 * D)),
                  const((1, 2 * D)), const((1, D)),
                  per_batch((M_HEADS, hd, hd)), per_batch((M_HEADS, hd)),
                  per_batch((1, M_HEADS)), per_batch((CONV_W - 1, 2 * D))],
        out_specs=[pl.BlockSpec((1, L, D), lambda i, c: (i, c, 0)),
                   per_batch((M_HEADS, hd, hd)), per_batch((M_HEADS, hd)),
                   per_batch((1, M_HEADS)), per_batch((CONV_W - 1, 2 * D))],
        out_shape=[jax.ShapeDtypeStruct((b, t, D), BF16),
                   jax.ShapeDtypeStruct((b, M_HEADS, hd, hd), F32),
                   jax.ShapeDtypeStruct((b, M_HEADS, hd), F32),
                   jax.ShapeDtypeStruct((b, 1, M_HEADS), F32),
                   jax.ShapeDtypeStruct((b, CONV_W - 1, 2 * D), F32)],
        scratch_shapes=[pltpu.VMEM((8 + L, 2 * D), F32)],
        compiler_params=_cparams(("parallel", "arbitrary")),
        name="mlstm",
    )(z3, z3, z3, z3, z3, bias_row, w_conv, b_conv, g_mnorm, c0, n0, m0, conv0)


def _eye_bf16(n):
    return (lax.broadcasted_iota(jnp.int32, (n, n), 0)
            == lax.broadcasted_iota(jnp.int32, (n, n), 1)).astype(BF16)


def _transpose_bf16(eye, a):
    return lax.dot_general(eye, a, (((1,), (1,)), ((), ())), preferred_element_type=F32)


def _qkv_kernel(aq_ref, ak_ref, av_ref, gq_ref, gk_ref, k32_ref, v32_ref, kb_ref, qt_ref, vt_ref):
    hd = D // A_HEADS
    eye = _eye_bf16(hd)
    for h in range(A_HEADS):
        hs = slice(h * hd, (h + 1) * hd)
        q = aq_ref[0, :, hs]
        k = ak_ref[0, :, hs]
        v = av_ref[0, :, hs]
        qn = (q * lax.rsqrt(jnp.mean(q * q, axis=-1, keepdims=True) + EPS)) * gq_ref[...]
        kn = (k * lax.rsqrt(jnp.mean(k * k, axis=-1, keepdims=True) + EPS)) * gk_ref[...]
        k32_ref[0, :, hs] = kn
        kb_ref[0, :, hs] = kn.astype(BF16)
        v32_ref[0, :, hs] = v
        qt_ref[0, hs, :] = _transpose_bf16(eye, (qn * Q_LOGIT_SCALE).astype(BF16)).astype(BF16)
        vt_ref[0, 0, hs, :] = _transpose_bf16(eye, v.astype(BF16)).astype(BF16)


def _qkv(z3, g_q, g_k, tm):
    b, t, _ = z3.shape
    hd = D // A_HEADS

    def zspec(col):
        return pl.BlockSpec((1, tm, D), lambda i, j: (i, j, col // D))

    row = pl.BlockSpec((1, tm, D), lambda i, j: (i, j, 0))
    gspec = pl.BlockSpec((1, hd), lambda i, j: (0, 0))
    return pl.pallas_call(
        _qkv_kernel,
        grid=(b, t // tm),
        in_specs=[zspec(COL_AQ), zspec(COL_AK), zspec(COL_AV), gspec, gspec],
        out_specs=[row, row, row, pl.BlockSpec((1, D, tm), lambda i, j: (i, 0, j)),
                   pl.BlockSpec((1, 1, D, tm), lambda i, j: (i, j, 0, 0))],
        out_shape=[jax.ShapeDtypeStruct((b, t, D), F32), jax.ShapeDtypeStruct((b, t, D), F32),
                   jax.ShapeDtypeStruct((b, t, D), BF16), jax.ShapeDtypeStruct((b, D, t), BF16),
                   jax.ShapeDtypeStruct((b, t // tm, D, tm), BF16)],
        compiler_params=_cparams(("parallel", "parallel")),
        name="qkv_norm",
    )(z3, z3, z3, g_q, g_k)


def _num_key_tiles(qi, tq, tile, q_pos0, s_real):
    last_chunk = (q_pos0 + (qi + 1) * tq - 1) // CHUNK
    kend = jnp.minimum((last_chunk + 1) * CHUNK, s_real)
    return (kend + tile - 1) // tile


def _num_key_tiles_static(qi, tq, tile, q_pos0, s_real):
    last_chunk = (q_pos0 + (qi + 1) * tq - 1) // CHUNK
    return -(-min((last_chunk + 1) * CHUNK, s_real) // tile)


def _select_kernel(iq_ref, sm_ref, ki_ref, mask_ref, score_ref, iqt_ref, *,
                   tq, kb, s_pad, s_real, q_pos0, topk):
    qi = pl.program_id(1)
    nkt = _num_key_tiles(qi, tq, kb, q_pos0, s_real)
    kf = float(topk)
    groups = kb // SUBLANES

    pick = (lax.broadcasted_iota(jnp.int32, (IDX_HEADS, LANES), 1)
            == lax.broadcasted_iota(jnp.int32, (IDX_HEADS, LANES), 0) + SM_IW).astype(F32)
    w = lax.dot_general(pick, sm_ref[0], (((1,), (1,)), ((), ())), precision=HIGHEST,
                        preferred_element_type=F32) * ((IDX_HEADS * IDX_DIM) ** -0.5)
    eye = _eye_bf16(IDX_DIM)
    for h in range(IDX_HEADS):
        a = iq_ref[0, :, h * IDX_DIM:(h + 1) * IDX_DIM]
        hi = a.astype(BF16)
        lo = (a - hi.astype(F32)).astype(BF16)
        hi_t = _transpose_bf16(eye, hi).astype(BF16)
        lo_t = _transpose_bf16(eye, lo).astype(BF16)
        cols = slice((h % 2) * tq, (h % 2 + 1) * tq)
        iqt_ref[h // 2, 0:IDX_DIM, cols] = hi_t
        iqt_ref[h // 2, IDX_DIM:2 * IDX_DIM, cols] = hi_t
        iqt_ref[h // 2, 2 * IDX_DIM:3 * IDX_DIM, cols] = lo_t

    q_pos = q_pos0 + qi * tq + lax.broadcasted_iota(jnp.int32, (1, tq), 1)
    q_chunk = q_pos >> CHUNK_SHIFT
    first_key = jnp.minimum(((q_pos0 + qi * tq) >> CHUNK_SHIFT) << CHUNK_SHIFT, s_real)
    n_full = first_key // kb

    def score_tile(j, carry, masked):
        rmin, rmax = carry
        k0 = pl.multiple_of(j * kb, kb)
        kt = ki_ref[0, pl.ds(k0, kb), :]
        acc = None
        for p in range(IDX_HEADS // 2):
            rel = jnp.dot(kt, iqt_ref[p], preferred_element_type=F32)
            part = (w[2 * p:2 * p + 1] * jnp.maximum(rel[:, 0:tq], 0.0)
                    + w[2 * p + 1:2 * p + 2] * jnp.maximum(rel[:, tq:2 * tq], 0.0))
            acc = part if acc is None else acc + part
        if masked:
            key = k0 + lax.broadcasted_iota(jnp.int32, (kb, tq), 0)
            adm = jnp.logical_and((key >> CHUNK_SHIFT) <= q_chunk, key < s_real)
            low = jnp.where(adm, acc, jnp.inf)
            acc = jnp.where(adm, acc, -jnp.inf)
        else:
            low = acc
        score_ref[pl.ds(k0, kb), :] = acc
        rmin = jnp.minimum(rmin, jnp.min(low.reshape(groups, SUBLANES, tq), axis=0))
        rmax = jnp.maximum(rmax, jnp.max(acc.reshape(groups, SUBLANES, tq), axis=0))
        return rmin, rmax

    stats = (jnp.full((SUBLANES, tq), jnp.inf, F32), jnp.full((SUBLANES, tq), -jnp.inf, F32))
    stats = lax.fori_loop(0, n_full, functools.partial(score_tile, masked=False), stats)
    rmin8, rmax8 = lax.fori_loop(n_full, nkt, functools.partial(score_tile, masked=True), stats)
    rmin = jnp.min(rmin8, axis=0, keepdims=True)
    rmax = jnp.max(rmax8, axis=0, keepdims=True)

    def count(pred):
        def body(j, acc):
            for c in range(kb // COUNT_ROWS):
                k0 = pl.multiple_of(j * kb + c * COUNT_ROWS, COUNT_ROWS)
                hit = jnp.where(pred(score_ref[pl.ds(k0, COUNT_ROWS), :], k0), 1.0, 0.0)
                acc = acc + jnp.sum(hit.reshape(COUNT_ROWS // (COUNT_PARTS * SUBLANES), COUNT_PARTS,
                                                SUBLANES, tq), axis=0)
            return acc
        acc = lax.fori_loop(0, nkt, body, jnp.zeros((COUNT_PARTS, SUBLANES, tq), F32))
        return jnp.sum(jnp.sum(acc, axis=0), axis=0, keepdims=True)

    def below(cur):
        shape = (COUNT_ROWS // (COUNT_PARTS * SUBLANES), COUNT_PARTS, SUBLANES, tq)

        def body(j, carry):
            top, num = carry
            for c in range(kb // COUNT_ROWS):
                k0 = pl.multiple_of(j * kb + c * COUNT_ROWS, COUNT_ROWS)
                s = score_ref[pl.ds(k0, COUNT_ROWS), :]
                under = s < cur
                top = jnp.maximum(top, jnp.max(jnp.where(under, s, -jnp.inf).reshape(shape), axis=0))
                num = num + jnp.sum(jnp.where(under, 0.0, 1.0).reshape(shape), axis=0)
            return top, num

        top, num = lax.fori_loop(0, nkt, body, (jnp.full(shape[1:], -jnp.inf, F32),
                                                jnp.zeros(shape[1:], F32)))
        return (jnp.max(jnp.max(top, axis=0), axis=0, keepdims=True),
                jnp.sum(jnp.sum(num, axis=0), axis=0, keepdims=True))

    n_adm = jnp.minimum((q_chunk + 1) << CHUNK_SHIFT, s_real).astype(F32)

    def open_rows(cnt_lo):
        return cnt_lo > kf

    def wide_rows(st):
        return jnp.logical_and(open_rows(st[2]), st[2] - st[3] > FINISH_MAX)

    def flag(pred):
        return jnp.max(jnp.where(pred, 1.0, 0.0))

    def bisect(st):
        lo, hi, cnt_lo, cnt_hi = st
        active = open_rows(cnt_lo)
        mid = 0.5 * jnp.maximum(lo, rmin) + 0.5 * jnp.minimum(hi, rmax)
        cm = count(lambda s, k0: s >= mid)
        up = jnp.logical_and(active, cm >= kf)
        dn = jnp.logical_and(active, cm < kf)
        return (jnp.where(up, mid, lo), jnp.where(dn, mid, hi),
                jnp.where(up, cm, cnt_lo), jnp.where(dn, cm, cnt_hi))

    st = (jnp.full((1, tq), -F32_MAX, F32), jnp.full((1, tq), F32_MAX, F32), n_adm,
          jnp.zeros((1, tq), F32))
    st = lax.fori_loop(0, SELECT_MIN_ITERS, lambda _, s: bisect(s), st)

    def narrow(c):
        nxt = bisect(c[2])
        return c[0] + 1, flag(wide_rows(nxt)), nxt

    _, _, st = lax.while_loop(
        lambda c: jnp.logical_and(c[0] < SELECT_MAX_ITERS, c[1] > 0.0), narrow,
        (jnp.int32(SELECT_MIN_ITERS), flag(wide_rows(st)), st))

    lo, hi, cnt_lo, cnt_hi = st
    cand, _ = below(hi)

    def walk(c):
        _, walking, lo, hi, cnt_lo, cnt_hi, cand = c
        nxt, cge = below(cand)
        settle = jnp.logical_and(walking > 0.0, cge >= kf)
        move = jnp.logical_and(walking > 0.0, cge < kf)
        walking = jnp.where(move, 1.0, 0.0)
        return (jnp.max(walking), walking, jnp.where(settle, cand, lo), jnp.where(move, cand, hi),
                jnp.where(settle, cge, cnt_lo), jnp.where(move, cge, cnt_hi), jnp.where(move, nxt, cand))

    walking = jnp.where(open_rows(cnt_lo), 1.0, 0.0)
    _, _, lo, hi, cnt_lo, cnt_hi, _ = lax.while_loop(
        lambda c: c[0] > 0.0, walk, (jnp.max(walking), walking, lo, hi, cnt_lo, cnt_hi, cand))
    n_open = flag(open_rows(cnt_lo))

    def clear_tile(j, carry):
        k0 = pl.multiple_of(j * kb, kb)
        mask_ref[0, 0, pl.ds(k0, kb), :] = jnp.zeros((kb, tq), jnp.int8)
        return carry

    lax.fori_loop(nkt, s_pad // kb, clear_tile, 0)

    @pl.when(n_open == 0.0)
    def _():
        def write_tile(j, carry):
            k0 = pl.multiple_of(j * kb, kb)
            keep = score_ref[pl.ds(k0, kb), :] >= lo
            mask_ref[0, 0, pl.ds(k0, kb), :] = jnp.where(keep, 1, 0).astype(jnp.int8)
            return carry

        lax.fori_loop(0, nkt, write_tile, 0)

    @pl.when(n_open > 0.0)
    def _():
        tied = open_rows(cnt_lo)
        need = jnp.where(tied, kf - cnt_hi, 0.0)
        key_iota = lax.broadcasted_iota(jnp.int32, (kb, tq), 0)
        strip_iota = lax.broadcasted_iota(jnp.int32, (COUNT_ROWS, tq), 0)
        shape = (COUNT_ROWS // (COUNT_PARTS * SUBLANES), COUNT_PARTS, SUBLANES, tq)

        def next_tied(prev):
            def body(j, acc):
                for c in range(kb // COUNT_ROWS):
                    k0 = pl.multiple_of(j * kb + c * COUNT_ROWS, COUNT_ROWS)
                    idx = k0 + strip_iota
                    hit = jnp.logical_and(score_ref[pl.ds(k0, COUNT_ROWS), :] == lo, idx > prev)
                    acc = jnp.minimum(acc, jnp.min(jnp.where(hit, idx, s_pad).reshape(shape), axis=0))
                return acc
            acc = lax.fori_loop(0, nkt, body, jnp.full(shape[1:], s_pad, jnp.int32))
            return jnp.min(jnp.min(acc, axis=0), axis=0, keepdims=True)

        def cut_by_walk(_):
            def step(c):
                _, left, last = c
                take = left > 0.0
                last = jnp.where(take, next_tied(last), last)
                left = jnp.where(take, left - 1.0, left)
                return jnp.max(left), left, last
            _, _, last = lax.while_loop(lambda c: c[0] > 0.0, step,
                                        (jnp.max(need), need, jnp.full((1, tq), -1, jnp.int32)))
            return last + 1

        def cut_by_bisection(_):
            def step(_, st):
                jlo, jhi = st
                jmid = (jlo + jhi) >> 1
                cm = count(lambda s, k0: jnp.logical_and(s == lo, k0 + strip_iota < jmid))
                ok = cm >= need
                return jnp.where(ok, jlo, jmid), jnp.where(ok, jmid, jhi)
            steps = int(math.ceil(math.log2(s_pad))) + 1
            return lax.fori_loop(0, steps, step, (jnp.zeros((1, tq), jnp.int32),
                                                  jnp.full((1, tq), s_pad, jnp.int32)))[1]

        jcut = lax.cond(jnp.max(need) <= TIE_WALK_MAX, cut_by_walk, cut_by_bisection, 0)
        jcut = jnp.where(tied, jcut, s_pad)

        def write_tile(j, carry):
            k0 = pl.multiple_of(j * kb, kb)
            s = score_ref[pl.ds(k0, kb), :]
            keep = jnp.logical_or(s >= hi, jnp.logical_and(s >= lo, k0 + key_iota < jcut))
            mask_ref[0, 0, pl.ds(k0, kb), :] = jnp.where(keep, 1, 0).astype(jnp.int8)
            return carry

        lax.fori_loop(0, nkt, write_tile, 0)


def _select(z3, ki3, tq, kb, s_real, q_pos0, topk):
    b, t, _ = z3.shape
    s_pad = ki3.shape[1]
    iq_w = IDX_HEADS * IDX_DIM
    kern = functools.partial(_select_kernel, tq=tq, kb=kb, s_pad=s_pad, s_real=s_real,
                             q_pos0=q_pos0, topk=topk)
    return pl.pallas_call(
        kern,
        grid=(b, t // tq),
        in_specs=[pl.BlockSpec((1, tq, iq_w), lambda i, j: (i, j, COL_IQ // iq_w)),
                  pl.BlockSpec((1, tq, LANES), lambda i, j: (i, j, COL_SMALL // LANES)),
                  pl.BlockSpec((1, s_pad, 3 * IDX_DIM), lambda i, j: (i, 0, 0))],
        out_specs=pl.BlockSpec((1, 1, s_pad, tq), lambda i, j: (i, j, 0, 0)),
        out_shape=jax.ShapeDtypeStruct((b, t // tq, s_pad, tq), jnp.int8),
        scratch_shapes=[pltpu.VMEM((s_pad, tq), F32),
                        pltpu.VMEM((IDX_HEADS // 2, 3 * IDX_DIM, 2 * tq), BF16)],
        compiler_params=_cparams(("parallel", "arbitrary")),
        name="index_select",
    )(z3, z3, ki3)


def _attn_kernel(q_of_ref, k_of_ref, qt_ref, k_ref, vt_ref, *refs, n_mask):
    mask_refs = refs[:n_mask]
    o_ref, acc_ref, m_ref, l_ref, s_ref, bias_ref = refs[n_mask:]
    step = pl.program_id(1)
    hd = D // A_HEADS
    tq = qt_ref.shape[2]
    tk = k_ref.shape[1]
    tq_mask = tq // n_mask
    pv_rows = math.gcd(tk, PV_ROWS)
    first = k_of_ref[step] == 0
    nxt = jnp.minimum(step + 1, pl.num_programs(1) - 1)
    last = jnp.logical_or(step == pl.num_programs(1) - 1, q_of_ref[nxt] != q_of_ref[step])

    @pl.when(first)
    def _():
        acc_ref[...] = jnp.zeros(acc_ref.shape, F32)
        m_ref[...] = jnp.full(m_ref.shape, MASK_NEG, F32)
        l_ref[...] = jnp.zeros(l_ref.shape, F32)

    for n, mask_ref in enumerate(mask_refs):
        bias_ref[:, n * tq_mask:(n + 1) * tq_mask] = jnp.where(
            mask_ref[0, 0].astype(jnp.int32) != 0, 0.0, MASK_NEG)
    def logits(h):
        hs = slice(h * hd, (h + 1) * hd)
        top = None
        for r in range(0, tk, QK_ROWS):
            s = jnp.dot(k_ref[0, r:r + QK_ROWS, hs], qt_ref[0, hs, :],
                        preferred_element_type=F32) + bias_ref[r:r + QK_ROWS, :]
            s_ref[h, r:r + QK_ROWS, :] = s
            part = jnp.max(s.reshape(QK_ROWS // SUBLANES, SUBLANES, tq), axis=0)
            top = part if top is None else jnp.maximum(top, part)
        return jnp.max(top, axis=0, keepdims=True)

    def weigh(h, tile_max):
        hs = slice(h * hd, (h + 1) * hd)
        m_prev = m_ref[h:h + 1, :]
        m_new = jnp.maximum(m_prev, tile_max)
        alpha = jnp.exp2(m_prev - m_new)
        psum = None
        pv = None
        for r in range(0, tk, pv_rows):
            p = jnp.exp2(s_ref[h, r:r + pv_rows, :] - m_new)
            part = jnp.sum(p.reshape(pv_rows // SUBLANES, SUBLANES, tq), axis=0)
            psum = part if psum is None else psum + part
            prod = jnp.dot(vt_ref[0, 0, hs, r:r + pv_rows], p.astype(BF16),
                           preferred_element_type=F32)
            pv = prod if pv is None else pv + prod
        l_ref[h:h + 1, :] = alpha * l_ref[h:h + 1, :] + jnp.sum(psum, axis=0, keepdims=True)
        acc_ref[hs, :] = alpha * acc_ref[hs, :] + pv
        m_ref[h:h + 1, :] = m_new

    tile_max = [logits(h) for h in range(A_HEADS)]

    @pl.when(k_of_ref[step] >= 0)
    def _():
        for h in range(A_HEADS):
            weigh(h, tile_max[h])

    @pl.when(last)
    def _():
        eye = _eye_bf16(tq)
        for h in range(A_HEADS):
            hs = slice(h * hd, (h + 1) * hd)
            out_t = (acc_ref[hs, :] / l_ref[h:h + 1, :]).astype(BF16)
            o_ref[0, :, hs] = _transpose_bf16(eye, out_t).astype(BF16)


def _attention(qt, kb, vt, mask_t, tq, tk, s_real, q_pos0):
    b, _, t = qt.shape
    n_mask = tq // mask_t.shape[3]
    steps = [(q, j) for q in range(t // tq) for j in range(_num_key_tiles_static(q, tq, tk, q_pos0, s_real))]
    q_of = jnp.array([q for q, _ in steps], jnp.int32)
    k_of = jnp.array([j for _, j in steps], jnp.int32)

    def mask_spec(n):
        return pl.BlockSpec((1, 1, tk, tq // n_mask),
                            lambda i, s, q_of, k_of: (i, q_of[s] * n_mask + n, k_of[s], 0))

    return pl.pallas_call(
        functools.partial(_attn_kernel, n_mask=n_mask),
        grid_spec=pltpu.PrefetchScalarGridSpec(
            num_scalar_prefetch=2,
            grid=(b, len(steps)),
            in_specs=[pl.BlockSpec((1, D, tq), lambda i, s, q_of, k_of: (i, 0, q_of[s])),
                      pl.BlockSpec((1, tk, D), lambda i, s, q_of, k_of: (i, k_of[s], 0)),
                      pl.BlockSpec((1, 1, D, tk), lambda i, s, q_of, k_of: (i, k_of[s], 0, 0))]
            + [mask_spec(n) for n in range(n_mask)],
            out_specs=pl.BlockSpec((1, tq, D), lambda i, s, q_of, k_of: (i, q_of[s], 0)),
            scratch_shapes=[pltpu.VMEM((D, tq), F32),
                            pltpu.VMEM((A_HEADS, tq), F32),
                            pltpu.VMEM((A_HEADS, tq), F32),
                            pltpu.VMEM((A_HEADS, tk, tq), F32),
                            pltpu.VMEM((tk, tq), F32)]),
        out_shape=jax.ShapeDtypeStruct((b, t, D), BF16),
        compiler_params=_cparams(("parallel", "arbitrary")),
        name="masked_attention",
    )(q_of, k_of, qt, kb, vt, *([mask_t] * n_mask))


def _merge_kernel(x_ref, ya_ref, yb_ref, ga_ref, gb_ref, wa_ref, wb_ref, wo_ref, o_ref):
    a = jnp.dot(ya_ref[...], wa_ref[...], preferred_element_type=F32)
    bb = jnp.dot(yb_ref[...], wb_ref[...], preferred_element_type=F32)
    mix = jax.nn.sigmoid(ga_ref[...]) * a + jax.nn.sigmoid(gb_ref[...]) * bb
    o_ref[...] = x_ref[...] + jnp.dot(mix.astype(BF16), wo_ref[...], preferred_element_type=F32)


def _merge(x, ya, yb, z, wa, wb, wo, tm):
    n = x.shape[0]
    row = pl.BlockSpec((tm, D), lambda i: (i, 0))
    wspec = pl.BlockSpec((D, D), lambda i: (0, 0))
    return pl.pallas_call(
        _merge_kernel,
        grid=(n // tm,),
        in_specs=[row, row, row,
                  pl.BlockSpec((tm, D), lambda i: (i, COL_GA // D)),
                  pl.BlockSpec((tm, D), lambda i: (i, COL_GB // D)),
                  wspec, wspec, wspec],
        out_specs=row,
        out_shape=jax.ShapeDtypeStruct((n, D), F32),
        compiler_params=_cparams(("parallel",)),
        name="merge_out_proj",
    )(x, ya, yb, z, z, wa, wb, wo)


def _ffn_kernel(x_ref, g_ref, wg_ref, wu_ref, wd_ref, o_ref, xn_ref, acc_ref):
    c = pl.program_id(1)

    @pl.when(c == 0)
    def _():
        x = x_ref[...]
        ms = jnp.mean(x * x, axis=-1, keepdims=True)
        xn_ref[...] = ((x * lax.rsqrt(ms + EPS)) * g_ref[...]).astype(BF16)
        acc_ref[...] = x

    xn = xn_ref[...]
    gate = jnp.dot(xn, wg_ref[...], preferred_element_type=F32)
    up = jnp.dot(xn, wu_ref[...], preferred_element_type=F32)
    act = (gate * jax.nn.sigmoid(gate)) * up
    acc_ref[...] += jnp.dot(act.astype(BF16), wd_ref[...], preferred_element_type=F32)

    @pl.when(c == pl.num_programs(1) - 1)
    def _():
        o_ref[...] = acc_ref[...]


def _ffn(x, g, w_in, w_out, tm, tf):
    n = x.shape[0]
    dff = w_out.shape[0]
    nf = dff // tf
    row = pl.BlockSpec((tm, D), lambda i, c: (i, 0))
    return pl.pallas_call(
        _ffn_kernel,
        grid=(n // tm, nf),
        in_specs=[row, pl.BlockSpec((1, D), lambda i, c: (0, 0)),
                  pl.BlockSpec((D, tf), lambda i, c: (0, c)),
                  pl.BlockSpec((D, tf), lambda i, c: (0, nf + c)),
                  pl.BlockSpec((tf, D), lambda i, c: (c, 0))],
        out_specs=row,
        out_shape=jax.ShapeDtypeStruct((n, D), F32),
        scratch_shapes=[pltpu.VMEM((tm, D), BF16), pltpu.VMEM((tm, D), F32)],
        compiler_params=_cparams(("parallel", "arbitrary")),
        name="swiglu_ffn",
    )(x, g, w_in, w_in, w_out)


def _split_hi_lo(a):
    hi = a.astype(BF16)
    lo = (a - hi.astype(F32)).astype(BF16)
    return hi, lo


def _layer(x, past, p, cfg):
    k_past, v_past, ki_past, c0, n0, m0, conv0 = past
    b, t, _ = x.shape
    n_tok = b * t
    past_len = k_past.shape[1]
    s_real = past_len + t
    topk = min(TOPK_MAX, s_real // 4)
    tq_sel, kb_sel, tq_att, tk_att = cfg["tq_sel"], cfg["kb_sel"], cfg["tq_att"], cfg["tk_att"]
    s_pad = -(-s_real // tk_att) * tk_att

    x2 = x.reshape(n_tok, D)
    z = _norm_matmul(x2, p["g_norm1"], p["w_in"], cfg["tm_proj"], cfg["tn_proj"])
    z3 = z.reshape(b, t, D_Z)

    y_a, c_new, n_new, m_new, conv_new = _mlstm(
        z3, p["gate_bias"], p["w_conv"], p["b_conv"], p["g_mnorm"], c0, n0,
        m0.reshape(b, 1, M_HEADS), conv0, cfg["mlstm_chunk"])

    k32, v32, kb, qt, vt = _qkv(z3, p["g_q"], p["g_k"], cfg["tm_qkv"])
    ik = z3[:, :, COL_SMALL:COL_SMALL + IDX_DIM]
    ki_all = ik if past_len == 0 else jnp.concatenate([ki_past.astype(F32), ik], axis=1)
    k_hi, k_lo = _split_hi_lo(ki_all)
    ki3 = jnp.concatenate([k_hi, k_lo, k_hi], axis=-1)
    if past_len == 0 and s_pad == s_real:
        assert cfg["tm_qkv"] == tk_att
        k_all, vt_all = kb, vt
    else:
        assert cfg["tm_qkv"] == t and tk_att == s_pad
        ki3 = jnp.pad(ki3, ((0, 0), (0, s_pad - s_real), (0, 0)))
        k_all = jnp.concatenate([k_past.reshape(b, past_len, D).astype(BF16), kb,
                                 jnp.zeros((b, s_pad - s_real, D), BF16)], axis=1)
        vt_all = jnp.concatenate([v_past.reshape(b, past_len, D).astype(BF16).transpose(0, 2, 1),
                                  vt[:, 0], jnp.zeros((b, D, s_pad - s_real), BF16)], axis=2)[:, None]
    mask_t = _select(z3, ki3, tq_sel, kb_sel, s_real, past_len, topk)
    y_b = _attention(qt, k_all, vt_all, mask_t, tq_att, tk_att, s_real, past_len)

    x1 = _merge(x2, y_a.reshape(n_tok, D), y_b.reshape(n_tok, D), z,
                p["w_a_out"], p["w_b_out"], p["w_o"], cfg["tm_rows"])
    y = _ffn(x1, p["g_norm2"], p["w_ffn_in"], p["w_ffn_out"], cfg["tm_ffn"], cfg["tf_ffn"])

    hd = D // A_HEADS
    return y.reshape(b, t, D), (k32.reshape(b, t, A_HEADS, hd), v32.reshape(b, t, A_HEADS, hd), ik,
                                c_new, n_new, m_new.reshape(b, M_HEADS), conv_new)


def _prep_params(g_norm1, w_in, b_if, w_conv, b_conv, g_mnorm, g_q, g_k, w_a_out, w_b_out, w_o,
                 g_norm2, w_ffn_in, w_ffn_out):
    o_mi = 4 * D
    o_aq = o_mi + 2 * M_HEADS
    o_iq = o_aq + 3 * D
    o_ik = o_iq + IDX_HEADS * IDX_DIM
    o_iw = o_ik + IDX_DIM
    o_ga = o_iw + IDX_HEADS
    w_perm = jnp.concatenate([
        w_in[:, 0:o_mi], w_in[:, o_aq:o_iq], w_in[:, o_ga:o_ga + 2 * D], w_in[:, o_iq:o_ik],
        w_in[:, o_ik:o_ga], w_in[:, o_mi:o_aq],
        jnp.zeros((D, D_Z - COL_SMALL - (SM_MF + M_HEADS)), w_in.dtype)], axis=1).astype(BF16)
    gate_bias = jnp.zeros((1, LANES), F32).at[0, SM_MI:SM_MI + 2 * M_HEADS].set(b_if.astype(F32))
    return {
        "g_norm1": g_norm1.reshape(1, D), "w_in": w_perm, "gate_bias": gate_bias,
        "w_conv": w_conv, "b_conv": b_conv.reshape(1, 2 * D), "g_mnorm": g_mnorm.reshape(1, D),
        "g_q": g_q.reshape(1, -1), "g_k": g_k.reshape(1, -1),
        "w_a_out": w_a_out.astype(BF16), "w_b_out": w_b_out.astype(BF16), "w_o": w_o.astype(BF16),
        "g_norm2": g_norm2.reshape(1, D), "w_ffn_in": w_ffn_in.astype(BF16),
        "w_ffn_out": w_ffn_out.astype(BF16),
    }


def _config(b, t, past_len):
    n_tok = b * t
    s_real = past_len + t
    tm = min(2048, n_tok)
    tq_sel = min(256, t)
    tq_att = min(512, t)
    if s_real % 512 == 0:
        kb_sel = tk_att = 512
    else:
        kb_sel = tk_att = -(-s_real // LANES) * LANES
    return {"tm_proj": tm, "tn_proj": 896, "mlstm_chunk": min(128, t), "tm_rows": min(512, n_tok),
            "tm_qkv": min(512, t), "tq_sel": tq_sel, "kb_sel": kb_sel, "tq_att": tq_att,
            "tk_att": tk_att, "tm_ffn": min(512, n_tok), "tf_ffn": 1408}


def kernel(x_prompt, x_sample, cache_k, cache_v, cache_kidx, state_C, state_n, state_m, state_conv,
           g_norm1, w_in, b_if, w_conv, b_conv, g_mnorm, g_q, g_k, w_a_out, w_b_out, w_o,
           g_norm2, w_ffn_in, w_ffn_out):
    depth = w_in.shape[0]
    bp = x_prompt.shape[0]
    hd_a = D // A_HEADS
    hd_m = D // M_HEADS
    yp, ys = x_prompt, x_sample
    new_p, new_s = [], []
    for l in range(depth):
        p = _prep_params(g_norm1[l], w_in[l], b_if[l], w_conv[l], b_conv[l], g_mnorm[l], g_q[l],
                         g_k[l], w_a_out[l], w_b_out[l], w_o[l], g_norm2[l], w_ffn_in[l],
                         w_ffn_out[l])
        empty = (jnp.zeros((bp, 0, A_HEADS, hd_a), F32), jnp.zeros((bp, 0, A_HEADS, hd_a), F32),
                 jnp.zeros((bp, 0, IDX_DIM), F32), jnp.zeros((bp, M_HEADS, hd_m, hd_m), F32),
                 jnp.zeros((bp, M_HEADS, hd_m), F32), jnp.zeros((bp, M_HEADS), F32),
                 jnp.zeros((bp, CONV_W - 1, 2 * D), F32))
        yp, sp = _layer(yp, empty, p, _config(bp, yp.shape[1], 0))
        ys, ss = _layer(ys, (cache_k[l], cache_v[l], cache_kidx[l], state_C[l], state_n[l],
                             state_m[l], state_conv[l]), p,
                        _config(ys.shape[0], ys.shape[1], cache_k.shape[2]))
        new_p.append(sp)
        new_s.append(ss)

    def stk(lst, i):
        return jnp.stack([s[i] for s in lst])

    return (yp, ys) + tuple(stk(new_p, i) for i in range(7)) + tuple(stk(new_s, i) for i in range(7))
```

```python
import functools
import math

import jax
import jax.numpy as jnp
from jax import lax
from jax.experimental import pallas as pl
from jax.experimental.pallas import tpu as pltpu

F32 = jnp.float32
BF16 = jnp.bfloat16
HIGHEST = lax.Precision.HIGHEST

EPS = 1e-6
CHUNK = 64
CHUNK_SHIFT = 6
M_HEADS = 4
A_HEADS = 8
IDX_HEADS = 8
IDX_DIM = 64
CONV_W = 4
TOPK_MAX = 256
LANES = 128
SUBLANES = 8
MASK_NEG = -1e30
F32_MAX = float(jnp.finfo(jnp.float32).max)
VMEM_LIMIT = 48 * 1024 * 1024
SELECT_MIN_ITERS = 12
SELECT_MAX_ITERS = 15
FINISH_MAX = 4
QK_ROWS = 128
PV_ROWS = 256
HEAD_LAG = 8
TIE_WALK_MAX = 8
COUNT_PARTS = 4
COUNT_ROWS = 128

D = 1024
COL_MQ, COL_MK, COL_MV, COL_MO = 0, 1024, 2048, 3072
COL_AQ, COL_AK, COL_AV = 4096, 5120, 6144
COL_GA, COL_GB = 7168, 8192
COL_IQ = 9216
COL_SMALL = 9728
D_Z = 9856
SM_IW, SM_MI, SM_MF = 64, 72, 76
Q_LOGIT_SCALE = math.log2(math.e) * (D // A_HEADS) ** -0.5


def _cparams(sem):
    return pltpu.CompilerParams(dimension_semantics=sem, vmem_limit_bytes=VMEM_LIMIT)


def _norm_matmul_kernel(x_ref, g_ref, w_ref, o_ref, xn_ref):
    @pl.when(pl.program_id(1) == 0)
    def _():
        x = x_ref[...]
        ms = jnp.mean(x * x, axis=-1, keepdims=True)
        xn_ref[...] = ((x * lax.rsqrt(ms + EPS)) * g_ref[...]).astype(BF16)

    o_ref[...] = jnp.dot(xn_ref[...], w_ref[...], preferred_element_type=F32)


def _norm_matmul(x, g, w, tm, tn):
    n, d = x.shape
    nout = w.shape[1]
    return pl.pallas_call(
        _norm_matmul_kernel,
        grid=(n // tm, nout // tn),
        in_specs=[pl.BlockSpec((tm, d), lambda i, j: (i, 0)),
                  pl.BlockSpec((1, d), lambda i, j: (0, 0)),
                  pl.BlockSpec((d, tn), lambda i, j: (0, j))],
        out_specs=pl.BlockSpec((tm, tn), lambda i, j: (i, j)),
        out_shape=jax.ShapeDtypeStruct((n, nout), F32),
        scratch_shapes=[pltpu.VMEM((tm, d), BF16)],
        compiler_params=_cparams(("parallel", "arbitrary")),
        name="norm_in_proj",
    )(x, g, w)


def _mlstm_kernel(mq_ref, mk_ref, mv_ref, mo_ref, sm_ref, bias_ref, wconv_ref, bconv_ref,
                  gm_ref, c0_ref, n0_ref, m0_ref, conv0_ref,
                  y_ref, c_ref, n_ref, m_ref, conv_ref, cbuf_ref, *, L):
    c = pl.program_id(1)
    hd = D // M_HEADS

    @pl.when(c == 0)
    def _():
        c_ref[...] = c0_ref[...]
        n_ref[...] = n0_ref[...]
        m_ref[...] = m0_ref[...]
        cbuf_ref[8 - (CONV_W - 1):8, :] = conv0_ref[0]

    cbuf_ref[8:8 + L, 0:D] = mq_ref[0]
    cbuf_ref[8:8 + L, D:2 * D] = mk_ref[0]
    wc = wconv_ref[...]
    qk = bconv_ref[...] + cbuf_ref[5:5 + L, :] * wc[0:1, :]
    for j in range(1, CONV_W):
        qk = qk + cbuf_ref[5 + j:5 + j + L, :] * wc[j:j + 1, :]
    tail = cbuf_ref[5 + L:8 + L, :]
    cbuf_ref[5:8, :] = tail
    conv_ref[0] = tail
    qk = qk * jax.nn.sigmoid(qk)

    g_all = sm_ref[0] + bias_ref[...]
    lf_all = jnp.minimum(g_all, 0.0) - jnp.log1p(jnp.exp(-jnp.abs(g_all)))
    lane = lax.broadcasted_iota(jnp.int32, (L, LANES), 1)
    gates = jnp.where(lane >= SM_MF, lf_all, g_all)
    r_i = lax.broadcasted_iota(jnp.int32, (L, L), 0)
    c_i = lax.broadcasted_iota(jnp.int32, (L, L), 1)
    tril = (c_i <= r_i).astype(F32)
    triu = (r_i <= c_i).astype(F32)
    b_col_all = jnp.dot(tril, lf_all, precision=HIGHEST, preferred_element_type=F32)
    sel = (lax.broadcasted_iota(jnp.int32, (8, LANES), 1)
           == lax.broadcasted_iota(jnp.int32, (8, LANES), 0) + SM_MI).astype(F32)
    rows = lax.dot_general(sel, gates, (((1,), (1,)), ((), ())), precision=HIGHEST,
                           preferred_element_type=F32)
    b_row_all = jnp.dot(rows, triu, precision=HIGHEST, preferred_element_type=F32)
    causal = c_i <= r_i

    mo = mo_ref[0]
    mv = mv_ref[0]
    for h in range(M_HEADS):
        hs = slice(h * hd, (h + 1) * hd)
        qh = qk[:, hs]
        kh = qk[:, D + h * hd:D + (h + 1) * hd] * (hd ** -0.5)
        vh = mv[:, hs]
        qb, kb, vb = qh.astype(BF16), kh.astype(BF16), vh.astype(BF16)
        b_col = b_col_all[:, SM_MF + h:SM_MF + h + 1]
        i_col = gates[:, SM_MI + h:SM_MI + h + 1]
        b_row = b_row_all[M_HEADS + h:M_HEADS + h + 1, :]
        i_row = rows[h:h + 1, :]
        m_prev = m_ref[0, :, h:h + 1]
        c_prev = c_ref[0, h]
        n_prev = n_ref[0, h:h + 1, :]

        dmat = jnp.where(causal, b_col - b_row + i_row, -jnp.inf)
        inter = b_col + m_prev
        m_t = jnp.maximum(inter, jnp.max(dmat, axis=-1, keepdims=True))
        w_intra = jnp.exp(dmat - m_t)
        w_inter = jnp.exp(inter - m_t)
        s = lax.dot_general(qb, kb, (((1,), (1,)), ((), ())), preferred_element_type=F32) * w_intra
        qc = lax.dot_general(qb, c_prev.astype(BF16), (((1,), (1,)), ((), ())),
                             preferred_element_type=F32)
        num = jnp.dot(s.astype(BF16), vb, preferred_element_type=F32) + w_inter * qc
        den = jnp.sum(s, axis=-1, keepdims=True) + w_inter * jnp.sum(qh * n_prev, axis=-1, keepdims=True)
        denom = jnp.maximum(jnp.abs(den), jnp.exp(-m_t))
        hh = num / denom

        m_new = m_t[L - 1:L, :]
        b_last = b_col[L - 1:L, :]
        g_col = jnp.exp(b_last - b_col + i_col - m_new)
        decay = jnp.exp(b_last + m_prev - m_new)
        gv = (g_col * vh).astype(BF16)
        c_ref[0, h] = decay * c_prev + lax.dot_general(
            gv, kb, (((0,), (0,)), ((), ())), preferred_element_type=F32)
        n_ref[0, h:h + 1, :] = decay * n_prev + jnp.sum(g_col * kh, axis=0, keepdims=True)
        m_ref[0, :, h:h + 1] = m_new

        hn = hh * lax.rsqrt(jnp.mean(hh * hh, axis=-1, keepdims=True) + EPS) * gm_ref[:, hs]
        y_ref[0, :, hs] = (hn * jax.nn.sigmoid(mo[:, hs])).astype(BF16)


def _mlstm(z3, bias_row, w_conv, b_conv, g_mnorm, c0, n0, m0, conv0, L):
    b, t, _ = z3.shape
    nc = t // L
    hd = D // M_HEADS

    def zspec(col, width):
        return pl.BlockSpec((1, L, width), lambda i, c: (i, c, col // width))

    def per_batch(shape):
        nd = len(shape)
        return pl.BlockSpec((1,) + shape, lambda i, c: (i,) + (0,) * nd)

    def const(shape):
        nd = len(shape)
        return pl.BlockSpec(shape, lambda i, c: (0,) * nd)

    return pl.pallas_call(
        functools.partial(_mlstm_kernel, L=L),
        grid=(b, nc),
        in_specs=[zspec(COL_MQ, D), zspec(COL_MK, D), zspec(COL_MV, D), zspec(COL_MO, D),
                  zspec(COL_SMALL, LANES), const((1, LANES)), const((CONV_W, 2 * D)),
                  const((1, 2 * D)), const((1, D)),
                  per_batch((M_HEADS, hd, hd)), per_batch((M_HEADS, hd)),
                  per_batch((1, M_HEADS)), per_batch((CONV_W - 1, 2 * D))],
        out_specs=[pl.BlockSpec((1, L, D), lambda i, c: (i, c, 0)),
                   per_batch((M_HEADS, hd, hd)), per_batch((M_HEADS, hd)),
                   per_batch((1, M_HEADS)), per_batch((CONV_W - 1, 2 * D))],
        out_shape=[jax.ShapeDtypeStruct((b, t, D), BF16),
                   jax.ShapeDtypeStruct((b, M_HEADS, hd, hd), F32),
                   jax.ShapeDtypeStruct((b, M_HEADS, hd), F32),
                   jax.ShapeDtypeStruct((b, 1, M_HEADS), F32),
                   jax.ShapeDtypeStruct((b, CONV_W - 1, 2 * D), F32)],
        scratch_shapes=[pltpu.VMEM((8 + L, 2 * D), F32)],
        compiler_params=_cparams(("parallel", "arbitrary")),
        name="mlstm",
    )(z3, z3, z3, z3, z3, bias_row, w_conv, b_conv, g_mnorm, c0, n0, m0, conv0)


def _eye_bf16(n):
    return (lax.broadcasted_iota(jnp.int32, (n, n), 0)
            == lax.broadcasted_iota(jnp.int32, (n, n), 1)).astype(BF16)


def _transpose_bf16(eye, a):
    return lax.dot_general(eye, a, (((1,), (1,)), ((), ())), preferred_element_type=F32)


def _qkv_kernel(aq_ref, ak_ref, av_ref, gq_ref, gk_ref, k32_ref, v32_ref, kb_ref, qt_ref, vt_ref):
    hd = D // A_HEADS
    eye = _eye_bf16(hd)
    for h in range(A_HEADS):
        hs = slice(h * hd, (h + 1) * hd)
        q = aq_ref[0, :, hs]
        k = ak_ref[0, :, hs]
        v = av_ref[0, :, hs]
        qn = (q * lax.rsqrt(jnp.mean(q * q, axis=-1, keepdims=True) + EPS)) * gq_ref[...]
        kn = (k * lax.rsqrt(jnp.mean(k * k, axis=-1, keepdims=True) + EPS)) * gk_ref[...]
        k32_ref[0, :, hs] = kn
        kb_ref[0, :, hs] = kn.astype(BF16)
        v32_ref[0, :, hs] = v
        qt_ref[0, hs, :] = _transpose_bf16(eye, (qn * Q_LOGIT_SCALE).astype(BF16)).astype(BF16)
        vt_ref[0, 0, hs, :] = _transpose_bf16(eye, v.astype(BF16)).astype(BF16)


def _qkv(z3, g_q, g_k, tm):
    b, t, _ = z3.shape
    hd = D // A_HEADS

    def zspec(col):
        return pl.BlockSpec((1, tm, D), lambda i, j: (i, j, col // D))

    row = pl.BlockSpec((1, tm, D), lambda i, j: (i, j, 0))
    gspec = pl.BlockSpec((1, hd), lambda i, j: (0, 0))
    return pl.pallas_call(
        _qkv_kernel,
        grid=(b, t // tm),
        in_specs=[zspec(COL_AQ), zspec(COL_AK), zspec(COL_AV), gspec, gspec],
        out_specs=[row, row, row, pl.BlockSpec((1, D, tm), lambda i, j: (i, 0, j)),
                   pl.BlockSpec((1, 1, D, tm), lambda i, j: (i, j, 0, 0))],
        out_shape=[jax.ShapeDtypeStruct((b, t, D), F32), jax.ShapeDtypeStruct((b, t, D), F32),
                   jax.ShapeDtypeStruct((b, t, D), BF16), jax.ShapeDtypeStruct((b, D, t), BF16),
                   jax.ShapeDtypeStruct((b, t // tm, D, tm), BF16)],
        compiler_params=_cparams(("parallel", "parallel")),
        name="qkv_norm",
    )(z3, z3, z3, g_q, g_k)


def _num_key_tiles(qi, tq, tile, q_pos0, s_real):
    last_chunk = (q_pos0 + (qi + 1) * tq - 1) // CHUNK
    kend = jnp.minimum((last_chunk + 1) * CHUNK, s_real)
    return (kend + tile - 1) // tile


def _num_key_tiles_static(qi, tq, tile, q_pos0, s_real):
    last_chunk = (q_pos0 + (qi + 1) * tq - 1) // CHUNK
    return -(-min((last_chunk + 1) * CHUNK, s_real) // tile)


def _select_kernel(iq_ref, sm_ref, ki_ref, mask_ref, score_ref, iqt_ref, *,
                   tq, kb, s_pad, s_real, q_pos0, topk):
    qi = pl.program_id(1)
    nkt = _num_key_tiles(qi, tq, kb, q_pos0, s_real)
    kf = float(topk)
    groups = kb // SUBLANES

    pick = (lax.broadcasted_iota(jnp.int32, (IDX_HEADS, LANES), 1)
            == lax.broadcasted_iota(jnp.int32, (IDX_HEADS, LANES), 0) + SM_IW).astype(F32)
    w = lax.dot_general(pick, sm_ref[0], (((1,), (1,)), ((), ())), precision=HIGHEST,
                        preferred_element_type=F32) * ((IDX_HEADS * IDX_DIM) ** -0.5)
    eye = _eye_bf16(IDX_DIM)
    for h in range(IDX_HEADS):
        a = iq_ref[0, :, h * IDX_DIM:(h + 1) * IDX_DIM]
        hi = a.astype(BF16)
        lo = (a - hi.astype(F32)).astype(BF16)
        hi_t = _transpose_bf16(eye, hi).astype(BF16)
        lo_t = _transpose_bf16(eye, lo).astype(BF16)
        cols = slice((h % 2) * tq, (h % 2 + 1) * tq)
        iqt_ref[h // 2, 0:IDX_DIM, cols] = hi_t
        iqt_ref[h // 2, IDX_DIM:2 * IDX_DIM, cols] = hi_t
        iqt_ref[h // 2, 2 * IDX_DIM:3 * IDX_DIM, cols] = lo_t

    q_pos = q_pos0 + qi * tq + lax.broadcasted_iota(jnp.int32, (1, tq), 1)
    q_chunk = q_pos >> CHUNK_SHIFT
    first_key = jnp.minimum(((q_pos0 + qi * tq) >> CHUNK_SHIFT) << CHUNK_SHIFT, s_real)
    n_full = first_key // kb

    def score_tile(j, carry, masked):
        rmin, rmax = carry
        k0 = pl.multiple_of(j * kb, kb)
        kt = ki_ref[0, pl.ds(k0, kb), :]
        acc = None
        for p in range(IDX_HEADS // 2):
            rel = jnp.dot(kt, iqt_ref[p], preferred_element_type=F32)
            part = (w[2 * p:2 * p + 1] * jnp.maximum(rel[:, 0:tq], 0.0)
                    + w[2 * p + 1:2 * p + 2] * jnp.maximum(rel[:, tq:2 * tq], 0.0))
            acc = part if acc is None else acc + part
        if masked:
            key = k0 + lax.broadcasted_iota(jnp.int32, (kb, tq), 0)
            adm = jnp.logical_and((key >> CHUNK_SHIFT) <= q_chunk, key < s_real)
            low = jnp.where(adm, acc, jnp.inf)
            acc = jnp.where(adm, acc, -jnp.inf)
        else:
            low = acc
        score_ref[pl.ds(k0, kb), :] = acc
        rmin = jnp.minimum(rmin, jnp.min(low.reshape(groups, SUBLANES, tq), axis=0))
        rmax = jnp.maximum(rmax, jnp.max(acc.reshape(groups, SUBLANES, tq), axis=0))
        return rmin, rmax

    stats = (jnp.full((SUBLANES, tq), jnp.inf, F32), jnp.full((SUBLANES, tq), -jnp.inf, F32))
    stats = lax.fori_loop(0, n_full, functools.partial(score_tile, masked=False), stats)
    rmin8, rmax8 = lax.fori_loop(n_full, nkt, functools.partial(score_tile, masked=True), stats)
    rmin = jnp.min(rmin8, axis=0, keepdims=True)
    rmax = jnp.max(rmax8, axis=0, keepdims=True)

    def count(pred):
        def body(j, acc):
            for c in range(kb // COUNT_ROWS):
                k0 = pl.multiple_of(j * kb + c * COUNT_ROWS, COUNT_ROWS)
                hit = jnp.where(pred(score_ref[pl.ds(k0, COUNT_ROWS), :], k0), 1.0, 0.0)
                acc = acc + jnp.sum(hit.reshape(COUNT_ROWS // (COUNT_PARTS * SUBLANES), COUNT_PARTS,
                                                SUBLANES, tq), axis=0)
            return acc
        acc = lax.fori_loop(0, nkt, body, jnp.zeros((COUNT_PARTS, SUBLANES, tq), F32))
        return jnp.sum(jnp.sum(acc, axis=0), axis=0, keepdims=True)

    def below(cur):
        shape = (COUNT_ROWS // (COUNT_PARTS * SUBLANES), COUNT_PARTS, SUBLANES, tq)

        def body(j, carry):
            top, num = carry
            for c in range(kb // COUNT_ROWS):
                k0 = pl.multiple_of(j * kb + c * COUNT_ROWS, COUNT_ROWS)
                s = score_ref[pl.ds(k0, COUNT_ROWS), :]
                under = s < cur
                top = jnp.maximum(top, jnp.max(jnp.where(under, s, -jnp.inf).reshape(shape), axis=0))
                num = num + jnp.sum(jnp.where(under, 0.0, 1.0).reshape(shape), axis=0)
            return top, num

        top, num = lax.fori_loop(0, nkt, body, (jnp.full(shape[1:], -jnp.inf, F32),
                                                jnp.zeros(shape[1:], F32)))
        return (jnp.max(jnp.max(top, axis=0), axis=0, keepdims=True),
                jnp.sum(jnp.sum(num, axis=0), axis=0, keepdims=True))

    n_adm = jnp.minimum((q_chunk + 1) << CHUNK_SHIFT, s_real).astype(F32)

    def open_rows(cnt_lo):
        return cnt_lo > kf

    def wide_rows(st):
        return jnp.logical_and(open_rows(st[2]), st[2] - st[3] > FINISH_MAX)

    def flag(pred):
        return jnp.max(jnp.where(pred, 1.0, 0.0))

    def bisect(st):
        lo, hi, cnt_lo, cnt_hi = st
        active = open_rows(cnt_lo)
        mid = 0.5 * jnp.maximum(lo, rmin) + 0.5 * jnp.minimum(hi, rmax)
        cm = count(lambda s, k0: s >= mid)
        up = jnp.logical_and(active, cm >= kf)
        dn = jnp.logical_and(active, cm < kf)
        return (jnp.where(up, mid, lo), jnp.where(dn, mid, hi),
                jnp.where(up, cm, cnt_lo), jnp.where(dn, cm, cnt_hi))

    st = (jnp.full((1, tq), -F32_MAX, F32), jnp.full((1, tq), F32_MAX, F32), n_adm,
          jnp.zeros((1, tq), F32))
    st = lax.fori_loop(0, SELECT_MIN_ITERS, lambda _, s: bisect(s), st)

    def narrow(c):
        nxt = bisect(c[2])
        return c[0] + 1, flag(wide_rows(nxt)), nxt

    _, _, st = lax.while_loop(
        lambda c: jnp.logical_and(c[0] < SELECT_MAX_ITERS, c[1] > 0.0), narrow,
        (jnp.int32(SELECT_MIN_ITERS), flag(wide_rows(st)), st))

    lo, hi, cnt_lo, cnt_hi = st
    cand, _ = below(hi)

    def walk(c):
        _, walking, lo, hi, cnt_lo, cnt_hi, cand = c
        nxt, cge = below(cand)
        settle = jnp.logical_and(walking > 0.0, cge >= kf)
        move = jnp.logical_and(walking > 0.0, cge < kf)
        walking = jnp.where(move, 1.0, 0.0)
        return (jnp.max(walking), walking, jnp.where(settle, cand, lo), jnp.where(move, cand, hi),
                jnp.where(settle, cge, cnt_lo), jnp.where(move, cge, cnt_hi), jnp.where(move, nxt, cand))

    walking = jnp.where(open_rows(cnt_lo), 1.0, 0.0)
    _, _, lo, hi, cnt_lo, cnt_hi, _ = lax.while_loop(
        lambda c: c[0] > 0.0, walk, (jnp.max(walking), walking, lo, hi, cnt_lo, cnt_hi, cand))
    n_open = flag(open_rows(cnt_lo))

    def clear_tile(j, carry):
        k0 = pl.multiple_of(j * kb, kb)
        mask_ref[0, 0, pl.ds(k0, kb), :] = jnp.zeros((kb, tq), jnp.int8)
        return carry

    lax.fori_loop(nkt, s_pad // kb, clear_tile, 0)

    @pl.when(n_open == 0.0)
    def _():
        def write_tile(j, carry):
            k0 = pl.multiple_of(j * kb, kb)
            keep = score_ref[pl.ds(k0, kb), :] >= lo
            mask_ref[0, 0, pl.ds(k0, kb), :] = jnp.where(keep, 1, 0).astype(jnp.int8)
            return carry

        lax.fori_loop(0, nkt, write_tile, 0)

    @pl.when(n_open > 0.0)
    def _():
        tied = open_rows(cnt_lo)
        need = jnp.where(tied, kf - cnt_hi, 0.0)
        key_iota = lax.broadcasted_iota(jnp.int32, (kb, tq), 0)
        strip_iota = lax.broadcasted_iota(jnp.int32, (COUNT_ROWS, tq), 0)
        shape = (COUNT_ROWS // (COUNT_PARTS * SUBLANES), COUNT_PARTS, SUBLANES, tq)

        def next_tied(prev):
            def body(j, acc):
                for c in range(kb // COUNT_ROWS):
                    k0 = pl.multiple_of(j * kb + c * COUNT_ROWS, COUNT_ROWS)
                    idx = k0 + strip_iota
                    hit = jnp.logical_and(score_ref[pl.ds(k0, COUNT_ROWS), :] == lo, idx > prev)
                    acc = jnp.minimum(acc, jnp.min(jnp.where(hit, idx, s_pad).reshape(shape), axis=0))
                return acc
            acc = lax.fori_loop(0, nkt, body, jnp.full(shape[1:], s_pad, jnp.int32))
            return jnp.min(jnp.min(acc, axis=0), axis=0, keepdims=True)

        def cut_by_walk(_):
            def step(c):
                _, left, last = c
                take = left > 0.0
                last = jnp.where(take, next_tied(last), last)
                left = jnp.where(take, left - 1.0, left)
                return jnp.max(left), left, last
            _, _, last = lax.while_loop(lambda c: c[0] > 0.0, step,
                                        (jnp.max(need), need, jnp.full((1, tq), -1, jnp.int32)))
            return last + 1

        def cut_by_bisection(_):
            def step(_, st):
                jlo, jhi = st
                jmid = (jlo + jhi) >> 1
                cm = count(lambda s, k0: jnp.logical_and(s == lo, k0 + strip_iota < jmid))
                ok = cm >= need
                return jnp.where(ok, jlo, jmid), jnp.where(ok, jmid, jhi)
            steps = int(math.ceil(math.log2(s_pad))) + 1
            return lax.fori_loop(0, steps, step, (jnp.zeros((1, tq), jnp.int32),
                                                  jnp.full((1, tq), s_pad, jnp.int32)))[1]

        jcut = lax.cond(jnp.max(need) <= TIE_WALK_MAX, cut_by_walk, cut_by_bisection, 0)
        jcut = jnp.where(tied, jcut, s_pad)

        def write_tile(j, carry):
            k0 = pl.multiple_of(j * kb, kb)
            s = score_ref[pl.ds(k0, kb), :]
            keep = jnp.logical_or(s >= hi, jnp.logical_and(s >= lo, k0 + key_iota < jcut))
            mask_ref[0, 0, pl.ds(k0, kb), :] = jnp.where(keep, 1, 0).astype(jnp.int8)
            return carry

        lax.fori_loop(0, nkt, write_tile, 0)


def _select(z3, ki3, tq, kb, s_real, q_pos0, topk):
    b, t, _ = z3.shape
    s_pad = ki3.shape[1]
    iq_w = IDX_HEADS * IDX_DIM
    kern = functools.partial(_select_kernel, tq=tq, kb=kb, s_pad=s_pad, s_real=s_real,
                             q_pos0=q_pos0, topk=topk)
    return pl.pallas_call(
        kern,
        grid=(b, t // tq),
        in_specs=[pl.BlockSpec((1, tq, iq_w), lambda i, j: (i, j, COL_IQ // iq_w)),
                  pl.BlockSpec((1, tq, LANES), lambda i, j: (i, j, COL_SMALL // LANES)),
                  pl.BlockSpec((1, s_pad, 3 * IDX_DIM), lambda i, j: (i, 0, 0))],
        out_specs=pl.BlockSpec((1, 1, s_pad, tq), lambda i, j: (i, j, 0, 0)),
        out_shape=jax.ShapeDtypeStruct((b, t // tq, s_pad, tq), jnp.int8),
        scratch_shapes=[pltpu.VMEM((s_pad, tq), F32),
                        pltpu.VMEM((IDX_HEADS // 2, 3 * IDX_DIM, 2 * tq), BF16)],
        compiler_params=_cparams(("parallel", "arbitrary")),
        name="index_select",
    )(z3, z3, ki3)


def _attn_kernel(q_of_ref, k_of_ref, qt_ref, k_ref, vt_ref, *refs, n_mask):
    mask_refs = refs[:n_mask]
    o_ref, acc_ref, m_ref, l_ref, s_ref, bias_ref = refs[n_mask:]
    step = pl.program_id(1)
    hd = D // A_HEADS
    tq = qt_ref.shape[2]
    tk = k_ref.shape[1]
    tq_mask = tq // n_mask
    pv_rows = math.gcd(tk, PV_ROWS)
    first = k_of_ref[step] == 0
    nxt = jnp.minimum(step + 1, pl.num_programs(1) - 1)
    last = jnp.logical_or(step == pl.num_programs(1) - 1, q_of_ref[nxt] != q_of_ref[step])

    @pl.when(first)
    def _():
        acc_ref[...] = jnp.zeros(acc_ref.shape, F32)
        m_ref[...] = jnp.full(m_ref.shape, MASK_NEG, F32)
        l_ref[...] = jnp.zeros(l_ref.shape, F32)

    for n, mask_ref in enumerate(mask_refs):
        bias_ref[:, n * tq_mask:(n + 1) * tq_mask] = jnp.where(
            mask_ref[0, 0].astype(jnp.int32) != 0, 0.0, MASK_NEG)
    def logits(h):
        hs = slice(h * hd, (h + 1) * hd)
        top = None
        for r in range(0, tk, QK_ROWS):
            s = jnp.dot(k_ref[0, r:r + QK_ROWS, hs], qt_ref[0, hs, :],
                        preferred_element_type=F32) + bias_ref[r:r + QK_ROWS, :]
            s_ref[h, r:r + QK_ROWS, :] = s
            part = jnp.max(s.reshape(QK_ROWS // SUBLANES, SUBLANES, tq), axis=0)
            top = part if top is None else jnp.maximum(top, part)
        return jnp.max(top, axis=0, keepdims=True)

    def weigh(h, tile_max):
        hs = slice(h * hd, (h + 1) * hd)
        m_prev = m_ref[h:h + 1, :]
        m_new = jnp.maximum(m_prev, tile_max)
        alpha = jnp.exp2(m_prev - m_new)
        psum = None
        pv = None
        for r in range(0, tk, pv_rows):
            p = jnp.exp2(s_ref[h, r:r + pv_rows, :] - m_new)
            part = jnp.sum(p.reshape(pv_rows // SUBLANES, SUBLANES, tq), axis=0)
            psum = part if psum is None else psum + part
            prod = jnp.dot(vt_ref[0, 0, hs, r:r + pv_rows], p.astype(BF16),
                           preferred_element_type=F32)
            pv = prod if pv is None else pv + prod
        l_ref[h:h + 1, :] = alpha * l_ref[h:h + 1, :] + jnp.sum(psum, axis=0, keepdims=True)
        acc_ref[hs, :] = alpha * acc_ref[hs, :] + pv
        m_ref[h:h + 1, :] = m_new

    tile_max = [logits(h) for h in range(A_HEADS)]

    @pl.when(k_of_ref[step] >= 0)
    def _():
        for h in range(A_HEADS):
            weigh(h, tile_max[h])

    @pl.when(last)
    def _():
        eye = _eye_bf16(tq)
        for h in range(A_HEADS):
            hs = slice(h * hd, (h + 1) * hd)
            out_t = (acc_ref[hs, :] / l_ref[h:h + 1, :]).astype(BF16)
            o_ref[0, :, hs] = _transpose_bf16(eye, out_t).astype(BF16)


def _attention(qt, kb, vt, mask_t, tq, tk, s_real, q_pos0):
    b, _, t = qt.shape
    n_mask = tq // mask_t.shape[3]
    steps = [(q, j) for q in range(t // tq) for j in range(_num_key_tiles_static(q, tq, tk, q_pos0, s_real))]
    q_of = jnp.array([q for q, _ in steps], jnp.int32)
    k_of = jnp.array([j for _, j in steps], jnp.int32)

    def mask_spec(n):
        return pl.BlockSpec((1, 1, tk, tq // n_mask),
                            lambda i, s, q_of, k_of: (i, q_of[s] * n_mask + n, k_of[s], 0))

    return pl.pallas_call(
        functools.partial(_attn_kernel, n_mask=n_mask),
        grid_spec=pltpu.PrefetchScalarGridSpec(
            num_scalar_prefetch=2,
            grid=(b, len(steps)),
            in_specs=[pl.BlockSpec((1, D, tq), lambda i, s, q_of, k_of: (i, 0, q_of[s])),
                      pl.BlockSpec((1, tk, D), lambda i, s, q_of, k_of: (i, k_of[s], 0)),
                      pl.BlockSpec((1, 1, D, tk), lambda i, s, q_of, k_of: (i, k_of[s], 0, 0))]
            + [mask_spec(n) for n in range(n_mask)],
            out_specs=pl.BlockSpec((1, tq, D), lambda i, s, q_of, k_of: (i, q_of[s], 0)),
            scratch_shapes=[pltpu.VMEM((D, tq), F32),
                            pltpu.VMEM((A_HEADS, tq), F32),
                            pltpu.VMEM((A_HEADS, tq), F32),
                            pltpu.VMEM((A_HEADS, tk, tq), F32),
                            pltpu.VMEM((tk, tq), F32)]),
        out_shape=jax.ShapeDtypeStruct((b, t, D), BF16),
        compiler_params=_cparams(("parallel", "arbitrary")),
        name="masked_attention",
    )(q_of, k_of, qt, kb, vt, *([mask_t] * n_mask))


def _attn_past_kernel(qt_ref, kn_ref, vtn_ref, kp_ref, vp_ref, mask_ref, o_ref):
    hd = D // A_HEADS
    t = qt_ref.shape[2]
    past = kp_ref.shape[1]
    bias = jnp.where(mask_ref[0, 0].astype(jnp.int32) != 0, 0.0, MASK_NEG)
    bias_p, bias_n = bias[0:past], bias[past:past + t]
    eye_hd, eye_t = _eye_bf16(hd), _eye_bf16(t)
    for h in range(A_HEADS):
        hs = slice(h * hd, (h + 1) * hd)
        q = qt_ref[0, hs, :]
        s_p = jnp.dot(kp_ref[0, :, h, :].astype(BF16), q, preferred_element_type=F32) + bias_p
        s_n = jnp.dot(kn_ref[0, :, hs], q, preferred_element_type=F32) + bias_n
        m = jnp.maximum(jnp.max(s_p, axis=0, keepdims=True), jnp.max(s_n, axis=0, keepdims=True))
        p_p = jnp.exp2(s_p - m)
        p_n = jnp.exp2(s_n - m)
        l = jnp.sum(p_p, axis=0, keepdims=True) + jnp.sum(p_n, axis=0, keepdims=True)
        vt_p = _transpose_bf16(eye_hd, vp_ref[0, :, h, :].astype(BF16)).astype(BF16)
        out_t = (jnp.dot(vt_p, p_p.astype(BF16), preferred_element_type=F32)
                 + jnp.dot(vtn_ref[0, 0, hs, :], p_n.astype(BF16), preferred_element_type=F32))
        o_ref[0, :, hs] = _transpose_bf16(eye_t, (out_t / l).astype(BF16)).astype(BF16)


def _attention_with_past(qt, kb, vt, k_past, v_past, mask_t):
    b, _, t = qt.shape
    past, heads, hd = k_past.shape[1:]
    s_pad = mask_t.shape[2]
    cache = pl.BlockSpec((1, past, heads, hd), lambda i: (i, 0, 0, 0))
    return pl.pallas_call(
        _attn_past_kernel,
        grid=(b,),
        in_specs=[pl.BlockSpec((1, D, t), lambda i: (i, 0, 0)),
                  pl.BlockSpec((1, t, D), lambda i: (i, 0, 0)),
                  pl.BlockSpec((1, 1, D, t), lambda i: (i, 0, 0, 0)),
                  cache, cache,
                  pl.BlockSpec((1, 1, s_pad, t), lambda i: (i, 0, 0, 0))],
        out_specs=pl.BlockSpec((1, t, D), lambda i: (i, 0, 0)),
        out_shape=jax.ShapeDtypeStruct((b, t, D), BF16),
        compiler_params=_cparams(("parallel",)),
        name="masked_attention_past",
    )(qt, kb, vt, k_past, v_past, mask_t)


def _merge_kernel(x_ref, ya_ref, yb_ref, ga_ref, gb_ref, wa_ref, wb_ref, wo_ref, o_ref):
    a = jnp.dot(ya_ref[...], wa_ref[...], preferred_element_type=F32)
    bb = jnp.dot(yb_ref[...], wb_ref[...], preferred_element_type=F32)
    mix = jax.nn.sigmoid(ga_ref[...]) * a + jax.nn.sigmoid(gb_ref[...]) * bb
    o_ref[...] = x_ref[...] + jnp.dot(mix.astype(BF16), wo_ref[...], preferred_element_type=F32)


def _merge(x, ya, yb, z, wa, wb, wo, tm):
    n = x.shape[0]
    row = pl.BlockSpec((tm, D), lambda i: (i, 0))
    wspec = pl.BlockSpec((D, D), lambda i: (0, 0))
    return pl.pallas_call(
        _merge_kernel,
        grid=(n // tm,),
        in_specs=[row, row, row,
                  pl.BlockSpec((tm, D), lambda i: (i, COL_GA // D)),
                  pl.BlockSpec((tm, D), lambda i: (i, COL_GB // D)),
                  wspec, wspec, wspec],
        out_specs=row,
        out_shape=jax.ShapeDtypeStruct((n, D), F32),
        compiler_params=_cparams(("parallel",)),
        name="merge_out_proj",
    )(x, ya, yb, z, z, wa, wb, wo)


def _ffn_kernel(x_ref, g_ref, wg_ref, wu_ref, wd_ref, o_ref, xn_ref, acc_ref):
    c = pl.program_id(1)

    @pl.when(c == 0)
    def _():
        x = x_ref[...]
        ms = jnp.mean(x * x, axis=-1, keepdims=True)
        xn_ref[...] = ((x * lax.rsqrt(ms + EPS)) * g_ref[...]).astype(BF16)
        acc_ref[...] = x

    xn = xn_ref[...]
    gate = jnp.dot(xn, wg_ref[...], preferred_element_type=F32)
    up = jnp.dot(xn, wu_ref[...], preferred_element_type=F32)
    act = (gate * jax.nn.sigmoid(gate)) * up
    acc_ref[...] += jnp.dot(act.astype(BF16), wd_ref[...], preferred_element_type=F32)

    @pl.when(c == pl.num_programs(1) - 1)
    def _():
        o_ref[...] = acc_ref[...]


def _ffn(x, g, w_in, w_out, tm, tf):
    n = x.shape[0]
    dff = w_out.shape[0]
    nf = dff // tf
    row = pl.BlockSpec((tm, D), lambda i, c: (i, 0))
    return pl.pallas_call(
        _ffn_kernel,
        grid=(n // tm, nf),
        in_specs=[row, pl.BlockSpec((1, D), lambda i, c: (0, 0)),
                  pl.BlockSpec((D, tf), lambda i, c: (0, c)),
                  pl.BlockSpec((D, tf), lambda i, c: (0, nf + c)),
                  pl.BlockSpec((tf, D), lambda i, c: (c, 0))],
        out_specs=row,
        out_shape=jax.ShapeDtypeStruct((n, D), F32),
        scratch_shapes=[pltpu.VMEM((tm, D), BF16), pltpu.VMEM((tm, D), F32)],
        compiler_params=_cparams(("parallel", "arbitrary")),
        name="swiglu_ffn",
    )(x, g, w_in, w_in, w_out)


def _split_hi_lo(a):
    hi = a.astype(BF16)
    lo = (a - hi.astype(F32)).astype(BF16)
    return hi, lo


def _layer(x, past, p, cfg):
    k_past, v_past, ki_past, c0, n0, m0, conv0 = past
    b, t, _ = x.shape
    n_tok = b * t
    past_len = k_past.shape[1]
    s_real = past_len + t
    topk = min(TOPK_MAX, s_real // 4)
    tq_sel, kb_sel, tq_att, tk_att = cfg["tq_sel"], cfg["kb_sel"], cfg["tq_att"], cfg["tk_att"]
    s_pad = -(-s_real // tk_att) * tk_att

    x2 = x.reshape(n_tok, D)
    z = _norm_matmul(x2, p["g_norm1"], p["w_in"], cfg["tm_proj"], cfg["tn_proj"])
    z3 = z.reshape(b, t, D_Z)

    y_a, c_new, n_new, m_new, conv_new = _mlstm(
        z3, p["gate_bias"], p["w_conv"], p["b_conv"], p["g_mnorm"], c0, n0,
        m0.reshape(b, 1, M_HEADS), conv0, cfg["mlstm_chunk"])

    k32, v32, kb, qt, vt = _qkv(z3, p["g_q"], p["g_k"], cfg["tm_qkv"])
    ik = z3[:, :, COL_SMALL:COL_SMALL + IDX_DIM]
    ki_all = ik if past_len == 0 else jnp.concatenate([ki_past.astype(F32), ik], axis=1)
    k_hi, k_lo = _split_hi_lo(ki_all)
    ki3 = jnp.concatenate([k_hi, k_lo, k_hi], axis=-1)
    ki3 = jnp.pad(ki3, ((0, 0), (0, s_pad - s_real), (0, 0)))
    mask_t = _select(z3, ki3, tq_sel, kb_sel, s_real, past_len, topk)
    if past_len == 0:
        assert cfg["tm_qkv"] == tk_att and s_pad == s_real
        y_b = _attention(qt, kb, vt, mask_t, tq_att, tk_att, s_real, past_len)
    else:
        assert cfg["tm_qkv"] == t == tq_sel
        y_b = _attention_with_past(qt, kb, vt, k_past, v_past, mask_t)

    x1 = _merge(x2, y_a.reshape(n_tok, D), y_b.reshape(n_tok, D), z,
                p["w_a_out"], p["w_b_out"], p["w_o"], cfg["tm_rows"])
    y = _ffn(x1, p["g_norm2"], p["w_ffn_in"], p["w_ffn_out"], cfg["tm_ffn"], cfg["tf_ffn"])

    hd = D // A_HEADS
    return y.reshape(b, t, D), (k32.reshape(b, t, A_HEADS, hd), v32.reshape(b, t, A_HEADS, hd), ik,
                                c_new, n_new, m_new.reshape(b, M_HEADS), conv_new)


def _prep_params(g_norm1, w_in, b_if, w_conv, b_conv, g_mnorm, g_q, g_k, w_a_out, w_b_out, w_o,
                 g_norm2, w_ffn_in, w_ffn_out):
    o_mi = 4 * D
    o_aq = o_mi + 2 * M_HEADS
    o_iq = o_aq + 3 * D
    o_ik = o_iq + IDX_HEADS * IDX_DIM
    o_iw = o_ik + IDX_DIM
    o_ga = o_iw + IDX_HEADS
    w_perm = jnp.concatenate([
        w_in[:, 0:o_mi], w_in[:, o_aq:o_iq], w_in[:, o_ga:o_ga + 2 * D], w_in[:, o_iq:o_ik],
        w_in[:, o_ik:o_ga], w_in[:, o_mi:o_aq],
        jnp.zeros((D, D_Z - COL_SMALL - (SM_MF + M_HEADS)), w_in.dtype)], axis=1).astype(BF16)
    gate_bias = jnp.zeros((1, LANES), F32).at[0, SM_MI:SM_MI + 2 * M_HEADS].set(b_if.astype(F32))
    return {
        "g_norm1": g_norm1.reshape(1, D), "w_in": w_perm, "gate_bias": gate_bias,
        "w_conv": w_conv, "b_conv": b_conv.reshape(1, 2 * D), "g_mnorm": g_mnorm.reshape(1, D),
        "g_q": g_q.reshape(1, -1), "g_k": g_k.reshape(1, -1),
        "w_a_out": w_a_out.astype(BF16), "w_b_out": w_b_out.astype(BF16), "w_o": w_o.astype(BF16),
        "g_norm2": g_norm2.reshape(1, D), "w_ffn_in": w_ffn_in.astype(BF16),
        "w_ffn_out": w_ffn_out.astype(BF16),
    }


def _config(b, t, past_len):
    n_tok = b * t
    s_real = past_len + t
    tm = min(2048, n_tok)
    tq_sel = min(256, t)
    tq_att = min(512, t)
    if s_real % 512 == 0:
        kb_sel = tk_att = 512
    else:
        kb_sel = tk_att = -(-s_real // LANES) * LANES
    return {"tm_proj": tm, "tn_proj": 896, "mlstm_chunk": min(128, t), "tm_rows": min(512, n_tok),
            "tm_qkv": min(512, t), "tq_sel": tq_sel, "kb_sel": kb_sel, "tq_att": tq_att,
            "tk_att": tk_att, "tm_ffn": min(512, n_tok), "tf_ffn": 1408}


def kernel(x_prompt, x_sample, cache_k, cache_v, cache_kidx, state_C, state_n, state_m, state_conv,
           g_norm1, w_in, b_if, w_conv, b_conv, g_mnorm, g_q, g_k, w_a_out, w_b_out, w_o,
           g_norm2, w_ffn_in, w_ffn_out):
    depth = w_in.shape[0]
    bp = x_prompt.shape[0]
    hd_a = D // A_HEADS
    hd_m = D // M_HEADS
    yp, ys = x_prompt, x_sample
    new_p, new_s = [], []
    for l in range(depth):
        p = _prep_params(g_norm1[l], w_in[l], b_if[l], w_conv[l], b_conv[l], g_mnorm[l], g_q[l],
                         g_k[l], w_a_out[l], w_b_out[l], w_o[l], g_norm2[l], w_ffn_in[l],
                         w_ffn_out[l])
        empty = (jnp.zeros((bp, 0, A_HEADS, hd_a), F32), jnp.zeros((bp, 0, A_HEADS, hd_a), F32),
                 jnp.zeros((bp, 0, IDX_DIM), F32), jnp.zeros((bp, M_HEADS, hd_m, hd_m), F32),
                 jnp.zeros((bp, M_HEADS, hd_m), F32), jnp.zeros((bp, M_HEADS), F32),
                 jnp.zeros((bp, CONV_W - 1, 2 * D), F32))
        yp, sp = _layer(yp, empty, p, _config(bp, yp.shape[1], 0))
        ys, ss = _layer(ys, (cache_k[l], cache_v[l], cache_kidx[l], state_C[l], state_n[l],
                             state_m[l], state_conv[l]), p,
                        _config(ys.shape[0], ys.shape[1], cache_k.shape[2]))
        new_p.append(sp)
        new_s.append(ss)

    def stk(lst, i):
        return jnp.stack([s[i] for s in lst])

    return (yp, ys) + tuple(stk(new_p, i) for i in range(7)) + tuple(stk(new_s, i) for i in range(7))
```

```python
import functools
import math

import jax
import jax.numpy as jnp
from jax import lax
from jax.experimental import pallas as pl
from jax.experimental.pallas import tpu as pltpu

F32 = jnp.float32
BF16 = jnp.bfloat16
HIGHEST = lax.Precision.HIGHEST

EPS = 1e-6
CHUNK = 64
CHUNK_SHIFT = 6
M_HEADS = 4
A_HEADS = 8
IDX_HEADS = 8
IDX_DIM = 64
CONV_W = 4
TOPK_MAX = 256
LANES = 128
SUBLANES = 8
MASK_NEG = -1e30
F32_MAX = float(jnp.finfo(jnp.float32).max)
VMEM_LIMIT = 48 * 1024 * 1024
SELECT_MIN_ITERS = 12
SELECT_MAX_ITERS = 15
FINISH_MAX = 4
QK_ROWS = 128
PV_ROWS = 256
HEAD_LAG = 8
TIE_WALK_MAX = 8
COUNT_PARTS = 4
COUNT_ROWS = 128

D = 1024
COL_MQ, COL_MK, COL_MV, COL_MO = 0, 1024, 2048, 3072
COL_GA, COL_GB = 4096, 5120
COL_IQ = 6144
COL_SMALL = 6656
D_Z = 7168
SM_IW, SM_MI, SM_MF = 64, 72, 76
Q_LOGIT_SCALE = math.log2(math.e) * (D // A_HEADS) ** -0.5


def _cparams(sem):
    return pltpu.CompilerParams(dimension_semantics=sem, vmem_limit_bytes=VMEM_LIMIT)


def _norm_matmul_kernel(x_ref, g_ref, w_ref, o_ref, xn_ref):
    @pl.when(pl.program_id(1) == 0)
    def _():
        x = x_ref[...]
        ms = jnp.mean(x * x, axis=-1, keepdims=True)
        xn_ref[...] = ((x * lax.rsqrt(ms + EPS)) * g_ref[...]).astype(BF16)

    o_ref[...] = jnp.dot(xn_ref[...], w_ref[...], preferred_element_type=F32)


def _norm_matmul(x, g, w, tm, tn):
    n, d = x.shape
    nout = w.shape[1]
    return pl.pallas_call(
        _norm_matmul_kernel,
        grid=(n // tm, nout // tn),
        in_specs=[pl.BlockSpec((tm, d), lambda i, j: (i, 0)),
                  pl.BlockSpec((1, d), lambda i, j: (0, 0)),
                  pl.BlockSpec((d, tn), lambda i, j: (0, j))],
        out_specs=pl.BlockSpec((tm, tn), lambda i, j: (i, j)),
        out_shape=jax.ShapeDtypeStruct((n, nout), F32),
        scratch_shapes=[pltpu.VMEM((tm, d), BF16)],
        compiler_params=_cparams(("parallel", "arbitrary")),
        name="norm_in_proj",
    )(x, g, w)


def _mlstm_kernel(mq_ref, mk_ref, mv_ref, mo_ref, sm_ref, bias_ref, wconv_ref, bconv_ref,
                  gm_ref, c0_ref, n0_ref, m0_ref, conv0_ref,
                  y_ref, c_ref, n_ref, m_ref, conv_ref, cbuf_ref, *, L):
    c = pl.program_id(1)
    hd = D // M_HEADS

    @pl.when(c == 0)
    def _():
        c_ref[...] = c0_ref[...]
        n_ref[...] = n0_ref[...]
        m_ref[...] = m0_ref[...]
        cbuf_ref[8 - (CONV_W - 1):8, :] = conv0_ref[0]

    cbuf_ref[8:8 + L, 0:D] = mq_ref[0]
    cbuf_ref[8:8 + L, D:2 * D] = mk_ref[0]
    wc = wconv_ref[...]
    qk = bconv_ref[...] + cbuf_ref[5:5 + L, :] * wc[0:1, :]
    for j in range(1, CONV_W):
        qk = qk + cbuf_ref[5 + j:5 + j + L, :] * wc[j:j + 1, :]
    tail = cbuf_ref[5 + L:8 + L, :]
    cbuf_ref[5:8, :] = tail
    conv_ref[0] = tail
    qk = qk * jax.nn.sigmoid(qk)

    g_all = sm_ref[0] + bias_ref[...]
    lf_all = jnp.minimum(g_all, 0.0) - jnp.log1p(jnp.exp(-jnp.abs(g_all)))
    lane = lax.broadcasted_iota(jnp.int32, (L, LANES), 1)
    gates = jnp.where(lane >= SM_MF, lf_all, g_all)
    r_i = lax.broadcasted_iota(jnp.int32, (L, L), 0)
    c_i = lax.broadcasted_iota(jnp.int32, (L, L), 1)
    tril = (c_i <= r_i).astype(F32)
    triu = (r_i <= c_i).astype(F32)
    b_col_all = jnp.dot(tril, lf_all, precision=HIGHEST, preferred_element_type=F32)
    sel = (lax.broadcasted_iota(jnp.int32, (8, LANES), 1)
           == lax.broadcasted_iota(jnp.int32, (8, LANES), 0) + SM_MI).astype(F32)
    rows = lax.dot_general(sel, gates, (((1,), (1,)), ((), ())), precision=HIGHEST,
                           preferred_element_type=F32)
    b_row_all = jnp.dot(rows, triu, precision=HIGHEST, preferred_element_type=F32)
    causal = c_i <= r_i

    mo = mo_ref[0]
    mv = mv_ref[0]
    for h in range(M_HEADS):
        hs = slice(h * hd, (h + 1) * hd)
        qh = qk[:, hs]
        kh = qk[:, D + h * hd:D + (h + 1) * hd] * (hd ** -0.5)
        vh = mv[:, hs]
        qb, kb, vb = qh.astype(BF16), kh.astype(BF16), vh.astype(BF16)
        b_col = b_col_all[:, SM_MF + h:SM_MF + h + 1]
        i_col = gates[:, SM_MI + h:SM_MI + h + 1]
        b_row = b_row_all[M_HEADS + h:M_HEADS + h + 1, :]
        i_row = rows[h:h + 1, :]
        m_prev = m_ref[0, :, h:h + 1]
        c_prev = c_ref[0, h]
        n_prev = n_ref[0, h:h + 1, :]

        dmat = jnp.where(causal, b_col - b_row + i_row, -jnp.inf)
        inter = b_col + m_prev
        m_t = jnp.maximum(inter, jnp.max(dmat, axis=-1, keepdims=True))
        w_intra = jnp.exp(dmat - m_t)
        w_inter = jnp.exp(inter - m_t)
        s = lax.dot_general(qb, kb, (((1,), (1,)), ((), ())), preferred_element_type=F32) * w_intra
        qc = lax.dot_general(qb, c_prev.astype(BF16), (((1,), (1,)), ((), ())),
                             preferred_element_type=F32)
        num = jnp.dot(s.astype(BF16), vb, preferred_element_type=F32) + w_inter * qc
        den = jnp.sum(s, axis=-1, keepdims=True) + w_inter * jnp.sum(qh * n_prev, axis=-1, keepdims=True)
        denom = jnp.maximum(jnp.abs(den), jnp.exp(-m_t))
        hh = num / denom

        m_new = m_t[L - 1:L, :]
        b_last = b_col[L - 1:L, :]
        g_col = jnp.exp(b_last - b_col + i_col - m_new)
        decay = jnp.exp(b_last + m_prev - m_new)
        gv = (g_col * vh).astype(BF16)
        c_ref[0, h] = decay * c_prev + lax.dot_general(
            gv, kb, (((0,), (0,)), ((), ())), preferred_element_type=F32)
        n_ref[0, h:h + 1, :] = decay * n_prev + jnp.sum(g_col * kh, axis=0, keepdims=True)
        m_ref[0, :, h:h + 1] = m_new

        hn = hh * lax.rsqrt(jnp.mean(hh * hh, axis=-1, keepdims=True) + EPS) * gm_ref[:, hs]
        y_ref[0, :, hs] = (hn * jax.nn.sigmoid(mo[:, hs])).astype(BF16)


def _mlstm(z3, bias_row, w_conv, b_conv, g_mnorm, c0, n0, m0, conv0, L):
    b, t, _ = z3.shape
    nc = t // L
    hd = D // M_HEADS

    def zspec(col, width):
        return pl.BlockSpec((1, L, width), lambda i, c: (i, c, col // width))

    def per_batch(shape):
        nd = len(shape)
        return pl.BlockSpec((1,) + shape, lambda i, c: (i,) + (0,) * nd)

    def const(shape):
        nd = len(shape)
        return pl.BlockSpec(shape, lambda i, c: (0,) * nd)

    return pl.pallas_call(
        functools.partial(_mlstm_kernel, L=L),
        grid=(b, nc),
        in_specs=[zspec(COL_MQ, D), zspec(COL_MK, D), zspec(COL_MV, D), zspec(COL_MO, D),
                  zspec(COL_SMALL, LANES), const((1, LANES)), const((CONV_W, 2 * D)),
                  const((1, 2 * D)), const((1, D)),
                  per_batch((M_HEADS, hd, hd)), per_batch((M_HEADS, hd)),
                  per_batch((1, M_HEADS)), per_batch((CONV_W - 1, 2 * D))],
        out_specs=[pl.BlockSpec((1, L, D), lambda i, c: (i, c, 0)),
                   per_batch((M_HEADS, hd, hd)), per_batch((M_HEADS, hd)),
                   per_batch((1, M_HEADS)), per_batch((CONV_W - 1, 2 * D))],
        out_shape=[jax.ShapeDtypeStruct((b, t, D), BF16),
                   jax.ShapeDtypeStruct((b, M_HEADS, hd, hd), F32),
                   jax.ShapeDtypeStruct((b, M_HEADS, hd), F32),
                   jax.ShapeDtypeStruct((b, 1, M_HEADS), F32),
                   jax.ShapeDtypeStruct((b, CONV_W - 1, 2 * D), F32)],
        scratch_shapes=[pltpu.VMEM((8 + L, 2 * D), F32)],
        compiler_params=_cparams(("parallel", "arbitrary")),
        name="mlstm",
    )(z3, z3, z3, z3, z3, bias_row, w_conv, b_conv, g_mnorm, c0, n0, m0, conv0)


def _eye_bf16(n):
    return (lax.broadcasted_iota(jnp.int32, (n, n), 0)
            == lax.broadcasted_iota(jnp.int32, (n, n), 1)).astype(BF16)


def _transpose_bf16(eye, a):
    return lax.dot_general(eye, a, (((1,), (1,)), ((), ())), preferred_element_type=F32)


def _qkv_kernel(x_ref, g_ref, w_ref, gq_ref, gk_ref, k32_ref, v32_ref, kb_ref, qt_ref, vt_ref, xn_ref):
    j = pl.program_id(2)
    hd = D // A_HEADS
    heads = [slice(h * hd, (h + 1) * hd) for h in range(A_HEADS)]

    @pl.when(j == 0)
    def _():
        x = x_ref[0]
        ms = jnp.mean(x * x, axis=-1, keepdims=True)
        xn_ref[...] = ((x * lax.rsqrt(ms + EPS)) * g_ref[...]).astype(BF16)

    a = jnp.dot(xn_ref[...], w_ref[...], preferred_element_type=F32)

    def head_norm(col, gain):
        return (col * lax.rsqrt(jnp.mean(col * col, axis=-1, keepdims=True) + EPS)) * gain

    @pl.when(j == 0)
    def _():
        eye = _eye_bf16(hd)
        for hs in heads:
            qn = head_norm(a[:, hs], gq_ref[...]) * Q_LOGIT_SCALE
            qt_ref[0, hs, :] = _transpose_bf16(eye, qn.astype(BF16)).astype(BF16)

    @pl.when(j == 1)
    def _():
        for hs in heads:
            kn = head_norm(a[:, hs], gk_ref[...])
            k32_ref[0, :, hs] = kn
            kb_ref[0, :, hs] = kn.astype(BF16)

    @pl.when(j == 2)
    def _():
        eye = _eye_bf16(hd)
        v32_ref[0] = a
        for hs in heads:
            vt_ref[0, 0, hs, :] = _transpose_bf16(eye, a[:, hs].astype(BF16)).astype(BF16)


def _qkv(x, g_norm, w_att, g_q, g_k, tm):
    b, t, _ = x.shape
    hd = D // A_HEADS
    row = pl.BlockSpec((1, tm, D), lambda i, r, j: (i, r, 0))
    gspec = pl.BlockSpec((1, hd), lambda i, r, j: (0, 0))
    return pl.pallas_call(
        _qkv_kernel,
        grid=(b, t // tm, 3),
        in_specs=[row, pl.BlockSpec((1, D), lambda i, r, j: (0, 0)),
                  pl.BlockSpec((D, D), lambda i, r, j: (0, j)), gspec, gspec],
        out_specs=[row, row, row, pl.BlockSpec((1, D, tm), lambda i, r, j: (i, 0, r)),
                   pl.BlockSpec((1, 1, D, tm), lambda i, r, j: (i, r, 0, 0))],
        out_shape=[jax.ShapeDtypeStruct((b, t, D), F32), jax.ShapeDtypeStruct((b, t, D), F32),
                   jax.ShapeDtypeStruct((b, t, D), BF16), jax.ShapeDtypeStruct((b, D, t), BF16),
                   jax.ShapeDtypeStruct((b, t // tm, D, tm), BF16)],
        scratch_shapes=[pltpu.VMEM((tm, D), BF16)],
        compiler_params=_cparams(("parallel", "parallel", "arbitrary")),
        name="qkv_proj_norm",
    )(x, g_norm, w_att, g_q, g_k)


def _num_key_tiles(qi, tq, tile, q_pos0, s_real):
    last_chunk = (q_pos0 + (qi + 1) * tq - 1) // CHUNK
    kend = jnp.minimum((last_chunk + 1) * CHUNK, s_real)
    return (kend + tile - 1) // tile


def _num_key_tiles_static(qi, tq, tile, q_pos0, s_real):
    last_chunk = (q_pos0 + (qi + 1) * tq - 1) // CHUNK
    return -(-min((last_chunk + 1) * CHUNK, s_real) // tile)


def _select_kernel(iq_ref, sm_ref, ki_ref, mask_ref, score_ref, iqt_ref, *,
                   tq, kb, s_pad, s_real, q_pos0, topk):
    qi = pl.program_id(1)
    nkt = _num_key_tiles(qi, tq, kb, q_pos0, s_real)
    kf = float(topk)
    groups = kb // SUBLANES

    pick = (lax.broadcasted_iota(jnp.int32, (IDX_HEADS, LANES), 1)
            == lax.broadcasted_iota(jnp.int32, (IDX_HEADS, LANES), 0) + SM_IW).astype(F32)
    w = lax.dot_general(pick, sm_ref[0], (((1,), (1,)), ((), ())), precision=HIGHEST,
                        preferred_element_type=F32) * ((IDX_HEADS * IDX_DIM) ** -0.5)
    eye = _eye_bf16(IDX_DIM)
    for h in range(IDX_HEADS):
        a = iq_ref[0, :, h * IDX_DIM:(h + 1) * IDX_DIM]
        hi = a.astype(BF16)
        lo = (a - hi.astype(F32)).astype(BF16)
        hi_t = _transpose_bf16(eye, hi).astype(BF16)
        lo_t = _transpose_bf16(eye, lo).astype(BF16)
        cols = slice((h % 2) * tq, (h % 2 + 1) * tq)
        iqt_ref[h // 2, 0:IDX_DIM, cols] = hi_t
        iqt_ref[h // 2, IDX_DIM:2 * IDX_DIM, cols] = hi_t
        iqt_ref[h // 2, 2 * IDX_DIM:3 * IDX_DIM, cols] = lo_t

    q_pos = q_pos0 + qi * tq + lax.broadcasted_iota(jnp.int32, (1, tq), 1)
    q_chunk = q_pos >> CHUNK_SHIFT
    first_key = jnp.minimum(((q_pos0 + qi * tq) >> CHUNK_SHIFT) << CHUNK_SHIFT, s_real)
    n_full = first_key // kb

    def score_tile(j, carry, masked):
        rmin, rmax = carry
        k0 = pl.multiple_of(j * kb, kb)
        kt = ki_ref[0, pl.ds(k0, kb), :]
        acc = None
        for p in range(IDX_HEADS // 2):
            rel = jnp.dot(kt, iqt_ref[p], preferred_element_type=F32)
            part = (w[2 * p:2 * p + 1] * jnp.maximum(rel[:, 0:tq], 0.0)
                    + w[2 * p + 1:2 * p + 2] * jnp.maximum(rel[:, tq:2 * tq], 0.0))
            acc = part if acc is None else acc + part
        if masked:
            key = k0 + lax.broadcasted_iota(jnp.int32, (kb, tq), 0)
            adm = jnp.logical_and((key >> CHUNK_SHIFT) <= q_chunk, key < s_real)
            low = jnp.where(adm, acc, jnp.inf)
            acc = jnp.where(adm, acc, -jnp.inf)
        else:
            low = acc
        score_ref[pl.ds(k0, kb), :] = acc
        rmin = jnp.minimum(rmin, jnp.min(low.reshape(groups, SUBLANES, tq), axis=0))
        rmax = jnp.maximum(rmax, jnp.max(acc.reshape(groups, SUBLANES, tq), axis=0))
        return rmin, rmax

    stats = (jnp.full((SUBLANES, tq), jnp.inf, F32), jnp.full((SUBLANES, tq), -jnp.inf, F32))
    stats = lax.fori_loop(0, n_full, functools.partial(score_tile, masked=False), stats)
    rmin8, rmax8 = lax.fori_loop(n_full, nkt, functools.partial(score_tile, masked=True), stats)
    rmin = jnp.min(rmin8, axis=0, keepdims=True)
    rmax = jnp.max(rmax8, axis=0, keepdims=True)

    def count(pred):
        def body(j, acc):
            for c in range(kb // COUNT_ROWS):
                k0 = pl.multiple_of(j * kb + c * COUNT_ROWS, COUNT_ROWS)
                hit = jnp.where(pred(score_ref[pl.ds(k0, COUNT_ROWS), :], k0), 1.0, 0.0)
                acc = acc + jnp.sum(hit.reshape(COUNT_ROWS // (COUNT_PARTS * SUBLANES), COUNT_PARTS,
                                                SUBLANES, tq), axis=0)
            return acc
        acc = lax.fori_loop(0, nkt, body, jnp.zeros((COUNT_PARTS, SUBLANES, tq), F32))
        return jnp.sum(jnp.sum(acc, axis=0), axis=0, keepdims=True)

    def below(cur):
        shape = (COUNT_ROWS // (COUNT_PARTS * SUBLANES), COUNT_PARTS, SUBLANES, tq)

        def body(j, carry):
            top, num = carry
            for c in range(kb // COUNT_ROWS):
                k0 = pl.multiple_of(j * kb + c * COUNT_ROWS, COUNT_ROWS)
                s = score_ref[pl.ds(k0, COUNT_ROWS), :]
                under = s < cur
                top = jnp.maximum(top, jnp.max(jnp.where(under, s, -jnp.inf).reshape(shape), axis=0))
                num = num + jnp.sum(jnp.where(under, 0.0, 1.0).reshape(shape), axis=0)
            return top, num

        top, num = lax.fori_loop(0, nkt, body, (jnp.full(shape[1:], -jnp.inf, F32),
                                                jnp.zeros(shape[1:], F32)))
        return (jnp.max(jnp.max(top, axis=0), axis=0, keepdims=True),
                jnp.sum(jnp.sum(num, axis=0), axis=0, keepdims=True))

    n_adm = jnp.minimum((q_chunk + 1) << CHUNK_SHIFT, s_real).astype(F32)

    def open_rows(cnt_lo):
        return cnt_lo > kf

    def wide_rows(st):
        return jnp.logical_and(open_rows(st[2]), st[2] - st[3] > FINISH_MAX)

    def flag(pred):
        return jnp.max(jnp.where(pred, 1.0, 0.0))

    def bisect(st):
        lo, hi, cnt_lo, cnt_hi = st
        active = open_rows(cnt_lo)
        mid = 0.5 * jnp.maximum(lo, rmin) + 0.5 * jnp.minimum(hi, rmax)
        cm = count(lambda s, k0: s >= mid)
        up = jnp.logical_and(active, cm >= kf)
        dn = jnp.logical_and(active, cm < kf)
        return (jnp.where(up, mid, lo), jnp.where(dn, mid, hi),
                jnp.where(up, cm, cnt_lo), jnp.where(dn, cm, cnt_hi))

    st = (jnp.full((1, tq), -F32_MAX, F32), jnp.full((1, tq), F32_MAX, F32), n_adm,
          jnp.zeros((1, tq), F32))
    st = lax.fori_loop(0, SELECT_MIN_ITERS, lambda _, s: bisect(s), st)

    def narrow(c):
        nxt = bisect(c[2])
        return c[0] + 1, flag(wide_rows(nxt)), nxt

    _, _, st = lax.while_loop(
        lambda c: jnp.logical_and(c[0] < SELECT_MAX_ITERS, c[1] > 0.0), narrow,
        (jnp.int32(SELECT_MIN_ITERS), flag(wide_rows(st)), st))

    lo, hi, cnt_lo, cnt_hi = st
    cand, _ = below(hi)

    def walk(c):
        _, walking, lo, hi, cnt_lo, cnt_hi, cand = c
        nxt, cge = below(cand)
        settle = jnp.logical_and(walking > 0.0, cge >= kf)
        move = jnp.logical_and(walking > 0.0, cge < kf)
        walking = jnp.where(move, 1.0, 0.0)
        return (jnp.max(walking), walking, jnp.where(settle, cand, lo), jnp.where(move, cand, hi),
                jnp.where(settle, cge, cnt_lo), jnp.where(move, cge, cnt_hi), jnp.where(move, nxt, cand))

    walking = jnp.where(open_rows(cnt_lo), 1.0, 0.0)
    _, _, lo, hi, cnt_lo, cnt_hi, _ = lax.while_loop(
        lambda c: c[0] > 0.0, walk, (jnp.max(walking), walking, lo, hi, cnt_lo, cnt_hi, cand))
    n_open = flag(open_rows(cnt_lo))

    def clear_tile(j, carry):
        k0 = pl.multiple_of(j * kb, kb)
        mask_ref[0, 0, pl.ds(k0, kb), :] = jnp.zeros((kb, tq), jnp.int8)
        return carry

    lax.fori_loop(nkt, s_pad // kb, clear_tile, 0)

    @pl.when(n_open == 0.0)
    def _():
        def write_tile(j, carry):
            k0 = pl.multiple_of(j * kb, kb)
            keep = score_ref[pl.ds(k0, kb), :] >= lo
            mask_ref[0, 0, pl.ds(k0, kb), :] = jnp.where(keep, 1, 0).astype(jnp.int8)
            return carry

        lax.fori_loop(0, nkt, write_tile, 0)

    @pl.when(n_open > 0.0)
    def _():
        tied = open_rows(cnt_lo)
        need = jnp.where(tied, kf - cnt_hi, 0.0)
        key_iota = lax.broadcasted_iota(jnp.int32, (kb, tq), 0)
        strip_iota = lax.broadcasted_iota(jnp.int32, (COUNT_ROWS, tq), 0)
        shape = (COUNT_ROWS // (COUNT_PARTS * SUBLANES), COUNT_PARTS, SUBLANES, tq)

        def next_tied(prev):
            def body(j, acc):
                for c in range(kb // COUNT_ROWS):
                    k0 = pl.multiple_of(j * kb + c * COUNT_ROWS, COUNT_ROWS)
                    idx = k0 + strip_iota
                    hit = jnp.logical_and(score_ref[pl.ds(k0, COUNT_ROWS), :] == lo, idx > prev)
                    acc = jnp.minimum(acc, jnp.min(jnp.where(hit, idx, s_pad).reshape(shape), axis=0))
                return acc
            acc = lax.fori_loop(0, nkt, body, jnp.full(shape[1:], s_pad, jnp.int32))
            return jnp.min(jnp.min(acc, axis=0), axis=0, keepdims=True)

        def cut_by_walk(_):
            def step(c):
                _, left, last = c
                take = left > 0.0
                last = jnp.where(take, next_tied(last), last)
                left = jnp.where(take, left - 1.0, left)
                return jnp.max(left), left, last
            _, _, last = lax.while_loop(lambda c: c[0] > 0.0, step,
                                        (jnp.max(need), need, jnp.full((1, tq), -1, jnp.int32)))
            return last + 1

        def cut_by_bisection(_):
            def step(_, st):
                jlo, jhi = st
                jmid = (jlo + jhi) >> 1
                cm = count(lambda s, k0: jnp.logical_and(s == lo, k0 + strip_iota < jmid))
                ok = cm >= need
                return jnp.where(ok, jlo, jmid), jnp.where(ok, jmid, jhi)
            steps = int(math.ceil(math.log2(s_pad))) + 1
            return lax.fori_loop(0, steps, step, (jnp.zeros((1, tq), jnp.int32),
                                                  jnp.full((1, tq), s_pad, jnp.int32)))[1]

        jcut = lax.cond(jnp.max(need) <= TIE_WALK_MAX, cut_by_walk, cut_by_bisection, 0)
        jcut = jnp.where(tied, jcut, s_pad)

        def write_tile(j, carry):
            k0 = pl.multiple_of(j * kb, kb)
            s = score_ref[pl.ds(k0, kb), :]
            keep = jnp.logical_or(s >= hi, jnp.logical_and(s >= lo, k0 + key_iota < jcut))
            mask_ref[0, 0, pl.ds(k0, kb), :] = jnp.where(keep, 1, 0).astype(jnp.int8)
            return carry

        lax.fori_loop(0, nkt, write_tile, 0)


def _select(z3, ki3, tq, kb, s_real, q_pos0, topk):
    b, t, _ = z3.shape
    s_pad = ki3.shape[1]
    iq_w = IDX_HEADS * IDX_DIM
    kern = functools.partial(_select_kernel, tq=tq, kb=kb, s_pad=s_pad, s_real=s_real,
                             q_pos0=q_pos0, topk=topk)
    return pl.pallas_call(
        kern,
        grid=(b, t // tq),
        in_specs=[pl.BlockSpec((1, tq, iq_w), lambda i, j: (i, j, COL_IQ // iq_w)),
                  pl.BlockSpec((1, tq, LANES), lambda i, j: (i, j, COL_SMALL // LANES)),
                  pl.BlockSpec((1, s_pad, 3 * IDX_DIM), lambda i, j: (i, 0, 0))],
        out_specs=pl.BlockSpec((1, 1, s_pad, tq), lambda i, j: (i, j, 0, 0)),
        out_shape=jax.ShapeDtypeStruct((b, t // tq, s_pad, tq), jnp.int8),
        scratch_shapes=[pltpu.VMEM((s_pad, tq), F32),
                        pltpu.VMEM((IDX_HEADS // 2, 3 * IDX_DIM, 2 * tq), BF16)],
        compiler_params=_cparams(("parallel", "arbitrary")),
        name="index_select",
    )(z3, z3, ki3)


def _attn_kernel(q_of_ref, k_of_ref, qt_ref, k_ref, vt_ref, *refs, n_mask):
    mask_refs = refs[:n_mask]
    o_ref, acc_ref, m_ref, l_ref, s_ref, bias_ref = refs[n_mask:]
    step = pl.program_id(1)
    hd = D // A_HEADS
    tq = qt_ref.shape[2]
    tk = k_ref.shape[1]
    tq_mask = tq // n_mask
    pv_rows = math.gcd(tk, PV_ROWS)
    first = k_of_ref[step] == 0
    nxt = jnp.minimum(step + 1, pl.num_programs(1) - 1)
    last = jnp.logical_or(step == pl.num_programs(1) - 1, q_of_ref[nxt] != q_of_ref[step])

    @pl.when(first)
    def _():
        acc_ref[...] = jnp.zeros(acc_ref.shape, F32)
        m_ref[...] = jnp.full(m_ref.shape, MASK_NEG, F32)
        l_ref[...] = jnp.zeros(l_ref.shape, F32)

    for n, mask_ref in enumerate(mask_refs):
        bias_ref[:, n * tq_mask:(n + 1) * tq_mask] = jnp.where(
            mask_ref[0, 0].astype(jnp.int32) != 0, 0.0, MASK_NEG)
    def logits(h):
        hs = slice(h * hd, (h + 1) * hd)
        top = None
        for r in range(0, tk, QK_ROWS):
            s = jnp.dot(k_ref[0, r:r + QK_ROWS, hs], qt_ref[0, hs, :],
                        preferred_element_type=F32) + bias_ref[r:r + QK_ROWS, :]
            s_ref[h, r:r + QK_ROWS, :] = s
            part = jnp.max(s.reshape(QK_ROWS // SUBLANES, SUBLANES, tq), axis=0)
            top = part if top is None else jnp.maximum(top, part)
        return jnp.max(top, axis=0, keepdims=True)

    def weigh(h, tile_max):
        hs = slice(h * hd, (h + 1) * hd)
        m_prev = m_ref[h:h + 1, :]
        m_new = jnp.maximum(m_prev, tile_max)
        alpha = jnp.exp2(m_prev - m_new)
        psum = None
        pv = None
        for r in range(0, tk, pv_rows):
            p = jnp.exp2(s_ref[h, r:r + pv_rows, :] - m_new)
            part = jnp.sum(p.reshape(pv_rows // SUBLANES, SUBLANES, tq), axis=0)
            psum = part if psum is None else psum + part
            prod = jnp.dot(vt_ref[0, 0, hs, r:r + pv_rows], p.astype(BF16),
                           preferred_element_type=F32)
            pv = prod if pv is None else pv + prod
        l_ref[h:h + 1, :] = alpha * l_ref[h:h + 1, :] + jnp.sum(psum, axis=0, keepdims=True)
        acc_ref[hs, :] = alpha * acc_ref[hs, :] + pv
        m_ref[h:h + 1, :] = m_new

    tile_max = [logits(h) for h in range(A_HEADS)]

    @pl.when(k_of_ref[step] >= 0)
    def _():
        for h in range(A_HEADS):
            weigh(h, tile_max[h])

    @pl.when(last)
    def _():
        eye = _eye_bf16(tq)
        for h in range(A_HEADS):
            hs = slice(h * hd, (h + 1) * hd)
            out_t = (acc_ref[hs, :] / l_ref[h:h + 1, :]).astype(BF16)
            o_ref[0, :, hs] = _transpose_bf16(eye, out_t).astype(BF16)


def _attention(qt, kb, vt, mask_t, tq, tk, s_real, q_pos0):
    b, _, t = qt.shape
    n_mask = tq // mask_t.shape[3]
    steps = [(q, j) for q in range(t // tq) for j in range(_num_key_tiles_static(q, tq, tk, q_pos0, s_real))]
    q_of = jnp.array([q for q, _ in steps], jnp.int32)
    k_of = jnp.array([j for _, j in steps], jnp.int32)

    def mask_spec(n):
        return pl.BlockSpec((1, 1, tk, tq // n_mask),
                            lambda i, s, q_of, k_of: (i, q_of[s] * n_mask + n, k_of[s], 0))

    return pl.pallas_call(
        functools.partial(_attn_kernel, n_mask=n_mask),
        grid_spec=pltpu.PrefetchScalarGridSpec(
            num_scalar_prefetch=2,
            grid=(b, len(steps)),
            in_specs=[pl.BlockSpec((1, D, tq), lambda i, s, q_of, k_of: (i, 0, q_of[s])),
                      pl.BlockSpec((1, tk, D), lambda i, s, q_of, k_of: (i, k_of[s], 0)),
                      pl.BlockSpec((1, 1, D, tk), lambda i, s, q_of, k_of: (i, k_of[s], 0, 0))]
            + [mask_spec(n) for n in range(n_mask)],
            out_specs=pl.BlockSpec((1, tq, D), lambda i, s, q_of, k_of: (i, q_of[s], 0)),
            scratch_shapes=[pltpu.VMEM((D, tq), F32),
                            pltpu.VMEM((A_HEADS, tq), F32),
                            pltpu.VMEM((A_HEADS, tq), F32),
                            pltpu.VMEM((A_HEADS, tk, tq), F32),
                            pltpu.VMEM((tk, tq), F32)]),
        out_shape=jax.ShapeDtypeStruct((b, t, D), BF16),
        compiler_params=_cparams(("parallel", "arbitrary")),
        name="masked_attention",
    )(q_of, k_of, qt, kb, vt, *([mask_t] * n_mask))


def _attn_past_kernel(qt_ref, kn_ref, vtn_ref, kp_ref, vp_ref, mask_ref, o_ref):
    hd = D // A_HEADS
    t = qt_ref.shape[2]
    past = kp_ref.shape[1] // A_HEADS

    def cached(ref, h):
        return ref[0, pl.ds(h, past, stride=A_HEADS), :].astype(BF16)

    bias = jnp.where(mask_ref[0, 0].astype(jnp.int32) != 0, 0.0, MASK_NEG)
    bias_p, bias_n = bias[0:past], bias[past:past + t]
    eye_hd, eye_t = _eye_bf16(hd), _eye_bf16(t)
    for h in range(A_HEADS):
        hs = slice(h * hd, (h + 1) * hd)
        q = qt_ref[0, hs, :]
        s_p = jnp.dot(cached(kp_ref, h), q, preferred_element_type=F32) + bias_p
        s_n = jnp.dot(kn_ref[0, :, hs], q, preferred_element_type=F32) + bias_n
        m = jnp.maximum(jnp.max(s_p, axis=0, keepdims=True), jnp.max(s_n, axis=0, keepdims=True))
        p_p = jnp.exp2(s_p - m)
        p_n = jnp.exp2(s_n - m)
        l = jnp.sum(p_p, axis=0, keepdims=True) + jnp.sum(p_n, axis=0, keepdims=True)
        vt_p = _transpose_bf16(eye_hd, cached(vp_ref, h)).astype(BF16)
        out_t = (jnp.dot(vt_p, p_p.astype(BF16), preferred_element_type=F32)
                 + jnp.dot(vtn_ref[0, 0, hs, :], p_n.astype(BF16), preferred_element_type=F32))
        o_ref[0, :, hs] = _transpose_bf16(eye_t, (out_t / l).astype(BF16)).astype(BF16)


def _attention_with_past(qt, kb, vt, k_past, v_past, mask_t):
    b, _, t = qt.shape
    past, heads, hd = k_past.shape[1:]
    s_pad = mask_t.shape[2]
    k_past = k_past.reshape(b, past * heads, hd)
    v_past = v_past.reshape(b, past * heads, hd)
    cache = pl.BlockSpec((1, past * heads, hd), lambda i: (i, 0, 0))
    return pl.pallas_call(
        _attn_past_kernel,
        grid=(b,),
        in_specs=[pl.BlockSpec((1, D, t), lambda i: (i, 0, 0)),
                  pl.BlockSpec((1, t, D), lambda i: (i, 0, 0)),
                  pl.BlockSpec((1, 1, D, t), lambda i: (i, 0, 0, 0)),
                  cache, cache,
                  pl.BlockSpec((1, 1, s_pad, t), lambda i: (i, 0, 0, 0))],
        out_specs=pl.BlockSpec((1, t, D), lambda i: (i, 0, 0)),
        out_shape=jax.ShapeDtypeStruct((b, t, D), BF16),
        compiler_params=_cparams(("parallel",)),
        name="masked_attention_past",
    )(qt, kb, vt, k_past, v_past, mask_t)


def _merge_kernel(x_ref, ya_ref, yb_ref, ga_ref, gb_ref, wa_ref, wb_ref, wo_ref, o_ref):
    a = jnp.dot(ya_ref[...], wa_ref[...], preferred_element_type=F32)
    bb = jnp.dot(yb_ref[...], wb_ref[...], preferred_element_type=F32)
    mix = jax.nn.sigmoid(ga_ref[...]) * a + jax.nn.sigmoid(gb_ref[...]) * bb
    o_ref[...] = x_ref[...] + jnp.dot(mix.astype(BF16), wo_ref[...], preferred_element_type=F32)


def _merge(x, ya, yb, z, wa, wb, wo, tm):
    n = x.shape[0]
    row = pl.BlockSpec((tm, D), lambda i: (i, 0))
    wspec = pl.BlockSpec((D, D), lambda i: (0, 0))
    return pl.pallas_call(
        _merge_kernel,
        grid=(n // tm,),
        in_specs=[row, row, row,
                  pl.BlockSpec((tm, D), lambda i: (i, COL_GA // D)),
                  pl.BlockSpec((tm, D), lambda i: (i, COL_GB // D)),
                  wspec, wspec, wspec],
        out_specs=row,
        out_shape=jax.ShapeDtypeStruct((n, D), F32),
        compiler_params=_cparams(("parallel",)),
        name="merge_out_proj",
    )(x, ya, yb, z, z, wa, wb, wo)


def _ffn_kernel(x_ref, g_ref, wg_ref, wu_ref, wd_ref, o_ref, xn_ref, acc_ref):
    c = pl.program_id(1)

    @pl.when(c == 0)
    def _():
        x = x_ref[...]
        ms = jnp.mean(x * x, axis=-1, keepdims=True)
        xn_ref[...] = ((x * lax.rsqrt(ms + EPS)) * g_ref[...]).astype(BF16)
        acc_ref[...] = x

    xn = xn_ref[...]
    gate = jnp.dot(xn, wg_ref[...], preferred_element_type=F32)
    up = jnp.dot(xn, wu_ref[...], preferred_element_type=F32)
    act = (gate * jax.nn.sigmoid(gate)) * up
    acc_ref[...] += jnp.dot(act.astype(BF16), wd_ref[...], preferred_element_type=F32)

    @pl.when(c == pl.num_programs(1) - 1)
    def _():
        o_ref[...] = acc_ref[...]


def _ffn(x, g, w_in, w_out, tm, tf):
    n = x.shape[0]
    dff = w_out.shape[0]
    nf = dff // tf
    row = pl.BlockSpec((tm, D), lambda i, c: (i, 0))
    return pl.pallas_call(
        _ffn_kernel,
        grid=(n // tm, nf),
        in_specs=[row, pl.BlockSpec((1, D), lambda i, c: (0, 0)),
                  pl.BlockSpec((D, tf), lambda i, c: (0, c)),
                  pl.BlockSpec((D, tf), lambda i, c: (0, nf + c)),
                  pl.BlockSpec((tf, D), lambda i, c: (c, 0))],
        out_specs=row,
        out_shape=jax.ShapeDtypeStruct((n, D), F32),
        scratch_shapes=[pltpu.VMEM((tm, D), BF16), pltpu.VMEM((tm, D), F32)],
        compiler_params=_cparams(("parallel", "arbitrary")),
        name="swiglu_ffn",
    )(x, g, w_in, w_in, w_out)


def _split_hi_lo(a):
    hi = a.astype(BF16)
    lo = (a - hi.astype(F32)).astype(BF16)
    return hi, lo


def _layer(x, past, p, cfg):
    k_past, v_past, ki_past, c0, n0, m0, conv0 = past
    b, t, _ = x.shape
    n_tok = b * t
    past_len = k_past.shape[1]
    s_real = past_len + t
    topk = min(TOPK_MAX, s_real // 4)
    tq_sel, kb_sel, tq_att, tk_att = cfg["tq_sel"], cfg["kb_sel"], cfg["tq_att"], cfg["tk_att"]
    s_pad = -(-s_real // tk_att) * tk_att

    x2 = x.reshape(n_tok, D)
    z = _norm_matmul(x2, p["g_norm1"], p["w_in"], cfg["tm_proj"], cfg["tn_proj"])
    z3 = z.reshape(b, t, D_Z)

    y_a, c_new, n_new, m_new, conv_new = _mlstm(
        z3, p["gate_bias"], p["w_conv"], p["b_conv"], p["g_mnorm"], c0, n0,
        m0.reshape(b, 1, M_HEADS), conv0, cfg["mlstm_chunk"])

    k32, v32, kb, qt, vt = _qkv(x, p["g_norm1"], p["w_att"], p["g_q"], p["g_k"], cfg["tm_qkv"])
    ik = z3[:, :, COL_SMALL:COL_SMALL + IDX_DIM]
    ki_all = ik if past_len == 0 else jnp.concatenate([ki_past.astype(F32), ik], axis=1)
    k_hi, k_lo = _split_hi_lo(ki_all)
    ki3 = jnp.concatenate([k_hi, k_lo, k_hi], axis=-1)
    ki3 = jnp.pad(ki3, ((0, 0), (0, s_pad - s_real), (0, 0)))
    mask_t = _select(z3, ki3, tq_sel, kb_sel, s_real, past_len, topk)
    if past_len == 0:
        assert cfg["tm_qkv"] == tk_att and s_pad == s_real
        y_b = _attention(qt, kb, vt, mask_t, tq_att, tk_att, s_real, past_len)
    else:
        assert cfg["tm_qkv"] == t == tq_sel
        y_b = _attention_with_past(qt, kb, vt, k_past, v_past, mask_t)

    x1 = _merge(x2, y_a.reshape(n_tok, D), y_b.reshape(n_tok, D), z,
                p["w_a_out"], p["w_b_out"], p["w_o"], cfg["tm_rows"])
    y = _ffn(x1, p["g_norm2"], p["w_ffn_in"], p["w_ffn_out"], cfg["tm_ffn"], cfg["tf_ffn"])

    hd = D // A_HEADS
    return y.reshape(b, t, D), (k32.reshape(b, t, A_HEADS, hd), v32.reshape(b, t, A_HEADS, hd), ik,
                                c_new, n_new, m_new.reshape(b, M_HEADS), conv_new)


def _prep_params(g_norm1, w_in, b_if, w_conv, b_conv, g_mnorm, g_q, g_k, w_a_out, w_b_out, w_o,
                 g_norm2, w_ffn_in, w_ffn_out):
    o_mi = 4 * D
    o_aq = o_mi + 2 * M_HEADS
    o_iq = o_aq + 3 * D
    o_ik = o_iq + IDX_HEADS * IDX_DIM
    o_iw = o_ik + IDX_DIM
    o_ga = o_iw + IDX_HEADS
    w_perm = jnp.concatenate([
        w_in[:, 0:o_mi], w_in[:, o_ga:o_ga + 2 * D], w_in[:, o_iq:o_ik],
        w_in[:, o_ik:o_ga], w_in[:, o_mi:o_aq],
        jnp.zeros((D, D_Z - COL_SMALL - (SM_MF + M_HEADS)), w_in.dtype)], axis=1).astype(BF16)
    gate_bias = jnp.zeros((1, LANES), F32).at[0, SM_MI:SM_MI + 2 * M_HEADS].set(b_if.astype(F32))
    return {
        "g_norm1": g_norm1.reshape(1, D), "w_in": w_perm, "w_att": w_in[:, o_aq:o_iq].astype(BF16),
        "gate_bias": gate_bias,
        "w_conv": w_conv, "b_conv": b_conv.reshape(1, 2 * D), "g_mnorm": g_mnorm.reshape(1, D),
        "g_q": g_q.reshape(1, -1), "g_k": g_k.reshape(1, -1),
        "w_a_out": w_a_out.astype(BF16), "w_b_out": w_b_out.astype(BF16), "w_o": w_o.astype(BF16),
        "g_norm2": g_norm2.reshape(1, D), "w_ffn_in": w_ffn_in.astype(BF16),
        "w_ffn_out": w_ffn_out.astype(BF16),
    }


def _config(b, t, past_len):
    n_tok = b * t
    s_real = past_len + t
    tm = min(2048, n_tok)
    tq_sel = min(256, t)
    tq_att = min(512, t)
    if s_real % 512 == 0:
        kb_sel = tk_att = 512
    else:
        kb_sel = tk_att = -(-s_real // LANES) * LANES
    return {"tm_proj": tm, "tn_proj": 896, "mlstm_chunk": min(128, t), "tm_rows": min(512, n_tok),
            "tm_qkv": min(512, t), "tq_sel": tq_sel, "kb_sel": kb_sel, "tq_att": tq_att,
            "tk_att": tk_att, "tm_ffn": min(512, n_tok), "tf_ffn": 1408}


def kernel(x_prompt, x_sample, cache_k, cache_v, cache_kidx, state_C, state_n, state_m, state_conv,
           g_norm1, w_in, b_if, w_conv, b_conv, g_mnorm, g_q, g_k, w_a_out, w_b_out, w_o,
           g_norm2, w_ffn_in, w_ffn_out):
    depth = w_in.shape[0]
    bp = x_prompt.shape[0]
    hd_a = D // A_HEADS
    hd_m = D // M_HEADS
    yp, ys = x_prompt, x_sample
    new_p, new_s = [], []
    for l in range(depth):
        p = _prep_params(g_norm1[l], w_in[l], b_if[l], w_conv[l], b_conv[l], g_mnorm[l], g_q[l],
                         g_k[l], w_a_out[l], w_b_out[l], w_o[l], g_norm2[l], w_ffn_in[l],
                         w_ffn_out[l])
        empty = (jnp.zeros((bp, 0, A_HEADS, hd_a), F32), jnp.zeros((bp, 0, A_HEADS, hd_a), F32),
                 jnp.zeros((bp, 0, IDX_DIM), F32), jnp.zeros((bp, M_HEADS, hd_m, hd_m), F32),
                 jnp.zeros((bp, M_HEADS, hd_m), F32), jnp.zeros((bp, M_HEADS), F32),
                 jnp.zeros((bp, CONV_W - 1, 2 * D), F32))
        yp, sp = _layer(yp, empty, p, _config(bp, yp.shape[1], 0))
        ys, ss = _layer(ys, (cache_k[l], cache_v[l], cache_kidx[l], state_C[l], state_n[l],
                             state_m[l], state_conv[l]), p,
                        _config(ys.shape[0], ys.shape[1], cache_k.shape[2]))
        new_p.append(sp)
        new_s.append(ss)

    def stk(lst, i):
        return jnp.stack([s[i] for s in lst])

    return (yp, ys) + tuple(stk(new_p, i) for i in range(7)) + tuple(stk(new_s, i) for i in range(7))
```

```python
import functools
import math

import jax
import jax.numpy as jnp
from jax import lax
from jax.experimental import pallas as pl
from jax.experimental.pallas import tpu as pltpu

F32 = jnp.float32
BF16 = jnp.bfloat16
HIGHEST = lax.Precision.HIGHEST

EPS = 1e-6
CHUNK = 64
CHUNK_SHIFT = 6
M_HEADS = 4
A_HEADS = 8
IDX_HEADS = 8
IDX_DIM = 64
CONV_W = 4
TOPK_MAX = 256
LANES = 128
SUBLANES = 8
MASK_NEG = -1e30
F32_MAX = float(jnp.finfo(jnp.float32).max)
VMEM_LIMIT = 48 * 1024 * 1024
SELECT_MIN_ITERS = 12
SELECT_MAX_ITERS = 15
FINISH_MAX = 4
QK_ROWS = 128
PV_ROWS = 256
HEAD_LAG = 8
TIE_WALK_MAX = 8
COUNT_PARTS = 4
COUNT_ROWS = 128

D = 1024
COL_MQ, COL_MK, COL_MV, COL_MO = 0, 1024, 2048, 3072
COL_GA, COL_GB = 4096, 5120
COL_IQ = 6144
COL_SMALL = 6656
D_Z = 7168
SM_IW, SM_MI, SM_MF = 64, 72, 76
Q_LOGIT_SCALE = math.log2(math.e) * (D // A_HEADS) ** -0.5


def _cparams(sem):
    return pltpu.CompilerParams(dimension_semantics=sem, vmem_limit_bytes=VMEM_LIMIT)


def _norm_matmul_kernel(x_ref, g_ref, w_ref, o_ref, xn_ref):
    @pl.when(pl.program_id(1) == 0)
    def _():
        x = x_ref[...]
        ms = jnp.mean(x * x, axis=-1, keepdims=True)
        xn_ref[...] = ((x * lax.rsqrt(ms + EPS)) * g_ref[...]).astype(BF16)

    o_ref[...] = jnp.dot(xn_ref[...], w_ref[...], preferred_element_type=F32)


def _norm_matmul(x, g, w, tm, tn):
    n, d = x.shape
    nout = w.shape[1]
    return pl.pallas_call(
        _norm_matmul_kernel,
        grid=(n // tm, nout // tn),
        in_specs=[pl.BlockSpec((tm, d), lambda i, j: (i, 0)),
                  pl.BlockSpec((1, d), lambda i, j: (0, 0)),
                  pl.BlockSpec((d, tn), lambda i, j: (0, j))],
        out_specs=pl.BlockSpec((tm, tn), lambda i, j: (i, j)),
        out_shape=jax.ShapeDtypeStruct((n, nout), F32),
        scratch_shapes=[pltpu.VMEM((tm, d), BF16)],
        compiler_params=_cparams(("parallel", "arbitrary")),
        name="norm_in_proj",
    )(x, g, w)


def _mlstm_kernel(mq_ref, mk_ref, mv_ref, mo_ref, sm_ref, bias_ref, wconv_ref, bconv_ref,
                  gm_ref, c0_ref, n0_ref, m0_ref, conv0_ref,
                  y_ref, c_ref, n_ref, m_ref, conv_ref, cbuf_ref, *, L):
    c = pl.program_id(1)
    hd = D // M_HEADS

    @pl.when(c == 0)
    def _():
        c_ref[...] = c0_ref[...]
        n_ref[...] = n0_ref[...]
        m_ref[...] = m0_ref[...]
        cbuf_ref[8 - (CONV_W - 1):8, :] = conv0_ref[0]

    cbuf_ref[8:8 + L, 0:D] = mq_ref[0]
    cbuf_ref[8:8 + L, D:2 * D] = mk_ref[0]
    wc = wconv_ref[...]
    qk = bconv_ref[...] + cbuf_ref[5:5 + L, :] * wc[0:1, :]
    for j in range(1, CONV_W):
        qk = qk + cbuf_ref[5 + j:5 + j + L, :] * wc[j:j + 1, :]
    tail = cbuf_ref[5 + L:8 + L, :]
    cbuf_ref[5:8, :] = tail
    conv_ref[0] = tail
    qk = qk * jax.nn.sigmoid(qk)

    g_all = sm_ref[0] + bias_ref[...]
    lf_all = jnp.minimum(g_all, 0.0) - jnp.log1p(jnp.exp(-jnp.abs(g_all)))
    lane = lax.broadcasted_iota(jnp.int32, (L, LANES), 1)
    gates = jnp.where(lane >= SM_MF, lf_all, g_all)
    r_i = lax.broadcasted_iota(jnp.int32, (L, L), 0)
    c_i = lax.broadcasted_iota(jnp.int32, (L, L), 1)
    tril = (c_i <= r_i).astype(F32)
    triu = (r_i <= c_i).astype(F32)
    b_col_all = jnp.dot(tril, lf_all, precision=HIGHEST, preferred_element_type=F32)
    sel = (lax.broadcasted_iota(jnp.int32, (8, LANES), 1)
           == lax.broadcasted_iota(jnp.int32, (8, LANES), 0) + SM_MI).astype(F32)
    rows = lax.dot_general(sel, gates, (((1,), (1,)), ((), ())), precision=HIGHEST,
                           preferred_element_type=F32)
    b_row_all = jnp.dot(rows, triu, precision=HIGHEST, preferred_element_type=F32)
    causal = c_i <= r_i

    mo = mo_ref[0]
    mv = mv_ref[0]
    for h in range(M_HEADS):
        hs = slice(h * hd, (h + 1) * hd)
        qh = qk[:, hs]
        kh = qk[:, D + h * hd:D + (h + 1) * hd] * (hd ** -0.5)
        vh = mv[:, hs]
        qb, kb, vb = qh.astype(BF16), kh.astype(BF16), vh.astype(BF16)
        b_col = b_col_all[:, SM_MF + h:SM_MF + h + 1]
        i_col = gates[:, SM_MI + h:SM_MI + h + 1]
        b_row = b_row_all[M_HEADS + h:M_HEADS + h + 1, :]
        i_row = rows[h:h + 1, :]
        m_prev = m_ref[0, :, h:h + 1]
        c_prev = c_ref[0, h]
        n_prev = n_ref[0, h:h + 1, :]

        dmat = jnp.where(causal, b_col - b_row + i_row, -jnp.inf)
        inter = b_col + m_prev
        m_t = jnp.maximum(inter, jnp.max(dmat, axis=-1, keepdims=True))
        w_intra = jnp.exp(dmat - m_t)
        w_inter = jnp.exp(inter - m_t)
        s = lax.dot_general(qb, kb, (((1,), (1,)), ((), ())), preferred_element_type=F32) * w_intra
        qc = lax.dot_general(qb, c_prev.astype(BF16), (((1,), (1,)), ((), ())),
                             preferred_element_type=F32)
        num = jnp.dot(s.astype(BF16), vb, preferred_element_type=F32) + w_inter * qc
        den = jnp.sum(s, axis=-1, keepdims=True) + w_inter * jnp.sum(qh * n_prev, axis=-1, keepdims=True)
        denom = jnp.maximum(jnp.abs(den), jnp.exp(-m_t))
        hh = num / denom

        m_new = m_t[L - 1:L, :]
        b_last = b_col[L - 1:L, :]
        g_col = jnp.exp(b_last - b_col + i_col - m_new)
        decay = jnp.exp(b_last + m_prev - m_new)
        gv = (g_col * vh).astype(BF16)
        c_ref[0, h] = decay * c_prev + lax.dot_general(
            gv, kb, (((0,), (0,)), ((), ())), preferred_element_type=F32)
        n_ref[0, h:h + 1, :] = decay * n_prev + jnp.sum(g_col * kh, axis=0, keepdims=True)
        m_ref[0, :, h:h + 1] = m_new

        hn = hh * lax.rsqrt(jnp.mean(hh * hh, axis=-1, keepdims=True) + EPS) * gm_ref[:, hs]
        y_ref[0, :, hs] = (hn * jax.nn.sigmoid(mo[:, hs])).astype(BF16)


def _mlstm(z3, bias_row, w_conv, b_conv, g_mnorm, c0, n0, m0, conv0, L):
    b, t, _ = z3.shape
    nc = t // L
    hd = D // M_HEADS

    def zspec(col, width):
        return pl.BlockSpec((1, L, width), lambda i, c: (i, c, col // width))

    def per_batch(shape):
        nd = len(shape)
        return pl.BlockSpec((1,) + shape, lambda i, c: (i,) + (0,) * nd)

    def const(shape):
        nd = len(shape)
        return pl.BlockSpec(shape, lambda i, c: (0,) * nd)

    return pl.pallas_call(
        functools.partial(_mlstm_kernel, L=L),
        grid=(b, nc),
        in_specs=[zspec(COL_MQ, D), zspec(COL_MK, D), zspec(COL_MV, D), zspec(COL_MO, D),
                  zspec(COL_SMALL, LANES), const((1, LANES)), const((CONV_W, 2 * D)),
                  const((1, 2 * D)), const((1, D)),
                  per_batch((M_HEADS, hd, hd)), per_batch((M_HEADS, hd)),
                  per_batch((1, M_HEADS)), per_batch((CONV_W - 1, 2 * D))],
        out_specs=[pl.BlockSpec((1, L, D), lambda i, c: (i, c, 0)),
                   per_batch((M_HEADS, hd, hd)), per_batch((M_HEADS, hd)),
                   per_batch((1, M_HEADS)), per_batch((CONV_W - 1, 2 * D))],
        out_shape=[jax.ShapeDtypeStruct((b, t, D), BF16),
                   jax.ShapeDtypeStruct((b, M_HEADS, hd, hd), F32),
                   jax.ShapeDtypeStruct((b, M_HEADS, hd), F32),
                   jax.ShapeDtypeStruct((b, 1, M_HEADS), F32),
                   jax.ShapeDtypeStruct((b, CONV_W - 1, 2 * D), F32)],
        scratch_shapes=[pltpu.VMEM((8 + L, 2 * D), F32)],
        compiler_params=_cparams(("parallel", "arbitrary")),
        name="mlstm",
    )(z3, z3, z3, z3, z3, bias_row, w_conv, b_conv, g_mnorm, c0, n0, m0, conv0)


def _eye_bf16(n):
    return (lax.broadcasted_iota(jnp.int32, (n, n), 0)
            == lax.broadcasted_iota(jnp.int32, (n, n), 1)).astype(BF16)


def _transpose_bf16(eye, a):
    return lax.dot_general(eye, a, (((1,), (1,)), ((), ())), preferred_element_type=F32)


def _frames_to_lanes(eye, a):
    if a.shape[0] % LANES == 0:
        return a.T.astype(BF16)
    return _transpose_bf16(eye, a.astype(BF16)).astype(BF16)


def _lanes_to_frames(eye, a):
    if a.shape[1] % LANES == 0:
        return a.T.astype(BF16)
    return _transpose_bf16(eye, a.astype(BF16)).astype(BF16)


def _qkv_kernel(x_ref, g_ref, w_ref, gq_ref, gk_ref, k32_ref, v32_ref, kb_ref, qt_ref, vt_ref):
    hd = D // A_HEADS
    x = x_ref[0]
    ms = jnp.mean(x * x, axis=-1, keepdims=True)
    xn = ((x * lax.rsqrt(ms + EPS)) * g_ref[...]).astype(BF16)
    eye = _eye_bf16(hd)

    def project(col):
        return jnp.dot(xn, w_ref[:, col:col + 2 * hd], preferred_element_type=F32)

    def head_norm(a, gain):
        return (a * lax.rsqrt(jnp.mean(a * a, axis=-1, keepdims=True) + EPS)) * gain

    for pair in range(0, D, 2 * hd):
        q2, k2, v2 = project(pair), project(D + pair), project(2 * D + pair)
        for half in range(2):
            cs = slice(half * hd, (half + 1) * hd)
            hs = slice(pair + half * hd, pair + (half + 1) * hd)
            qn = head_norm(q2[:, cs], gq_ref[...]) * Q_LOGIT_SCALE
            qt_ref[0, hs, :] = _frames_to_lanes(eye, qn)
            kn = head_norm(k2[:, cs], gk_ref[...])
            k32_ref[0, :, hs] = kn
            kb_ref[0, :, hs] = kn.astype(BF16)
            v32_ref[0, :, hs] = v2[:, cs]
            vt_ref[0, 0, hs, :] = _frames_to_lanes(eye, v2[:, cs])


def _qkv(x, g_norm, w_att, g_q, g_k, tm):
    b, t, _ = x.shape
    hd = D // A_HEADS
    row = pl.BlockSpec((1, tm, D), lambda i, r: (i, r, 0))
    gspec = pl.BlockSpec((1, hd), lambda i, r: (0, 0))
    return pl.pallas_call(
        _qkv_kernel,
        grid=(b, t // tm),
        in_specs=[row, pl.BlockSpec((1, D), lambda i, r: (0, 0)),
                  pl.BlockSpec((D, 3 * D), lambda i, r: (0, 0)), gspec, gspec],
        out_specs=[row, row, row, pl.BlockSpec((1, D, tm), lambda i, r: (i, 0, r)),
                   pl.BlockSpec((1, 1, D, tm), lambda i, r: (i, r, 0, 0))],
        out_shape=[jax.ShapeDtypeStruct((b, t, D), F32), jax.ShapeDtypeStruct((b, t, D), F32),
                   jax.ShapeDtypeStruct((b, t, D), BF16), jax.ShapeDtypeStruct((b, D, t), BF16),
                   jax.ShapeDtypeStruct((b, t // tm, D, tm), BF16)],
        compiler_params=_cparams(("parallel", "parallel")),
        name="qkv_proj_norm",
    )(x, g_norm, w_att, g_q, g_k)


def _num_key_tiles(qi, tq, tile, q_pos0, s_real):
    last_chunk = (q_pos0 + (qi + 1) * tq - 1) // CHUNK
    kend = jnp.minimum((last_chunk + 1) * CHUNK, s_real)
    return (kend + tile - 1) // tile


def _num_key_tiles_static(qi, tq, tile, q_pos0, s_real):
    last_chunk = (q_pos0 + (qi + 1) * tq - 1) // CHUNK
    return -(-min((last_chunk + 1) * CHUNK, s_real) // tile)


def _select_kernel(iq_ref, sm_ref, ki_ref, mask_ref, score_ref, iqt_ref, *,
                   tq, kb, s_pad, s_real, q_pos0, topk):
    qi = pl.program_id(1)
    nkt = _num_key_tiles(qi, tq, kb, q_pos0, s_real)
    kf = float(topk)
    groups = kb // SUBLANES

    pick = (lax.broadcasted_iota(jnp.int32, (IDX_HEADS, LANES), 1)
            == lax.broadcasted_iota(jnp.int32, (IDX_HEADS, LANES), 0) + SM_IW).astype(F32)
    w = lax.dot_general(pick, sm_ref[0], (((1,), (1,)), ((), ())), precision=HIGHEST,
                        preferred_element_type=F32) * ((IDX_HEADS * IDX_DIM) ** -0.5)
    eye = _eye_bf16(IDX_DIM)
    for h in range(IDX_HEADS):
        a = iq_ref[0, :, h * IDX_DIM:(h + 1) * IDX_DIM]
        hi = a.astype(BF16)
        lo = (a - hi.astype(F32)).astype(BF16)
        hi_t = _transpose_bf16(eye, hi).astype(BF16)
        lo_t = _transpose_bf16(eye, lo).astype(BF16)
        cols = slice((h % 2) * tq, (h % 2 + 1) * tq)
        iqt_ref[h // 2, 0:IDX_DIM, cols] = hi_t
        iqt_ref[h // 2, IDX_DIM:2 * IDX_DIM, cols] = hi_t
        iqt_ref[h // 2, 2 * IDX_DIM:3 * IDX_DIM, cols] = lo_t

    q_pos = q_pos0 + qi * tq + lax.broadcasted_iota(jnp.int32, (1, tq), 1)
    q_chunk = q_pos >> CHUNK_SHIFT
    first_key = jnp.minimum(((q_pos0 + qi * tq) >> CHUNK_SHIFT) << CHUNK_SHIFT, s_real)
    n_full = first_key // kb

    def score_tile(j, carry, masked):
        rmin, rmax = carry
        k0 = pl.multiple_of(j * kb, kb)
        kt = ki_ref[0, pl.ds(k0, kb), :]
        acc = None
        for p in range(IDX_HEADS // 2):
            rel = jnp.dot(kt, iqt_ref[p], preferred_element_type=F32)
            part = (w[2 * p:2 * p + 1] * jnp.maximum(rel[:, 0:tq], 0.0)
                    + w[2 * p + 1:2 * p + 2] * jnp.maximum(rel[:, tq:2 * tq], 0.0))
            acc = part if acc is None else acc + part
        if masked:
            key = k0 + lax.broadcasted_iota(jnp.int32, (kb, tq), 0)
            adm = jnp.logical_and((key >> CHUNK_SHIFT) <= q_chunk, key < s_real)
            low = jnp.where(adm, acc, jnp.inf)
            acc = jnp.where(adm, acc, -jnp.inf)
        else:
            low = acc
        score_ref[pl.ds(k0, kb), :] = acc
        rmin = jnp.minimum(rmin, jnp.min(low.reshape(groups, SUBLANES, tq), axis=0))
        rmax = jnp.maximum(rmax, jnp.max(acc.reshape(groups, SUBLANES, tq), axis=0))
        return rmin, rmax

    stats = (jnp.full((SUBLANES, tq), jnp.inf, F32), jnp.full((SUBLANES, tq), -jnp.inf, F32))
    stats = lax.fori_loop(0, n_full, functools.partial(score_tile, masked=False), stats)
    rmin8, rmax8 = lax.fori_loop(n_full, nkt, functools.partial(score_tile, masked=True), stats)
    rmin = jnp.min(rmin8, axis=0, keepdims=True)
    rmax = jnp.max(rmax8, axis=0, keepdims=True)

    def count(pred):
        def body(j, acc):
            for c in range(kb // COUNT_ROWS):
                k0 = pl.multiple_of(j * kb + c * COUNT_ROWS, COUNT_ROWS)
                hit = jnp.where(pred(score_ref[pl.ds(k0, COUNT_ROWS), :], k0), 1.0, 0.0)
                acc = acc + jnp.sum(hit.reshape(COUNT_ROWS // (COUNT_PARTS * SUBLANES), COUNT_PARTS,
                                                SUBLANES, tq), axis=0)
            return acc
        acc = lax.fori_loop(0, nkt, body, jnp.zeros((COUNT_PARTS, SUBLANES, tq), F32))
        return jnp.sum(jnp.sum(acc, axis=0), axis=0, keepdims=True)

    def below(cur):
        shape = (COUNT_ROWS // (COUNT_PARTS * SUBLANES), COUNT_PARTS, SUBLANES, tq)

        def body(j, carry):
            top, num = carry
            for c in range(kb // COUNT_ROWS):
                k0 = pl.multiple_of(j * kb + c * COUNT_ROWS, COUNT_ROWS)
                s = score_ref[pl.ds(k0, COUNT_ROWS), :]
                under = s < cur
                top = jnp.maximum(top, jnp.max(jnp.where(under, s, -jnp.inf).reshape(shape), axis=0))
                num = num + jnp.sum(jnp.where(under, 0.0, 1.0).reshape(shape), axis=0)
            return top, num

        top, num = lax.fori_loop(0, nkt, body, (jnp.full(shape[1:], -jnp.inf, F32),
                                                jnp.zeros(shape[1:], F32)))
        return (jnp.max(jnp.max(top, axis=0), axis=0, keepdims=True),
                jnp.sum(jnp.sum(num, axis=0), axis=0, keepdims=True))

    n_adm = jnp.minimum((q_chunk + 1) << CHUNK_SHIFT, s_real).astype(F32)

    def open_rows(cnt_lo):
        return cnt_lo > kf

    def wide_rows(st):
        return jnp.logical_and(open_rows(st[2]), st[2] - st[3] > FINISH_MAX)

    def flag(pred):
        return jnp.max(jnp.where(pred, 1.0, 0.0))

    def bisect(st):
        lo, hi, cnt_lo, cnt_hi = st
        active = open_rows(cnt_lo)
        mid = 0.5 * jnp.maximum(lo, rmin) + 0.5 * jnp.minimum(hi, rmax)
        cm = count(lambda s, k0: s >= mid)
        up = jnp.logical_and(active, cm >= kf)
        dn = jnp.logical_and(active, cm < kf)
        return (jnp.where(up, mid, lo), jnp.where(dn, mid, hi),
                jnp.where(up, cm, cnt_lo), jnp.where(dn, cm, cnt_hi))

    st = (jnp.full((1, tq), -F32_MAX, F32), jnp.full((1, tq), F32_MAX, F32), n_adm,
          jnp.zeros((1, tq), F32))
    st = lax.fori_loop(0, SELECT_MIN_ITERS, lambda _, s: bisect(s), st)

    def narrow(c):
        nxt = bisect(c[2])
        return c[0] + 1, flag(wide_rows(nxt)), nxt

    _, _, st = lax.while_loop(
        lambda c: jnp.logical_and(c[0] < SELECT_MAX_ITERS, c[1] > 0.0), narrow,
        (jnp.int32(SELECT_MIN_ITERS), flag(wide_rows(st)), st))

    lo, hi, cnt_lo, cnt_hi = st
    cand, _ = below(hi)

    def walk(c):
        _, walking, lo, hi, cnt_lo, cnt_hi, cand = c
        nxt, cge = below(cand)
        settle = jnp.logical_and(walking > 0.0, cge >= kf)
        move = jnp.logical_and(walking > 0.0, cge < kf)
        walking = jnp.where(move, 1.0, 0.0)
        return (jnp.max(walking), walking, jnp.where(settle, cand, lo), jnp.where(move, cand, hi),
                jnp.where(settle, cge, cnt_lo), jnp.where(move, cge, cnt_hi), jnp.where(move, nxt, cand))

    walking = jnp.where(open_rows(cnt_lo), 1.0, 0.0)
    _, _, lo, hi, cnt_lo, cnt_hi, _ = lax.while_loop(
        lambda c: c[0] > 0.0, walk, (jnp.max(walking), walking, lo, hi, cnt_lo, cnt_hi, cand))
    n_open = flag(open_rows(cnt_lo))

    def clear_tile(j, carry):
        k0 = pl.multiple_of(j * kb, kb)
        mask_ref[0, 0, pl.ds(k0, kb), :] = jnp.zeros((kb, tq), jnp.int8)
        return carry

    lax.fori_loop(nkt, s_pad // kb, clear_tile, 0)

    @pl.when(n_open == 0.0)
    def _():
        def write_tile(j, carry):
            k0 = pl.multiple_of(j * kb, kb)
            keep = score_ref[pl.ds(k0, kb), :] >= lo
            mask_ref[0, 0, pl.ds(k0, kb), :] = jnp.where(keep, 1, 0).astype(jnp.int8)
            return carry

        lax.fori_loop(0, nkt, write_tile, 0)

    @pl.when(n_open > 0.0)
    def _():
        tied = open_rows(cnt_lo)
        need = jnp.where(tied, kf - cnt_hi, 0.0)
        key_iota = lax.broadcasted_iota(jnp.int32, (kb, tq), 0)
        strip_iota = lax.broadcasted_iota(jnp.int32, (COUNT_ROWS, tq), 0)
        shape = (COUNT_ROWS // (COUNT_PARTS * SUBLANES), COUNT_PARTS, SUBLANES, tq)

        def next_tied(prev):
            def body(j, acc):
                for c in range(kb // COUNT_ROWS):
                    k0 = pl.multiple_of(j * kb + c * COUNT_ROWS, COUNT_ROWS)
                    idx = k0 + strip_iota
                    hit = jnp.logical_and(score_ref[pl.ds(k0, COUNT_ROWS), :] == lo, idx > prev)
                    acc = jnp.minimum(acc, jnp.min(jnp.where(hit, idx, s_pad).reshape(shape), axis=0))
                return acc
            acc = lax.fori_loop(0, nkt, body, jnp.full(shape[1:], s_pad, jnp.int32))
            return jnp.min(jnp.min(acc, axis=0), axis=0, keepdims=True)

        def cut_by_walk(_):
            def step(c):
                _, left, last = c
                take = left > 0.0
                last = jnp.where(take, next_tied(last), last)
                left = jnp.where(take, left - 1.0, left)
                return jnp.max(left), left, last
            _, _, last = lax.while_loop(lambda c: c[0] > 0.0, step,
                                        (jnp.max(need), need, jnp.full((1, tq), -1, jnp.int32)))
            return last + 1

        def cut_by_bisection(_):
            def step(_, st):
                jlo, jhi = st
                jmid = (jlo + jhi) >> 1
                cm = count(lambda s, k0: jnp.logical_and(s == lo, k0 + strip_iota < jmid))
                ok = cm >= need
                return jnp.where(ok, jlo, jmid), jnp.where(ok, jmid, jhi)
            steps = int(math.ceil(math.log2(s_pad))) + 1
            return lax.fori_loop(0, steps, step, (jnp.zeros((1, tq), jnp.int32),
                                                  jnp.full((1, tq), s_pad, jnp.int32)))[1]

        jcut = lax.cond(jnp.max(need) <= TIE_WALK_MAX, cut_by_walk, cut_by_bisection, 0)
        jcut = jnp.where(tied, jcut, s_pad)

        def write_tile(j, carry):
            k0 = pl.multiple_of(j * kb, kb)
            s = score_ref[pl.ds(k0, kb), :]
            keep = jnp.logical_or(s >= hi, jnp.logical_and(s >= lo, k0 + key_iota < jcut))
            mask_ref[0, 0, pl.ds(k0, kb), :] = jnp.where(keep, 1, 0).astype(jnp.int8)
            return carry

        lax.fori_loop(0, nkt, write_tile, 0)


def _select(z3, ki3, tq, kb, s_real, q_pos0, topk):
    b, t, _ = z3.shape
    s_pad = ki3.shape[1]
    iq_w = IDX_HEADS * IDX_DIM
    kern = functools.partial(_select_kernel, tq=tq, kb=kb, s_pad=s_pad, s_real=s_real,
                             q_pos0=q_pos0, topk=topk)
    return pl.pallas_call(
        kern,
        grid=(b, t // tq),
        in_specs=[pl.BlockSpec((1, tq, iq_w), lambda i, j: (i, j, COL_IQ // iq_w)),
                  pl.BlockSpec((1, tq, LANES), lambda i, j: (i, j, COL_SMALL // LANES)),
                  pl.BlockSpec((1, s_pad, 3 * IDX_DIM), lambda i, j: (i, 0, 0))],
        out_specs=pl.BlockSpec((1, 1, s_pad, tq), lambda i, j: (i, j, 0, 0)),
        out_shape=jax.ShapeDtypeStruct((b, t // tq, s_pad, tq), jnp.int8),
        scratch_shapes=[pltpu.VMEM((s_pad, tq), F32),
                        pltpu.VMEM((IDX_HEADS // 2, 3 * IDX_DIM, 2 * tq), BF16)],
        compiler_params=_cparams(("parallel", "arbitrary")),
        name="index_select",
    )(z3, z3, ki3)


def _attn_kernel(q_of_ref, k_of_ref, qt_ref, k_ref, vt_ref, *refs, n_mask):
    mask_refs = refs[:n_mask]
    o_ref, acc_ref, m_ref, l_ref, s_ref, bias_ref = refs[n_mask:]
    step = pl.program_id(1)
    hd = D // A_HEADS
    tq = qt_ref.shape[2]
    tk = k_ref.shape[1]
    tq_mask = tq // n_mask
    pv_rows = math.gcd(tk, PV_ROWS)
    first = k_of_ref[step] == 0
    nxt = jnp.minimum(step + 1, pl.num_programs(1) - 1)
    last = jnp.logical_or(step == pl.num_programs(1) - 1, q_of_ref[nxt] != q_of_ref[step])

    @pl.when(first)
    def _():
        acc_ref[...] = jnp.zeros(acc_ref.shape, F32)
        m_ref[...] = jnp.full(m_ref.shape, MASK_NEG, F32)
        l_ref[...] = jnp.zeros(l_ref.shape, F32)

    for n, mask_ref in enumerate(mask_refs):
        bias_ref[:, n * tq_mask:(n + 1) * tq_mask] = jnp.where(
            mask_ref[0, 0].astype(jnp.int32) != 0, 0.0, MASK_NEG)
    def logits(h):
        hs = slice(h * hd, (h + 1) * hd)
        top = None
        for r in range(0, tk, QK_ROWS):
            s = jnp.dot(k_ref[0, r:r + QK_ROWS, hs], qt_ref[0, hs, :],
                        preferred_element_type=F32) + bias_ref[r:r + QK_ROWS, :]
            s_ref[h, r:r + QK_ROWS, :] = s
            part = jnp.max(s.reshape(QK_ROWS // SUBLANES, SUBLANES, tq), axis=0)
            top = part if top is None else jnp.maximum(top, part)
        return jnp.max(top, axis=0, keepdims=True)

    def weigh(h, tile_max):
        hs = slice(h * hd, (h + 1) * hd)
        m_prev = m_ref[h:h + 1, :]
        m_new = jnp.maximum(m_prev, tile_max)
        alpha = jnp.exp2(m_prev - m_new)
        psum = None
        pv = None
        for r in range(0, tk, pv_rows):
            p = jnp.exp2(s_ref[h, r:r + pv_rows, :] - m_new)
            part = jnp.sum(p.reshape(pv_rows // SUBLANES, SUBLANES, tq), axis=0)
            psum = part if psum is None else psum + part
            prod = jnp.dot(vt_ref[0, 0, hs, r:r + pv_rows], p.astype(BF16),
                           preferred_element_type=F32)
            pv = prod if pv is None else pv + prod
        l_ref[h:h + 1, :] = alpha * l_ref[h:h + 1, :] + jnp.sum(psum, axis=0, keepdims=True)
        acc_ref[hs, :] = alpha * acc_ref[hs, :] + pv
        m_ref[h:h + 1, :] = m_new

    tile_max = [logits(h) for h in range(A_HEADS)]

    @pl.when(k_of_ref[step] >= 0)
    def _():
        for h in range(A_HEADS):
            weigh(h, tile_max[h])

    @pl.when(last)
    def _():
        eye = _eye_bf16(tq)
        for h in range(A_HEADS):
            hs = slice(h * hd, (h + 1) * hd)
            o_ref[0, :, hs] = _lanes_to_frames(eye, acc_ref[hs, :] / l_ref[h:h + 1, :])


def _attention(qt, kb, vt, mask_t, tq, tk, s_real, q_pos0):
    b, _, t = qt.shape
    n_mask = tq // mask_t.shape[3]
    steps = [(q, j) for q in range(t // tq) for j in range(_num_key_tiles_static(q, tq, tk, q_pos0, s_real))]
    q_of = jnp.array([q for q, _ in steps], jnp.int32)
    k_of = jnp.array([j for _, j in steps], jnp.int32)

    def mask_spec(n):
        return pl.BlockSpec((1, 1, tk, tq // n_mask),
                            lambda i, s, q_of, k_of: (i, q_of[s] * n_mask + n, k_of[s], 0))

    return pl.pallas_call(
        functools.partial(_attn_kernel, n_mask=n_mask),
        grid_spec=pltpu.PrefetchScalarGridSpec(
            num_scalar_prefetch=2,
            grid=(b, len(steps)),
            in_specs=[pl.BlockSpec((1, D, tq), lambda i, s, q_of, k_of: (i, 0, q_of[s])),
                      pl.BlockSpec((1, tk, D), lambda i, s, q_of, k_of: (i, k_of[s], 0)),
                      pl.BlockSpec((1, 1, D, tk), lambda i, s, q_of, k_of: (i, k_of[s], 0, 0))]
            + [mask_spec(n) for n in range(n_mask)],
            out_specs=pl.BlockSpec((1, tq, D), lambda i, s, q_of, k_of: (i, q_of[s], 0)),
            scratch_shapes=[pltpu.VMEM((D, tq), F32),
                            pltpu.VMEM((A_HEADS, tq), F32),
                            pltpu.VMEM((A_HEADS, tq), F32),
                            pltpu.VMEM((A_HEADS, tk, tq), F32),
                            pltpu.VMEM((tk, tq), F32)]),
        out_shape=jax.ShapeDtypeStruct((b, t, D), BF16),
        compiler_params=_cparams(("parallel", "arbitrary")),
        name="masked_attention",
    )(q_of, k_of, qt, kb, vt, *([mask_t] * n_mask))


def _attn_past_kernel(qt_ref, kn_ref, vtn_ref, kp_ref, vp_ref, mask_ref, o_ref):
    hd = D // A_HEADS
    t = qt_ref.shape[2]
    past = kp_ref.shape[1] // A_HEADS

    def cached(ref, h):
        return ref[0, pl.ds(h, past, stride=A_HEADS), :].astype(BF16)

    bias = jnp.where(mask_ref[0, 0].astype(jnp.int32) != 0, 0.0, MASK_NEG)
    bias_p, bias_n = bias[0:past], bias[past:past + t]
    eye_hd, eye_t = _eye_bf16(hd), _eye_bf16(t)
    for h in range(A_HEADS):
        hs = slice(h * hd, (h + 1) * hd)
        q = qt_ref[0, hs, :]
        s_p = jnp.dot(cached(kp_ref, h), q, preferred_element_type=F32) + bias_p
        s_n = jnp.dot(kn_ref[0, :, hs], q, preferred_element_type=F32) + bias_n
        m = jnp.maximum(jnp.max(s_p, axis=0, keepdims=True), jnp.max(s_n, axis=0, keepdims=True))
        p_p = jnp.exp2(s_p - m)
        p_n = jnp.exp2(s_n - m)
        l = jnp.sum(p_p, axis=0, keepdims=True) + jnp.sum(p_n, axis=0, keepdims=True)
        vt_p = _transpose_bf16(eye_hd, cached(vp_ref, h)).astype(BF16)
        out_t = (jnp.dot(vt_p, p_p.astype(BF16), preferred_element_type=F32)
                 + jnp.dot(vtn_ref[0, 0, hs, :], p_n.astype(BF16), preferred_element_type=F32))
        o_ref[0, :, hs] = _transpose_bf16(eye_t, (out_t / l).astype(BF16)).astype(BF16)


def _attention_with_past(qt, kb, vt, k_past, v_past, mask_t):
    b, _, t = qt.shape
    past, heads, hd = k_past.shape[1:]
    s_pad = mask_t.shape[2]
    k_past = k_past.reshape(b, past * heads, hd)
    v_past = v_past.reshape(b, past * heads, hd)
    cache = pl.BlockSpec((1, past * heads, hd), lambda i: (i, 0, 0))
    return pl.pallas_call(
        _attn_past_kernel,
        grid=(b,),
        in_specs=[pl.BlockSpec((1, D, t), lambda i: (i, 0, 0)),
                  pl.BlockSpec((1, t, D), lambda i: (i, 0, 0)),
                  pl.BlockSpec((1, 1, D, t), lambda i: (i, 0, 0, 0)),
                  cache, cache,
                  pl.BlockSpec((1, 1, s_pad, t), lambda i: (i, 0, 0, 0))],
        out_specs=pl.BlockSpec((1, t, D), lambda i: (i, 0, 0)),
        out_shape=jax.ShapeDtypeStruct((b, t, D), BF16),
        compiler_params=_cparams(("parallel",)),
        name="masked_attention_past",
    )(qt, kb, vt, k_past, v_past, mask_t)


def _merge_kernel(x_ref, ya_ref, yb_ref, ga_ref, gb_ref, wa_ref, wb_ref, wo_ref, o_ref):
    a = jnp.dot(ya_ref[...], wa_ref[...], preferred_element_type=F32)
    bb = jnp.dot(yb_ref[...], wb_ref[...], preferred_element_type=F32)
    mix = jax.nn.sigmoid(ga_ref[...]) * a + jax.nn.sigmoid(gb_ref[...]) * bb
    o_ref[...] = x_ref[...] + jnp.dot(mix.astype(BF16), wo_ref[...], preferred_element_type=F32)


def _merge(x, ya, yb, z, wa, wb, wo, tm):
    n = x.shape[0]
    row = pl.BlockSpec((tm, D), lambda i: (i, 0))
    wspec = pl.BlockSpec((D, D), lambda i: (0, 0))
    return pl.pallas_call(
        _merge_kernel,
        grid=(n // tm,),
        in_specs=[row, row, row,
                  pl.BlockSpec((tm, D), lambda i: (i, COL_GA // D)),
                  pl.BlockSpec((tm, D), lambda i: (i, COL_GB // D)),
                  wspec, wspec, wspec],
        out_specs=row,
        out_shape=jax.ShapeDtypeStruct((n, D), F32),
        compiler_params=_cparams(("parallel",)),
        name="merge_out_proj",
    )(x, ya, yb, z, z, wa, wb, wo)


def _ffn_kernel(x_ref, g_ref, wg_ref, wu_ref, wd_ref, o_ref, xn_ref, acc_ref):
    c = pl.program_id(1)

    @pl.when(c == 0)
    def _():
        x = x_ref[...]
        ms = jnp.mean(x * x, axis=-1, keepdims=True)
        xn_ref[...] = ((x * lax.rsqrt(ms + EPS)) * g_ref[...]).astype(BF16)
        acc_ref[...] = x

    xn = xn_ref[...]
    gate = jnp.dot(xn, wg_ref[...], preferred_element_type=F32)
    up = jnp.dot(xn, wu_ref[...], preferred_element_type=F32)
    act = (gate * jax.nn.sigmoid(gate)) * up
    acc_ref[...] += jnp.dot(act.astype(BF16), wd_ref[...], preferred_element_type=F32)

    @pl.when(c == pl.num_programs(1) - 1)
    def _():
        o_ref[...] = acc_ref[...]


def _ffn(x, g, w_in, w_out, tm, tf):
    n = x.shape[0]
    dff = w_out.shape[0]
    nf = dff // tf
    row = pl.BlockSpec((tm, D), lambda i, c: (i, 0))
    return pl.pallas_call(
        _ffn_kernel,
        grid=(n // tm, nf),
        in_specs=[row, pl.BlockSpec((1, D), lambda i, c: (0, 0)),
                  pl.BlockSpec((D, tf), lambda i, c: (0, c)),
                  pl.BlockSpec((D, tf), lambda i, c: (0, nf + c)),
                  pl.BlockSpec((tf, D), lambda i, c: (c, 0))],
        out_specs=row,
        out_shape=jax.ShapeDtypeStruct((n, D), F32),
        scratch_shapes=[pltpu.VMEM((tm, D), BF16), pltpu.VMEM((tm, D), F32)],
        compiler_params=_cparams(("parallel", "arbitrary")),
        name="swiglu_ffn",
    )(x, g, w_in, w_in, w_out)


def _split_hi_lo(a):
    hi = a.astype(BF16)
    lo = (a - hi.astype(F32)).astype(BF16)
    return hi, lo


def _layer(x, past, p, cfg):
    k_past, v_past, ki_past, c0, n0, m0, conv0 = past
    b, t, _ = x.shape
    n_tok = b * t
    past_len = k_past.shape[1]
    s_real = past_len + t
    topk = min(TOPK_MAX, s_real // 4)
    tq_sel, kb_sel, tq_att, tk_att = cfg["tq_sel"], cfg["kb_sel"], cfg["tq_att"], cfg["tk_att"]
    s_pad = -(-s_real // tk_att) * tk_att

    x2 = x.reshape(n_tok, D)
    z = _norm_matmul(x2, p["g_norm1"], p["w_in"], cfg["tm_proj"], cfg["tn_proj"])
    z3 = z.reshape(b, t, D_Z)

    y_a, c_new, n_new, m_new, conv_new = _mlstm(
        z3, p["gate_bias"], p["w_conv"], p["b_conv"], p["g_mnorm"], c0, n0,
        m0.reshape(b, 1, M_HEADS), conv0, cfg["mlstm_chunk"])

    k32, v32, kb, qt, vt = _qkv(x, p["g_norm1"], p["w_att"], p["g_q"], p["g_k"], cfg["tm_qkv"])
    ik = z3[:, :, COL_SMALL:COL_SMALL + IDX_DIM]
    ki_all = ik if past_len == 0 else jnp.concatenate([ki_past.astype(F32), ik], axis=1)
    k_hi, k_lo = _split_hi_lo(ki_all)
    ki3 = jnp.concatenate([k_hi, k_lo, k_hi], axis=-1)
    ki3 = jnp.pad(ki3, ((0, 0), (0, s_pad - s_real), (0, 0)))
    mask_t = _select(z3, ki3, tq_sel, kb_sel, s_real, past_len, topk)
    if past_len == 0:
        assert cfg["tm_qkv"] == tk_att and s_pad == s_real
        y_b = _attention(qt, kb, vt, mask_t, tq_att, tk_att, s_real, past_len)
    else:
        assert cfg["tm_qkv"] == t == tq_sel
        y_b = _attention_with_past(qt, kb, vt, k_past, v_past, mask_t)

    x1 = _merge(x2, y_a.reshape(n_tok, D), y_b.reshape(n_tok, D), z,
                p["w_a_out"], p["w_b_out"], p["w_o"], cfg["tm_rows"])
    y = _ffn(x1, p["g_norm2"], p["w_ffn_in"], p["w_ffn_out"], cfg["tm_ffn"], cfg["tf_ffn"])

    hd = D // A_HEADS
    return y.reshape(b, t, D), (k32.reshape(b, t, A_HEADS, hd), v32.reshape(b, t, A_HEADS, hd), ik,
                                c_new, n_new, m_new.reshape(b, M_HEADS), conv_new)


def _prep_params(g_norm1, w_in, b_if, w_conv, b_conv, g_mnorm, g_q, g_k, w_a_out, w_b_out, w_o,
                 g_norm2, w_ffn_in, w_ffn_out):
    o_mi = 4 * D
    o_aq = o_mi + 2 * M_HEADS
    o_iq = o_aq + 3 * D
    o_ik = o_iq + IDX_HEADS * IDX_DIM
    o_iw = o_ik + IDX_DIM
    o_ga = o_iw + IDX_HEADS
    w_perm = jnp.concatenate([
        w_in[:, 0:o_mi], w_in[:, o_ga:o_ga + 2 * D], w_in[:, o_iq:o_ik],
        w_in[:, o_ik:o_ga], w_in[:, o_mi:o_aq],
        jnp.zeros((D, D_Z - COL_SMALL - (SM_MF + M_HEADS)), w_in.dtype)], axis=1).astype(BF16)
    gate_bias = jnp.zeros((1, LANES), F32).at[0, SM_MI:SM_MI + 2 * M_HEADS].set(b_if.astype(F32))
    return {
        "g_norm1": g_norm1.reshape(1, D), "w_in": w_perm, "w_att": w_in[:, o_aq:o_iq].astype(BF16),
        "gate_bias": gate_bias,
        "w_conv": w_conv, "b_conv": b_conv.reshape(1, 2 * D), "g_mnorm": g_mnorm.reshape(1, D),
        "g_q": g_q.reshape(1, -1), "g_k": g_k.reshape(1, -1),
        "w_a_out": w_a_out.astype(BF16), "w_b_out": w_b_out.astype(BF16), "w_o": w_o.astype(BF16),
        "g_norm2": g_norm2.reshape(1, D), "w_ffn_in": w_ffn_in.astype(BF16),
        "w_ffn_out": w_ffn_out.astype(BF16),
    }


def _config(b, t, past_len):
    n_tok = b * t
    s_real = past_len + t
    tm = min(2048, n_tok)
    tq_sel = min(256, t)
    tq_att = min(512, t)
    if s_real % 512 == 0:
        kb_sel = tk_att = 512
    else:
        kb_sel = tk_att = -(-s_real // LANES) * LANES
    return {"tm_proj": tm, "tn_proj": 896, "mlstm_chunk": min(128, t), "tm_rows": min(512, n_tok),
            "tm_qkv": min(512, t), "tq_sel": tq_sel, "kb_sel": kb_sel, "tq_att": tq_att,
            "tk_att": tk_att, "tm_ffn": min(512, n_tok), "tf_ffn": 1408}


def kernel(x_prompt, x_sample, cache_k, cache_v, cache_kidx, state_C, state_n, state_m, state_conv,
           g_norm1, w_in, b_if, w_conv, b_conv, g_mnorm, g_q, g_k, w_a_out, w_b_out, w_o,
           g_norm2, w_ffn_in, w_ffn_out):
    depth = w_in.shape[0]
    bp = x_prompt.shape[0]
    hd_a = D // A_HEADS
    hd_m = D // M_HEADS
    yp, ys = x_prompt, x_sample
    new_p, new_s = [], []
    for l in range(depth):
        p = _prep_params(g_norm1[l], w_in[l], b_if[l], w_conv[l], b_conv[l], g_mnorm[l], g_q[l],
                         g_k[l], w_a_out[l], w_b_out[l], w_o[l], g_norm2[l], w_ffn_in[l],
                         w_ffn_out[l])
        empty = (jnp.zeros((bp, 0, A_HEADS, hd_a), F32), jnp.zeros((bp, 0, A_HEADS, hd_a), F32),
                 jnp.zeros((bp, 0, IDX_DIM), F32), jnp.zeros((bp, M_HEADS, hd_m, hd_m), F32),
                 jnp.zeros((bp, M_HEADS, hd_m), F32), jnp.zeros((bp, M_HEADS), F32),
                 jnp.zeros((bp, CONV_W - 1, 2 * D), F32))
        yp, sp = _layer(yp, empty, p, _config(bp, yp.shape[1], 0))
        ys, ss = _layer(ys, (cache_k[l], cache_v[l], cache_kidx[l], state_C[l], state_n[l],
                             state_m[l], state_conv[l]), p,
                        _config(ys.shape[0], ys.shape[1], cache_k.shape[2]))
        new_p.append(sp)
        new_s.append(ss)

    def stk(lst, i):
        return jnp.stack([s[i] for s in lst])

    return (yp, ys) + tuple(stk(new_p, i) for i in range(7)) + tuple(stk(new_s, i) for i in range(7))
```

```python
import functools
import math

import jax
import jax.numpy as jnp
from jax import lax
from jax.experimental import pallas as pl
from jax.experimental.pallas import tpu as pltpu

F32 = jnp.float32
BF16 = jnp.bfloat16
HIGHEST = lax.Precision.HIGHEST

EPS = 1e-6
CHUNK = 64
CHUNK_SHIFT = 6
M_HEADS = 4
A_HEADS = 8
IDX_HEADS = 8
IDX_DIM = 64
CONV_W = 4
TOPK_MAX = 256
LANES = 128
SUBLANES = 8
MASK_NEG = -1e30
F32_MAX = float(jnp.finfo(jnp.float32).max)
VMEM_LIMIT = 48 * 1024 * 1024
SELECT_MIN_ITERS = 14
SELECT_MAX_ITERS = 16
FINISH_MAX = 4
QK_ROWS = 128
PV_ROWS = 256
HEAD_LAG = 8
TIE_WALK_MAX = 8
COUNT_PARTS = 4
COUNT_ROWS = 128

D = 1024
COL_MQ, COL_MK, COL_MV, COL_MO = 0, 1024, 2048, 3072
COL_GA, COL_GB = 4096, 5120
COL_IQ = 6144
COL_SMALL = 6656
D_Z = 7168
SM_IW, SM_MI, SM_MF = 64, 72, 76
Q_LOGIT_SCALE = math.log2(math.e) * (D // A_HEADS) ** -0.5


def _cparams(sem):
    return pltpu.CompilerParams(dimension_semantics=sem, vmem_limit_bytes=VMEM_LIMIT)


def _norm_matmul_kernel(x_ref, g_ref, w_ref, o_ref, xn_ref):
    @pl.when(pl.program_id(1) == 0)
    def _():
        x = x_ref[...]
        ms = jnp.mean(x * x, axis=-1, keepdims=True)
        xn_ref[...] = ((x * lax.rsqrt(ms + EPS)) * g_ref[...]).astype(BF16)

    o_ref[...] = jnp.dot(xn_ref[...], w_ref[...], preferred_element_type=F32)


def _norm_matmul(x, g, w, tm, tn):
    n, d = x.shape
    nout = w.shape[1]
    return pl.pallas_call(
        _norm_matmul_kernel,
        grid=(n // tm, nout // tn),
        in_specs=[pl.BlockSpec((tm, d), lambda i, j: (i, 0)),
                  pl.BlockSpec((1, d), lambda i, j: (0, 0)),
                  pl.BlockSpec((d, tn), lambda i, j: (0, j))],
        out_specs=pl.BlockSpec((tm, tn), lambda i, j: (i, j)),
        out_shape=jax.ShapeDtypeStruct((n, nout), F32),
        scratch_shapes=[pltpu.VMEM((tm, d), BF16)],
        compiler_params=_cparams(("parallel", "arbitrary")),
        name="norm_in_proj",
    )(x, g, w)


def _mlstm_kernel(mq_ref, mk_ref, mv_ref, mo_ref, sm_ref, bias_ref, wconv_ref, bconv_ref,
                  gm_ref, c0_ref, n0_ref, m0_ref, conv0_ref,
                  y_ref, c_ref, n_ref, m_ref, conv_ref, cbuf_ref, *, L):
    c = pl.program_id(1)
    hd = D // M_HEADS

    @pl.when(c == 0)
    def _():
        c_ref[...] = c0_ref[...]
        n_ref[...] = n0_ref[...]
        m_ref[...] = m0_ref[...]
        cbuf_ref[8 - (CONV_W - 1):8, :] = conv0_ref[0]

    cbuf_ref[8:8 + L, 0:D] = mq_ref[0]
    cbuf_ref[8:8 + L, D:2 * D] = mk_ref[0]
    wc = wconv_ref[...]
    qk = bconv_ref[...] + cbuf_ref[5:5 + L, :] * wc[0:1, :]
    for j in range(1, CONV_W):
        qk = qk + cbuf_ref[5 + j:5 + j + L, :] * wc[j:j + 1, :]
    tail = cbuf_ref[5 + L:8 + L, :]
    cbuf_ref[5:8, :] = tail
    conv_ref[0] = tail
    qk = qk * jax.nn.sigmoid(qk)

    g_all = sm_ref[0] + bias_ref[...]
    lf_all = jnp.minimum(g_all, 0.0) - jnp.log1p(jnp.exp(-jnp.abs(g_all)))
    lane = lax.broadcasted_iota(jnp.int32, (L, LANES), 1)
    gates = jnp.where(lane >= SM_MF, lf_all, g_all)
    r_i = lax.broadcasted_iota(jnp.int32, (L, L), 0)
    c_i = lax.broadcasted_iota(jnp.int32, (L, L), 1)
    tril = (c_i <= r_i).astype(F32)
    triu = (r_i <= c_i).astype(F32)
    b_col_all = jnp.dot(tril, lf_all, precision=HIGHEST, preferred_element_type=F32)
    sel = (lax.broadcasted_iota(jnp.int32, (8, LANES), 1)
           == lax.broadcasted_iota(jnp.int32, (8, LANES), 0) + SM_MI).astype(F32)
    rows = lax.dot_general(sel, gates, (((1,), (1,)), ((), ())), precision=HIGHEST,
                           preferred_element_type=F32)
    b_row_all = jnp.dot(rows, triu, precision=HIGHEST, preferred_element_type=F32)
    causal = c_i <= r_i

    mo = mo_ref[0]
    mv = mv_ref[0]
    for h in range(M_HEADS):
        hs = slice(h * hd, (h + 1) * hd)
        qh = qk[:, hs]
        kh = qk[:, D + h * hd:D + (h + 1) * hd] * (hd ** -0.5)
        vh = mv[:, hs]
        qb, kb, vb = qh.astype(BF16), kh.astype(BF16), vh.astype(BF16)
        b_col = b_col_all[:, SM_MF + h:SM_MF + h + 1]
        i_col = gates[:, SM_MI + h:SM_MI + h + 1]
        b_row = b_row_all[M_HEADS + h:M_HEADS + h + 1, :]
        i_row = rows[h:h + 1, :]
        m_prev = m_ref[0, :, h:h + 1]
        c_prev = c_ref[0, h]
        n_prev = n_ref[0, h:h + 1, :]

        dmat = jnp.where(causal, b_col - b_row + i_row, -jnp.inf)
        inter = b_col + m_prev
        m_t = jnp.maximum(inter, jnp.max(dmat, axis=-1, keepdims=True))
        w_intra = jnp.exp(dmat - m_t)
        w_inter = jnp.exp(inter - m_t)
        s = lax.dot_general(qb, kb, (((1,), (1,)), ((), ())), preferred_element_type=F32) * w_intra
        qc = lax.dot_general(qb, c_prev.astype(BF16), (((1,), (1,)), ((), ())),
                             preferred_element_type=F32)
        num = jnp.dot(s.astype(BF16), vb, preferred_element_type=F32) + w_inter * qc
        den = jnp.sum(s, axis=-1, keepdims=True) + w_inter * jnp.sum(qh * n_prev, axis=-1, keepdims=True)
        denom = jnp.maximum(jnp.abs(den), jnp.exp(-m_t))
        hh = num / denom

        m_new = m_t[L - 1:L, :]
        b_last = b_col[L - 1:L, :]
        g_col = jnp.exp(b_last - b_col + i_col - m_new)
        decay = jnp.exp(b_last + m_prev - m_new)
        gv = (g_col * vh).astype(BF16)
        c_ref[0, h] = decay * c_prev + lax.dot_general(
            gv, kb, (((0,), (0,)), ((), ())), preferred_element_type=F32)
        n_ref[0, h:h + 1, :] = decay * n_prev + jnp.sum(g_col * kh, axis=0, keepdims=True)
        m_ref[0, :, h:h + 1] = m_new

        hn = hh * lax.rsqrt(jnp.mean(hh * hh, axis=-1, keepdims=True) + EPS) * gm_ref[:, hs]
        y_ref[0, :, hs] = (hn * jax.nn.sigmoid(mo[:, hs])).astype(BF16)


def _mlstm(z3, bias_row, w_conv, b_conv, g_mnorm, c0, n0, m0, conv0, L):
    b, t, _ = z3.shape
    nc = t // L
    hd = D // M_HEADS

    def zspec(col, width):
        return pl.BlockSpec((1, L, width), lambda i, c: (i, c, col // width))

    def per_batch(shape):
        nd = len(shape)
        return pl.BlockSpec((1,) + shape, lambda i, c: (i,) + (0,) * nd)

    def const(shape):
        nd = len(shape)
        return pl.BlockSpec(shape, lambda i, c: (0,) * nd)

    return pl.pallas_call(
        functools.partial(_mlstm_kernel, L=L),
        grid=(b, nc),
        in_specs=[zspec(COL_MQ, D), zspec(COL_MK, D), zspec(COL_MV, D), zspec(COL_MO, D),
                  zspec(COL_SMALL, LANES), const((1, LANES)), const((CONV_W, 2 * D)),
                  const((1, 2 * D)), const((1, D)),
                  per_batch((M_HEADS, hd, hd)), per_batch((M_HEADS, hd)),
                  per_batch((1, M_HEADS)), per_batch((CONV_W - 1, 2 * D))],
        out_specs=[pl.BlockSpec((1, L, D), lambda i, c: (i, c, 0)),
                   per_batch((M_HEADS, hd, hd)), per_batch((M_HEADS, hd)),
                   per_batch((1, M_HEADS)), per_batch((CONV_W - 1, 2 * D))],
        out_shape=[jax.ShapeDtypeStruct((b, t, D), BF16),
                   jax.ShapeDtypeStruct((b, M_HEADS, hd, hd), F32),
                   jax.ShapeDtypeStruct((b, M_HEADS, hd), F32),
                   jax.ShapeDtypeStruct((b, 1, M_HEADS), F32),
                   jax.ShapeDtypeStruct((b, CONV_W - 1, 2 * D), F32)],
        scratch_shapes=[pltpu.VMEM((8 + L, 2 * D), F32)],
        compiler_params=_cparams(("parallel", "arbitrary")),
        name="mlstm",
    )(z3, z3, z3, z3, z3, bias_row, w_conv, b_conv, g_mnorm, c0, n0, m0, conv0)


def _eye_bf16(n):
    return (lax.broadcasted_iota(jnp.int32, (n, n), 0)
            == lax.broadcasted_iota(jnp.int32, (n, n), 1)).astype(BF16)


def _transpose_bf16(eye, a):
    return lax.dot_general(eye, a, (((1,), (1,)), ((), ())), preferred_element_type=F32)


def _frames_to_lanes(eye, a):
    if a.shape[0] % LANES == 0:
        return a.T.astype(BF16)
    return _transpose_bf16(eye, a.astype(BF16)).astype(BF16)


def _lanes_to_frames(eye, a):
    if a.shape[1] % LANES == 0:
        return a.T.astype(BF16)
    return _transpose_bf16(eye, a.astype(BF16)).astype(BF16)


def _qkv_kernel(x_ref, g_ref, w_ref, gq_ref, gk_ref, k32_ref, v32_ref, kb_ref, qt_ref, vt_ref):
    hd = D // A_HEADS
    x = x_ref[0]
    ms = jnp.mean(x * x, axis=-1, keepdims=True)
    xn = ((x * lax.rsqrt(ms + EPS)) * g_ref[...]).astype(BF16)
    eye = _eye_bf16(hd)

    def project(col):
        return jnp.dot(xn, w_ref[:, col:col + 2 * hd], preferred_element_type=F32)

    def head_norm(a, gain):
        return (a * lax.rsqrt(jnp.mean(a * a, axis=-1, keepdims=True) + EPS)) * gain

    for pair in range(0, D, 2 * hd):
        q2, k2, v2 = project(pair), project(D + pair), project(2 * D + pair)
        for half in range(2):
            cs = slice(half * hd, (half + 1) * hd)
            hs = slice(pair + half * hd, pair + (half + 1) * hd)
            qn = head_norm(q2[:, cs], gq_ref[...]) * Q_LOGIT_SCALE
            qt_ref[0, hs, :] = _frames_to_lanes(eye, qn)
            kn = head_norm(k2[:, cs], gk_ref[...])
            k32_ref[0, :, hs] = kn
            kb_ref[0, :, hs] = kn.astype(BF16)
            v32_ref[0, :, hs] = v2[:, cs]
            vt_ref[0, 0, hs, :] = _frames_to_lanes(eye, v2[:, cs])


def _qkv(x, g_norm, w_att, g_q, g_k, tm):
    b, t, _ = x.shape
    hd = D // A_HEADS
    row = pl.BlockSpec((1, tm, D), lambda i, r: (i, r, 0))
    gspec = pl.BlockSpec((1, hd), lambda i, r: (0, 0))
    return pl.pallas_call(
        _qkv_kernel,
        grid=(b, t // tm),
        in_specs=[row, pl.BlockSpec((1, D), lambda i, r: (0, 0)),
                  pl.BlockSpec((D, 3 * D), lambda i, r: (0, 0)), gspec, gspec],
        out_specs=[row, row, row, pl.BlockSpec((1, D, tm), lambda i, r: (i, 0, r)),
                   pl.BlockSpec((1, 1, D, tm), lambda i, r: (i, r, 0, 0))],
        out_shape=[jax.ShapeDtypeStruct((b, t, D), F32), jax.ShapeDtypeStruct((b, t, D), F32),
                   jax.ShapeDtypeStruct((b, t, D), BF16), jax.ShapeDtypeStruct((b, D, t), BF16),
                   jax.ShapeDtypeStruct((b, t // tm, D, tm), BF16)],
        compiler_params=_cparams(("parallel", "parallel")),
        name="qkv_proj_norm",
    )(x, g_norm, w_att, g_q, g_k)


def _num_key_tiles(qi, tq, tile, q_pos0, s_real):
    last_chunk = (q_pos0 + (qi + 1) * tq - 1) // CHUNK
    kend = jnp.minimum((last_chunk + 1) * CHUNK, s_real)
    return (kend + tile - 1) // tile


def _num_key_tiles_static(qi, tq, tile, q_pos0, s_real):
    last_chunk = (q_pos0 + (qi + 1) * tq - 1) // CHUNK
    return -(-min((last_chunk + 1) * CHUNK, s_real) // tile)


def _select_kernel(iq_ref, sm_ref, ki_ref, mask_ref, score_ref, iqt_ref, *,
                   tq, kb, s_pad, s_real, q_pos0, topk):
    qi = pl.program_id(1)
    nkt = _num_key_tiles(qi, tq, kb, q_pos0, s_real)
    kf = float(topk)
    groups = kb // SUBLANES

    pick = (lax.broadcasted_iota(jnp.int32, (IDX_HEADS, LANES), 1)
            == lax.broadcasted_iota(jnp.int32, (IDX_HEADS, LANES), 0) + SM_IW).astype(F32)
    w = lax.dot_general(pick, sm_ref[0], (((1,), (1,)), ((), ())), precision=HIGHEST,
                        preferred_element_type=F32) * ((IDX_HEADS * IDX_DIM) ** -0.5)
    eye = _eye_bf16(IDX_DIM)
    for h in range(IDX_HEADS):
        a = iq_ref[0, :, h * IDX_DIM:(h + 1) * IDX_DIM]
        hi = a.astype(BF16)
        lo = (a - hi.astype(F32)).astype(BF16)
        hi_t = _transpose_bf16(eye, hi).astype(BF16)
        lo_t = _transpose_bf16(eye, lo).astype(BF16)
        cols = slice((h % 2) * tq, (h % 2 + 1) * tq)
        iqt_ref[h // 2, 0:IDX_DIM, cols] = hi_t
        iqt_ref[h // 2, IDX_DIM:2 * IDX_DIM, cols] = hi_t
        iqt_ref[h // 2, 2 * IDX_DIM:3 * IDX_DIM, cols] = lo_t

    q_pos = q_pos0 + qi * tq + lax.broadcasted_iota(jnp.int32, (1, tq), 1)
    q_chunk = q_pos >> CHUNK_SHIFT
    first_key = jnp.minimum(((q_pos0 + qi * tq) >> CHUNK_SHIFT) << CHUNK_SHIFT, s_real)
    n_full = first_key // kb

    def score_tile(j, carry, masked):
        rmin, rmax = carry
        k0 = pl.multiple_of(j * kb, kb)
        kt = ki_ref[0, pl.ds(k0, kb), :]
        acc = None
        for p in range(IDX_HEADS // 2):
            rel = jnp.dot(kt, iqt_ref[p], preferred_element_type=F32)
            part = (w[2 * p:2 * p + 1] * jnp.maximum(rel[:, 0:tq], 0.0)
                    + w[2 * p + 1:2 * p + 2] * jnp.maximum(rel[:, tq:2 * tq], 0.0))
            acc = part if acc is None else acc + part
        if masked:
            key = k0 + lax.broadcasted_iota(jnp.int32, (kb, tq), 0)
            adm = jnp.logical_and((key >> CHUNK_SHIFT) <= q_chunk, key < s_real)
            low = jnp.where(adm, acc, jnp.inf)
            acc = jnp.where(adm, acc, -jnp.inf)
        else:
            low = acc
        score_ref[pl.ds(k0, kb), :] = acc
        rmin = jnp.minimum(rmin, jnp.min(low.reshape(groups, SUBLANES, tq), axis=0))
        rmax = jnp.maximum(rmax, jnp.max(acc.reshape(groups, SUBLANES, tq), axis=0))
        return rmin, rmax

    stats = (jnp.full((SUBLANES, tq), jnp.inf, F32), jnp.full((SUBLANES, tq), -jnp.inf, F32))
    stats = lax.fori_loop(0, n_full, functools.partial(score_tile, masked=False), stats)
    rmin8, rmax8 = lax.fori_loop(n_full, nkt, functools.partial(score_tile, masked=True), stats)
    rmin = jnp.min(rmin8, axis=0, keepdims=True)
    rmax = jnp.max(rmax8, axis=0, keepdims=True)

    def count(pred):
        def body(j, acc):
            for c in range(kb // COUNT_ROWS):
                k0 = pl.multiple_of(j * kb + c * COUNT_ROWS, COUNT_ROWS)
                hit = jnp.where(pred(score_ref[pl.ds(k0, COUNT_ROWS), :], k0), 1.0, 0.0)
                acc = acc + jnp.sum(hit.reshape(COUNT_ROWS // (COUNT_PARTS * SUBLANES), COUNT_PARTS,
                                                SUBLANES, tq), axis=0)
            return acc
        acc = lax.fori_loop(0, nkt, body, jnp.zeros((COUNT_PARTS, SUBLANES, tq), F32))
        return jnp.sum(jnp.sum(acc, axis=0), axis=0, keepdims=True)

    def below(cur):
        shape = (COUNT_ROWS // (COUNT_PARTS * SUBLANES), COUNT_PARTS, SUBLANES, tq)

        def body(j, carry):
            top, num = carry
            for c in range(kb // COUNT_ROWS):
                k0 = pl.multiple_of(j * kb + c * COUNT_ROWS, COUNT_ROWS)
                s = score_ref[pl.ds(k0, COUNT_ROWS), :]
                under = s < cur
                top = jnp.maximum(top, jnp.max(jnp.where(under, s, -jnp.inf).reshape(shape), axis=0))
                num = num + jnp.sum(jnp.where(under, 0.0, 1.0).reshape(shape), axis=0)
            return top, num

        top, num = lax.fori_loop(0, nkt, body, (jnp.full(shape[1:], -jnp.inf, F32),
                                                jnp.zeros(shape[1:], F32)))
        return (jnp.max(jnp.max(top, axis=0), axis=0, keepdims=True),
                jnp.sum(jnp.sum(num, axis=0), axis=0, keepdims=True))

    n_adm = jnp.minimum((q_chunk + 1) << CHUNK_SHIFT, s_real).astype(F32)

    def open_rows(cnt_lo):
        return cnt_lo > kf

    def wide_rows(st):
        return jnp.logical_and(open_rows(st[2]), st[2] - st[3] > FINISH_MAX)

    def flag(pred):
        return jnp.max(jnp.where(pred, 1.0, 0.0))

    def bisect(st):
        lo, hi, cnt_lo, cnt_hi = st
        active = open_rows(cnt_lo)
        mid = 0.5 * jnp.maximum(lo, rmin) + 0.5 * jnp.minimum(hi, rmax)
        cm = count(lambda s, k0: s >= mid)
        up = jnp.logical_and(active, cm >= kf)
        dn = jnp.logical_and(active, cm < kf)
        return (jnp.where(up, mid, lo), jnp.where(dn, mid, hi),
                jnp.where(up, cm, cnt_lo), jnp.where(dn, cm, cnt_hi))

    st = (jnp.full((1, tq), -F32_MAX, F32), jnp.full((1, tq), F32_MAX, F32), n_adm,
          jnp.zeros((1, tq), F32))
    st = lax.fori_loop(0, SELECT_MIN_ITERS, lambda _, s: bisect(s), st)

    def narrow(c):
        nxt = bisect(c[2])
        return c[0] + 1, flag(wide_rows(nxt)), nxt

    _, _, st = lax.while_loop(
        lambda c: jnp.logical_and(c[0] < SELECT_MAX_ITERS, c[1] > 0.0), narrow,
        (jnp.int32(SELECT_MIN_ITERS), flag(wide_rows(st)), st))

    lo, hi, cnt_lo, cnt_hi = st
    cand, _ = below(hi)

    def walk(c):
        _, walking, lo, hi, cnt_lo, cnt_hi, cand = c
        nxt, cge = below(cand)
        settle = jnp.logical_and(walking > 0.0, cge >= kf)
        move = jnp.logical_and(walking > 0.0, cge < kf)
        walking = jnp.where(move, 1.0, 0.0)
        return (jnp.max(walking), walking, jnp.where(settle, cand, lo), jnp.where(move, cand, hi),
                jnp.where(settle, cge, cnt_lo), jnp.where(move, cge, cnt_hi), jnp.where(move, nxt, cand))

    walking = jnp.where(open_rows(cnt_lo), 1.0, 0.0)
    _, _, lo, hi, cnt_lo, cnt_hi, _ = lax.while_loop(
        lambda c: c[0] > 0.0, walk, (jnp.max(walking), walking, lo, hi, cnt_lo, cnt_hi, cand))
    n_open = flag(open_rows(cnt_lo))

    def clear_tile(j, carry):
        k0 = pl.multiple_of(j * kb, kb)
        mask_ref[0, 0, pl.ds(k0, kb), :] = jnp.zeros((kb, tq), jnp.int8)
        return carry

    lax.fori_loop(nkt, s_pad // kb, clear_tile, 0)

    @pl.when(n_open == 0.0)
    def _():
        def write_tile(j, carry):
            k0 = pl.multiple_of(j * kb, kb)
            keep = score_ref[pl.ds(k0, kb), :] >= lo
            mask_ref[0, 0, pl.ds(k0, kb), :] = jnp.where(keep, 1, 0).astype(jnp.int8)
            return carry

        lax.fori_loop(0, nkt, write_tile, 0)

    @pl.when(n_open > 0.0)
    def _():
        tied = open_rows(cnt_lo)
        need = jnp.where(tied, kf - cnt_hi, 0.0)
        key_iota = lax.broadcasted_iota(jnp.int32, (kb, tq), 0)
        strip_iota = lax.broadcasted_iota(jnp.int32, (COUNT_ROWS, tq), 0)
        shape = (COUNT_ROWS // (COUNT_PARTS * SUBLANES), COUNT_PARTS, SUBLANES, tq)

        def next_tied(prev):
            def body(j, acc):
                for c in range(kb // COUNT_ROWS):
                    k0 = pl.multiple_of(j * kb + c * COUNT_ROWS, COUNT_ROWS)
                    idx = k0 + strip_iota
                    hit = jnp.logical_and(score_ref[pl.ds(k0, COUNT_ROWS), :] == lo, idx > prev)
                    acc = jnp.minimum(acc, jnp.min(jnp.where(hit, idx, s_pad).reshape(shape), axis=0))
                return acc
            acc = lax.fori_loop(0, nkt, body, jnp.full(shape[1:], s_pad, jnp.int32))
            return jnp.min(jnp.min(acc, axis=0), axis=0, keepdims=True)

        def cut_by_walk(_):
            def step(c):
                _, left, last = c
                take = left > 0.0
                last = jnp.where(take, next_tied(last), last)
                left = jnp.where(take, left - 1.0, left)
                return jnp.max(left), left, last
            _, _, last = lax.while_loop(lambda c: c[0] > 0.0, step,
                                        (jnp.max(need), need, jnp.full((1, tq), -1, jnp.int32)))
            return last + 1

        def cut_by_bisection(_):
            def step(_, st):
                jlo, jhi = st
                jmid = (jlo + jhi) >> 1
                cm = count(lambda s, k0: jnp.logical_and(s == lo, k0 + strip_iota < jmid))
                ok = cm >= need
                return jnp.where(ok, jlo, jmid), jnp.where(ok, jmid, jhi)
            steps = int(math.ceil(math.log2(s_pad))) + 1
            return lax.fori_loop(0, steps, step, (jnp.zeros((1, tq), jnp.int32),
                                                  jnp.full((1, tq), s_pad, jnp.int32)))[1]

        jcut = lax.cond(jnp.max(need) <= TIE_WALK_MAX, cut_by_walk, cut_by_bisection, 0)
        jcut = jnp.where(tied, jcut, s_pad)

        def write_tile(j, carry):
            k0 = pl.multiple_of(j * kb, kb)
            s = score_ref[pl.ds(k0, kb), :]
            keep = jnp.logical_or(s >= hi, jnp.logical_and(s >= lo, k0 + key_iota < jcut))
            mask_ref[0, 0, pl.ds(k0, kb), :] = jnp.where(keep, 1, 0).astype(jnp.int8)
            return carry

        lax.fori_loop(0, nkt, write_tile, 0)


def _select(z3, ki3, tq, kb, s_real, q_pos0, topk):
    b, t, _ = z3.shape
    s_pad = ki3.shape[1]
    iq_w = IDX_HEADS * IDX_DIM
    kern = functools.partial(_select_kernel, tq=tq, kb=kb, s_pad=s_pad, s_real=s_real,
                             q_pos0=q_pos0, topk=topk)
    return pl.pallas_call(
        kern,
        grid=(b, t // tq),
        in_specs=[pl.BlockSpec((1, tq, iq_w), lambda i, j: (i, j, COL_IQ // iq_w)),
                  pl.BlockSpec((1, tq, LANES), lambda i, j: (i, j, COL_SMALL // LANES)),
                  pl.BlockSpec((1, s_pad, 3 * IDX_DIM), lambda i, j: (i, 0, 0))],
        out_specs=pl.BlockSpec((1, 1, s_pad, tq), lambda i, j: (i, j, 0, 0)),
        out_shape=jax.ShapeDtypeStruct((b, t // tq, s_pad, tq), jnp.int8),
        scratch_shapes=[pltpu.VMEM((s_pad, tq), F32),
                        pltpu.VMEM((IDX_HEADS // 2, 3 * IDX_DIM, 2 * tq), BF16)],
        compiler_params=_cparams(("parallel", "arbitrary")),
        name="index_select",
    )(z3, z3, ki3)


def _attn_kernel(q_of_ref, k_of_ref, qt_ref, k_ref, vt_ref, *refs, n_mask):
    mask_refs = refs[:n_mask]
    o_ref, acc_ref, m_ref, l_ref, s_ref, bias_ref = refs[n_mask:]
    step = pl.program_id(1)
    hd = D // A_HEADS
    tq = qt_ref.shape[2]
    tk = k_ref.shape[1]
    tq_mask = tq // n_mask
    pv_rows = math.gcd(tk, PV_ROWS)
    first = k_of_ref[step] == 0
    nxt = jnp.minimum(step + 1, pl.num_programs(1) - 1)
    last = jnp.logical_or(step == pl.num_programs(1) - 1, q_of_ref[nxt] != q_of_ref[step])

    @pl.when(first)
    def _():
        acc_ref[...] = jnp.zeros(acc_ref.shape, F32)
        m_ref[...] = jnp.full(m_ref.shape, MASK_NEG, F32)
        l_ref[...] = jnp.zeros(l_ref.shape, F32)

    for n, mask_ref in enumerate(mask_refs):
        bias_ref[:, n * tq_mask:(n + 1) * tq_mask] = jnp.where(
            mask_ref[0, 0].astype(jnp.int32) != 0, 0.0, MASK_NEG)
    def logits(h):
        hs = slice(h * hd, (h + 1) * hd)
        top = None
        for r in range(0, tk, QK_ROWS):
            s = jnp.dot(k_ref[0, r:r + QK_ROWS, hs], qt_ref[0, hs, :],
                        preferred_element_type=F32) + bias_ref[r:r + QK_ROWS, :]
            s_ref[h, r:r + QK_ROWS, :] = s
            part = jnp.max(s.reshape(QK_ROWS // SUBLANES, SUBLANES, tq), axis=0)
            top = part if top is None else jnp.maximum(top, part)
        return jnp.max(top, axis=0, keepdims=True)

    def weigh(h, tile_max):
        hs = slice(h * hd, (h + 1) * hd)
        m_prev = m_ref[h:h + 1, :]
        m_new = jnp.maximum(m_prev, tile_max)
        alpha = jnp.exp2(m_prev - m_new)
        psum = None
        pv = None
        for r in range(0, tk, pv_rows):
            p = jnp.exp2(s_ref[h, r:r + pv_rows, :] - m_new)
            part = jnp.sum(p.reshape(pv_rows // SUBLANES, SUBLANES, tq), axis=0)
            psum = part if psum is None else psum + part
            prod = jnp.dot(vt_ref[0, 0, hs, r:r + pv_rows], p.astype(BF16),
                           preferred_element_type=F32)
            pv = prod if pv is None else pv + prod
        l_ref[h:h + 1, :] = alpha * l_ref[h:h + 1, :] + jnp.sum(psum, axis=0, keepdims=True)
        acc_ref[hs, :] = alpha * acc_ref[hs, :] + pv
        m_ref[h:h + 1, :] = m_new

    tile_max = [logits(h) for h in range(A_HEADS)]

    @pl.when(k_of_ref[step] >= 0)
    def _():
        for h in range(A_HEADS):
            weigh(h, tile_max[h])

    @pl.when(last)
    def _():
        eye = _eye_bf16(tq)
        for h in range(A_HEADS):
            hs = slice(h * hd, (h + 1) * hd)
            o_ref[0, :, hs] = _lanes_to_frames(eye, acc_ref[hs, :] / l_ref[h:h + 1, :])


def _attention(qt, kb, vt, mask_t, tq, tk, s_real, q_pos0):
    b, _, t = qt.shape
    n_mask = tq // mask_t.shape[3]
    steps = [(q, j) for q in range(t // tq) for j in range(_num_key_tiles_static(q, tq, tk, q_pos0, s_real))]
    q_of = jnp.array([q for q, _ in steps], jnp.int32)
    k_of = jnp.array([j for _, j in steps], jnp.int32)

    def mask_spec(n):
        return pl.BlockSpec((1, 1, tk, tq // n_mask),
                            lambda i, s, q_of, k_of: (i, q_of[s] * n_mask + n, k_of[s], 0))

    return pl.pallas_call(
        functools.partial(_attn_kernel, n_mask=n_mask),
        grid_spec=pltpu.PrefetchScalarGridSpec(
            num_scalar_prefetch=2,
            grid=(b, len(steps)),
            in_specs=[pl.BlockSpec((1, D, tq), lambda i, s, q_of, k_of: (i, 0, q_of[s])),
                      pl.BlockSpec((1, tk, D), lambda i, s, q_of, k_of: (i, k_of[s], 0)),
                      pl.BlockSpec((1, 1, D, tk), lambda i, s, q_of, k_of: (i, k_of[s], 0, 0))]
            + [mask_spec(n) for n in range(n_mask)],
            out_specs=pl.BlockSpec((1, tq, D), lambda i, s, q_of, k_of: (i, q_of[s], 0)),
            scratch_shapes=[pltpu.VMEM((D, tq), F32),
                            pltpu.VMEM((A_HEADS, tq), F32),
                            pltpu.VMEM((A_HEADS, tq), F32),
                            pltpu.VMEM((A_HEADS, tk, tq), F32),
                            pltpu.VMEM((tk, tq), F32)]),
        out_shape=jax.ShapeDtypeStruct((b, t, D), BF16),
        compiler_params=_cparams(("parallel", "arbitrary")),
        name="masked_attention",
    )(q_of, k_of, qt, kb, vt, *([mask_t] * n_mask))


def _attn_past_kernel(qt_ref, kn_ref, vtn_ref, kp_ref, vp_ref, mask_ref, o_ref):
    hd = D // A_HEADS
    t = qt_ref.shape[2]
    past = kp_ref.shape[1] // A_HEADS

    def cached(ref, h):
        return ref[0, pl.ds(h, past, stride=A_HEADS), :].astype(BF16)

    bias = jnp.where(mask_ref[0, 0].astype(jnp.int32) != 0, 0.0, MASK_NEG)
    bias_p, bias_n = bias[0:past], bias[past:past + t]
    eye_hd, eye_t = _eye_bf16(hd), _eye_bf16(t)
    for h in range(A_HEADS):
        hs = slice(h * hd, (h + 1) * hd)
        q = qt_ref[0, hs, :]
        s_p = jnp.dot(cached(kp_ref, h), q, preferred_element_type=F32) + bias_p
        s_n = jnp.dot(kn_ref[0, :, hs], q, preferred_element_type=F32) + bias_n
        m = jnp.maximum(jnp.max(s_p, axis=0, keepdims=True), jnp.max(s_n, axis=0, keepdims=True))
        p_p = jnp.exp2(s_p - m)
        p_n = jnp.exp2(s_n - m)
        l = jnp.sum(p_p, axis=0, keepdims=True) + jnp.sum(p_n, axis=0, keepdims=True)
        vt_p = _transpose_bf16(eye_hd, cached(vp_ref, h)).astype(BF16)
        out_t = (jnp.dot(vt_p, p_p.astype(BF16), preferred_element_type=F32)
                 + jnp.dot(vtn_ref[0, 0, hs, :], p_n.astype(BF16), preferred_element_type=F32))
        o_ref[0, :, hs] = _transpose_bf16(eye_t, (out_t / l).astype(BF16)).astype(BF16)


def _attention_with_past(qt, kb, vt, k_past, v_past, mask_t):
    b, _, t = qt.shape
    past, heads, hd = k_past.shape[1:]
    s_pad = mask_t.shape[2]
    k_past = k_past.reshape(b, past * heads, hd)
    v_past = v_past.reshape(b, past * heads, hd)
    cache = pl.BlockSpec((1, past * heads, hd), lambda i: (i, 0, 0))
    return pl.pallas_call(
        _attn_past_kernel,
        grid=(b,),
        in_specs=[pl.BlockSpec((1, D, t), lambda i: (i, 0, 0)),
                  pl.BlockSpec((1, t, D), lambda i: (i, 0, 0)),
                  pl.BlockSpec((1, 1, D, t), lambda i: (i, 0, 0, 0)),
                  cache, cache,
                  pl.BlockSpec((1, 1, s_pad, t), lambda i: (i, 0, 0, 0))],
        out_specs=pl.BlockSpec((1, t, D), lambda i: (i, 0, 0)),
        out_shape=jax.ShapeDtypeStruct((b, t, D), BF16),
        compiler_params=_cparams(("parallel",)),
        name="masked_attention_past",
    )(qt, kb, vt, k_past, v_past, mask_t)


def _merge_kernel(x_ref, ya_ref, yb_ref, ga_ref, gb_ref, wa_ref, wb_ref, wo_ref, o_ref):
    a = jnp.dot(ya_ref[...], wa_ref[...], preferred_element_type=F32)
    bb = jnp.dot(yb_ref[...], wb_ref[...], preferred_element_type=F32)
    mix = jax.nn.sigmoid(ga_ref[...]) * a + jax.nn.sigmoid(gb_ref[...]) * bb
    o_ref[...] = x_ref[...] + jnp.dot(mix.astype(BF16), wo_ref[...], preferred_element_type=F32)


def _merge(x, ya, yb, z, wa, wb, wo, tm):
    n = x.shape[0]
    row = pl.BlockSpec((tm, D), lambda i: (i, 0))
    wspec = pl.BlockSpec((D, D), lambda i: (0, 0))
    return pl.pallas_call(
        _merge_kernel,
        grid=(n // tm,),
        in_specs=[row, row, row,
                  pl.BlockSpec((tm, D), lambda i: (i, COL_GA // D)),
                  pl.BlockSpec((tm, D), lambda i: (i, COL_GB // D)),
                  wspec, wspec, wspec],
        out_specs=row,
        out_shape=jax.ShapeDtypeStruct((n, D), F32),
        compiler_params=_cparams(("parallel",)),
        name="merge_out_proj",
    )(x, ya, yb, z, z, wa, wb, wo)


def _ffn_kernel(x_ref, g_ref, wg_ref, wu_ref, wd_ref, o_ref, xn_ref, acc_ref):
    c = pl.program_id(1)

    @pl.when(c == 0)
    def _():
        x = x_ref[...]
        ms = jnp.mean(x * x, axis=-1, keepdims=True)
        xn_ref[...] = ((x * lax.rsqrt(ms + EPS)) * g_ref[...]).astype(BF16)
        acc_ref[...] = x

    xn = xn_ref[...]
    gate = jnp.dot(xn, wg_ref[...], preferred_element_type=F32)
    up = jnp.dot(xn, wu_ref[...], preferred_element_type=F32)
    act = (gate * jax.nn.sigmoid(gate)) * up
    acc_ref[...] += jnp.dot(act.astype(BF16), wd_ref[...], preferred_element_type=F32)

    @pl.when(c == pl.num_programs(1) - 1)
    def _():
        o_ref[...] = acc_ref[...]


def _ffn(x, g, w_in, w_out, tm, tf):
    n = x.shape[0]
    dff = w_out.shape[0]
    nf = dff // tf
    row = pl.BlockSpec((tm, D), lambda i, c: (i, 0))
    return pl.pallas_call(
        _ffn_kernel,
        grid=(n // tm, nf),
        in_specs=[row, pl.BlockSpec((1, D), lambda i, c: (0, 0)),
                  pl.BlockSpec((D, tf), lambda i, c: (0, c)),
                  pl.BlockSpec((D, tf), lambda i, c: (0, nf + c)),
                  pl.BlockSpec((tf, D), lambda i, c: (c, 0))],
        out_specs=row,
        out_shape=jax.ShapeDtypeStruct((n, D), F32),
        scratch_shapes=[pltpu.VMEM((tm, D), BF16), pltpu.VMEM((tm, D), F32)],
        compiler_params=_cparams(("parallel", "arbitrary")),
        name="swiglu_ffn",
    )(x, g, w_in, w_in, w_out)


def _split_hi_lo(a):
    hi = a.astype(BF16)
    lo = (a - hi.astype(F32)).astype(BF16)
    return hi, lo


def _layer(x, past, p, cfg):
    k_past, v_past, ki_past, c0, n0, m0, conv0 = past
    b, t, _ = x.shape
    n_tok = b * t
    past_len = k_past.shape[1]
    s_real = past_len + t
    topk = min(TOPK_MAX, s_real // 4)
    tq_sel, kb_sel, tq_att, tk_att = cfg["tq_sel"], cfg["kb_sel"], cfg["tq_att"], cfg["tk_att"]
    s_pad = -(-s_real // tk_att) * tk_att

    x2 = x.reshape(n_tok, D)
    z = _norm_matmul(x2, p["g_norm1"], p["w_in"], cfg["tm_proj"], cfg["tn_proj"])
    z3 = z.reshape(b, t, D_Z)

    y_a, c_new, n_new, m_new, conv_new = _mlstm(
        z3, p["gate_bias"], p["w_conv"], p["b_conv"], p["g_mnorm"], c0, n0,
        m0.reshape(b, 1, M_HEADS), conv0, cfg["mlstm_chunk"])

    k32, v32, kb, qt, vt = _qkv(x, p["g_norm1"], p["w_att"], p["g_q"], p["g_k"], cfg["tm_qkv"])
    ik = z3[:, :, COL_SMALL:COL_SMALL + IDX_DIM]
    ki_all = ik if past_len == 0 else jnp.concatenate([ki_past.astype(F32), ik], axis=1)
    k_hi, k_lo = _split_hi_lo(ki_all)
    ki3 = jnp.concatenate([k_hi, k_lo, k_hi], axis=-1)
    ki3 = jnp.pad(ki3, ((0, 0), (0, s_pad - s_real), (0, 0)))
    mask_t = _select(z3, ki3, tq_sel, kb_sel, s_real, past_len, topk)
    if past_len == 0:
        assert cfg["tm_qkv"] == tk_att and s_pad == s_real
        y_b = _attention(qt, kb, vt, mask_t, tq_att, tk_att, s_real, past_len)
    else:
        assert cfg["tm_qkv"] == t == tq_sel
        y_b = _attention_with_past(qt, kb, vt, k_past, v_past, mask_t)

    x1 = _merge(x2, y_a.reshape(n_tok, D), y_b.reshape(n_tok, D), z,
                p["w_a_out"], p["w_b_out"], p["w_o"], cfg["tm_rows"])
    y = _ffn(x1, p["g_norm2"], p["w_ffn_in"], p["w_ffn_out"], cfg["tm_ffn"], cfg["tf_ffn"])

    hd = D // A_HEADS
    return y.reshape(b, t, D), (k32.reshape(b, t, A_HEADS, hd), v32.reshape(b, t, A_HEADS, hd), ik,
                                c_new, n_new, m_new.reshape(b, M_HEADS), conv_new)


O_MI = 4 * D
O_AQ = O_MI + 2 * M_HEADS
O_IQ = O_AQ + 3 * D
O_IK = O_IQ + IDX_HEADS * IDX_DIM
O_IW = O_IK + IDX_DIM
O_GA = O_IW + IDX_HEADS
D_IN = O_GA + 2 * D


def _split_w_kernel(w_ref, main_ref, att_ref):
    rows = w_ref.shape[0]

    def columns(first, width):
        start = first // LANES * LANES
        win = width + LANES
        if start + win <= D_IN:
            x = w_ref[:, start:start + win]
        else:
            tail = D_IN - (start + width)
            x = jnp.concatenate([w_ref[:, start:start + width],
                                 w_ref[:, start + width:D_IN],
                                 jnp.zeros((rows, LANES - tail), F32)], axis=1)
        return pltpu.roll(x, win - (first - start), axis=1)[:, 0:width]

    att_ref[...] = columns(O_AQ, 3 * D).astype(BF16)
    main_ref[:, 0:O_MI] = w_ref[:, 0:O_MI].astype(BF16)
    main_ref[:, COL_GA:COL_GA + 2 * D] = columns(O_GA, 2 * D).astype(BF16)
    main_ref[:, COL_IQ:COL_IQ + IDX_HEADS * IDX_DIM] = columns(O_IQ, IDX_HEADS * IDX_DIM).astype(BF16)
    lane = lax.broadcasted_iota(jnp.int32, (rows, LANES), 1)
    ik_iw = columns(O_IK, LANES)
    gates = pltpu.roll(w_ref[:, O_MI:O_MI + LANES], SM_MI, axis=1)
    small = jnp.where(lane < SM_MI, ik_iw, jnp.where(lane < SM_MF + M_HEADS, gates, 0.0))
    main_ref[:, COL_SMALL:COL_SMALL + LANES] = small.astype(BF16)
    main_ref[:, COL_SMALL + LANES:D_Z] = jnp.zeros((rows, D_Z - COL_SMALL - LANES), BF16)


def _split_w(w_in, tr):
    return pl.pallas_call(
        _split_w_kernel,
        grid=(D // tr,),
        in_specs=[pl.BlockSpec((tr, D_IN), lambda i: (i, 0))],
        out_specs=[pl.BlockSpec((tr, D_Z), lambda i: (i, 0)), pl.BlockSpec((tr, 3 * D), lambda i: (i, 0))],
        out_shape=[jax.ShapeDtypeStruct((D, D_Z), BF16), jax.ShapeDtypeStruct((D, 3 * D), BF16)],
        compiler_params=_cparams(("parallel",)),
        name="split_in_proj_weight",
    )(w_in)


def _prep_params(g_norm1, w_in, b_if, w_conv, b_conv, g_mnorm, g_q, g_k, w_a_out, w_b_out, w_o,
                 g_norm2, w_ffn_in, w_ffn_out):
    assert w_in.shape == (D, D_IN)
    w_perm, w_att = _split_w(w_in, 256)
    gate_bias = jnp.zeros((1, LANES), F32).at[0, SM_MI:SM_MI + 2 * M_HEADS].set(b_if.astype(F32))
    return {
        "g_norm1": g_norm1.reshape(1, D), "w_in": w_perm, "w_att": w_att,
        "gate_bias": gate_bias,
        "w_conv": w_conv, "b_conv": b_conv.reshape(1, 2 * D), "g_mnorm": g_mnorm.reshape(1, D),
        "g_q": g_q.reshape(1, -1), "g_k": g_k.reshape(1, -1),
        "w_a_out": w_a_out.astype(BF16), "w_b_out": w_b_out.astype(BF16), "w_o": w_o.astype(BF16),
        "g_norm2": g_norm2.reshape(1, D), "w_ffn_in": w_ffn_in.astype(BF16),
        "w_ffn_out": w_ffn_out.astype(BF16),
    }


def _config(b, t, past_len):
    n_tok = b * t
    s_real = past_len + t
    tm = min(2048, n_tok)
    tq_sel = min(256, t)
    tq_att = min(512, t)
    if s_real % 512 == 0:
        kb_sel = tk_att = 512
    else:
        kb_sel = tk_att = -(-s_real // LANES) * LANES
    return {"tm_proj": tm, "tn_proj": 896, "mlstm_chunk": min(128, t), "tm_rows": min(512, n_tok),
            "tm_qkv": min(512, t), "tq_sel": tq_sel, "kb_sel": kb_sel, "tq_att": tq_att,
            "tk_att": tk_att, "tm_ffn": min(512, n_tok), "tf_ffn": 1408}


def kernel(x_prompt, x_sample, cache_k, cache_v, cache_kidx, state_C, state_n, state_m, state_conv,
           g_norm1, w_in, b_if, w_conv, b_conv, g_mnorm, g_q, g_k, w_a_out, w_b_out, w_o,
           g_norm2, w_ffn_in, w_ffn_out):
    depth = w_in.shape[0]
    bp = x_prompt.shape[0]
    hd_a = D // A_HEADS
    hd_m = D // M_HEADS
    yp, ys = x_prompt, x_sample
    new_p, new_s = [], []
    for l in range(depth):
        p = _prep_params(g_norm1[l], w_in[l], b_if[l], w_conv[l], b_conv[l], g_mnorm[l], g_q[l],
                         g_k[l], w_a_out[l], w_b_out[l], w_o[l], g_norm2[l], w_ffn_in[l],
                         w_ffn_out[l])
        empty = (jnp.zeros((bp, 0, A_HEADS, hd_a), F32), jnp.zeros((bp, 0, A_HEADS, hd_a), F32),
                 jnp.zeros((bp, 0, IDX_DIM), F32), jnp.zeros((bp, M_HEADS, hd_m, hd_m), F32),
                 jnp.zeros((bp, M_HEADS, hd_m), F32), jnp.zeros((bp, M_HEADS), F32),
                 jnp.zeros((bp, CONV_W - 1, 2 * D), F32))
        yp, sp = _layer(yp, empty, p, _config(bp, yp.shape[1], 0))
        ys, ss = _layer(ys, (cache_k[l], cache_v[l], cache_kidx[l], state_C[l], state_n[l],
                             state_m[l], state_conv[l]), p,
                        _config(ys.shape[0], ys.shape[1], cache_k.shape[2]))
        new_p.append(sp)
        new_s.append(ss)

    def stk(lst, i):
        return jnp.stack([s[i] for s in lst])

    return (yp, ys) + tuple(stk(new_p, i) for i in range(7)) + tuple(stk(new_s, i) for i in range(7))
```

```python
import functools
import math

import jax
import jax.numpy as jnp
from jax import lax
from jax.experimental import pallas as pl
from jax.experimental.pallas import tpu as pltpu

F32 = jnp.float32
BF16 = jnp.bfloat16
HIGHEST = lax.Precision.HIGHEST

EPS = 1e-6
CHUNK = 64
CHUNK_SHIFT = 6
M_HEADS = 4
A_HEADS = 8
IDX_HEADS = 8
IDX_DIM = 64
CONV_W = 4
TOPK_MAX = 256
LANES = 128
SUBLANES = 8
MASK_NEG = -1e30
F32_MAX = float(jnp.finfo(jnp.float32).max)
VMEM_LIMIT = 48 * 1024 * 1024
SELECT_MIN_ITERS = 14
SELECT_MAX_ITERS = 16
FINISH_MAX = 4
QK_ROWS = 128
PV_ROWS = 256
TIE_WALK_MAX = 8
COUNT_PARTS = 4
COUNT_ROWS = 128

D = 1024
COL_MQ, COL_MK, COL_MV, COL_MO = 0, 1024, 2048, 3072
COL_GA, COL_GB = 4096, 5120
COL_IQ = 6144
COL_SMALL = 6656
D_Z = 7168
SM_IW, SM_MI, SM_MF = 64, 72, 76
Q_LOGIT_SCALE = math.log2(math.e) * (D // A_HEADS) ** -0.5


def _cparams(sem):
    return pltpu.CompilerParams(dimension_semantics=sem, vmem_limit_bytes=VMEM_LIMIT)


def _norm_matmul_kernel(x_ref, g_ref, w_ref, o_ref, xn_ref):
    @pl.when(pl.program_id(1) == 0)
    def _():
        x = x_ref[...]
        ms = jnp.mean(x * x, axis=-1, keepdims=True)
        xn_ref[...] = ((x * lax.rsqrt(ms + EPS)) * g_ref[...]).astype(BF16)

    o_ref[...] = jnp.dot(xn_ref[...], w_ref[...], preferred_element_type=F32)


def _norm_matmul(x, g, w, tm, tn):
    n, d = x.shape
    nout = w.shape[1]
    return pl.pallas_call(
        _norm_matmul_kernel,
        grid=(n // tm, nout // tn),
        in_specs=[pl.BlockSpec((tm, d), lambda i, j: (i, 0)),
                  pl.BlockSpec((1, d), lambda i, j: (0, 0)),
                  pl.BlockSpec((d, tn), lambda i, j: (0, j))],
        out_specs=pl.BlockSpec((tm, tn), lambda i, j: (i, j)),
        out_shape=jax.ShapeDtypeStruct((n, nout), F32),
        scratch_shapes=[pltpu.VMEM((tm, d), BF16)],
        compiler_params=_cparams(("parallel", "arbitrary")),
        name="norm_in_proj",
    )(x, g, w)


def _mlstm_kernel(mq_ref, mk_ref, mv_ref, mo_ref, sm_ref, bias_ref, wconv_ref, bconv_ref,
                  gm_ref, c0_ref, n0_ref, m0_ref, conv0_ref,
                  y_ref, c_ref, n_ref, m_ref, conv_ref, cbuf_ref, *, L):
    c = pl.program_id(1)
    hd = D // M_HEADS

    @pl.when(c == 0)
    def _():
        c_ref[...] = c0_ref[...]
        n_ref[...] = n0_ref[...]
        m_ref[...] = m0_ref[...]
        cbuf_ref[8 - (CONV_W - 1):8, :] = conv0_ref[0]

    cbuf_ref[8:8 + L, 0:D] = mq_ref[0]
    cbuf_ref[8:8 + L, D:2 * D] = mk_ref[0]
    wc = wconv_ref[...]
    qk = bconv_ref[...] + cbuf_ref[5:5 + L, :] * wc[0:1, :]
    for j in range(1, CONV_W):
        qk = qk + cbuf_ref[5 + j:5 + j + L, :] * wc[j:j + 1, :]
    tail = cbuf_ref[5 + L:8 + L, :]
    cbuf_ref[5:8, :] = tail
    conv_ref[0] = tail
    qk = qk * jax.nn.sigmoid(qk)

    g_all = sm_ref[0] + bias_ref[...]
    lf_all = jnp.minimum(g_all, 0.0) - jnp.log1p(jnp.exp(-jnp.abs(g_all)))
    lane = lax.broadcasted_iota(jnp.int32, (L, LANES), 1)
    gates = jnp.where(lane >= SM_MF, lf_all, g_all)
    r_i = lax.broadcasted_iota(jnp.int32, (L, L), 0)
    c_i = lax.broadcasted_iota(jnp.int32, (L, L), 1)
    tril = (c_i <= r_i).astype(F32)
    triu = (r_i <= c_i).astype(F32)
    b_col_all = jnp.dot(tril, lf_all, precision=HIGHEST, preferred_element_type=F32)
    sel = (lax.broadcasted_iota(jnp.int32, (8, LANES), 1)
           == lax.broadcasted_iota(jnp.int32, (8, LANES), 0) + SM_MI).astype(F32)
    rows = lax.dot_general(sel, gates, (((1,), (1,)), ((), ())), precision=HIGHEST,
                           preferred_element_type=F32)
    b_row_all = jnp.dot(rows, triu, precision=HIGHEST, preferred_element_type=F32)
    causal = c_i <= r_i

    mo = mo_ref[0]
    mv = mv_ref[0]
    for h in range(M_HEADS):
        hs = slice(h * hd, (h + 1) * hd)
        qh = qk[:, hs]
        kh = qk[:, D + h * hd:D + (h + 1) * hd] * (hd ** -0.5)
        vh = mv[:, hs]
        qb, kb, vb = qh.astype(BF16), kh.astype(BF16), vh.astype(BF16)
        b_col = b_col_all[:, SM_MF + h:SM_MF + h + 1]
        i_col = gates[:, SM_MI + h:SM_MI + h + 1]
        b_row = b_row_all[M_HEADS + h:M_HEADS + h + 1, :]
        i_row = rows[h:h + 1, :]
        m_prev = m_ref[0, :, h:h + 1]
        c_prev = c_ref[0, h]
        n_prev = n_ref[0, h:h + 1, :]

        dmat = jnp.where(causal, b_col - b_row + i_row, -jnp.inf)
        inter = b_col + m_prev
        m_t = jnp.maximum(inter, jnp.max(dmat, axis=-1, keepdims=True))
        w_intra = jnp.exp(dmat - m_t)
        w_inter = jnp.exp(inter - m_t)
        s = lax.dot_general(qb, kb, (((1,), (1,)), ((), ())), preferred_element_type=F32) * w_intra
        qc = lax.dot_general(qb, c_prev.astype(BF16), (((1,), (1,)), ((), ())),
                             preferred_element_type=F32)
        num = jnp.dot(s.astype(BF16), vb, preferred_element_type=F32) + w_inter * qc
        den = jnp.sum(s, axis=-1, keepdims=True) + w_inter * jnp.sum(qh * n_prev, axis=-1, keepdims=True)
        denom = jnp.maximum(jnp.abs(den), jnp.exp(-m_t))
        hh = num / denom

        m_new = m_t[L - 1:L, :]
        b_last = b_col[L - 1:L, :]
        g_col = jnp.exp(b_last - b_col + i_col - m_new)
        decay = jnp.exp(b_last + m_prev - m_new)
        gv = (g_col * vh).astype(BF16)
        c_ref[0, h] = decay * c_prev + lax.dot_general(
            gv, kb, (((0,), (0,)), ((), ())), preferred_element_type=F32)
        n_ref[0, h:h + 1, :] = decay * n_prev + jnp.sum(g_col * kh, axis=0, keepdims=True)
        m_ref[0, :, h:h + 1] = m_new

        hn = hh * lax.rsqrt(jnp.mean(hh * hh, axis=-1, keepdims=True) + EPS) * gm_ref[:, hs]
        y_ref[0, :, hs] = (hn * jax.nn.sigmoid(mo[:, hs])).astype(BF16)


def _mlstm(z3, bias_row, w_conv, b_conv, g_mnorm, c0, n0, m0, conv0, L):
    b, t, _ = z3.shape
    nc = t // L
    hd = D // M_HEADS

    def zspec(col, width):
        return pl.BlockSpec((1, L, width), lambda i, c: (i, c, col // width))

    def per_batch(shape):
        nd = len(shape)
        return pl.BlockSpec((1,) + shape, lambda i, c: (i,) + (0,) * nd)

    def const(shape):
        nd = len(shape)
        return pl.BlockSpec(shape, lambda i, c: (0,) * nd)

    return pl.pallas_call(
        functools.partial(_mlstm_kernel, L=L),
        grid=(b, nc),
        in_specs=[zspec(COL_MQ, D), zspec(COL_MK, D), zspec(COL_MV, D), zspec(COL_MO, D),
                  zspec(COL_SMALL, LANES), const((1, LANES)), const((CONV_W, 2 * D)),
                  const((1, 2 * D)), const((1, D)),
                  per_batch((M_HEADS, hd, hd)), per_batch((M_HEADS, hd)),
                  per_batch((1, M_HEADS)), per_batch((CONV_W - 1, 2 * D))],
        out_specs=[pl.BlockSpec((1, L, D), lambda i, c: (i, c, 0)),
                   per_batch((M_HEADS, hd, hd)), per_batch((M_HEADS, hd)),
                   per_batch((1, M_HEADS)), per_batch((CONV_W - 1, 2 * D))],
        out_shape=[jax.ShapeDtypeStruct((b, t, D), BF16),
                   jax.ShapeDtypeStruct((b, M_HEADS, hd, hd), F32),
                   jax.ShapeDtypeStruct((b, M_HEADS, hd), F32),
                   jax.ShapeDtypeStruct((b, 1, M_HEADS), F32),
                   jax.ShapeDtypeStruct((b, CONV_W - 1, 2 * D), F32)],
        scratch_shapes=[pltpu.VMEM((8 + L, 2 * D), F32)],
        compiler_params=_cparams(("parallel", "arbitrary")),
        name="mlstm",
    )(z3, z3, z3, z3, z3, bias_row, w_conv, b_conv, g_mnorm, c0, n0, m0, conv0)


def _eye_bf16(n):
    return (lax.broadcasted_iota(jnp.int32, (n, n), 0)
            == lax.broadcasted_iota(jnp.int32, (n, n), 1)).astype(BF16)


def _transpose_bf16(eye, a):
    return lax.dot_general(eye, a, (((1,), (1,)), ((), ())), preferred_element_type=F32)


def _frames_to_lanes(eye, a):
    if a.shape[0] % LANES == 0:
        return a.T.astype(BF16)
    return _transpose_bf16(eye, a.astype(BF16)).astype(BF16)


def _lanes_to_frames(eye, a):
    if a.shape[1] % LANES == 0:
        return a.T.astype(BF16)
    return _transpose_bf16(eye, a.astype(BF16)).astype(BF16)


def _qkv_kernel(x_ref, g_ref, w_ref, gq_ref, gk_ref, k32_ref, v32_ref, kb_ref, qt_ref, vt_ref):
    hd = D // A_HEADS
    x = x_ref[0]
    ms = jnp.mean(x * x, axis=-1, keepdims=True)
    xn = ((x * lax.rsqrt(ms + EPS)) * g_ref[...]).astype(BF16)
    eye = _eye_bf16(hd)

    def project(col):
        return jnp.dot(xn, w_ref[:, col:col + 2 * hd], preferred_element_type=F32)

    def head_norm(a, gain):
        return (a * lax.rsqrt(jnp.mean(a * a, axis=-1, keepdims=True) + EPS)) * gain

    for pair in range(0, D, 2 * hd):
        q2, k2, v2 = project(pair), project(D + pair), project(2 * D + pair)
        for half in range(2):
            cs = slice(half * hd, (half + 1) * hd)
            hs = slice(pair + half * hd, pair + (half + 1) * hd)
            qn = head_norm(q2[:, cs], gq_ref[...]) * Q_LOGIT_SCALE
            qt_ref[0, hs, :] = _frames_to_lanes(eye, qn)
            kn = head_norm(k2[:, cs], gk_ref[...])
            k32_ref[0, :, hs] = kn
            kb_ref[0, :, hs] = kn.astype(BF16)
            v32_ref[0, :, hs] = v2[:, cs]
            vt_ref[0, 0, hs, :] = _frames_to_lanes(eye, v2[:, cs])


def _qkv(x, g_norm, w_att, g_q, g_k, tm):
    b, t, _ = x.shape
    hd = D // A_HEADS
    row = pl.BlockSpec((1, tm, D), lambda i, r: (i, r, 0))
    gspec = pl.BlockSpec((1, hd), lambda i, r: (0, 0))
    return pl.pallas_call(
        _qkv_kernel,
        grid=(b, t // tm),
        in_specs=[row, pl.BlockSpec((1, D), lambda i, r: (0, 0)),
                  pl.BlockSpec((D, 3 * D), lambda i, r: (0, 0)), gspec, gspec],
        out_specs=[row, row, row, pl.BlockSpec((1, D, tm), lambda i, r: (i, 0, r)),
                   pl.BlockSpec((1, 1, D, tm), lambda i, r: (i, r, 0, 0))],
        out_shape=[jax.ShapeDtypeStruct((b, t, D), F32), jax.ShapeDtypeStruct((b, t, D), F32),
                   jax.ShapeDtypeStruct((b, t, D), BF16), jax.ShapeDtypeStruct((b, D, t), BF16),
                   jax.ShapeDtypeStruct((b, t // tm, D, tm), BF16)],
        compiler_params=_cparams(("parallel", "parallel")),
        name="qkv_proj_norm",
    )(x, g_norm, w_att, g_q, g_k)


def _num_key_tiles(qi, tq, tile, q_pos0, s_real):
    last_chunk = (q_pos0 + (qi + 1) * tq - 1) // CHUNK
    kend = jnp.minimum((last_chunk + 1) * CHUNK, s_real)
    return (kend + tile - 1) // tile


def _num_key_tiles_static(qi, tq, tile, q_pos0, s_real):
    last_chunk = (q_pos0 + (qi + 1) * tq - 1) // CHUNK
    return -(-min((last_chunk + 1) * CHUNK, s_real) // tile)


def _select_kernel(iq_ref, sm_ref, ki_ref, mask_ref, score_ref, iqt_ref, *,
                   tq, kb, s_pad, s_real, q_pos0, topk):
    qi = pl.program_id(1)
    nkt = _num_key_tiles(qi, tq, kb, q_pos0, s_real)
    kf = float(topk)
    groups = kb // SUBLANES

    pick = (lax.broadcasted_iota(jnp.int32, (IDX_HEADS, LANES), 1)
            == lax.broadcasted_iota(jnp.int32, (IDX_HEADS, LANES), 0) + SM_IW).astype(F32)
    w = lax.dot_general(pick, sm_ref[0], (((1,), (1,)), ((), ())), precision=HIGHEST,
                        preferred_element_type=F32) * ((IDX_HEADS * IDX_DIM) ** -0.5)
    eye = _eye_bf16(IDX_DIM)
    for h in range(IDX_HEADS):
        a = iq_ref[0, :, h * IDX_DIM:(h + 1) * IDX_DIM]
        hi = a.astype(BF16)
        lo = (a - hi.astype(F32)).astype(BF16)
        hi_t = _transpose_bf16(eye, hi).astype(BF16)
        lo_t = _transpose_bf16(eye, lo).astype(BF16)
        cols = slice((h % 2) * tq, (h % 2 + 1) * tq)
        iqt_ref[h // 2, 0:IDX_DIM, cols] = hi_t
        iqt_ref[h // 2, IDX_DIM:2 * IDX_DIM, cols] = hi_t
        iqt_ref[h // 2, 2 * IDX_DIM:3 * IDX_DIM, cols] = lo_t

    q_pos = q_pos0 + qi * tq + lax.broadcasted_iota(jnp.int32, (1, tq), 1)
    q_chunk = q_pos >> CHUNK_SHIFT
    first_key = jnp.minimum(((q_pos0 + qi * tq) >> CHUNK_SHIFT) << CHUNK_SHIFT, s_real)
    n_full = first_key // kb

    def score_tile(j, carry, masked):
        rmin, rmax = carry
        k0 = pl.multiple_of(j * kb, kb)
        kt = ki_ref[0, pl.ds(k0, kb), :]
        acc = None
        for p in range(IDX_HEADS // 2):
            rel = jnp.dot(kt, iqt_ref[p], preferred_element_type=F32)
            part = (w[2 * p:2 * p + 1] * jnp.maximum(rel[:, 0:tq], 0.0)
                    + w[2 * p + 1:2 * p + 2] * jnp.maximum(rel[:, tq:2 * tq], 0.0))
            acc = part if acc is None else acc + part
        if masked:
            key = k0 + lax.broadcasted_iota(jnp.int32, (kb, tq), 0)
            adm = jnp.logical_and((key >> CHUNK_SHIFT) <= q_chunk, key < s_real)
            low = jnp.where(adm, acc, jnp.inf)
            acc = jnp.where(adm, acc, -jnp.inf)
        else:
            low = acc
        score_ref[pl.ds(k0, kb), :] = acc
        rmin = jnp.minimum(rmin, jnp.min(low.reshape(groups, SUBLANES, tq), axis=0))
        rmax = jnp.maximum(rmax, jnp.max(acc.reshape(groups, SUBLANES, tq), axis=0))
        return rmin, rmax

    stats = (jnp.full((SUBLANES, tq), jnp.inf, F32), jnp.full((SUBLANES, tq), -jnp.inf, F32))
    stats = lax.fori_loop(0, n_full, functools.partial(score_tile, masked=False), stats)
    rmin8, rmax8 = lax.fori_loop(n_full, nkt, functools.partial(score_tile, masked=True), stats)
    rmin = jnp.min(rmin8, axis=0, keepdims=True)
    rmax = jnp.max(rmax8, axis=0, keepdims=True)

    def count(pred):
        def body(j, acc):
            for c in range(kb // COUNT_ROWS):
                k0 = pl.multiple_of(j * kb + c * COUNT_ROWS, COUNT_ROWS)
                hit = jnp.where(pred(score_ref[pl.ds(k0, COUNT_ROWS), :], k0), 1.0, 0.0)
                acc = acc + jnp.sum(hit.reshape(COUNT_ROWS // (COUNT_PARTS * SUBLANES), COUNT_PARTS,
                                                SUBLANES, tq), axis=0)
            return acc
        acc = lax.fori_loop(0, nkt, body, jnp.zeros((COUNT_PARTS, SUBLANES, tq), F32))
        return jnp.sum(jnp.sum(acc, axis=0), axis=0, keepdims=True)

    def below(cur):
        shape = (COUNT_ROWS // (COUNT_PARTS * SUBLANES), COUNT_PARTS, SUBLANES, tq)

        def body(j, carry):
            top, num = carry
            for c in range(kb // COUNT_ROWS):
                k0 = pl.multiple_of(j * kb + c * COUNT_ROWS, COUNT_ROWS)
                s = score_ref[pl.ds(k0, COUNT_ROWS), :]
                under = s < cur
                top = jnp.maximum(top, jnp.max(jnp.where(under, s, -jnp.inf).reshape(shape), axis=0))
                num = num + jnp.sum(jnp.where(under, 0.0, 1.0).reshape(shape), axis=0)
            return top, num

        top, num = lax.fori_loop(0, nkt, body, (jnp.full(shape[1:], -jnp.inf, F32),
                                                jnp.zeros(shape[1:], F32)))
        return (jnp.max(jnp.max(top, axis=0), axis=0, keepdims=True),
                jnp.sum(jnp.sum(num, axis=0), axis=0, keepdims=True))

    n_adm = jnp.minimum((q_chunk + 1) << CHUNK_SHIFT, s_real).astype(F32)

    def open_rows(cnt_lo):
        return cnt_lo > kf

    def wide_rows(st):
        return jnp.logical_and(open_rows(st[2]), st[2] - st[3] > FINISH_MAX)

    def flag(pred):
        return jnp.max(jnp.where(pred, 1.0, 0.0))

    def bisect(st):
        lo, hi, cnt_lo, cnt_hi = st
        active = open_rows(cnt_lo)
        mid = 0.5 * jnp.maximum(lo, rmin) + 0.5 * jnp.minimum(hi, rmax)
        cm = count(lambda s, k0: s >= mid)
        up = jnp.logical_and(active, cm >= kf)
        dn = jnp.logical_and(active, cm < kf)
        return (jnp.where(up, mid, lo), jnp.where(dn, mid, hi),
                jnp.where(up, cm, cnt_lo), jnp.where(dn, cm, cnt_hi))

    st = (jnp.full((1, tq), -F32_MAX, F32), jnp.full((1, tq), F32_MAX, F32), n_adm,
          jnp.zeros((1, tq), F32))
    st = lax.fori_loop(0, SELECT_MIN_ITERS, lambda _, s: bisect(s), st)

    def narrow(c):
        nxt = bisect(c[2])
        return c[0] + 1, flag(wide_rows(nxt)), nxt

    _, _, st = lax.while_loop(
        lambda c: jnp.logical_and(c[0] < SELECT_MAX_ITERS, c[1] > 0.0), narrow,
        (jnp.int32(SELECT_MIN_ITERS), flag(wide_rows(st)), st))

    lo, hi, cnt_lo, cnt_hi = st
    cand, _ = below(hi)

    def walk(c):
        _, walking, lo, hi, cnt_lo, cnt_hi, cand = c
        nxt, cge = below(cand)
        settle = jnp.logical_and(walking > 0.0, cge >= kf)
        move = jnp.logical_and(walking > 0.0, cge < kf)
        walking = jnp.where(move, 1.0, 0.0)
        return (jnp.max(walking), walking, jnp.where(settle, cand, lo), jnp.where(move, cand, hi),
                jnp.where(settle, cge, cnt_lo), jnp.where(move, cge, cnt_hi), jnp.where(move, nxt, cand))

    walking = jnp.where(open_rows(cnt_lo), 1.0, 0.0)
    _, _, lo, hi, cnt_lo, cnt_hi, _ = lax.while_loop(
        lambda c: c[0] > 0.0, walk, (jnp.max(walking), walking, lo, hi, cnt_lo, cnt_hi, cand))
    n_open = flag(open_rows(cnt_lo))

    def clear_tile(j, carry):
        k0 = pl.multiple_of(j * kb, kb)
        mask_ref[0, 0, pl.ds(k0, kb), :] = jnp.zeros((kb, tq), jnp.int8)
        return carry

    lax.fori_loop(nkt, s_pad // kb, clear_tile, 0)

    @pl.when(n_open == 0.0)
    def _():
        def write_tile(j, carry):
            k0 = pl.multiple_of(j * kb, kb)
            keep = score_ref[pl.ds(k0, kb), :] >= lo
            mask_ref[0, 0, pl.ds(k0, kb), :] = jnp.where(keep, 1, 0).astype(jnp.int8)
            return carry

        lax.fori_loop(0, nkt, write_tile, 0)

    @pl.when(n_open > 0.0)
    def _():
        tied = open_rows(cnt_lo)
        need = jnp.where(tied, kf - cnt_hi, 0.0)
        key_iota = lax.broadcasted_iota(jnp.int32, (kb, tq), 0)
        strip_iota = lax.broadcasted_iota(jnp.int32, (COUNT_ROWS, tq), 0)
        shape = (COUNT_ROWS // (COUNT_PARTS * SUBLANES), COUNT_PARTS, SUBLANES, tq)

        def next_tied(prev):
            def body(j, acc):
                for c in range(kb // COUNT_ROWS):
                    k0 = pl.multiple_of(j * kb + c * COUNT_ROWS, COUNT_ROWS)
                    idx = k0 + strip_iota
                    hit = jnp.logical_and(score_ref[pl.ds(k0, COUNT_ROWS), :] == lo, idx > prev)
                    acc = jnp.minimum(acc, jnp.min(jnp.where(hit, idx, s_pad).reshape(shape), axis=0))
                return acc
            acc = lax.fori_loop(0, nkt, body, jnp.full(shape[1:], s_pad, jnp.int32))
            return jnp.min(jnp.min(acc, axis=0), axis=0, keepdims=True)

        def cut_by_walk(_):
            def step(c):
                _, left, last = c
                take = left > 0.0
                last = jnp.where(take, next_tied(last), last)
                left = jnp.where(take, left - 1.0, left)
                return jnp.max(left), left, last
            _, _, last = lax.while_loop(lambda c: c[0] > 0.0, step,
                                        (jnp.max(need), need, jnp.full((1, tq), -1, jnp.int32)))
            return last + 1

        def cut_by_bisection(_):
            def step(_, st):
                jlo, jhi = st
                jmid = (jlo + jhi) >> 1
                cm = count(lambda s, k0: jnp.logical_and(s == lo, k0 + strip_iota < jmid))
                ok = cm >= need
                return jnp.where(ok, jlo, jmid), jnp.where(ok, jmid, jhi)
            steps = int(math.ceil(math.log2(s_pad))) + 1
            return lax.fori_loop(0, steps, step, (jnp.zeros((1, tq), jnp.int32),
                                                  jnp.full((1, tq), s_pad, jnp.int32)))[1]

        jcut = lax.cond(jnp.max(need) <= TIE_WALK_MAX, cut_by_walk, cut_by_bisection, 0)
        jcut = jnp.where(tied, jcut, s_pad)

        def write_tile(j, carry):
            k0 = pl.multiple_of(j * kb, kb)
            s = score_ref[pl.ds(k0, kb), :]
            keep = jnp.logical_or(s >= hi, jnp.logical_and(s >= lo, k0 + key_iota < jcut))
            mask_ref[0, 0, pl.ds(k0, kb), :] = jnp.where(keep, 1, 0).astype(jnp.int8)
            return carry

        lax.fori_loop(0, nkt, write_tile, 0)


def _select(z3, ki3, tq, kb, s_real, q_pos0, topk):
    b, t, _ = z3.shape
    s_pad = ki3.shape[1]
    iq_w = IDX_HEADS * IDX_DIM
    kern = functools.partial(_select_kernel, tq=tq, kb=kb, s_pad=s_pad, s_real=s_real,
                             q_pos0=q_pos0, topk=topk)
    return pl.pallas_call(
        kern,
        grid=(b, t // tq),
        in_specs=[pl.BlockSpec((1, tq, iq_w), lambda i, j: (i, j, COL_IQ // iq_w)),
                  pl.BlockSpec((1, tq, LANES), lambda i, j: (i, j, COL_SMALL // LANES)),
                  pl.BlockSpec((1, s_pad, 3 * IDX_DIM), lambda i, j: (i, 0, 0))],
        out_specs=pl.BlockSpec((1, 1, s_pad, tq), lambda i, j: (i, j, 0, 0)),
        out_shape=jax.ShapeDtypeStruct((b, t // tq, s_pad, tq), jnp.int8),
        scratch_shapes=[pltpu.VMEM((s_pad, tq), F32),
                        pltpu.VMEM((IDX_HEADS // 2, 3 * IDX_DIM, 2 * tq), BF16)],
        compiler_params=_cparams(("parallel", "arbitrary")),
        name="index_select",
    )(z3, z3, ki3)


def _attn_kernel(q_of_ref, k_of_ref, qt_ref, k_ref, vt_ref, *refs, n_mask):
    mask_refs = refs[:n_mask]
    o_ref, acc_ref, m_ref, l_ref, s_ref, bias_ref = refs[n_mask:]
    step = pl.program_id(1)
    hd = D // A_HEADS
    tq = qt_ref.shape[2]
    tk = k_ref.shape[1]
    tq_mask = tq // n_mask
    pv_rows = math.gcd(tk, PV_ROWS)
    first = k_of_ref[step] == 0
    nxt = jnp.minimum(step + 1, pl.num_programs(1) - 1)
    last = jnp.logical_or(step == pl.num_programs(1) - 1, q_of_ref[nxt] != q_of_ref[step])

    @pl.when(first)
    def _():
        acc_ref[...] = jnp.zeros(acc_ref.shape, F32)
        m_ref[...] = jnp.full(m_ref.shape, MASK_NEG, F32)
        l_ref[...] = jnp.zeros(l_ref.shape, F32)

    for n, mask_ref in enumerate(mask_refs):
        bias_ref[:, n * tq_mask:(n + 1) * tq_mask] = jnp.where(
            mask_ref[0, 0].astype(jnp.int32) != 0, 0.0, MASK_NEG)
    def logits(h):
        hs = slice(h * hd, (h + 1) * hd)
        top = None
        for r in range(0, tk, QK_ROWS):
            s = jnp.dot(k_ref[0, r:r + QK_ROWS, hs], qt_ref[0, hs, :],
                        preferred_element_type=F32) + bias_ref[r:r + QK_ROWS, :]
            s_ref[h, r:r + QK_ROWS, :] = s
            part = jnp.max(s.reshape(QK_ROWS // SUBLANES, SUBLANES, tq), axis=0)
            top = part if top is None else jnp.maximum(top, part)
        return jnp.max(top, axis=0, keepdims=True)

    def weigh(h, tile_max):
        hs = slice(h * hd, (h + 1) * hd)
        m_prev = m_ref[h:h + 1, :]
        m_new = jnp.maximum(m_prev, tile_max)
        alpha = jnp.exp2(m_prev - m_new)
        psum = None
        pv = None
        for r in range(0, tk, pv_rows):
            p = jnp.exp2(s_ref[h, r:r + pv_rows, :] - m_new)
            part = jnp.sum(p.reshape(pv_rows // SUBLANES, SUBLANES, tq), axis=0)
            psum = part if psum is None else psum + part
            prod = jnp.dot(vt_ref[0, 0, hs, r:r + pv_rows], p.astype(BF16),
                           preferred_element_type=F32)
            pv = prod if pv is None else pv + prod
        l_ref[h:h + 1, :] = alpha * l_ref[h:h + 1, :] + jnp.sum(psum, axis=0, keepdims=True)
        acc_ref[hs, :] = alpha * acc_ref[hs, :] + pv
        m_ref[h:h + 1, :] = m_new

    tile_max = [logits(h) for h in range(A_HEADS)]

    @pl.when(k_of_ref[step] >= 0)
    def _():
        for h in range(A_HEADS):
            weigh(h, tile_max[h])

    @pl.when(last)
    def _():
        eye = _eye_bf16(tq)
        for h in range(A_HEADS):
            hs = slice(h * hd, (h + 1) * hd)
            o_ref[0, :, hs] = _lanes_to_frames(eye, acc_ref[hs, :] / l_ref[h:h + 1, :])


def _attention(qt, kb, vt, mask_t, tq, tk, s_real, q_pos0):
    b, _, t = qt.shape
    n_mask = tq // mask_t.shape[3]
    steps = [(q, j) for q in range(t // tq) for j in range(_num_key_tiles_static(q, tq, tk, q_pos0, s_real))]
    q_of = jnp.array([q for q, _ in steps], jnp.int32)
    k_of = jnp.array([j for _, j in steps], jnp.int32)

    def mask_spec(n):
        return pl.BlockSpec((1, 1, tk, tq // n_mask),
                            lambda i, s, q_of, k_of: (i, q_of[s] * n_mask + n, k_of[s], 0))

    return pl.pallas_call(
        functools.partial(_attn_kernel, n_mask=n_mask),
        grid_spec=pltpu.PrefetchScalarGridSpec(
            num_scalar_prefetch=2,
            grid=(b, len(steps)),
            in_specs=[pl.BlockSpec((1, D, tq), lambda i, s, q_of, k_of: (i, 0, q_of[s])),
                      pl.BlockSpec((1, tk, D), lambda i, s, q_of, k_of: (i, k_of[s], 0)),
                      pl.BlockSpec((1, 1, D, tk), lambda i, s, q_of, k_of: (i, k_of[s], 0, 0))]
            + [mask_spec(n) for n in range(n_mask)],
            out_specs=pl.BlockSpec((1, tq, D), lambda i, s, q_of, k_of: (i, q_of[s], 0)),
            scratch_shapes=[pltpu.VMEM((D, tq), F32),
                            pltpu.VMEM((A_HEADS, tq), F32),
                            pltpu.VMEM((A_HEADS, tq), F32),
                            pltpu.VMEM((A_HEADS, tk, tq), F32),
                            pltpu.VMEM((tk, tq), F32)]),
        out_shape=jax.ShapeDtypeStruct((b, t, D), BF16),
        compiler_params=_cparams(("parallel", "arbitrary")),
        name="masked_attention",
    )(q_of, k_of, qt, kb, vt, *([mask_t] * n_mask))


def _attn_past_kernel(qt_ref, kn_ref, vtn_ref, kp_ref, vp_ref, mask_ref, o_ref):
    hd = D // A_HEADS
    t = qt_ref.shape[2]
    past = kp_ref.shape[1] // A_HEADS

    def cached(ref, h):
        return ref[0, pl.ds(h, past, stride=A_HEADS), :].astype(BF16)

    bias = jnp.where(mask_ref[0, 0].astype(jnp.int32) != 0, 0.0, MASK_NEG)
    bias_p, bias_n = bias[0:past], bias[past:past + t]
    eye_hd, eye_t = _eye_bf16(hd), _eye_bf16(t)
    for h in range(A_HEADS):
        hs = slice(h * hd, (h + 1) * hd)
        q = qt_ref[0, hs, :]
        s_p = jnp.dot(cached(kp_ref, h), q, preferred_element_type=F32) + bias_p
        s_n = jnp.dot(kn_ref[0, :, hs], q, preferred_element_type=F32) + bias_n
        m = jnp.maximum(jnp.max(s_p, axis=0, keepdims=True), jnp.max(s_n, axis=0, keepdims=True))
        p_p = jnp.exp2(s_p - m)
        p_n = jnp.exp2(s_n - m)
        l = jnp.sum(p_p, axis=0, keepdims=True) + jnp.sum(p_n, axis=0, keepdims=True)
        vt_p = _transpose_bf16(eye_hd, cached(vp_ref, h)).astype(BF16)
        out_t = (jnp.dot(vt_p, p_p.astype(BF16), preferred_element_type=F32)
                 + jnp.dot(vtn_ref[0, 0, hs, :], p_n.astype(BF16), preferred_element_type=F32))
        o_ref[0, :, hs] = _transpose_bf16(eye_t, (out_t / l).astype(BF16)).astype(BF16)


def _attention_with_past(qt, kb, vt, k_past, v_past, mask_t):
    b, _, t = qt.shape
    past, heads, hd = k_past.shape[1:]
    s_pad = mask_t.shape[2]
    k_past = k_past.reshape(b, past * heads, hd)
    v_past = v_past.reshape(b, past * heads, hd)
    cache = pl.BlockSpec((1, past * heads, hd), lambda i: (i, 0, 0))
    return pl.pallas_call(
        _attn_past_kernel,
        grid=(b,),
        in_specs=[pl.BlockSpec((1, D, t), lambda i: (i, 0, 0)),
                  pl.BlockSpec((1, t, D), lambda i: (i, 0, 0)),
                  pl.BlockSpec((1, 1, D, t), lambda i: (i, 0, 0, 0)),
                  cache, cache,
                  pl.BlockSpec((1, 1, s_pad, t), lambda i: (i, 0, 0, 0))],
        out_specs=pl.BlockSpec((1, t, D), lambda i: (i, 0, 0)),
        out_shape=jax.ShapeDtypeStruct((b, t, D), BF16),
        compiler_params=_cparams(("parallel",)),
        name="masked_attention_past",
    )(qt, kb, vt, k_past, v_past, mask_t)


def _merge_kernel(x_ref, ya_ref, yb_ref, ga_ref, gb_ref, wa_ref, wb_ref, wo_ref, o_ref):
    a = jnp.dot(ya_ref[...], wa_ref[...], preferred_element_type=F32)
    bb = jnp.dot(yb_ref[...], wb_ref[...], preferred_element_type=F32)
    mix = jax.nn.sigmoid(ga_ref[...]) * a + jax.nn.sigmoid(gb_ref[...]) * bb
    o_ref[...] = x_ref[...] + jnp.dot(mix.astype(BF16), wo_ref[...], preferred_element_type=F32)


def _merge(x, ya, yb, z, wa, wb, wo, tm):
    n = x.shape[0]
    row = pl.BlockSpec((tm, D), lambda i: (i, 0))
    wspec = pl.BlockSpec((D, D), lambda i: (0, 0))
    return pl.pallas_call(
        _merge_kernel,
        grid=(n // tm,),
        in_specs=[row, row, row,
                  pl.BlockSpec((tm, D), lambda i: (i, COL_GA // D)),
                  pl.BlockSpec((tm, D), lambda i: (i, COL_GB // D)),
                  wspec, wspec, wspec],
        out_specs=row,
        out_shape=jax.ShapeDtypeStruct((n, D), F32),
        compiler_params=_cparams(("parallel",)),
        name="merge_out_proj",
    )(x, ya, yb, z, z, wa, wb, wo)


def _ffn_kernel(x_ref, g_ref, wg_ref, wu_ref, wd_ref, o_ref, xn_ref, acc_ref):
    c = pl.program_id(1)

    @pl.when(c == 0)
    def _():
        x = x_ref[...]
        ms = jnp.mean(x * x, axis=-1, keepdims=True)
        xn_ref[...] = ((x * lax.rsqrt(ms + EPS)) * g_ref[...]).astype(BF16)
        acc_ref[...] = x

    xn = xn_ref[...]
    gate = jnp.dot(xn, wg_ref[...], preferred_element_type=F32)
    up = jnp.dot(xn, wu_ref[...], preferred_element_type=F32)
    act = (gate * jax.nn.sigmoid(gate)) * up
    acc_ref[...] += jnp.dot(act.astype(BF16), wd_ref[...], preferred_element_type=F32)

    @pl.when(c == pl.num_programs(1) - 1)
    def _():
        o_ref[...] = acc_ref[...]


def _ffn(x, g, w_in, w_out, tm, tf):
    n = x.shape[0]
    dff = w_out.shape[0]
    nf = dff // tf
    row = pl.BlockSpec((tm, D), lambda i, c: (i, 0))
    once = {"pipeline_mode": pl.Buffered(1)} if nf == 1 else {}
    return pl.pallas_call(
        _ffn_kernel,
        grid=(n // tm, nf),
        in_specs=[row, pl.BlockSpec((1, D), lambda i, c: (0, 0)),
                  pl.BlockSpec((D, tf), lambda i, c: (0, c), **once),
                  pl.BlockSpec((D, tf), lambda i, c: (0, nf + c), **once),
                  pl.BlockSpec((tf, D), lambda i, c: (c, 0), **once)],
        out_specs=row,
        out_shape=jax.ShapeDtypeStruct((n, D), F32),
        scratch_shapes=[pltpu.VMEM((tm, D), BF16), pltpu.VMEM((tm, D), F32)],
        compiler_params=_cparams(("parallel", "arbitrary")),
        name="swiglu_ffn",
    )(x, g, w_in, w_in, w_out)


def _split_hi_lo(a):
    hi = a.astype(BF16)
    lo = (a - hi.astype(F32)).astype(BF16)
    return hi, lo


def _layer(x, past, p, cfg):
    k_past, v_past, ki_past, c0, n0, m0, conv0 = past
    b, t, _ = x.shape
    n_tok = b * t
    past_len = k_past.shape[1]
    s_real = past_len + t
    topk = min(TOPK_MAX, s_real // 4)
    tq_sel, kb_sel, tq_att, tk_att = cfg["tq_sel"], cfg["kb_sel"], cfg["tq_att"], cfg["tk_att"]
    s_pad = -(-s_real // tk_att) * tk_att

    x2 = x.reshape(n_tok, D)
    z = _norm_matmul(x2, p["g_norm1"], p["w_in"], cfg["tm_proj"], cfg["tn_proj"])
    z3 = z.reshape(b, t, D_Z)

    y_a, c_new, n_new, m_new, conv_new = _mlstm(
        z3, p["gate_bias"], p["w_conv"], p["b_conv"], p["g_mnorm"], c0, n0,
        m0.reshape(b, 1, M_HEADS), conv0, cfg["mlstm_chunk"])

    k32, v32, kb, qt, vt = _qkv(x, p["g_norm1"], p["w_att"], p["g_q"], p["g_k"], cfg["tm_qkv"])
    ik = z3[:, :, COL_SMALL:COL_SMALL + IDX_DIM]
    ki_all = ik if past_len == 0 else jnp.concatenate([ki_past.astype(F32), ik], axis=1)
    k_hi, k_lo = _split_hi_lo(ki_all)
    ki3 = jnp.concatenate([k_hi, k_lo, k_hi], axis=-1)
    ki3 = jnp.pad(ki3, ((0, 0), (0, s_pad - s_real), (0, 0)))
    mask_t = _select(z3, ki3, tq_sel, kb_sel, s_real, past_len, topk)
    if past_len == 0:
        assert cfg["tm_qkv"] == tk_att and s_pad == s_real
        y_b = _attention(qt, kb, vt, mask_t, tq_att, tk_att, s_real, past_len)
    else:
        assert cfg["tm_qkv"] == t == tq_sel
        y_b = _attention_with_past(qt, kb, vt, k_past, v_past, mask_t)

    x1 = _merge(x2, y_a.reshape(n_tok, D), y_b.reshape(n_tok, D), z,
                p["w_a_out"], p["w_b_out"], p["w_o"], cfg["tm_rows"])
    y = _ffn(x1, p["g_norm2"], p["w_ffn_in"], p["w_ffn_out"], cfg["tm_ffn"], cfg["tf_ffn"])

    hd = D // A_HEADS
    return y.reshape(b, t, D), (k32.reshape(b, t, A_HEADS, hd), v32.reshape(b, t, A_HEADS, hd), ik,
                                c_new, n_new, m_new.reshape(b, M_HEADS), conv_new)


O_MI = 4 * D
O_AQ = O_MI + 2 * M_HEADS
O_IQ = O_AQ + 3 * D
O_IK = O_IQ + IDX_HEADS * IDX_DIM
O_IW = O_IK + IDX_DIM
O_GA = O_IW + IDX_HEADS
D_IN = O_GA + 2 * D


def _split_w_kernel(w_ref, main_ref, att_ref):
    rows = w_ref.shape[0]

    def columns(first, width):
        start = first // LANES * LANES
        win = width + LANES
        if start + win <= D_IN:
            x = w_ref[:, start:start + win]
        else:
            tail = D_IN - (start + width)
            x = jnp.concatenate([w_ref[:, start:start + width],
                                 w_ref[:, start + width:D_IN],
                                 jnp.zeros((rows, LANES - tail), F32)], axis=1)
        return pltpu.roll(x, win - (first - start), axis=1)[:, 0:width]

    att_ref[...] = columns(O_AQ, 3 * D).astype(BF16)
    main_ref[:, 0:O_MI] = w_ref[:, 0:O_MI].astype(BF16)
    main_ref[:, COL_GA:COL_GA + 2 * D] = columns(O_GA, 2 * D).astype(BF16)
    main_ref[:, COL_IQ:COL_IQ + IDX_HEADS * IDX_DIM] = columns(O_IQ, IDX_HEADS * IDX_DIM).astype(BF16)
    lane = lax.broadcasted_iota(jnp.int32, (rows, LANES), 1)
    ik_iw = columns(O_IK, LANES)
    gates = pltpu.roll(w_ref[:, O_MI:O_MI + LANES], SM_MI, axis=1)
    small = jnp.where(lane < SM_MI, ik_iw, jnp.where(lane < SM_MF + M_HEADS, gates, 0.0))
    main_ref[:, COL_SMALL:COL_SMALL + LANES] = small.astype(BF16)
    main_ref[:, COL_SMALL + LANES:D_Z] = jnp.zeros((rows, D_Z - COL_SMALL - LANES), BF16)


def _split_w(w_in, layer, tr):
    return pl.pallas_call(
        _split_w_kernel,
        grid=(D // tr,),
        in_specs=[pl.BlockSpec((None, tr, D_IN), lambda i: (layer, i, 0))],
        out_specs=[pl.BlockSpec((tr, D_Z), lambda i: (i, 0)), pl.BlockSpec((tr, 3 * D), lambda i: (i, 0))],
        out_shape=[jax.ShapeDtypeStruct((D, D_Z), BF16), jax.ShapeDtypeStruct((D, 3 * D), BF16)],
        compiler_params=_cparams(("parallel",)),
        name="split_in_proj_weight",
    )(w_in)


def _prep_params(g_norm1, w_in, layer, b_if, w_conv, b_conv, g_mnorm, g_q, g_k, w_a_out, w_b_out, w_o,
                 g_norm2, w_ffn_in, w_ffn_out):
    assert w_in.shape[1:] == (D, D_IN)
    w_perm, w_att = _split_w(w_in, layer, 256)
    gate_bias = jnp.zeros((1, LANES), F32).at[0, SM_MI:SM_MI + 2 * M_HEADS].set(b_if.astype(F32))
    return {
        "g_norm1": g_norm1.reshape(1, D), "w_in": w_perm, "w_att": w_att,
        "gate_bias": gate_bias,
        "w_conv": w_conv, "b_conv": b_conv.reshape(1, 2 * D), "g_mnorm": g_mnorm.reshape(1, D),
        "g_q": g_q.reshape(1, -1), "g_k": g_k.reshape(1, -1),
        "w_a_out": w_a_out.astype(BF16), "w_b_out": w_b_out.astype(BF16), "w_o": w_o.astype(BF16),
        "g_norm2": g_norm2.reshape(1, D), "w_ffn_in": w_ffn_in.astype(BF16),
        "w_ffn_out": w_ffn_out.astype(BF16),
    }


def _config(b, t, past_len):
    n_tok = b * t
    s_real = past_len + t
    tm = min(2048, n_tok)
    tq_sel = min(256, t)
    tq_att = min(512, t)
    if s_real % 512 == 0:
        kb_sel = tk_att = 512
    else:
        kb_sel = tk_att = -(-s_real // LANES) * LANES
    return {"tm_proj": tm, "tn_proj": 1024,"mlstm_chunk": min(128, t), "tm_rows": min(512, n_tok),
            "tm_qkv": min(512, t), "tq_sel": tq_sel, "kb_sel": kb_sel, "tq_att": tq_att,
            "tk_att": tk_att, "tm_ffn": min(512, n_tok), "tf_ffn": 2816}


def kernel(x_prompt, x_sample, cache_k, cache_v, cache_kidx, state_C, state_n, state_m, state_conv,
           g_norm1, w_in, b_if, w_conv, b_conv, g_mnorm, g_q, g_k, w_a_out, w_b_out, w_o,
           g_norm2, w_ffn_in, w_ffn_out):
    depth = w_in.shape[0]
    bp = x_prompt.shape[0]
    hd_a = D // A_HEADS
    hd_m = D // M_HEADS
    yp, ys = x_prompt, x_sample
    new_p, new_s = [], []
    for l in range(depth):
        p = _prep_params(g_norm1[l], w_in, l, b_if[l], w_conv[l], b_conv[l], g_mnorm[l], g_q[l],
                         g_k[l], w_a_out[l], w_b_out[l], w_o[l], g_norm2[l], w_ffn_in[l],
                         w_ffn_out[l])
        empty = (jnp.zeros((bp, 0, A_HEADS, hd_a), F32), jnp.zeros((bp, 0, A_HEADS, hd_a), F32),
                 jnp.zeros((bp, 0, IDX_DIM), F32), jnp.zeros((bp, M_HEADS, hd_m, hd_m), F32),
                 jnp.zeros((bp, M_HEADS, hd_m), F32), jnp.zeros((bp, M_HEADS), F32),
                 jnp.zeros((bp, CONV_W - 1, 2 * D), F32))
        yp, sp = _layer(yp, empty, p, _config(bp, yp.shape[1], 0))
        ys, ss = _layer(ys, (cache_k[l], cache_v[l], cache_kidx[l], state_C[l], state_n[l],
                             state_m[l], state_conv[l]), p,
                        _config(ys.shape[0], ys.shape[1], cache_k.shape[2]))
        new_p.append(sp)
        new_s.append(ss)

    def stk(lst, i):
        return jnp.stack([s[i] for s in lst])

    return (yp, ys) + tuple(stk(new_p, i) for i in range(7)) + tuple(stk(new_s, i) for i in range(7))
```

```python
import functools
import math

import jax
import jax.numpy as jnp
from jax import lax
from jax.experimental import pallas as pl
from jax.experimental.pallas import tpu as pltpu

F32 = jnp.float32
BF16 = jnp.bfloat16
HIGHEST = lax.Precision.HIGHEST

EPS = 1e-6
CHUNK = 64
CHUNK_SHIFT = 6
M_HEADS = 4
A_HEADS = 8
IDX_HEADS = 8
IDX_DIM = 64
CONV_W = 4
TOPK_MAX = 256
LANES = 128
SUBLANES = 8
MASK_NEG = -1e30
F32_MAX = float(jnp.finfo(jnp.float32).max)
VMEM_LIMIT = 48 * 1024 * 1024
SELECT_MIN_ITERS = 14
SELECT_MAX_ITERS = 16
FINISH_MAX = 4
VT_HEAD_ROWS = 144
QK_ROWS = 128
PV_ROWS = 256
TIE_WALK_MAX = 8
COUNT_PARTS = 4
COUNT_ROWS = 128
KEY_PREP_ROWS = 512

D = 1024
COL_MQ, COL_MK, COL_MV, COL_MO = 0, 1024, 2048, 3072
COL_GA, COL_GB = 4096, 5120
COL_IQ = 6144
COL_SMALL = 6656
D_Z = 7168
SM_IW, SM_MI, SM_MF = 64, 72, 76
Q_LOGIT_SCALE = math.log2(math.e) * (D // A_HEADS) ** -0.5


def _cparams(sem):
    return pltpu.CompilerParams(dimension_semantics=sem, vmem_limit_bytes=VMEM_LIMIT)


def _norm_matmul_kernel(x_ref, g_ref, w_ref, o_ref, xn_ref):
    @pl.when(pl.program_id(1) == 0)
    def _():
        x = x_ref[...]
        ms = jnp.mean(x * x, axis=-1, keepdims=True)
        xn_ref[...] = ((x * lax.rsqrt(ms + EPS)) * g_ref[...]).astype(BF16)

    o_ref[...] = jnp.dot(xn_ref[...], w_ref[...], preferred_element_type=F32)


def _norm_matmul(x, g, w, tm, tn):
    n, d = x.shape
    nout = w.shape[1]
    return pl.pallas_call(
        _norm_matmul_kernel,
        grid=(n // tm, nout // tn),
        in_specs=[pl.BlockSpec((tm, d), lambda i, j: (i, 0)),
                  pl.BlockSpec((1, d), lambda i, j: (0, 0)),
                  pl.BlockSpec((d, tn), lambda i, j: (0, j))],
        out_specs=pl.BlockSpec((tm, tn), lambda i, j: (i, j)),
        out_shape=jax.ShapeDtypeStruct((n, nout), F32),
        scratch_shapes=[pltpu.VMEM((tm, d), BF16)],
        compiler_params=_cparams(("parallel", "arbitrary")),
        name="norm_in_proj",
    )(x, g, w)


def _mlstm_kernel(mq_ref, mk_ref, mv_ref, mo_ref, sm_ref, bias_ref, wconv_ref, bconv_ref,
                  gm_ref, c0_ref, n0_ref, m0_ref, conv0_ref,
                  y_ref, c_ref, n_ref, m_ref, conv_ref, cbuf_ref, *, L):
    c = pl.program_id(1)
    hd = D // M_HEADS

    @pl.when(c == 0)
    def _():
        c_ref[...] = c0_ref[...]
        n_ref[...] = n0_ref[...]
        m_ref[...] = m0_ref[...]
        cbuf_ref[8 - (CONV_W - 1):8, :] = conv0_ref[0]

    cbuf_ref[8:8 + L, 0:D] = mq_ref[0]
    cbuf_ref[8:8 + L, D:2 * D] = mk_ref[0]
    wc = wconv_ref[...]
    qk = bconv_ref[...] + cbuf_ref[5:5 + L, :] * wc[0:1, :]
    for j in range(1, CONV_W):
        qk = qk + cbuf_ref[5 + j:5 + j + L, :] * wc[j:j + 1, :]
    tail = cbuf_ref[5 + L:8 + L, :]
    cbuf_ref[5:8, :] = tail
    conv_ref[0] = tail
    qk = qk * jax.nn.sigmoid(qk)

    g_all = sm_ref[0] + bias_ref[...]
    lf_all = jnp.minimum(g_all, 0.0) - jnp.log1p(jnp.exp(-jnp.abs(g_all)))
    lane = lax.broadcasted_iota(jnp.int32, (L, LANES), 1)
    gates = jnp.where(lane >= SM_MF, lf_all, g_all)
    r_i = lax.broadcasted_iota(jnp.int32, (L, L), 0)
    c_i = lax.broadcasted_iota(jnp.int32, (L, L), 1)
    tril = (c_i <= r_i).astype(F32)
    triu = (r_i <= c_i).astype(F32)
    b_col_all = jnp.dot(tril, lf_all, precision=HIGHEST, preferred_element_type=F32)
    sel = (lax.broadcasted_iota(jnp.int32, (8, LANES), 1)
           == lax.broadcasted_iota(jnp.int32, (8, LANES), 0) + SM_MI).astype(F32)
    rows = lax.dot_general(sel, gates, (((1,), (1,)), ((), ())), precision=HIGHEST,
                           preferred_element_type=F32)
    b_row_all = jnp.dot(rows, triu, precision=HIGHEST, preferred_element_type=F32)
    causal = c_i <= r_i

    mo = mo_ref[0]
    mv = mv_ref[0]
    for h in range(M_HEADS):
        hs = slice(h * hd, (h + 1) * hd)
        qh = qk[:, hs]
        kh = qk[:, D + h * hd:D + (h + 1) * hd] * (hd ** -0.5)
        vh = mv[:, hs]
        qb, kb, vb = qh.astype(BF16), kh.astype(BF16), vh.astype(BF16)
        b_col = b_col_all[:, SM_MF + h:SM_MF + h + 1]
        i_col = gates[:, SM_MI + h:SM_MI + h + 1]
        b_row = b_row_all[M_HEADS + h:M_HEADS + h + 1, :]
        i_row = rows[h:h + 1, :]
        m_prev = m_ref[0, :, h:h + 1]
        c_prev = c_ref[0, h]
        n_prev = n_ref[0, h:h + 1, :]

        dmat = jnp.where(causal, b_col - b_row + i_row, -jnp.inf)
        inter = b_col + m_prev
        m_t = jnp.maximum(inter, jnp.max(dmat, axis=-1, keepdims=True))
        w_intra = jnp.exp(dmat - m_t)
        w_inter = jnp.exp(inter - m_t)
        s = lax.dot_general(qb, kb, (((1,), (1,)), ((), ())), preferred_element_type=F32) * w_intra
        qc = lax.dot_general(qb, c_prev.astype(BF16), (((1,), (1,)), ((), ())),
                             preferred_element_type=F32)
        num = jnp.dot(s.astype(BF16), vb, preferred_element_type=F32) + w_inter * qc
        den = jnp.sum(s, axis=-1, keepdims=True) + w_inter * jnp.sum(qh * n_prev, axis=-1, keepdims=True)
        denom = jnp.maximum(jnp.abs(den), jnp.exp(-m_t))
        hh = num / denom

        m_new = m_t[L - 1:L, :]
        b_last = b_col[L - 1:L, :]
        g_col = jnp.exp(b_last - b_col + i_col - m_new)
        decay = jnp.exp(b_last + m_prev - m_new)
        gv = (g_col * vh).astype(BF16)
        c_ref[0, h] = decay * c_prev + lax.dot_general(
            gv, kb, (((0,), (0,)), ((), ())), preferred_element_type=F32)
        n_ref[0, h:h + 1, :] = decay * n_prev + jnp.sum(g_col * kh, axis=0, keepdims=True)
        m_ref[0, :, h:h + 1] = m_new

        hn = hh * lax.rsqrt(jnp.mean(hh * hh, axis=-1, keepdims=True) + EPS) * gm_ref[:, hs]
        y_ref[0, :, hs] = (hn * jax.nn.sigmoid(mo[:, hs])).astype(BF16)


def _mlstm(z3, bias_row, w_conv, b_conv, g_mnorm, c0, n0, m0, conv0, L):
    b, t, _ = z3.shape
    nc = t // L
    hd = D // M_HEADS

    def zspec(col, width):
        return pl.BlockSpec((1, L, width), lambda i, c: (i, c, col // width))

    def per_batch(shape):
        nd = len(shape)
        return pl.BlockSpec((1,) + shape, lambda i, c: (i,) + (0,) * nd)

    def const(shape):
        nd = len(shape)
        return pl.BlockSpec(shape, lambda i, c: (0,) * nd)

    return pl.pallas_call(
        functools.partial(_mlstm_kernel, L=L),
        grid=(b, nc),
        in_specs=[zspec(COL_MQ, D), zspec(COL_MK, D), zspec(COL_MV, D), zspec(COL_MO, D),
                  zspec(COL_SMALL, LANES), const((1, LANES)), const((CONV_W, 2 * D)),
                  const((1, 2 * D)), const((1, D)),
                  per_batch((M_HEADS, hd, hd)), per_batch((M_HEADS, hd)),
                  per_batch((1, M_HEADS)), per_batch((CONV_W - 1, 2 * D))],
        out_specs=[pl.BlockSpec((1, L, D), lambda i, c: (i, c, 0)),
                   per_batch((M_HEADS, hd, hd)), per_batch((M_HEADS, hd)),
                   per_batch((1, M_HEADS)), per_batch((CONV_W - 1, 2 * D))],
        out_shape=[jax.ShapeDtypeStruct((b, t, D), BF16),
                   jax.ShapeDtypeStruct((b, M_HEADS, hd, hd), F32),
                   jax.ShapeDtypeStruct((b, M_HEADS, hd), F32),
                   jax.ShapeDtypeStruct((b, 1, M_HEADS), F32),
                   jax.ShapeDtypeStruct((b, CONV_W - 1, 2 * D), F32)],
        scratch_shapes=[pltpu.VMEM((8 + L, 2 * D), F32)],
        compiler_params=_cparams(("parallel", "arbitrary")),
        name="mlstm",
    )(z3, z3, z3, z3, z3, bias_row, w_conv, b_conv, g_mnorm, c0, n0, m0, conv0)


def _eye_bf16(n):
    return (lax.broadcasted_iota(jnp.int32, (n, n), 0)
            == lax.broadcasted_iota(jnp.int32, (n, n), 1)).astype(BF16)


def _transpose_bf16(eye, a):
    return lax.dot_general(eye, a, (((1,), (1,)), ((), ())), preferred_element_type=F32)


def _frames_to_lanes(eye, a):
    if a.shape[0] % LANES == 0:
        return a.T.astype(BF16)
    return _transpose_bf16(eye, a.astype(BF16)).astype(BF16)


def _lanes_to_frames(eye, a):
    if a.shape[1] % LANES == 0:
        return a.T.astype(BF16)
    return _transpose_bf16(eye, a.astype(BF16)).astype(BF16)


def _qkv_kernel(x_ref, g_ref, w_ref, gq_ref, gk_ref, k32_ref, v32_ref, kb_ref, qt_ref, vt_ref):
    hd = D // A_HEADS
    x = x_ref[0]
    ms = jnp.mean(x * x, axis=-1, keepdims=True)
    xn = ((x * lax.rsqrt(ms + EPS)) * g_ref[...]).astype(BF16)
    eye = _eye_bf16(hd)

    def project(col):
        return jnp.dot(xn, w_ref[:, col:col + 2 * hd], preferred_element_type=F32)

    def head_norm(a, gain):
        return (a * lax.rsqrt(jnp.mean(a * a, axis=-1, keepdims=True) + EPS)) * gain

    for pair in range(0, D, 2 * hd):
        q2, k2, v2 = project(pair), project(D + pair), project(2 * D + pair)
        for half in range(2):
            cs = slice(half * hd, (half + 1) * hd)
            hs = slice(pair + half * hd, pair + (half + 1) * hd)
            qn = head_norm(q2[:, cs], gq_ref[...]) * Q_LOGIT_SCALE
            qt_ref[0, hs, :] = _frames_to_lanes(eye, qn)
            kn = head_norm(k2[:, cs], gk_ref[...])
            k32_ref[0, :, hs] = kn
            kb_ref[0, :, hs] = kn.astype(BF16)
            v32_ref[0, :, hs] = v2[:, cs]
            h = (pair + half * hd) // hd
            vt_ref[0, 0, h * VT_HEAD_ROWS:h * VT_HEAD_ROWS + hd, :] = _frames_to_lanes(eye, v2[:, cs])
            vt_ref[0, 0, h * VT_HEAD_ROWS + hd:(h + 1) * VT_HEAD_ROWS, :] = jnp.ones(
                (VT_HEAD_ROWS - hd, x.shape[0]), BF16)


def _qkv(x, g_norm, w_att, g_q, g_k, tm):
    b, t, _ = x.shape
    hd = D // A_HEADS
    row = pl.BlockSpec((1, tm, D), lambda i, r: (i, r, 0))
    gspec = pl.BlockSpec((1, hd), lambda i, r: (0, 0))
    return pl.pallas_call(
        _qkv_kernel,
        grid=(b, t // tm),
        in_specs=[row, pl.BlockSpec((1, D), lambda i, r: (0, 0)),
                  pl.BlockSpec((D, 3 * D), lambda i, r: (0, 0)), gspec, gspec],
        out_specs=[row, row, row, pl.BlockSpec((1, D, tm), lambda i, r: (i, 0, r)),
                   pl.BlockSpec((1, 1, A_HEADS * VT_HEAD_ROWS, tm), lambda i, r: (i, r, 0, 0))],
        out_shape=[jax.ShapeDtypeStruct((b, t, D), F32), jax.ShapeDtypeStruct((b, t, D), F32),
                   jax.ShapeDtypeStruct((b, t, D), BF16), jax.ShapeDtypeStruct((b, D, t), BF16),
                   jax.ShapeDtypeStruct((b, t // tm, A_HEADS * VT_HEAD_ROWS, tm), BF16)],
        compiler_params=_cparams(("parallel", "parallel")),
        name="qkv_proj_norm",
    )(x, g_norm, w_att, g_q, g_k)


def _num_key_tiles(qi, tq, tile, q_pos0, s_real):
    last_chunk = (q_pos0 + (qi + 1) * tq - 1) // CHUNK
    kend = jnp.minimum((last_chunk + 1) * CHUNK, s_real)
    return (kend + tile - 1) // tile


def _num_key_tiles_static(qi, tq, tile, q_pos0, s_real):
    last_chunk = (q_pos0 + (qi + 1) * tq - 1) // CHUNK
    return -(-min((last_chunk + 1) * CHUNK, s_real) // tile)


def _select_kernel(iq_ref, sm_ref, knew_ref, *refs, tq, kb, s_pad, s_real, q_pos0, topk):
    kpast_ref = refs[0] if q_pos0 > 0 else None
    mask_ref, score_ref, iqt_ref, ki_ref = refs[-4:]
    qi = pl.program_id(1)
    nkt = _num_key_tiles(qi, tq, kb, q_pos0, s_real)
    kf = float(topk)
    groups = kb // SUBLANES

    @pl.when(qi == 0)
    def _():
        def put(row0, x):
            hi = x.astype(BF16)
            lo = (x - hi.astype(F32)).astype(BF16)
            ki_ref[row0:row0 + x.shape[0], :] = jnp.concatenate([hi, lo, hi], axis=1)

        if kpast_ref is not None:
            put(0, kpast_ref[0])
        t_new = knew_ref.shape[1]
        for r in range(0, t_new, KEY_PREP_ROWS):
            n = min(KEY_PREP_ROWS, t_new - r)
            put(q_pos0 + r, knew_ref[0, r:r + n, 0:IDX_DIM])
        if s_pad > s_real:
            ki_ref[s_real:s_pad, :] = jnp.zeros((s_pad - s_real, 3 * IDX_DIM), BF16)

    pick = (lax.broadcasted_iota(jnp.int32, (IDX_HEADS, LANES), 1)
            == lax.broadcasted_iota(jnp.int32, (IDX_HEADS, LANES), 0) + SM_IW).astype(F32)
    w = lax.dot_general(pick, sm_ref[0], (((1,), (1,)), ((), ())), precision=HIGHEST,
                        preferred_element_type=F32) * ((IDX_HEADS * IDX_DIM) ** -0.5)
    eye = _eye_bf16(IDX_DIM)
    for h in range(IDX_HEADS):
        a = iq_ref[0, :, h * IDX_DIM:(h + 1) * IDX_DIM]
        hi = a.astype(BF16)
        lo = (a - hi.astype(F32)).astype(BF16)
        hi_t = _transpose_bf16(eye, hi).astype(BF16)
        lo_t = _transpose_bf16(eye, lo).astype(BF16)
        cols = slice((h % 2) * tq, (h % 2 + 1) * tq)
        iqt_ref[h // 2, 0:IDX_DIM, cols] = hi_t
        iqt_ref[h // 2, IDX_DIM:2 * IDX_DIM, cols] = hi_t
        iqt_ref[h // 2, 2 * IDX_DIM:3 * IDX_DIM, cols] = lo_t

    q_pos = q_pos0 + qi * tq + lax.broadcasted_iota(jnp.int32, (1, tq), 1)
    q_chunk = q_pos >> CHUNK_SHIFT
    first_key = jnp.minimum(((q_pos0 + qi * tq) >> CHUNK_SHIFT) << CHUNK_SHIFT, s_real)
    n_full = first_key // kb

    def score_tile(j, carry, masked):
        rmin, rmax = carry
        k0 = pl.multiple_of(j * kb, kb)
        kt = ki_ref[pl.ds(k0, kb), :]
        acc = None
        for p in range(IDX_HEADS // 2):
            rel = jnp.dot(kt, iqt_ref[p], preferred_element_type=F32)
            part = (w[2 * p:2 * p + 1] * jnp.maximum(rel[:, 0:tq], 0.0)
                    + w[2 * p + 1:2 * p + 2] * jnp.maximum(rel[:, tq:2 * tq], 0.0))
            acc = part if acc is None else acc + part
        if masked:
            key = k0 + lax.broadcasted_iota(jnp.int32, (kb, tq), 0)
            adm = jnp.logical_and((key >> CHUNK_SHIFT) <= q_chunk, key < s_real)
            low = jnp.where(adm, acc, jnp.inf)
            acc = jnp.where(adm, acc, -jnp.inf)
        else:
            low = acc
        score_ref[pl.ds(k0, kb), :] = acc
        rmin = jnp.minimum(rmin, jnp.min(low.reshape(groups, SUBLANES, tq), axis=0))
        rmax = jnp.maximum(rmax, jnp.max(acc.reshape(groups, SUBLANES, tq), axis=0))
        return rmin, rmax

    stats = (jnp.full((SUBLANES, tq), jnp.inf, F32), jnp.full((SUBLANES, tq), -jnp.inf, F32))
    stats = lax.fori_loop(0, n_full, functools.partial(score_tile, masked=False), stats)
    rmin8, rmax8 = lax.fori_loop(n_full, nkt, functools.partial(score_tile, masked=True), stats)
    rmin = jnp.min(rmin8, axis=0, keepdims=True)
    rmax = jnp.max(rmax8, axis=0, keepdims=True)

    def count(pred):
        def body(j, acc):
            for c in range(kb // COUNT_ROWS):
                k0 = pl.multiple_of(j * kb + c * COUNT_ROWS, COUNT_ROWS)
                hit = jnp.where(pred(score_ref[pl.ds(k0, COUNT_ROWS), :], k0), 1.0, 0.0)
                acc = acc + jnp.sum(hit.reshape(COUNT_ROWS // (COUNT_PARTS * SUBLANES), COUNT_PARTS,
                                                SUBLANES, tq), axis=0)
            return acc
        acc = lax.fori_loop(0, nkt, body, jnp.zeros((COUNT_PARTS, SUBLANES, tq), F32))
        return jnp.sum(jnp.sum(acc, axis=0), axis=0, keepdims=True)

    def below(cur):
        shape = (COUNT_ROWS // (COUNT_PARTS * SUBLANES), COUNT_PARTS, SUBLANES, tq)

        def body(j, carry):
            top, num = carry
            for c in range(kb // COUNT_ROWS):
                k0 = pl.multiple_of(j * kb + c * COUNT_ROWS, COUNT_ROWS)
                s = score_ref[pl.ds(k0, COUNT_ROWS), :]
                under = s < cur
                top = jnp.maximum(top, jnp.max(jnp.where(under, s, -jnp.inf).reshape(shape), axis=0))
                num = num + jnp.sum(jnp.where(under, 0.0, 1.0).reshape(shape), axis=0)
            return top, num

        top, num = lax.fori_loop(0, nkt, body, (jnp.full(shape[1:], -jnp.inf, F32),
                                                jnp.zeros(shape[1:], F32)))
        return (jnp.max(jnp.max(top, axis=0), axis=0, keepdims=True),
                jnp.sum(jnp.sum(num, axis=0), axis=0, keepdims=True))

    n_adm = jnp.minimum((q_chunk + 1) << CHUNK_SHIFT, s_real).astype(F32)

    def open_rows(cnt_lo):
        return cnt_lo > kf

    def wide_rows(st):
        return jnp.logical_and(open_rows(st[2]), st[2] - st[3] > FINISH_MAX)

    def flag(pred):
        return jnp.max(jnp.where(pred, 1.0, 0.0))

    def bisect(st):
        lo, hi, cnt_lo, cnt_hi = st
        active = open_rows(cnt_lo)
        mid = 0.5 * jnp.maximum(lo, rmin) + 0.5 * jnp.minimum(hi, rmax)
        cm = count(lambda s, k0: s >= mid)
        up = jnp.logical_and(active, cm >= kf)
        dn = jnp.logical_and(active, cm < kf)
        return (jnp.where(up, mid, lo), jnp.where(dn, mid, hi),
                jnp.where(up, cm, cnt_lo), jnp.where(dn, cm, cnt_hi))

    st = (jnp.full((1, tq), -F32_MAX, F32), jnp.full((1, tq), F32_MAX, F32), n_adm,
          jnp.zeros((1, tq), F32))
    st = lax.fori_loop(0, SELECT_MIN_ITERS, lambda _, s: bisect(s), st)

    def narrow(c):
        nxt = bisect(c[2])
        return c[0] + 1, flag(wide_rows(nxt)), nxt

    _, _, st = lax.while_loop(
        lambda c: jnp.logical_and(c[0] < SELECT_MAX_ITERS, c[1] > 0.0), narrow,
        (jnp.int32(SELECT_MIN_ITERS), flag(wide_rows(st)), st))

    lo, hi, cnt_lo, cnt_hi = st
    cand, _ = below(hi)

    def walk(c):
        _, walking, lo, hi, cnt_lo, cnt_hi, cand = c
        nxt, cge = below(cand)
        settle = jnp.logical_and(walking > 0.0, cge >= kf)
        move = jnp.logical_and(walking > 0.0, cge < kf)
        walking = jnp.where(move, 1.0, 0.0)
        return (jnp.max(walking), walking, jnp.where(settle, cand, lo), jnp.where(move, cand, hi),
                jnp.where(settle, cge, cnt_lo), jnp.where(move, cge, cnt_hi), jnp.where(move, nxt, cand))

    walking = jnp.where(open_rows(cnt_lo), 1.0, 0.0)
    _, _, lo, hi, cnt_lo, cnt_hi, _ = lax.while_loop(
        lambda c: c[0] > 0.0, walk, (jnp.max(walking), walking, lo, hi, cnt_lo, cnt_hi, cand))
    n_open = flag(open_rows(cnt_lo))

    def clear_tile(j, carry):
        k0 = pl.multiple_of(j * kb, kb)
        mask_ref[0, 0, pl.ds(k0, kb), :] = jnp.zeros((kb, tq), jnp.int8)
        return carry

    lax.fori_loop(nkt, s_pad // kb, clear_tile, 0)

    @pl.when(n_open == 0.0)
    def _():
        def write_tile(j, carry):
            k0 = pl.multiple_of(j * kb, kb)
            keep = score_ref[pl.ds(k0, kb), :] >= lo
            mask_ref[0, 0, pl.ds(k0, kb), :] = jnp.where(keep, 1, 0).astype(jnp.int8)
            return carry

        lax.fori_loop(0, nkt, write_tile, 0)

    @pl.when(n_open > 0.0)
    def _():
        tied = open_rows(cnt_lo)
        need = jnp.where(tied, kf - cnt_hi, 0.0)
        key_iota = lax.broadcasted_iota(jnp.int32, (kb, tq), 0)
        strip_iota = lax.broadcasted_iota(jnp.int32, (COUNT_ROWS, tq), 0)
        shape = (COUNT_ROWS // (COUNT_PARTS * SUBLANES), COUNT_PARTS, SUBLANES, tq)

        def next_tied(prev):
            def body(j, acc):
                for c in range(kb // COUNT_ROWS):
                    k0 = pl.multiple_of(j * kb + c * COUNT_ROWS, COUNT_ROWS)
                    idx = k0 + strip_iota
                    hit = jnp.logical_and(score_ref[pl.ds(k0, COUNT_ROWS), :] == lo, idx > prev)
                    acc = jnp.minimum(acc, jnp.min(jnp.where(hit, idx, s_pad).reshape(shape), axis=0))
                return acc
            acc = lax.fori_loop(0, nkt, body, jnp.full(shape[1:], s_pad, jnp.int32))
            return jnp.min(jnp.min(acc, axis=0), axis=0, keepdims=True)

        def cut_by_walk(_):
            def step(c):
                _, left, last = c
                take = left > 0.0
                last = jnp.where(take, next_tied(last), last)
                left = jnp.where(take, left - 1.0, left)
                return jnp.max(left), left, last
            _, _, last = lax.while_loop(lambda c: c[0] > 0.0, step,
                                        (jnp.max(need), need, jnp.full((1, tq), -1, jnp.int32)))
            return last + 1

        def cut_by_bisection(_):
            def step(_, st):
                jlo, jhi = st
                jmid = (jlo + jhi) >> 1
                cm = count(lambda s, k0: jnp.logical_and(s == lo, k0 + strip_iota < jmid))
                ok = cm >= need
                return jnp.where(ok, jlo, jmid), jnp.where(ok, jmid, jhi)
            steps = int(math.ceil(math.log2(s_pad))) + 1
            return lax.fori_loop(0, steps, step, (jnp.zeros((1, tq), jnp.int32),
                                                  jnp.full((1, tq), s_pad, jnp.int32)))[1]

        jcut = lax.cond(jnp.max(need) <= TIE_WALK_MAX, cut_by_walk, cut_by_bisection, 0)
        jcut = jnp.where(tied, jcut, s_pad)

        def write_tile(j, carry):
            k0 = pl.multiple_of(j * kb, kb)
            s = score_ref[pl.ds(k0, kb), :]
            keep = jnp.logical_or(s >= hi, jnp.logical_and(s >= lo, k0 + key_iota < jcut))
            mask_ref[0, 0, pl.ds(k0, kb), :] = jnp.where(keep, 1, 0).astype(jnp.int8)
            return carry

        lax.fori_loop(0, nkt, write_tile, 0)


def _select(z3, ki_past, tq, kb, s_pad, topk):
    b, t, _ = z3.shape
    past = ki_past.shape[1]
    iq_w = IDX_HEADS * IDX_DIM
    kern = functools.partial(_select_kernel, tq=tq, kb=kb, s_pad=s_pad, s_real=past + t,
                             q_pos0=past, topk=topk)
    cached = ([pl.BlockSpec((1, past, IDX_DIM), lambda i, j: (i, 0, 0))], [ki_past]) if past else ([], [])
    return pl.pallas_call(
        kern,
        grid=(b, t // tq),
        in_specs=[pl.BlockSpec((1, tq, iq_w), lambda i, j: (i, j, COL_IQ // iq_w)),
                  pl.BlockSpec((1, tq, LANES), lambda i, j: (i, j, COL_SMALL // LANES)),
                  pl.BlockSpec((1, t, LANES), lambda i, j: (i, 0, COL_SMALL // LANES))] + cached[0],
        out_specs=pl.BlockSpec((1, 1, s_pad, tq), lambda i, j: (i, j, 0, 0)),
        out_shape=jax.ShapeDtypeStruct((b, t // tq, s_pad, tq), jnp.int8),
        scratch_shapes=[pltpu.VMEM((s_pad, tq), F32),
                        pltpu.VMEM((IDX_HEADS // 2, 3 * IDX_DIM, 2 * tq), BF16),
                        pltpu.VMEM((s_pad, 3 * IDX_DIM), BF16)],
        compiler_params=_cparams(("parallel", "arbitrary")),
        name="index_select",
    )(z3, z3, z3, *cached[1])


def _attn_kernel(q_of_ref, k_of_ref, qt_ref, k_ref, vt_ref, *refs, n_mask):
    mask_refs = refs[:n_mask]
    o_ref, acc_ref, m_ref, l_ref, s_ref, bias_ref = refs[n_mask:]
    step = pl.program_id(1)
    hd = D // A_HEADS
    tq = qt_ref.shape[2]
    tk = k_ref.shape[1]
    tq_mask = tq // n_mask
    pv_rows = math.gcd(tk, PV_ROWS)
    first = k_of_ref[step] == 0
    nxt = jnp.minimum(step + 1, pl.num_programs(1) - 1)
    last = jnp.logical_or(step == pl.num_programs(1) - 1, q_of_ref[nxt] != q_of_ref[step])

    @pl.when(first)
    def _():
        acc_ref[...] = jnp.zeros(acc_ref.shape, F32)
        m_ref[...] = jnp.full(m_ref.shape, MASK_NEG, F32)
        l_ref[...] = jnp.zeros(l_ref.shape, F32)

    for n, mask_ref in enumerate(mask_refs):
        bias_ref[:, n * tq_mask:(n + 1) * tq_mask] = jnp.where(
            mask_ref[0, 0].astype(jnp.int32) != 0, 0.0, MASK_NEG)
    def logits(h):
        hs = slice(h * hd, (h + 1) * hd)
        top = None
        for r in range(0, tk, QK_ROWS):
            s = jnp.dot(k_ref[0, r:r + QK_ROWS, hs], qt_ref[0, hs, :],
                        preferred_element_type=F32) + bias_ref[r:r + QK_ROWS, :]
            s_ref[h, r:r + QK_ROWS, :] = s
            part = jnp.max(s.reshape(QK_ROWS // SUBLANES, SUBLANES, tq), axis=0)
            top = part if top is None else jnp.maximum(top, part)
        return jnp.max(top, axis=0, keepdims=True)

    def weigh(h, tile_max):
        hs = slice(h * hd, (h + 1) * hd)
        m_prev = m_ref[h:h + 1, :]
        m_new = jnp.maximum(m_prev, tile_max)
        alpha = jnp.exp2(m_prev - m_new)
        pv = None
        for r in range(0, tk, pv_rows):
            p = jnp.exp2(s_ref[h, r:r + pv_rows, :] - m_new)
            prod = jnp.dot(vt_ref[0, 0, h * VT_HEAD_ROWS:(h + 1) * VT_HEAD_ROWS, r:r + pv_rows],
                           p.astype(BF16), preferred_element_type=F32)
            pv = prod if pv is None else pv + prod
        l_ref[h:h + 1, :] = alpha * l_ref[h:h + 1, :] + pv[hd:hd + 1, :]
        acc_ref[hs, :] = alpha * acc_ref[hs, :] + pv[0:hd, :]
        m_ref[h:h + 1, :] = m_new

    tile_max = [logits(h) for h in range(A_HEADS)]

    @pl.when(k_of_ref[step] >= 0)
    def _():
        for h in range(A_HEADS):
            weigh(h, tile_max[h])

    @pl.when(last)
    def _():
        eye = _eye_bf16(tq)
        for h in range(A_HEADS):
            hs = slice(h * hd, (h + 1) * hd)
            o_ref[0, :, hs] = _lanes_to_frames(eye, acc_ref[hs, :] / l_ref[h:h + 1, :])


def _attention(qt, kb, vt, mask_t, tq, tk, s_real, q_pos0):
    b, _, t = qt.shape
    n_mask = tq // mask_t.shape[3]
    steps = [(q, j) for q in range(t // tq) for j in range(_num_key_tiles_static(q, tq, tk, q_pos0, s_real))]
    q_of = jnp.array([q for q, _ in steps], jnp.int32)
    k_of = jnp.array([j for _, j in steps], jnp.int32)

    def mask_spec(n):
        return pl.BlockSpec((1, 1, tk, tq // n_mask),
                            lambda i, s, q_of, k_of: (i, q_of[s] * n_mask + n, k_of[s], 0))

    return pl.pallas_call(
        functools.partial(_attn_kernel, n_mask=n_mask),
        grid_spec=pltpu.PrefetchScalarGridSpec(
            num_scalar_prefetch=2,
            grid=(b, len(steps)),
            in_specs=[pl.BlockSpec((1, D, tq), lambda i, s, q_of, k_of: (i, 0, q_of[s])),
                      pl.BlockSpec((1, tk, D), lambda i, s, q_of, k_of: (i, k_of[s], 0)),
                      pl.BlockSpec((1, 1, A_HEADS * VT_HEAD_ROWS, tk), lambda i, s, q_of, k_of: (i, k_of[s], 0, 0))]
            + [mask_spec(n) for n in range(n_mask)],
            out_specs=pl.BlockSpec((1, tq, D), lambda i, s, q_of, k_of: (i, q_of[s], 0)),
            scratch_shapes=[pltpu.VMEM((D, tq), F32),
                            pltpu.VMEM((A_HEADS, tq), F32),
                            pltpu.VMEM((A_HEADS, tq), F32),
                            pltpu.VMEM((A_HEADS, tk, tq), F32),
                            pltpu.VMEM((tk, tq), F32)]),
        out_shape=jax.ShapeDtypeStruct((b, t, D), BF16),
        compiler_params=_cparams(("parallel", "arbitrary")),
        name="masked_attention",
    )(q_of, k_of, qt, kb, vt, *([mask_t] * n_mask))


def _attn_past_kernel(qt_ref, kn_ref, vtn_ref, kp_ref, vp_ref, mask_ref, o_ref):
    hd = D // A_HEADS
    t = qt_ref.shape[2]
    past = kp_ref.shape[1] // A_HEADS

    def cached(ref, h):
        return ref[0, pl.ds(h, past, stride=A_HEADS), :].astype(BF16)

    bias = jnp.where(mask_ref[0, 0].astype(jnp.int32) != 0, 0.0, MASK_NEG)
    bias_p, bias_n = bias[0:past], bias[past:past + t]
    eye_hd, eye_t = _eye_bf16(hd), _eye_bf16(t)
    for h in range(A_HEADS):
        hs = slice(h * hd, (h + 1) * hd)
        q = qt_ref[0, hs, :]
        s_p = jnp.dot(cached(kp_ref, h), q, preferred_element_type=F32) + bias_p
        s_n = jnp.dot(kn_ref[0, :, hs], q, preferred_element_type=F32) + bias_n
        m = jnp.maximum(jnp.max(s_p, axis=0, keepdims=True), jnp.max(s_n, axis=0, keepdims=True))
        p_p = jnp.exp2(s_p - m)
        p_n = jnp.exp2(s_n - m)
        l = jnp.sum(p_p, axis=0, keepdims=True) + jnp.sum(p_n, axis=0, keepdims=True)
        vt_p = _transpose_bf16(eye_hd, cached(vp_ref, h)).astype(BF16)
        out_t = (jnp.dot(vt_p, p_p.astype(BF16), preferred_element_type=F32)
                 + jnp.dot(vtn_ref[0, 0, h * VT_HEAD_ROWS:h * VT_HEAD_ROWS + hd, :], p_n.astype(BF16),
                           preferred_element_type=F32))
        o_ref[0, :, hs] = _transpose_bf16(eye_t, (out_t / l).astype(BF16)).astype(BF16)


def _attention_with_past(qt, kb, vt, k_past, v_past, mask_t):
    b, _, t = qt.shape
    past, heads, hd = k_past.shape[1:]
    s_pad = mask_t.shape[2]
    k_past = k_past.reshape(b, past * heads, hd)
    v_past = v_past.reshape(b, past * heads, hd)
    cache = pl.BlockSpec((1, past * heads, hd), lambda i: (i, 0, 0))
    return pl.pallas_call(
        _attn_past_kernel,
        grid=(b,),
        in_specs=[pl.BlockSpec((1, D, t), lambda i: (i, 0, 0)),
                  pl.BlockSpec((1, t, D), lambda i: (i, 0, 0)),
                  pl.BlockSpec((1, 1, A_HEADS * VT_HEAD_ROWS, t), lambda i: (i, 0, 0, 0)),
                  cache, cache,
                  pl.BlockSpec((1, 1, s_pad, t), lambda i: (i, 0, 0, 0))],
        out_specs=pl.BlockSpec((1, t, D), lambda i: (i, 0, 0)),
        out_shape=jax.ShapeDtypeStruct((b, t, D), BF16),
        compiler_params=_cparams(("parallel",)),
        name="masked_attention_past",
    )(qt, kb, vt, k_past, v_past, mask_t)


def _merge_kernel(x_ref, ya_ref, yb_ref, ga_ref, gb_ref, wa_ref, wb_ref, wo_ref, o_ref):
    a = jnp.dot(ya_ref[...], wa_ref[...], preferred_element_type=F32)
    bb = jnp.dot(yb_ref[...], wb_ref[...], preferred_element_type=F32)
    mix = jax.nn.sigmoid(ga_ref[...]) * a + jax.nn.sigmoid(gb_ref[...]) * bb
    o_ref[...] = x_ref[...] + jnp.dot(mix.astype(BF16), wo_ref[...], preferred_element_type=F32)


def _merge(x, ya, yb, z, wa, wb, wo, tm):
    n = x.shape[0]
    row = pl.BlockSpec((tm, D), lambda i: (i, 0))
    wspec = pl.BlockSpec((D, D), lambda i: (0, 0))
    return pl.pallas_call(
        _merge_kernel,
        grid=(n // tm,),
        in_specs=[row, row, row,
                  pl.BlockSpec((tm, D), lambda i: (i, COL_GA // D)),
                  pl.BlockSpec((tm, D), lambda i: (i, COL_GB // D)),
                  wspec, wspec, wspec],
        out_specs=row,
        out_shape=jax.ShapeDtypeStruct((n, D), F32),
        compiler_params=_cparams(("parallel",)),
        name="merge_out_proj",
    )(x, ya, yb, z, z, wa, wb, wo)


def _ffn_kernel(x_ref, g_ref, wg_ref, wu_ref, wd_ref, o_ref, xn_ref, acc_ref):
    c = pl.program_id(1)

    @pl.when(c == 0)
    def _():
        x = x_ref[...]
        ms = jnp.mean(x * x, axis=-1, keepdims=True)
        xn_ref[...] = ((x * lax.rsqrt(ms + EPS)) * g_ref[...]).astype(BF16)
        acc_ref[...] = x

    xn = xn_ref[...]
    gate = jnp.dot(xn, wg_ref[...], preferred_element_type=F32)
    up = jnp.dot(xn, wu_ref[...], preferred_element_type=F32)
    act = (gate * jax.nn.sigmoid(gate)) * up
    acc_ref[...] += jnp.dot(act.astype(BF16), wd_ref[...], preferred_element_type=F32)

    @pl.when(c == pl.num_programs(1) - 1)
    def _():
        o_ref[...] = acc_ref[...]


def _ffn(x, g, w_in, w_out, tm, tf):
    n = x.shape[0]
    dff = w_out.shape[0]
    nf = dff // tf
    row = pl.BlockSpec((tm, D), lambda i, c: (i, 0))
    once = {"pipeline_mode": pl.Buffered(1)} if nf == 1 else {}
    return pl.pallas_call(
        _ffn_kernel,
        grid=(n // tm, nf),
        in_specs=[row, pl.BlockSpec((1, D), lambda i, c: (0, 0)),
                  pl.BlockSpec((D, tf), lambda i, c: (0, c), **once),
                  pl.BlockSpec((D, tf), lambda i, c: (0, nf + c), **once),
                  pl.BlockSpec((tf, D), lambda i, c: (c, 0), **once)],
        out_specs=row,
        out_shape=jax.ShapeDtypeStruct((n, D), F32),
        scratch_shapes=[pltpu.VMEM((tm, D), BF16), pltpu.VMEM((tm, D), F32)],
        compiler_params=_cparams(("parallel", "arbitrary")),
        name="swiglu_ffn",
    )(x, g, w_in, w_in, w_out)


def _layer(x, past, p, cfg):
    k_past, v_past, ki_past, c0, n0, m0, conv0 = past
    b, t, _ = x.shape
    n_tok = b * t
    past_len = k_past.shape[1]
    s_real = past_len + t
    topk = min(TOPK_MAX, s_real // 4)
    tq_sel, kb_sel, tq_att, tk_att = cfg["tq_sel"], cfg["kb_sel"], cfg["tq_att"], cfg["tk_att"]
    s_pad = -(-s_real // tk_att) * tk_att

    x2 = x.reshape(n_tok, D)
    z = _norm_matmul(x2, p["g_norm1"], p["w_in"], cfg["tm_proj"], cfg["tn_proj"])
    z3 = z.reshape(b, t, D_Z)

    y_a, c_new, n_new, m_new, conv_new = _mlstm(
        z3, p["gate_bias"], p["w_conv"], p["b_conv"], p["g_mnorm"], c0, n0,
        m0.reshape(b, 1, M_HEADS), conv0, cfg["mlstm_chunk"])

    k32, v32, kb, qt, vt = _qkv(x, p["g_norm1"], p["w_att"], p["g_q"], p["g_k"], cfg["tm_qkv"])
    ik = z3[:, :, COL_SMALL:COL_SMALL + IDX_DIM]
    mask_t = _select(z3, ki_past.astype(F32), tq_sel, kb_sel, s_pad, topk)
    if past_len == 0:
        assert cfg["tm_qkv"] == tk_att and s_pad == s_real
        y_b = _attention(qt, kb, vt, mask_t, tq_att, tk_att, s_real, past_len)
    else:
        assert cfg["tm_qkv"] == t == tq_sel
        y_b = _attention_with_past(qt, kb, vt, k_past, v_past, mask_t)

    x1 = _merge(x2, y_a.reshape(n_tok, D), y_b.reshape(n_tok, D), z,
                p["w_a_out"], p["w_b_out"], p["w_o"], cfg["tm_rows"])
    y = _ffn(x1, p["g_norm2"], p["w_ffn_in"], p["w_ffn_out"], cfg["tm_ffn"], cfg["tf_ffn"])

    hd = D // A_HEADS
    return y.reshape(b, t, D), (k32.reshape(b, t, A_HEADS, hd), v32.reshape(b, t, A_HEADS, hd), ik,
                                c_new, n_new, m_new.reshape(b, M_HEADS), conv_new)


O_MI = 4 * D
O_AQ = O_MI + 2 * M_HEADS
O_IQ = O_AQ + 3 * D
O_IK = O_IQ + IDX_HEADS * IDX_DIM
O_IW = O_IK + IDX_DIM
O_GA = O_IW + IDX_HEADS
D_IN = O_GA + 2 * D


def _split_w_kernel(w_ref, main_ref, att_ref):
    rows = w_ref.shape[0]

    def columns(first, width):
        start = first // LANES * LANES
        win = width + LANES
        if start + win <= D_IN:
            x = w_ref[:, start:start + win]
        else:
            tail = D_IN - (start + width)
            x = jnp.concatenate([w_ref[:, start:start + width],
                                 w_ref[:, start + width:D_IN],
                                 jnp.zeros((rows, LANES - tail), F32)], axis=1)
        return pltpu.roll(x, win - (first - start), axis=1)[:, 0:width]

    att_ref[...] = columns(O_AQ, 3 * D).astype(BF16)
    main_ref[:, 0:O_MI] = w_ref[:, 0:O_MI].astype(BF16)
    main_ref[:, COL_GA:COL_GA + 2 * D] = columns(O_GA, 2 * D).astype(BF16)
    main_ref[:, COL_IQ:COL_IQ + IDX_HEADS * IDX_DIM] = columns(O_IQ, IDX_HEADS * IDX_DIM).astype(BF16)
    lane = lax.broadcasted_iota(jnp.int32, (rows, LANES), 1)
    ik_iw = columns(O_IK, LANES)
    gates = pltpu.roll(w_ref[:, O_MI:O_MI + LANES], SM_MI, axis=1)
    small = jnp.where(lane < SM_MI, ik_iw, jnp.where(lane < SM_MF + M_HEADS, gates, 0.0))
    main_ref[:, COL_SMALL:COL_SMALL + LANES] = small.astype(BF16)
    main_ref[:, COL_SMALL + LANES:D_Z] = jnp.zeros((rows, D_Z - COL_SMALL - LANES), BF16)


def _split_w(w_in, layer, tr):
    return pl.pallas_call(
        _split_w_kernel,
        grid=(D // tr,),
        in_specs=[pl.BlockSpec((None, tr, D_IN), lambda i: (layer, i, 0))],
        out_specs=[pl.BlockSpec((tr, D_Z), lambda i: (i, 0)), pl.BlockSpec((tr, 3 * D), lambda i: (i, 0))],
        out_shape=[jax.ShapeDtypeStruct((D, D_Z), BF16), jax.ShapeDtypeStruct((D, 3 * D), BF16)],
        compiler_params=_cparams(("parallel",)),
        name="split_in_proj_weight",
    )(w_in)


def _prep_params(g_norm1, w_in, layer, b_if, w_conv, b_conv, g_mnorm, g_q, g_k, w_a_out, w_b_out, w_o,
                 g_norm2, w_ffn_in, w_ffn_out):
    assert w_in.shape[1:] == (D, D_IN)
    w_perm, w_att = _split_w(w_in, layer, 256)
    gate_bias = jnp.zeros((1, LANES), F32).at[0, SM_MI:SM_MI + 2 * M_HEADS].set(b_if.astype(F32))
    return {
        "g_norm1": g_norm1.reshape(1, D), "w_in": w_perm, "w_att": w_att,
        "gate_bias": gate_bias,
        "w_conv": w_conv, "b_conv": b_conv.reshape(1, 2 * D), "g_mnorm": g_mnorm.reshape(1, D),
        "g_q": g_q.reshape(1, -1), "g_k": g_k.reshape(1, -1),
        "w_a_out": w_a_out.astype(BF16), "w_b_out": w_b_out.astype(BF16), "w_o": w_o.astype(BF16),
        "g_norm2": g_norm2.reshape(1, D), "w_ffn_in": w_ffn_in.astype(BF16),
        "w_ffn_out": w_ffn_out.astype(BF16),
    }


def _config(b, t, past_len):
    n_tok = b * t
    s_real = past_len + t
    tm = min(2048, n_tok)
    tq_sel = min(256, t)
    tq_att = min(512, t)
    if s_real % 512 == 0:
        kb_sel = tk_att = 512
    else:
        kb_sel = tk_att = -(-s_real // LANES) * LANES
    return {"tm_proj": tm, "tn_proj": 1024,"mlstm_chunk": min(128, t), "tm_rows": min(512, n_tok),
            "tm_qkv": min(512, t), "tq_sel": tq_sel, "kb_sel": kb_sel, "tq_att": tq_att,
            "tk_att": tk_att, "tm_ffn": min(512, n_tok), "tf_ffn": 2816}


def kernel(x_prompt, x_sample, cache_k, cache_v, cache_kidx, state_C, state_n, state_m, state_conv,
           g_norm1, w_in, b_if, w_conv, b_conv, g_mnorm, g_q, g_k, w_a_out, w_b_out, w_o,
           g_norm2, w_ffn_in, w_ffn_out):
    depth = w_in.shape[0]
    bp = x_prompt.shape[0]
    hd_a = D // A_HEADS
    hd_m = D // M_HEADS
    yp, ys = x_prompt, x_sample
    new_p, new_s = [], []
    for l in range(depth):
        p = _prep_params(g_norm1[l], w_in, l, b_if[l], w_conv[l], b_conv[l], g_mnorm[l], g_q[l],
                         g_k[l], w_a_out[l], w_b_out[l], w_o[l], g_norm2[l], w_ffn_in[l],
                         w_ffn_out[l])
        empty = (jnp.zeros((bp, 0, A_HEADS, hd_a), F32), jnp.zeros((bp, 0, A_HEADS, hd_a), F32),
                 jnp.zeros((bp, 0, IDX_DIM), F32), jnp.zeros((bp, M_HEADS, hd_m, hd_m), F32),
                 jnp.zeros((bp, M_HEADS, hd_m), F32), jnp.zeros((bp, M_HEADS), F32),
                 jnp.zeros((bp, CONV_W - 1, 2 * D), F32))
        yp, sp = _layer(yp, empty, p, _config(bp, yp.shape[1], 0))
        ys, ss = _layer(ys, (cache_k[l], cache_v[l], cache_kidx[l], state_C[l], state_n[l],
                             state_m[l], state_conv[l]), p,
                        _config(ys.shape[0], ys.shape[1], cache_k.shape[2]))
        new_p.append(sp)
        new_s.append(ss)

    def stk(lst, i):
        return jnp.stack([s[i] for s in lst])

    return (yp, ys) + tuple(stk(new_p, i) for i in range(7)) + tuple(stk(new_s, i) for i in range(7))
```

```python
import functools
import math

import jax
import jax.numpy as jnp
from jax import lax
from jax.experimental import pallas as pl
from jax.experimental.pallas import tpu as pltpu

F32 = jnp.float32
BF16 = jnp.bfloat16
HIGHEST = lax.Precision.HIGHEST

EPS = 1e-6
CHUNK = 64
CHUNK_SHIFT = 6
M_HEADS = 4
A_HEADS = 8
IDX_HEADS = 8
IDX_DIM = 64
CONV_W = 4
TOPK_MAX = 256
LANES = 128
SUBLANES = 8
MASK_NEG = -1e30
F32_MAX = float(jnp.finfo(jnp.float32).max)
VMEM_LIMIT = 48 * 1024 * 1024
SELECT_MIN_ITERS = 14
SELECT_MAX_ITERS = 16
FINISH_MAX = 4
VT_HEAD_ROWS = 144
QK_ROWS = 128
PV_ROWS = 256
TIE_WALK_MAX = 8
COUNT_PARTS = 4
COUNT_ROWS = 128
KEY_PREP_ROWS = 512
WEIGHT_SPLIT_ROWS = 256

D = 1024
COL_MQ, COL_MK, COL_MV, COL_MO = 0, 1024, 2048, 3072
COL_GA, COL_GB = 4096, 5120
COL_IQ = 6144
COL_SMALL = 6656
D_Z = 7168
SM_IW, SM_MI, SM_MF = 64, 72, 76
Q_LOGIT_SCALE = math.log2(math.e) * (D // A_HEADS) ** -0.5


def _cparams(sem):
    return pltpu.CompilerParams(dimension_semantics=sem, vmem_limit_bytes=VMEM_LIMIT)


def _norm_matmul_kernel(x_ref, g_ref, w_ref, o_ref, xn_ref):
    @pl.when(pl.program_id(1) == 0)
    def _():
        x = x_ref[...]
        ms = jnp.mean(x * x, axis=-1, keepdims=True)
        xn_ref[...] = ((x * lax.rsqrt(ms + EPS)) * g_ref[...]).astype(BF16)

    o_ref[...] = jnp.dot(xn_ref[...], w_ref[...], preferred_element_type=F32)


def _norm_matmul(x, g, w, tm, tn):
    n, d = x.shape
    nout = w.shape[1]
    return pl.pallas_call(
        _norm_matmul_kernel,
        grid=(n // tm, nout // tn),
        in_specs=[pl.BlockSpec((tm, d), lambda i, j: (i, 0)),
                  pl.BlockSpec((1, d), lambda i, j: (0, 0)),
                  pl.BlockSpec((d, tn), lambda i, j: (0, j))],
        out_specs=pl.BlockSpec((tm, tn), lambda i, j: (i, j)),
        out_shape=jax.ShapeDtypeStruct((n, nout), F32),
        scratch_shapes=[pltpu.VMEM((tm, d), BF16)],
        compiler_params=_cparams(("parallel", "arbitrary")),
        name="norm_in_proj",
    )(x, g, w)


def _mlstm_kernel(mq_ref, mk_ref, mv_ref, mo_ref, sm_ref, bias_ref, wconv_ref, bconv_ref,
                  gm_ref, c0_ref, n0_ref, m0_ref, conv0_ref,
                  y_ref, c_ref, n_ref, m_ref, conv_ref, cbuf_ref, *, L):
    c = pl.program_id(1)
    hd = D // M_HEADS

    @pl.when(c == 0)
    def _():
        c_ref[...] = c0_ref[...]
        n_ref[...] = n0_ref[...]
        m_ref[...] = m0_ref[...]
        cbuf_ref[8 - (CONV_W - 1):8, :] = conv0_ref[0]

    cbuf_ref[8:8 + L, 0:D] = mq_ref[0]
    cbuf_ref[8:8 + L, D:2 * D] = mk_ref[0]
    wc = wconv_ref[...]
    qk = bconv_ref[...] + cbuf_ref[5:5 + L, :] * wc[0:1, :]
    for j in range(1, CONV_W):
        qk = qk + cbuf_ref[5 + j:5 + j + L, :] * wc[j:j + 1, :]
    tail = cbuf_ref[5 + L:8 + L, :]
    cbuf_ref[5:8, :] = tail
    conv_ref[0] = tail
    qk = qk * jax.nn.sigmoid(qk)

    g_all = sm_ref[0] + bias_ref[...]
    lf_all = jnp.minimum(g_all, 0.0) - jnp.log1p(jnp.exp(-jnp.abs(g_all)))
    lane = lax.broadcasted_iota(jnp.int32, (L, LANES), 1)
    gates = jnp.where(lane >= SM_MF, lf_all, g_all)
    r_i = lax.broadcasted_iota(jnp.int32, (L, L), 0)
    c_i = lax.broadcasted_iota(jnp.int32, (L, L), 1)
    tril = (c_i <= r_i).astype(F32)
    triu = (r_i <= c_i).astype(F32)
    b_col_all = jnp.dot(tril, lf_all, precision=HIGHEST, preferred_element_type=F32)
    sel = (lax.broadcasted_iota(jnp.int32, (8, LANES), 1)
           == lax.broadcasted_iota(jnp.int32, (8, LANES), 0) + SM_MI).astype(F32)
    rows = lax.dot_general(sel, gates, (((1,), (1,)), ((), ())), precision=HIGHEST,
                           preferred_element_type=F32)
    b_row_all = jnp.dot(rows, triu, precision=HIGHEST, preferred_element_type=F32)
    causal = c_i <= r_i

    mo = mo_ref[0]
    mv = mv_ref[0]
    for h in range(M_HEADS):
        hs = slice(h * hd, (h + 1) * hd)
        qh = qk[:, hs]
        kh = qk[:, D + h * hd:D + (h + 1) * hd] * (hd ** -0.5)
        vh = mv[:, hs]
        qb, kb, vb = qh.astype(BF16), kh.astype(BF16), vh.astype(BF16)
        b_col = b_col_all[:, SM_MF + h:SM_MF + h + 1]
        i_col = gates[:, SM_MI + h:SM_MI + h + 1]
        b_row = b_row_all[M_HEADS + h:M_HEADS + h + 1, :]
        i_row = rows[h:h + 1, :]
        m_prev = m_ref[0, :, h:h + 1]
        c_prev = c_ref[0, h]
        n_prev = n_ref[0, h:h + 1, :]

        dmat = jnp.where(causal, b_col - b_row + i_row, -jnp.inf)
        inter = b_col + m_prev
        m_t = jnp.maximum(inter, jnp.max(dmat, axis=-1, keepdims=True))
        w_intra = jnp.exp(dmat - m_t)
        w_inter = jnp.exp(inter - m_t)
        s = lax.dot_general(qb, kb, (((1,), (1,)), ((), ())), preferred_element_type=F32) * w_intra
        qc = lax.dot_general(qb, c_prev.astype(BF16), (((1,), (1,)), ((), ())),
                             preferred_element_type=F32)
        num = jnp.dot(s.astype(BF16), vb, preferred_element_type=F32) + w_inter * qc
        den = jnp.sum(s, axis=-1, keepdims=True) + w_inter * jnp.sum(qh * n_prev, axis=-1, keepdims=True)
        denom = jnp.maximum(jnp.abs(den), jnp.exp(-m_t))
        hh = num / denom

        m_new = m_t[L - 1:L, :]
        b_last = b_col[L - 1:L, :]
        g_col = jnp.exp(b_last - b_col + i_col - m_new)
        decay = jnp.exp(b_last + m_prev - m_new)
        gv = (g_col * vh).astype(BF16)
        c_ref[0, h] = decay * c_prev + lax.dot_general(
            gv, kb, (((0,), (0,)), ((), ())), preferred_element_type=F32)
        n_ref[0, h:h + 1, :] = decay * n_prev + jnp.sum(g_col * kh, axis=0, keepdims=True)
        m_ref[0, :, h:h + 1] = m_new

        hn = hh * lax.rsqrt(jnp.mean(hh * hh, axis=-1, keepdims=True) + EPS) * gm_ref[:, hs]
        y_ref[0, :, hs] = (hn * jax.nn.sigmoid(mo[:, hs])).astype(BF16)


def _mlstm(z3, bias_row, w_conv, b_conv, g_mnorm, c0, n0, m0, conv0, L):
    b, t, _ = z3.shape
    nc = t // L
    hd = D // M_HEADS

    def zspec(col, width):
        return pl.BlockSpec((1, L, width), lambda i, c: (i, c, col // width))

    def per_batch(shape):
        nd = len(shape)
        return pl.BlockSpec((1,) + shape, lambda i, c: (i,) + (0,) * nd)

    def const(shape):
        nd = len(shape)
        return pl.BlockSpec(shape, lambda i, c: (0,) * nd)

    return pl.pallas_call(
        functools.partial(_mlstm_kernel, L=L),
        grid=(b, nc),
        in_specs=[zspec(COL_MQ, D), zspec(COL_MK, D), zspec(COL_MV, D), zspec(COL_MO, D),
                  zspec(COL_SMALL, LANES), const((1, LANES)), const((CONV_W, 2 * D)),
                  const((1, 2 * D)), const((1, D)),
                  per_batch((M_HEADS, hd, hd)), per_batch((M_HEADS, hd)),
                  per_batch((1, M_HEADS)), per_batch((CONV_W - 1, 2 * D))],
        out_specs=[pl.BlockSpec((1, L, D), lambda i, c: (i, c, 0)),
                   per_batch((M_HEADS, hd, hd)), per_batch((M_HEADS, hd)),
                   per_batch((1, M_HEADS)), per_batch((CONV_W - 1, 2 * D))],
        out_shape=[jax.ShapeDtypeStruct((b, t, D), BF16),
                   jax.ShapeDtypeStruct((b, M_HEADS, hd, hd), F32),
                   jax.ShapeDtypeStruct((b, M_HEADS, hd), F32),
                   jax.ShapeDtypeStruct((b, 1, M_HEADS), F32),
                   jax.ShapeDtypeStruct((b, CONV_W - 1, 2 * D), F32)],
        scratch_shapes=[pltpu.VMEM((8 + L, 2 * D), F32)],
        compiler_params=_cparams(("parallel", "arbitrary")),
        name="mlstm",
    )(z3, z3, z3, z3, z3, bias_row, w_conv, b_conv, g_mnorm, c0, n0, m0, conv0)


def _eye_bf16(n):
    return (lax.broadcasted_iota(jnp.int32, (n, n), 0)
            == lax.broadcasted_iota(jnp.int32, (n, n), 1)).astype(BF16)


def _transpose_bf16(eye, a):
    return lax.dot_general(eye, a, (((1,), (1,)), ((), ())), preferred_element_type=F32)


def _frames_to_lanes(eye, a):
    if a.shape[0] % LANES == 0:
        return a.T.astype(BF16)
    return _transpose_bf16(eye, a.astype(BF16)).astype(BF16)


def _lanes_to_frames(eye, a):
    if a.shape[1] % LANES == 0:
        return a.T.astype(BF16)
    return _transpose_bf16(eye, a.astype(BF16)).astype(BF16)


def _qkv_kernel(x_ref, g_ref, w_ref, gq_ref, gk_ref, k32_ref, v32_ref, kb_ref, qt_ref, vt_ref):
    hd = D // A_HEADS
    x = x_ref[0]
    ms = jnp.mean(x * x, axis=-1, keepdims=True)
    xn = ((x * lax.rsqrt(ms + EPS)) * g_ref[...]).astype(BF16)
    eye = _eye_bf16(hd)

    def project(col):
        return jnp.dot(xn, w_ref[:, col:col + 2 * hd], preferred_element_type=F32)

    def head_norm(a, gain):
        return (a * lax.rsqrt(jnp.mean(a * a, axis=-1, keepdims=True) + EPS)) * gain

    for pair in range(0, D, 2 * hd):
        q2, k2, v2 = project(pair), project(D + pair), project(2 * D + pair)
        for half in range(2):
            cs = slice(half * hd, (half + 1) * hd)
            hs = slice(pair + half * hd, pair + (half + 1) * hd)
            qn = head_norm(q2[:, cs], gq_ref[...]) * Q_LOGIT_SCALE
            qt_ref[0, hs, :] = _frames_to_lanes(eye, qn)
            kn = head_norm(k2[:, cs], gk_ref[...])
            k32_ref[0, :, hs] = kn
            kb_ref[0, :, hs] = kn.astype(BF16)
            v32_ref[0, :, hs] = v2[:, cs]
            h = (pair + half * hd) // hd
            vt_ref[0, 0, h * VT_HEAD_ROWS:h * VT_HEAD_ROWS + hd, :] = _frames_to_lanes(eye, v2[:, cs])
            vt_ref[0, 0, h * VT_HEAD_ROWS + hd:(h + 1) * VT_HEAD_ROWS, :] = jnp.ones(
                (VT_HEAD_ROWS - hd, x.shape[0]), BF16)


def _qkv(x, g_norm, w_att, g_q, g_k, tm):
    b, t, _ = x.shape
    hd = D // A_HEADS
    row = pl.BlockSpec((1, tm, D), lambda i, r: (i, r, 0))
    gspec = pl.BlockSpec((1, hd), lambda i, r: (0, 0))
    return pl.pallas_call(
        _qkv_kernel,
        grid=(b, t // tm),
        in_specs=[row, pl.BlockSpec((1, D), lambda i, r: (0, 0)),
                  pl.BlockSpec((D, 3 * D), lambda i, r: (0, 0)), gspec, gspec],
        out_specs=[row, row, row, pl.BlockSpec((1, D, tm), lambda i, r: (i, 0, r)),
                   pl.BlockSpec((1, 1, A_HEADS * VT_HEAD_ROWS, tm), lambda i, r: (i, r, 0, 0))],
        out_shape=[jax.ShapeDtypeStruct((b, t, D), F32), jax.ShapeDtypeStruct((b, t, D), F32),
                   jax.ShapeDtypeStruct((b, t, D), BF16), jax.ShapeDtypeStruct((b, D, t), BF16),
                   jax.ShapeDtypeStruct((b, t // tm, A_HEADS * VT_HEAD_ROWS, tm), BF16)],
        compiler_params=_cparams(("parallel", "parallel")),
        name="qkv_proj_norm",
    )(x, g_norm, w_att, g_q, g_k)


def _num_key_tiles(qi, tq, tile, q_pos0, s_real):
    last_chunk = (q_pos0 + (qi + 1) * tq - 1) // CHUNK
    kend = jnp.minimum((last_chunk + 1) * CHUNK, s_real)
    return (kend + tile - 1) // tile


def _num_key_tiles_static(qi, tq, tile, q_pos0, s_real):
    last_chunk = (q_pos0 + (qi + 1) * tq - 1) // CHUNK
    return -(-min((last_chunk + 1) * CHUNK, s_real) // tile)


def _select_kernel(iq_ref, sm_ref, knew_ref, *refs, tq, kb, s_pad, s_real, q_pos0, topk):
    kpast_ref = refs[0] if q_pos0 > 0 else None
    mask_ref, score_ref, iqt_ref, ki_ref = refs[-4:]
    qi = pl.program_id(1)
    nkt = _num_key_tiles(qi, tq, kb, q_pos0, s_real)
    kf = float(topk)
    groups = kb // SUBLANES

    @pl.when(qi == 0)
    def _():
        def put(row0, x):
            hi = x.astype(BF16)
            lo = (x - hi.astype(F32)).astype(BF16)
            ki_ref[row0:row0 + x.shape[0], :] = jnp.concatenate([hi, lo, hi], axis=1)

        if kpast_ref is not None:
            put(0, kpast_ref[0])
        t_new = knew_ref.shape[1]
        for r in range(0, t_new, KEY_PREP_ROWS):
            n = min(KEY_PREP_ROWS, t_new - r)
            put(q_pos0 + r, knew_ref[0, r:r + n, 0:IDX_DIM])
        if s_pad > s_real:
            ki_ref[s_real:s_pad, :] = jnp.zeros((s_pad - s_real, 3 * IDX_DIM), BF16)

    pick = (lax.broadcasted_iota(jnp.int32, (IDX_HEADS, LANES), 1)
            == lax.broadcasted_iota(jnp.int32, (IDX_HEADS, LANES), 0) + SM_IW).astype(F32)
    w = lax.dot_general(pick, sm_ref[0], (((1,), (1,)), ((), ())), precision=HIGHEST,
                        preferred_element_type=F32) * ((IDX_HEADS * IDX_DIM) ** -0.5)
    eye = _eye_bf16(IDX_DIM)
    for h in range(IDX_HEADS):
        a = iq_ref[0, :, h * IDX_DIM:(h + 1) * IDX_DIM]
        hi = a.astype(BF16)
        lo = (a - hi.astype(F32)).astype(BF16)
        hi_t = _transpose_bf16(eye, hi).astype(BF16)
        lo_t = _transpose_bf16(eye, lo).astype(BF16)
        cols = slice((h % 2) * tq, (h % 2 + 1) * tq)
        iqt_ref[h // 2, 0:IDX_DIM, cols] = hi_t
        iqt_ref[h // 2, IDX_DIM:2 * IDX_DIM, cols] = hi_t
        iqt_ref[h // 2, 2 * IDX_DIM:3 * IDX_DIM, cols] = lo_t

    q_pos = q_pos0 + qi * tq + lax.broadcasted_iota(jnp.int32, (1, tq), 1)
    q_chunk = q_pos >> CHUNK_SHIFT
    first_key = jnp.minimum(((q_pos0 + qi * tq) >> CHUNK_SHIFT) << CHUNK_SHIFT, s_real)
    n_full = first_key // kb

    def score_tile(j, carry, masked):
        rmin, rmax = carry
        k0 = pl.multiple_of(j * kb, kb)
        kt = ki_ref[pl.ds(k0, kb), :]
        acc = None
        for p in range(IDX_HEADS // 2):
            rel = jnp.dot(kt, iqt_ref[p], preferred_element_type=F32)
            part = (w[2 * p:2 * p + 1] * jnp.maximum(rel[:, 0:tq], 0.0)
                    + w[2 * p + 1:2 * p + 2] * jnp.maximum(rel[:, tq:2 * tq], 0.0))
            acc = part if acc is None else acc + part
        if masked:
            key = k0 + lax.broadcasted_iota(jnp.int32, (kb, tq), 0)
            adm = jnp.logical_and((key >> CHUNK_SHIFT) <= q_chunk, key < s_real)
            low = jnp.where(adm, acc, jnp.inf)
            acc = jnp.where(adm, acc, -jnp.inf)
        else:
            low = acc
        score_ref[pl.ds(k0, kb), :] = acc
        rmin = jnp.minimum(rmin, jnp.min(low.reshape(groups, SUBLANES, tq), axis=0))
        rmax = jnp.maximum(rmax, jnp.max(acc.reshape(groups, SUBLANES, tq), axis=0))
        return rmin, rmax

    stats = (jnp.full((SUBLANES, tq), jnp.inf, F32), jnp.full((SUBLANES, tq), -jnp.inf, F32))
    stats = lax.fori_loop(0, n_full, functools.partial(score_tile, masked=False), stats)
    rmin8, rmax8 = lax.fori_loop(n_full, nkt, functools.partial(score_tile, masked=True), stats)
    rmin = jnp.min(rmin8, axis=0, keepdims=True)
    rmax = jnp.max(rmax8, axis=0, keepdims=True)

    def count(pred):
        def body(j, acc):
            for c in range(kb // COUNT_ROWS):
                k0 = pl.multiple_of(j * kb + c * COUNT_ROWS, COUNT_ROWS)
                hit = jnp.where(pred(score_ref[pl.ds(k0, COUNT_ROWS), :], k0), 1.0, 0.0)
                acc = acc + jnp.sum(hit.reshape(COUNT_ROWS // (COUNT_PARTS * SUBLANES), COUNT_PARTS,
                                                SUBLANES, tq), axis=0)
            return acc
        acc = lax.fori_loop(0, nkt, body, jnp.zeros((COUNT_PARTS, SUBLANES, tq), F32))
        return jnp.sum(jnp.sum(acc, axis=0), axis=0, keepdims=True)

    def below(cur):
        shape = (COUNT_ROWS // (COUNT_PARTS * SUBLANES), COUNT_PARTS, SUBLANES, tq)

        def body(j, carry):
            top, num = carry
            for c in range(kb // COUNT_ROWS):
                k0 = pl.multiple_of(j * kb + c * COUNT_ROWS, COUNT_ROWS)
                s = score_ref[pl.ds(k0, COUNT_ROWS), :]
                under = s < cur
                top = jnp.maximum(top, jnp.max(jnp.where(under, s, -jnp.inf).reshape(shape), axis=0))
                num = num + jnp.sum(jnp.where(under, 0.0, 1.0).reshape(shape), axis=0)
            return top, num

        top, num = lax.fori_loop(0, nkt, body, (jnp.full(shape[1:], -jnp.inf, F32),
                                                jnp.zeros(shape[1:], F32)))
        return (jnp.max(jnp.max(top, axis=0), axis=0, keepdims=True),
                jnp.sum(jnp.sum(num, axis=0), axis=0, keepdims=True))

    n_adm = jnp.minimum((q_chunk + 1) << CHUNK_SHIFT, s_real).astype(F32)

    def open_rows(cnt_lo):
        return cnt_lo > kf

    def wide_rows(st):
        return jnp.logical_and(open_rows(st[2]), st[2] - st[3] > FINISH_MAX)

    def flag(pred):
        return jnp.max(jnp.where(pred, 1.0, 0.0))

    def bisect(st):
        lo, hi, cnt_lo, cnt_hi = st
        active = open_rows(cnt_lo)
        mid = 0.5 * jnp.maximum(lo, rmin) + 0.5 * jnp.minimum(hi, rmax)
        cm = count(lambda s, k0: s >= mid)
        up = jnp.logical_and(active, cm >= kf)
        dn = jnp.logical_and(active, cm < kf)
        return (jnp.where(up, mid, lo), jnp.where(dn, mid, hi),
                jnp.where(up, cm, cnt_lo), jnp.where(dn, cm, cnt_hi))

    st = (jnp.full((1, tq), -F32_MAX, F32), jnp.full((1, tq), F32_MAX, F32), n_adm,
          jnp.zeros((1, tq), F32))
    st = lax.fori_loop(0, SELECT_MIN_ITERS, lambda _, s: bisect(s), st)

    def narrow(c):
        nxt = bisect(c[2])
        return c[0] + 1, flag(wide_rows(nxt)), nxt

    _, _, st = lax.while_loop(
        lambda c: jnp.logical_and(c[0] < SELECT_MAX_ITERS, c[1] > 0.0), narrow,
        (jnp.int32(SELECT_MIN_ITERS), flag(wide_rows(st)), st))

    lo, hi, cnt_lo, cnt_hi = st
    cand, _ = below(hi)

    def walk(c):
        _, walking, lo, hi, cnt_lo, cnt_hi, cand = c
        nxt, cge = below(cand)
        settle = jnp.logical_and(walking > 0.0, cge >= kf)
        move = jnp.logical_and(walking > 0.0, cge < kf)
        walking = jnp.where(move, 1.0, 0.0)
        return (jnp.max(walking), walking, jnp.where(settle, cand, lo), jnp.where(move, cand, hi),
                jnp.where(settle, cge, cnt_lo), jnp.where(move, cge, cnt_hi), jnp.where(move, nxt, cand))

    walking = jnp.where(open_rows(cnt_lo), 1.0, 0.0)
    _, _, lo, hi, cnt_lo, cnt_hi, _ = lax.while_loop(
        lambda c: c[0] > 0.0, walk, (jnp.max(walking), walking, lo, hi, cnt_lo, cnt_hi, cand))
    n_open = flag(open_rows(cnt_lo))

    def clear_tile(j, carry):
        k0 = pl.multiple_of(j * kb, kb)
        mask_ref[0, 0, pl.ds(k0, kb), :] = jnp.zeros((kb, tq), jnp.int8)
        return carry

    lax.fori_loop(nkt, s_pad // kb, clear_tile, 0)

    @pl.when(n_open == 0.0)
    def _():
        def write_tile(j, carry):
            k0 = pl.multiple_of(j * kb, kb)
            keep = score_ref[pl.ds(k0, kb), :] >= lo
            mask_ref[0, 0, pl.ds(k0, kb), :] = jnp.where(keep, 1, 0).astype(jnp.int8)
            return carry

        lax.fori_loop(0, nkt, write_tile, 0)

    @pl.when(n_open > 0.0)
    def _():
        tied = open_rows(cnt_lo)
        need = jnp.where(tied, kf - cnt_hi, 0.0)
        key_iota = lax.broadcasted_iota(jnp.int32, (kb, tq), 0)
        strip_iota = lax.broadcasted_iota(jnp.int32, (COUNT_ROWS, tq), 0)
        shape = (COUNT_ROWS // (COUNT_PARTS * SUBLANES), COUNT_PARTS, SUBLANES, tq)

        def next_tied(prev):
            def body(j, acc):
                for c in range(kb // COUNT_ROWS):
                    k0 = pl.multiple_of(j * kb + c * COUNT_ROWS, COUNT_ROWS)
                    idx = k0 + strip_iota
                    hit = jnp.logical_and(score_ref[pl.ds(k0, COUNT_ROWS), :] == lo, idx > prev)
                    acc = jnp.minimum(acc, jnp.min(jnp.where(hit, idx, s_pad).reshape(shape), axis=0))
                return acc
            acc = lax.fori_loop(0, nkt, body, jnp.full(shape[1:], s_pad, jnp.int32))
            return jnp.min(jnp.min(acc, axis=0), axis=0, keepdims=True)

        def cut_by_walk(_):
            def step(c):
                _, left, last = c
                take = left > 0.0
                last = jnp.where(take, next_tied(last), last)
                left = jnp.where(take, left - 1.0, left)
                return jnp.max(left), left, last
            _, _, last = lax.while_loop(lambda c: c[0] > 0.0, step,
                                        (jnp.max(need), need, jnp.full((1, tq), -1, jnp.int32)))
            return last + 1

        def cut_by_bisection(_):
            def step(_, st):
                jlo, jhi = st
                jmid = (jlo + jhi) >> 1
                cm = count(lambda s, k0: jnp.logical_and(s == lo, k0 + strip_iota < jmid))
                ok = cm >= need
                return jnp.where(ok, jlo, jmid), jnp.where(ok, jmid, jhi)
            steps = int(math.ceil(math.log2(s_pad))) + 1
            return lax.fori_loop(0, steps, step, (jnp.zeros((1, tq), jnp.int32),
                                                  jnp.full((1, tq), s_pad, jnp.int32)))[1]

        jcut = lax.cond(jnp.max(need) <= TIE_WALK_MAX, cut_by_walk, cut_by_bisection, 0)
        jcut = jnp.where(tied, jcut, s_pad)

        def write_tile(j, carry):
            k0 = pl.multiple_of(j * kb, kb)
            s = score_ref[pl.ds(k0, kb), :]
            keep = jnp.logical_or(s >= hi, jnp.logical_and(s >= lo, k0 + key_iota < jcut))
            mask_ref[0, 0, pl.ds(k0, kb), :] = jnp.where(keep, 1, 0).astype(jnp.int8)
            return carry

        lax.fori_loop(0, nkt, write_tile, 0)


def _select(z3, ki_past, tq, kb, s_pad, topk):
    b, t, _ = z3.shape
    past = ki_past.shape[1]
    iq_w = IDX_HEADS * IDX_DIM
    kern = functools.partial(_select_kernel, tq=tq, kb=kb, s_pad=s_pad, s_real=past + t,
                             q_pos0=past, topk=topk)
    cached = ([pl.BlockSpec((1, past, IDX_DIM), lambda i, j: (i, 0, 0))], [ki_past]) if past else ([], [])
    return pl.pallas_call(
        kern,
        grid=(b, t // tq),
        in_specs=[pl.BlockSpec((1, tq, iq_w), lambda i, j: (i, j, COL_IQ // iq_w)),
                  pl.BlockSpec((1, tq, LANES), lambda i, j: (i, j, COL_SMALL // LANES)),
                  pl.BlockSpec((1, t, LANES), lambda i, j: (i, 0, COL_SMALL // LANES))] + cached[0],
        out_specs=pl.BlockSpec((1, 1, s_pad, tq), lambda i, j: (i, j, 0, 0)),
        out_shape=jax.ShapeDtypeStruct((b, t // tq, s_pad, tq), jnp.int8),
        scratch_shapes=[pltpu.VMEM((s_pad, tq), F32),
                        pltpu.VMEM((IDX_HEADS // 2, 3 * IDX_DIM, 2 * tq), BF16),
                        pltpu.VMEM((s_pad, 3 * IDX_DIM), BF16)],
        compiler_params=_cparams(("parallel", "arbitrary")),
        name="index_select",
    )(z3, z3, z3, *cached[1])


def _attn_kernel(q_of_ref, k_of_ref, qt_ref, k_ref, vt_ref, *refs, n_mask):
    mask_refs = refs[:n_mask]
    o_ref, acc_ref, m_ref, l_ref, s_ref, bias_ref = refs[n_mask:]
    step = pl.program_id(1)
    hd = D // A_HEADS
    tq = qt_ref.shape[2]
    tk = k_ref.shape[1]
    tq_mask = tq // n_mask
    pv_rows = math.gcd(tk, PV_ROWS)
    first = k_of_ref[step] == 0
    nxt = jnp.minimum(step + 1, pl.num_programs(1) - 1)
    last = jnp.logical_or(step == pl.num_programs(1) - 1, q_of_ref[nxt] != q_of_ref[step])

    @pl.when(first)
    def _():
        acc_ref[...] = jnp.zeros(acc_ref.shape, F32)
        m_ref[...] = jnp.full(m_ref.shape, MASK_NEG, F32)
        l_ref[...] = jnp.zeros(l_ref.shape, F32)

    for n, mask_ref in enumerate(mask_refs):
        bias_ref[:, n * tq_mask:(n + 1) * tq_mask] = jnp.where(
            mask_ref[0, 0].astype(jnp.int32) != 0, 0.0, MASK_NEG)
    def logits(h):
        hs = slice(h * hd, (h + 1) * hd)
        top = None
        for r in range(0, tk, QK_ROWS):
            s = jnp.dot(k_ref[0, r:r + QK_ROWS, hs], qt_ref[0, hs, :],
                        preferred_element_type=F32) + bias_ref[r:r + QK_ROWS, :]
            s_ref[h, r:r + QK_ROWS, :] = s
            part = jnp.max(s.reshape(QK_ROWS // SUBLANES, SUBLANES, tq), axis=0)
            top = part if top is None else jnp.maximum(top, part)
        return jnp.max(top, axis=0, keepdims=True)

    def weigh(h, tile_max):
        hs = slice(h * hd, (h + 1) * hd)
        m_prev = m_ref[h:h + 1, :]
        m_new = jnp.maximum(m_prev, tile_max)
        alpha = jnp.exp2(m_prev - m_new)
        pv = None
        for r in range(0, tk, pv_rows):
            p = jnp.exp2(s_ref[h, r:r + pv_rows, :] - m_new)
            prod = jnp.dot(vt_ref[0, 0, h * VT_HEAD_ROWS:(h + 1) * VT_HEAD_ROWS, r:r + pv_rows],
                           p.astype(BF16), preferred_element_type=F32)
            pv = prod if pv is None else pv + prod
        l_ref[h:h + 1, :] = alpha * l_ref[h:h + 1, :] + pv[hd:hd + 1, :]
        acc_ref[hs, :] = alpha * acc_ref[hs, :] + pv[0:hd, :]
        m_ref[h:h + 1, :] = m_new

    tile_max = [logits(h) for h in range(A_HEADS)]

    @pl.when(k_of_ref[step] >= 0)
    def _():
        for h in range(A_HEADS):
            weigh(h, tile_max[h])

    @pl.when(last)
    def _():
        eye = _eye_bf16(tq)
        for h in range(A_HEADS):
            hs = slice(h * hd, (h + 1) * hd)
            o_ref[0, :, hs] = _lanes_to_frames(eye, acc_ref[hs, :] / l_ref[h:h + 1, :])


def _attention(qt, kb, vt, mask_t, tq, tk, s_real, q_pos0):
    b, _, t = qt.shape
    n_mask = tq // mask_t.shape[3]
    steps = [(q, j) for q in range(t // tq) for j in range(_num_key_tiles_static(q, tq, tk, q_pos0, s_real))]
    q_of = jnp.array([q for q, _ in steps], jnp.int32)
    k_of = jnp.array([j for _, j in steps], jnp.int32)

    def mask_spec(n):
        return pl.BlockSpec((1, 1, tk, tq // n_mask),
                            lambda i, s, q_of, k_of: (i, q_of[s] * n_mask + n, k_of[s], 0))

    return pl.pallas_call(
        functools.partial(_attn_kernel, n_mask=n_mask),
        grid_spec=pltpu.PrefetchScalarGridSpec(
            num_scalar_prefetch=2,
            grid=(b, len(steps)),
            in_specs=[pl.BlockSpec((1, D, tq), lambda i, s, q_of, k_of: (i, 0, q_of[s])),
                      pl.BlockSpec((1, tk, D), lambda i, s, q_of, k_of: (i, k_of[s], 0)),
                      pl.BlockSpec((1, 1, A_HEADS * VT_HEAD_ROWS, tk), lambda i, s, q_of, k_of: (i, k_of[s], 0, 0))]
            + [mask_spec(n) for n in range(n_mask)],
            out_specs=pl.BlockSpec((1, tq, D), lambda i, s, q_of, k_of: (i, q_of[s], 0)),
            scratch_shapes=[pltpu.VMEM((D, tq), F32),
                            pltpu.VMEM((A_HEADS, tq), F32),
                            pltpu.VMEM((A_HEADS, tq), F32),
                            pltpu.VMEM((A_HEADS, tk, tq), F32),
                            pltpu.VMEM((tk, tq), F32)]),
        out_shape=jax.ShapeDtypeStruct((b, t, D), BF16),
        compiler_params=_cparams(("parallel", "arbitrary")),
        name="masked_attention",
    )(q_of, k_of, qt, kb, vt, *([mask_t] * n_mask))


def _attn_past_kernel(qt_ref, kn_ref, vtn_ref, kp_ref, vp_ref, mask_ref, o_ref):
    hd = D // A_HEADS
    t = qt_ref.shape[2]
    past = kp_ref.shape[1] // A_HEADS

    def cached(ref, h):
        return ref[0, pl.ds(h, past, stride=A_HEADS), :].astype(BF16)

    bias = jnp.where(mask_ref[0, 0].astype(jnp.int32) != 0, 0.0, MASK_NEG)
    bias_p, bias_n = bias[0:past], bias[past:past + t]
    eye_hd, eye_t = _eye_bf16(hd), _eye_bf16(t)
    for h in range(A_HEADS):
        hs = slice(h * hd, (h + 1) * hd)
        q = qt_ref[0, hs, :]
        s_p = jnp.dot(cached(kp_ref, h), q, preferred_element_type=F32) + bias_p
        s_n = jnp.dot(kn_ref[0, :, hs], q, preferred_element_type=F32) + bias_n
        m = jnp.maximum(jnp.max(s_p, axis=0, keepdims=True), jnp.max(s_n, axis=0, keepdims=True))
        p_p = jnp.exp2(s_p - m)
        p_n = jnp.exp2(s_n - m)
        l = jnp.sum(p_p, axis=0, keepdims=True) + jnp.sum(p_n, axis=0, keepdims=True)
        vt_p = _transpose_bf16(eye_hd, cached(vp_ref, h)).astype(BF16)
        out_t = (jnp.dot(vt_p, p_p.astype(BF16), preferred_element_type=F32)
                 + jnp.dot(vtn_ref[0, 0, h * VT_HEAD_ROWS:h * VT_HEAD_ROWS + hd, :], p_n.astype(BF16),
                           preferred_element_type=F32))
        o_ref[0, :, hs] = _transpose_bf16(eye_t, (out_t / l).astype(BF16)).astype(BF16)


def _attention_with_past(qt, kb, vt, k_past, v_past, mask_t):
    b, _, t = qt.shape
    past, heads, hd = k_past.shape[1:]
    s_pad = mask_t.shape[2]
    k_past = k_past.reshape(b, past * heads, hd)
    v_past = v_past.reshape(b, past * heads, hd)
    cache = pl.BlockSpec((1, past * heads, hd), lambda i: (i, 0, 0))
    return pl.pallas_call(
        _attn_past_kernel,
        grid=(b,),
        in_specs=[pl.BlockSpec((1, D, t), lambda i: (i, 0, 0)),
                  pl.BlockSpec((1, t, D), lambda i: (i, 0, 0)),
                  pl.BlockSpec((1, 1, A_HEADS * VT_HEAD_ROWS, t), lambda i: (i, 0, 0, 0)),
                  cache, cache,
                  pl.BlockSpec((1, 1, s_pad, t), lambda i: (i, 0, 0, 0))],
        out_specs=pl.BlockSpec((1, t, D), lambda i: (i, 0, 0)),
        out_shape=jax.ShapeDtypeStruct((b, t, D), BF16),
        compiler_params=_cparams(("parallel",)),
        name="masked_attention_past",
    )(qt, kb, vt, k_past, v_past, mask_t)


def _merge_kernel(x_ref, ya_ref, yb_ref, ga_ref, gb_ref, wa_ref, wb_ref, wo_ref, o_ref):
    a = jnp.dot(ya_ref[...], wa_ref[...], preferred_element_type=F32)
    bb = jnp.dot(yb_ref[...], wb_ref[...], preferred_element_type=F32)
    mix = jax.nn.sigmoid(ga_ref[...]) * a + jax.nn.sigmoid(gb_ref[...]) * bb
    o_ref[...] = x_ref[...] + jnp.dot(mix.astype(BF16), wo_ref[...], preferred_element_type=F32)


def _merge(x, ya, yb, z, wa, wb, wo, tm):
    n = x.shape[0]
    row = pl.BlockSpec((tm, D), lambda i: (i, 0))
    wspec = pl.BlockSpec((D, D), lambda i: (0, 0))
    return pl.pallas_call(
        _merge_kernel,
        grid=(n // tm,),
        in_specs=[row, row, row,
                  pl.BlockSpec((tm, D), lambda i: (i, COL_GA // D)),
                  pl.BlockSpec((tm, D), lambda i: (i, COL_GB // D)),
                  wspec, wspec, wspec],
        out_specs=row,
        out_shape=jax.ShapeDtypeStruct((n, D), F32),
        compiler_params=_cparams(("parallel",)),
        name="merge_out_proj",
    )(x, ya, yb, z, z, wa, wb, wo)


def _ffn_kernel(x_ref, g_ref, wg_ref, wu_ref, wd_ref, o_ref, xn_ref, acc_ref):
    c = pl.program_id(1)

    @pl.when(c == 0)
    def _():
        x = x_ref[...]
        ms = jnp.mean(x * x, axis=-1, keepdims=True)
        xn_ref[...] = ((x * lax.rsqrt(ms + EPS)) * g_ref[...]).astype(BF16)
        acc_ref[...] = x

    xn = xn_ref[...]
    gate = jnp.dot(xn, wg_ref[...], preferred_element_type=F32)
    up = jnp.dot(xn, wu_ref[...], preferred_element_type=F32)
    act = (gate * jax.nn.sigmoid(gate)) * up
    acc_ref[...] += jnp.dot(act.astype(BF16), wd_ref[...], preferred_element_type=F32)

    @pl.when(c == pl.num_programs(1) - 1)
    def _():
        o_ref[...] = acc_ref[...]


def _ffn(x, g, w_in, w_out, tm, tf):
    n = x.shape[0]
    dff = w_out.shape[0]
    nf = dff // tf
    row = pl.BlockSpec((tm, D), lambda i, c: (i, 0))
    once = {"pipeline_mode": pl.Buffered(1)} if nf == 1 else {}
    return pl.pallas_call(
        _ffn_kernel,
        grid=(n // tm, nf),
        in_specs=[row, pl.BlockSpec((1, D), lambda i, c: (0, 0)),
                  pl.BlockSpec((D, tf), lambda i, c: (0, c), **once),
                  pl.BlockSpec((D, tf), lambda i, c: (0, nf + c), **once),
                  pl.BlockSpec((tf, D), lambda i, c: (c, 0), **once)],
        out_specs=row,
        out_shape=jax.ShapeDtypeStruct((n, D), F32),
        scratch_shapes=[pltpu.VMEM((tm, D), BF16), pltpu.VMEM((tm, D), F32)],
        compiler_params=_cparams(("parallel", "arbitrary")),
        name="swiglu_ffn",
    )(x, g, w_in, w_in, w_out)


def _layer(x, past, p, cfg):
    k_past, v_past, ki_past, c0, n0, m0, conv0 = past
    b, t, _ = x.shape
    n_tok = b * t
    past_len = k_past.shape[1]
    s_real = past_len + t
    topk = min(TOPK_MAX, s_real // 4)
    tq_sel, kb_sel, tq_att, tk_att = cfg["tq_sel"], cfg["kb_sel"], cfg["tq_att"], cfg["tk_att"]
    s_pad = -(-s_real // tk_att) * tk_att

    x2 = x.reshape(n_tok, D)
    z = _norm_matmul(x2, p["g_norm1"], p["w_in"], cfg["tm_proj"], cfg["tn_proj"])
    z3 = z.reshape(b, t, D_Z)

    y_a, c_new, n_new, m_new, conv_new = _mlstm(
        z3, p["gate_bias"], p["w_conv"], p["b_conv"], p["g_mnorm"], c0, n0,
        m0.reshape(b, 1, M_HEADS), conv0, cfg["mlstm_chunk"])

    k32, v32, kb, qt, vt = _qkv(x, p["g_norm1"], p["w_att"], p["g_q"], p["g_k"], cfg["tm_qkv"])
    ik = z3[:, :, COL_SMALL:COL_SMALL + IDX_DIM]
    mask_t = _select(z3, ki_past.astype(F32), tq_sel, kb_sel, s_pad, topk)
    if past_len == 0:
        assert cfg["tm_qkv"] == tk_att and s_pad == s_real
        y_b = _attention(qt, kb, vt, mask_t, tq_att, tk_att, s_real, past_len)
    else:
        assert cfg["tm_qkv"] == t == tq_sel
        y_b = _attention_with_past(qt, kb, vt, k_past, v_past, mask_t)

    x1 = _merge(x2, y_a.reshape(n_tok, D), y_b.reshape(n_tok, D), z,
                p["w_a_out"], p["w_b_out"], p["w_o"], cfg["tm_rows"])
    y = _ffn(x1, p["g_norm2"], p["w_ffn_in"], p["w_ffn_out"], cfg["tm_ffn"], p["w_ffn_out"].shape[0])

    hd = D // A_HEADS
    return y.reshape(b, t, D), (k32.reshape(b, t, A_HEADS, hd), v32.reshape(b, t, A_HEADS, hd), ik,
                                c_new, n_new, m_new.reshape(b, M_HEADS), conv_new)


O_MI = 4 * D
O_AQ = O_MI + 2 * M_HEADS
O_IQ = O_AQ + 3 * D
O_IK = O_IQ + IDX_HEADS * IDX_DIM
O_IW = O_IK + IDX_DIM
O_GA = O_IW + IDX_HEADS
D_IN = O_GA + 2 * D


def _split_w_kernel(w_ref, main_ref, att_ref):
    rows = w_ref.shape[0]

    def columns(first, width):
        start = first // LANES * LANES
        win = width + LANES
        if start + win <= D_IN:
            x = w_ref[:, start:start + win]
        else:
            tail = D_IN - (start + width)
            x = jnp.concatenate([w_ref[:, start:start + width],
                                 w_ref[:, start + width:D_IN],
                                 jnp.zeros((rows, LANES - tail), F32)], axis=1)
        return pltpu.roll(x, win - (first - start), axis=1)[:, 0:width]

    att_ref[...] = columns(O_AQ, 3 * D).astype(BF16)
    main_ref[:, 0:O_MI] = w_ref[:, 0:O_MI].astype(BF16)
    main_ref[:, COL_GA:COL_GA + 2 * D] = columns(O_GA, 2 * D).astype(BF16)
    main_ref[:, COL_IQ:COL_IQ + IDX_HEADS * IDX_DIM] = columns(O_IQ, IDX_HEADS * IDX_DIM).astype(BF16)
    lane = lax.broadcasted_iota(jnp.int32, (rows, LANES), 1)
    ik_iw = columns(O_IK, LANES)
    gates = pltpu.roll(w_ref[:, O_MI:O_MI + LANES], SM_MI, axis=1)
    small = jnp.where(lane < SM_MI, ik_iw, jnp.where(lane < SM_MF + M_HEADS, gates, 0.0))
    main_ref[:, COL_SMALL:COL_SMALL + LANES] = small.astype(BF16)
    main_ref[:, COL_SMALL + LANES:D_Z] = jnp.zeros((rows, D_Z - COL_SMALL - LANES), BF16)


def _split_w(w_in, layer, tr):
    return pl.pallas_call(
        _split_w_kernel,
        grid=(D // tr,),
        in_specs=[pl.BlockSpec((None, tr, D_IN), lambda i: (layer, i, 0))],
        out_specs=[pl.BlockSpec((tr, D_Z), lambda i: (i, 0)), pl.BlockSpec((tr, 3 * D), lambda i: (i, 0))],
        out_shape=[jax.ShapeDtypeStruct((D, D_Z), BF16), jax.ShapeDtypeStruct((D, 3 * D), BF16)],
        compiler_params=_cparams(("parallel",)),
        name="split_in_proj_weight",
    )(w_in)


def _prep_params(g_norm1, w_in, layer, b_if, w_conv, b_conv, g_mnorm, g_q, g_k, w_a_out, w_b_out, w_o,
                 g_norm2, w_ffn_in, w_ffn_out):
    assert w_in.shape[1:] == (D, D_IN)
    w_perm, w_att = _split_w(w_in, layer, WEIGHT_SPLIT_ROWS)
    gate_bias = jnp.zeros((1, LANES), F32).at[0, SM_MI:SM_MI + 2 * M_HEADS].set(b_if.astype(F32))
    return {
        "g_norm1": g_norm1.reshape(1, D), "w_in": w_perm, "w_att": w_att,
        "gate_bias": gate_bias,
        "w_conv": w_conv, "b_conv": b_conv.reshape(1, 2 * D), "g_mnorm": g_mnorm.reshape(1, D),
        "g_q": g_q.reshape(1, -1), "g_k": g_k.reshape(1, -1),
        "w_a_out": w_a_out.astype(BF16), "w_b_out": w_b_out.astype(BF16), "w_o": w_o.astype(BF16),
        "g_norm2": g_norm2.reshape(1, D), "w_ffn_in": w_ffn_in.astype(BF16),
        "w_ffn_out": w_ffn_out.astype(BF16),
    }


def _config(b, t, past_len):
    n_tok = b * t
    s_real = past_len + t
    tm = min(2048, n_tok)
    tq_sel = min(256, t)
    tq_att = min(512, t)
    if s_real % 512 == 0:
        kb_sel = tk_att = 512
    else:
        kb_sel = tk_att = -(-s_real // LANES) * LANES
    return {"tm_proj": tm, "tn_proj": 1024,"mlstm_chunk": min(128, t), "tm_rows": min(512, n_tok),
            "tm_qkv": min(512, t), "tq_sel": tq_sel, "kb_sel": kb_sel, "tq_att": tq_att,
            "tk_att": tk_att, "tm_ffn": min(512, n_tok)}


def kernel(x_prompt, x_sample, cache_k, cache_v, cache_kidx, state_C, state_n, state_m, state_conv,
           g_norm1, w_in, b_if, w_conv, b_conv, g_mnorm, g_q, g_k, w_a_out, w_b_out, w_o,
           g_norm2, w_ffn_in, w_ffn_out):
    depth = w_in.shape[0]
    bp = x_prompt.shape[0]
    hd_a = D // A_HEADS
    hd_m = D // M_HEADS
    yp, ys = x_prompt, x_sample
    new_p, new_s = [], []
    for l in range(depth):
        p = _prep_params(g_norm1[l], w_in, l, b_if[l], w_conv[l], b_conv[l], g_mnorm[l], g_q[l],
                         g_k[l], w_a_out[l], w_b_out[l], w_o[l], g_norm2[l], w_ffn_in[l],
                         w_ffn_out[l])
        empty = (jnp.zeros((bp, 0, A_HEADS, hd_a), F32), jnp.zeros((bp, 0, A_HEADS, hd_a), F32),
                 jnp.zeros((bp, 0, IDX_DIM), F32), jnp.zeros((bp, M_HEADS, hd_m, hd_m), F32),
                 jnp.zeros((bp, M_HEADS, hd_m), F32), jnp.zeros((bp, M_HEADS), F32),
                 jnp.zeros((bp, CONV_W - 1, 2 * D), F32))
        yp, sp = _layer(yp, empty, p, _config(bp, yp.shape[1], 0))
        ys, ss = _layer(ys, (cache_k[l], cache_v[l], cache_kidx[l], state_C[l], state_n[l],
                             state_m[l], state_conv[l]), p,
                        _config(ys.shape[0], ys.shape[1], cache_k.shape[2]))
        new_p.append(sp)
        new_s.append(ss)

    def stk(lst, i):
        return jnp.stack([s[i] for s in lst])

    return (yp, ys) + tuple(stk(new_p, i) for i in range(7)) + tuple(stk(new_s, i) for i in range(7))
```

```python
import functools
import math

import jax
import jax.numpy as jnp
from jax import lax
from jax.experimental import pallas as pl
from jax.experimental.pallas import tpu as pltpu

F32 = jnp.float32
BF16 = jnp.bfloat16
HIGHEST = lax.Precision.HIGHEST

EPS = 1e-6
CHUNK = 64
CHUNK_SHIFT = 6
M_HEADS = 4
A_HEADS = 8
IDX_HEADS = 8
IDX_DIM = 64
CONV_W = 4
TOPK_MAX = 256
LANES = 128
SUBLANES = 8
MASK_NEG = -1e30
F32_MAX = float(jnp.finfo(jnp.float32).max)
VMEM_LIMIT = 48 * 1024 * 1024
SELECT_MIN_ITERS = 14
SELECT_MAX_ITERS = 16
FINISH_MAX = 4
VT_HEAD_ROWS = 144
QK_ROWS = 128
PV_ROWS = 256
TIE_WALK_MAX = 8
COUNT_PARTS = 4
COUNT_ROWS = 128
KEY_PREP_ROWS = 512
WEIGHT_SPLIT_ROWS = 256
MLSTM_STREAMS = 2

D = 1024
COL_MQ, COL_MK, COL_MV, COL_MO = 0, 1024, 2048, 3072
COL_GA, COL_GB = 4096, 5120
COL_IQ = 6144
COL_SMALL = 6656
D_Z = 7168
SM_IW, SM_MI, SM_MF = 64, 72, 76
Q_LOGIT_SCALE = math.log2(math.e) * (D // A_HEADS) ** -0.5


def _cparams(sem):
    return pltpu.CompilerParams(dimension_semantics=sem, vmem_limit_bytes=VMEM_LIMIT)


def _norm_matmul_kernel(x_ref, g_ref, w_ref, o_ref, xn_ref):
    @pl.when(pl.program_id(1) == 0)
    def _():
        x = x_ref[...]
        ms = jnp.mean(x * x, axis=-1, keepdims=True)
        xn_ref[...] = ((x * lax.rsqrt(ms + EPS)) * g_ref[...]).astype(BF16)

    o_ref[...] = jnp.dot(xn_ref[...], w_ref[...], preferred_element_type=F32)


def _norm_matmul(x, g, w, tm, tn):
    n, d = x.shape
    nout = w.shape[1]
    return pl.pallas_call(
        _norm_matmul_kernel,
        grid=(n // tm, nout // tn),
        in_specs=[pl.BlockSpec((tm, d), lambda i, j: (i, 0)),
                  pl.BlockSpec((1, d), lambda i, j: (0, 0)),
                  pl.BlockSpec((d, tn), lambda i, j: (0, j))],
        out_specs=pl.BlockSpec((tm, tn), lambda i, j: (i, j)),
        out_shape=jax.ShapeDtypeStruct((n, nout), F32),
        scratch_shapes=[pltpu.VMEM((tm, d), BF16)],
        compiler_params=_cparams(("parallel", "arbitrary")),
        name="norm_in_proj",
    )(x, g, w)


def _mlstm_kernel(*refs, L, streams):
    c = pl.program_id(1)
    for bi in range(streams):
        _mlstm_stream(bi, c, *refs, L=L)


def _mlstm_stream(bi, c, mq_ref, mk_ref, mv_ref, mo_ref, sm_ref, bias_ref, wconv_ref, bconv_ref,
                  gm_ref, c0_ref, n0_ref, m0_ref, conv0_ref,
                  y_ref, c_ref, n_ref, m_ref, conv_ref, cbuf_ref, *, L):
    hd = D // M_HEADS

    @pl.when(c == 0)
    def _():
        c_ref[bi] = c0_ref[bi]
        n_ref[bi] = n0_ref[bi]
        m_ref[bi] = m0_ref[bi]
        cbuf_ref[bi, 8 - (CONV_W - 1):8, :] = conv0_ref[bi]

    cbuf_ref[bi, 8:8 + L, 0:D] = mq_ref[bi]
    cbuf_ref[bi, 8:8 + L, D:2 * D] = mk_ref[bi]
    wc = wconv_ref[...]
    qk = bconv_ref[...] + cbuf_ref[bi, 5:5 + L, :] * wc[0:1, :]
    for j in range(1, CONV_W):
        qk = qk + cbuf_ref[bi, 5 + j:5 + j + L, :] * wc[j:j + 1, :]
    tail = cbuf_ref[bi, 5 + L:8 + L, :]
    cbuf_ref[bi, 5:8, :] = tail
    conv_ref[bi] = tail
    qk = qk * jax.nn.sigmoid(qk)

    g_all = sm_ref[bi] + bias_ref[...]
    lf_all = jnp.minimum(g_all, 0.0) - jnp.log1p(jnp.exp(-jnp.abs(g_all)))
    lane = lax.broadcasted_iota(jnp.int32, (L, LANES), 1)
    gates = jnp.where(lane >= SM_MF, lf_all, g_all)
    r_i = lax.broadcasted_iota(jnp.int32, (L, L), 0)
    c_i = lax.broadcasted_iota(jnp.int32, (L, L), 1)
    tril = (c_i <= r_i).astype(F32)
    triu = (r_i <= c_i).astype(F32)
    b_col_all = jnp.dot(tril, lf_all, precision=HIGHEST, preferred_element_type=F32)
    sel = (lax.broadcasted_iota(jnp.int32, (8, LANES), 1)
           == lax.broadcasted_iota(jnp.int32, (8, LANES), 0) + SM_MI).astype(F32)
    rows = lax.dot_general(sel, gates, (((1,), (1,)), ((), ())), precision=HIGHEST,
                           preferred_element_type=F32)
    b_row_all = jnp.dot(rows, triu, precision=HIGHEST, preferred_element_type=F32)
    causal = c_i <= r_i

    mo = mo_ref[bi]
    mv = mv_ref[bi]
    for h in range(M_HEADS):
        hs = slice(h * hd, (h + 1) * hd)
        qh = qk[:, hs]
        kh = qk[:, D + h * hd:D + (h + 1) * hd] * (hd ** -0.5)
        vh = mv[:, hs]
        qb, kb, vb = qh.astype(BF16), kh.astype(BF16), vh.astype(BF16)
        b_col = b_col_all[:, SM_MF + h:SM_MF + h + 1]
        i_col = gates[:, SM_MI + h:SM_MI + h + 1]
        b_row = b_row_all[M_HEADS + h:M_HEADS + h + 1, :]
        i_row = rows[h:h + 1, :]
        m_prev = m_ref[bi, :, h:h + 1]
        c_prev = c_ref[bi, h]
        n_prev = n_ref[bi, h:h + 1, :]

        dmat = jnp.where(causal, b_col - b_row + i_row, -jnp.inf)
        inter = b_col + m_prev
        m_t = jnp.maximum(inter, jnp.max(dmat, axis=-1, keepdims=True))
        w_intra = jnp.exp(dmat - m_t)
        w_inter = jnp.exp(inter - m_t)
        s = lax.dot_general(qb, kb, (((1,), (1,)), ((), ())), preferred_element_type=F32) * w_intra
        qc = lax.dot_general(qb, c_prev.astype(BF16), (((1,), (1,)), ((), ())),
                             preferred_element_type=F32)
        num = jnp.dot(s.astype(BF16), vb, preferred_element_type=F32) + w_inter * qc
        den = jnp.sum(s, axis=-1, keepdims=True) + w_inter * jnp.sum(qh * n_prev, axis=-1, keepdims=True)
        denom = jnp.maximum(jnp.abs(den), jnp.exp(-m_t))
        hh = num / denom

        m_new = m_t[L - 1:L, :]
        b_last = b_col[L - 1:L, :]
        g_col = jnp.exp(b_last - b_col + i_col - m_new)
        decay = jnp.exp(b_last + m_prev - m_new)
        gv = (g_col * vh).astype(BF16)
        c_ref[bi, h] = decay * c_prev + lax.dot_general(
            gv, kb, (((0,), (0,)), ((), ())), preferred_element_type=F32)
        n_ref[bi, h:h + 1, :] = decay * n_prev + jnp.sum(g_col * kh, axis=0, keepdims=True)
        m_ref[bi, :, h:h + 1] = m_new

        hn = hh * lax.rsqrt(jnp.mean(hh * hh, axis=-1, keepdims=True) + EPS) * gm_ref[:, hs]
        y_ref[bi, :, hs] = (hn * jax.nn.sigmoid(mo[:, hs])).astype(BF16)


def _mlstm(z3, bias_row, w_conv, b_conv, g_mnorm, c0, n0, m0, conv0, L):
    b, t, _ = z3.shape
    nc = t // L
    streams = math.gcd(b, MLSTM_STREAMS)
    hd = D // M_HEADS

    def zspec(col, width):
        return pl.BlockSpec((streams, L, width), lambda i, c: (i, c, col // width))

    def per_batch(shape):
        nd = len(shape)
        return pl.BlockSpec((streams,) + shape, lambda i, c: (i,) + (0,) * nd)

    def const(shape):
        nd = len(shape)
        return pl.BlockSpec(shape, lambda i, c: (0,) * nd)

    return pl.pallas_call(
        functools.partial(_mlstm_kernel, L=L, streams=streams),
        grid=(b // streams, nc),
        in_specs=[zspec(COL_MQ, D), zspec(COL_MK, D), zspec(COL_MV, D), zspec(COL_MO, D),
                  zspec(COL_SMALL, LANES), const((1, LANES)), const((CONV_W, 2 * D)),
                  const((1, 2 * D)), const((1, D)),
                  per_batch((M_HEADS, hd, hd)), per_batch((M_HEADS, hd)),
                  per_batch((1, M_HEADS)), per_batch((CONV_W - 1, 2 * D))],
        out_specs=[pl.BlockSpec((streams, L, D), lambda i, c: (i, c, 0)),
                   per_batch((M_HEADS, hd, hd)), per_batch((M_HEADS, hd)),
                   per_batch((1, M_HEADS)), per_batch((CONV_W - 1, 2 * D))],
        out_shape=[jax.ShapeDtypeStruct((b, t, D), BF16),
                   jax.ShapeDtypeStruct((b, M_HEADS, hd, hd), F32),
                   jax.ShapeDtypeStruct((b, M_HEADS, hd), F32),
                   jax.ShapeDtypeStruct((b, 1, M_HEADS), F32),
                   jax.ShapeDtypeStruct((b, CONV_W - 1, 2 * D), F32)],
        scratch_shapes=[pltpu.VMEM((streams, 8 + L, 2 * D), F32)],
        compiler_params=_cparams(("parallel", "arbitrary")),
        name="mlstm",
    )(z3, z3, z3, z3, z3, bias_row, w_conv, b_conv, g_mnorm, c0, n0, m0, conv0)


def _eye_bf16(n):
    return (lax.broadcasted_iota(jnp.int32, (n, n), 0)
            == lax.broadcasted_iota(jnp.int32, (n, n), 1)).astype(BF16)


def _transpose_bf16(eye, a):
    return lax.dot_general(eye, a, (((1,), (1,)), ((), ())), preferred_element_type=F32)


def _frames_to_lanes(eye, a):
    if a.shape[0] % LANES == 0:
        return a.T.astype(BF16)
    return _transpose_bf16(eye, a.astype(BF16)).astype(BF16)


def _lanes_to_frames(eye, a):
    if a.shape[1] % LANES == 0:
        return a.T.astype(BF16)
    return _transpose_bf16(eye, a.astype(BF16)).astype(BF16)


def _qkv_kernel(x_ref, g_ref, w_ref, gq_ref, gk_ref, k32_ref, v32_ref, kb_ref, qt_ref, vt_ref):
    hd = D // A_HEADS
    x = x_ref[0]
    ms = jnp.mean(x * x, axis=-1, keepdims=True)
    xn = ((x * lax.rsqrt(ms + EPS)) * g_ref[...]).astype(BF16)
    eye = _eye_bf16(hd)

    def project(col):
        return jnp.dot(xn, w_ref[:, col:col + 2 * hd], preferred_element_type=F32)

    def head_norm(a, gain):
        return (a * lax.rsqrt(jnp.mean(a * a, axis=-1, keepdims=True) + EPS)) * gain

    for pair in range(0, D, 2 * hd):
        q2, k2, v2 = project(pair), project(D + pair), project(2 * D + pair)
        for half in range(2):
            cs = slice(half * hd, (half + 1) * hd)
            hs = slice(pair + half * hd, pair + (half + 1) * hd)
            qn = head_norm(q2[:, cs], gq_ref[...]) * Q_LOGIT_SCALE
            qt_ref[0, hs, :] = _frames_to_lanes(eye, qn)
            kn = head_norm(k2[:, cs], gk_ref[...])
            k32_ref[0, :, hs] = kn
            kb_ref[0, :, hs] = kn.astype(BF16)
            v32_ref[0, :, hs] = v2[:, cs]
            h = (pair + half * hd) // hd
            vt_ref[0, 0, h * VT_HEAD_ROWS:h * VT_HEAD_ROWS + hd, :] = _frames_to_lanes(eye, v2[:, cs])
            vt_ref[0, 0, h * VT_HEAD_ROWS + hd:(h + 1) * VT_HEAD_ROWS, :] = jnp.ones(
                (VT_HEAD_ROWS - hd, x.shape[0]), BF16)


def _qkv(x, g_norm, w_att, g_q, g_k, tm):
    b, t, _ = x.shape
    hd = D // A_HEADS
    row = pl.BlockSpec((1, tm, D), lambda i, r: (i, r, 0))
    gspec = pl.BlockSpec((1, hd), lambda i, r: (0, 0))
    return pl.pallas_call(
        _qkv_kernel,
        grid=(b, t // tm),
        in_specs=[row, pl.BlockSpec((1, D), lambda i, r: (0, 0)),
                  pl.BlockSpec((D, 3 * D), lambda i, r: (0, 0)), gspec, gspec],
        out_specs=[row, row, row, pl.BlockSpec((1, D, tm), lambda i, r: (i, 0, r)),
                   pl.BlockSpec((1, 1, A_HEADS * VT_HEAD_ROWS, tm), lambda i, r: (i, r, 0, 0))],
        out_shape=[jax.ShapeDtypeStruct((b, t, D), F32), jax.ShapeDtypeStruct((b, t, D), F32),
                   jax.ShapeDtypeStruct((b, t, D), BF16), jax.ShapeDtypeStruct((b, D, t), BF16),
                   jax.ShapeDtypeStruct((b, t // tm, A_HEADS * VT_HEAD_ROWS, tm), BF16)],
        compiler_params=_cparams(("parallel", "parallel")),
        name="qkv_proj_norm",
    )(x, g_norm, w_att, g_q, g_k)


def _num_key_tiles(qi, tq, tile, q_pos0, s_real):
    last_chunk = (q_pos0 + (qi + 1) * tq - 1) // CHUNK
    kend = jnp.minimum((last_chunk + 1) * CHUNK, s_real)
    return (kend + tile - 1) // tile


def _num_key_tiles_static(qi, tq, tile, q_pos0, s_real):
    last_chunk = (q_pos0 + (qi + 1) * tq - 1) // CHUNK
    return -(-min((last_chunk + 1) * CHUNK, s_real) // tile)


def _select_kernel(iq_ref, sm_ref, knew_ref, *refs, tq, kb, s_pad, s_real, q_pos0, topk):
    kpast_ref = refs[0] if q_pos0 > 0 else None
    mask_ref, score_ref, iqt_ref, ki_ref = refs[-4:]
    qi = pl.program_id(1)
    nkt = _num_key_tiles(qi, tq, kb, q_pos0, s_real)
    kf = float(topk)
    groups = kb // SUBLANES

    @pl.when(qi == 0)
    def _():
        def put(row0, x):
            hi = x.astype(BF16)
            lo = (x - hi.astype(F32)).astype(BF16)
            ki_ref[row0:row0 + x.shape[0], :] = jnp.concatenate([hi, lo, hi], axis=1)

        if kpast_ref is not None:
            put(0, kpast_ref[0])
        t_new = knew_ref.shape[1]
        for r in range(0, t_new, KEY_PREP_ROWS):
            n = min(KEY_PREP_ROWS, t_new - r)
            put(q_pos0 + r, knew_ref[0, r:r + n, 0:IDX_DIM])
        if s_pad > s_real:
            ki_ref[s_real:s_pad, :] = jnp.zeros((s_pad - s_real, 3 * IDX_DIM), BF16)

    pick = (lax.broadcasted_iota(jnp.int32, (IDX_HEADS, LANES), 1)
            == lax.broadcasted_iota(jnp.int32, (IDX_HEADS, LANES), 0) + SM_IW).astype(F32)
    w = lax.dot_general(pick, sm_ref[0], (((1,), (1,)), ((), ())), precision=HIGHEST,
                        preferred_element_type=F32) * ((IDX_HEADS * IDX_DIM) ** -0.5)
    eye = _eye_bf16(IDX_DIM)
    for h in range(IDX_HEADS):
        a = iq_ref[0, :, h * IDX_DIM:(h + 1) * IDX_DIM]
        hi = a.astype(BF16)
        lo = (a - hi.astype(F32)).astype(BF16)
        hi_t = _transpose_bf16(eye, hi).astype(BF16)
        lo_t = _transpose_bf16(eye, lo).astype(BF16)
        cols = slice((h % 2) * tq, (h % 2 + 1) * tq)
        iqt_ref[h // 2, 0:IDX_DIM, cols] = hi_t
        iqt_ref[h // 2, IDX_DIM:2 * IDX_DIM, cols] = hi_t
        iqt_ref[h // 2, 2 * IDX_DIM:3 * IDX_DIM, cols] = lo_t

    q_pos = q_pos0 + qi * tq + lax.broadcasted_iota(jnp.int32, (1, tq), 1)
    q_chunk = q_pos >> CHUNK_SHIFT
    first_key = jnp.minimum(((q_pos0 + qi * tq) >> CHUNK_SHIFT) << CHUNK_SHIFT, s_real)
    n_full = first_key // kb

    def score_tile(j, carry, masked):
        rmin, rmax = carry
        k0 = pl.multiple_of(j * kb, kb)
        kt = ki_ref[pl.ds(k0, kb), :]
        acc = None
        for p in range(IDX_HEADS // 2):
            rel = jnp.dot(kt, iqt_ref[p], preferred_element_type=F32)
            part = (w[2 * p:2 * p + 1] * jnp.maximum(rel[:, 0:tq], 0.0)
                    + w[2 * p + 1:2 * p + 2] * jnp.maximum(rel[:, tq:2 * tq], 0.0))
            acc = part if acc is None else acc + part
        if masked:
            key = k0 + lax.broadcasted_iota(jnp.int32, (kb, tq), 0)
            adm = jnp.logical_and((key >> CHUNK_SHIFT) <= q_chunk, key < s_real)
            low = jnp.where(adm, acc, jnp.inf)
            acc = jnp.where(adm, acc, -jnp.inf)
        else:
            low = acc
        score_ref[pl.ds(k0, kb), :] = acc
        rmin = jnp.minimum(rmin, jnp.min(low.reshape(groups, SUBLANES, tq), axis=0))
        rmax = jnp.maximum(rmax, jnp.max(acc.reshape(groups, SUBLANES, tq), axis=0))
        return rmin, rmax

    stats = (jnp.full((SUBLANES, tq), jnp.inf, F32), jnp.full((SUBLANES, tq), -jnp.inf, F32))
    stats = lax.fori_loop(0, n_full, functools.partial(score_tile, masked=False), stats)
    rmin8, rmax8 = lax.fori_loop(n_full, nkt, functools.partial(score_tile, masked=True), stats)
    rmin = jnp.min(rmin8, axis=0, keepdims=True)
    rmax = jnp.max(rmax8, axis=0, keepdims=True)

    def count(pred):
        def body(j, acc):
            for c in range(kb // COUNT_ROWS):
                k0 = pl.multiple_of(j * kb + c * COUNT_ROWS, COUNT_ROWS)
                hit = jnp.where(pred(score_ref[pl.ds(k0, COUNT_ROWS), :], k0), 1.0, 0.0)
                acc = acc + jnp.sum(hit.reshape(COUNT_ROWS // (COUNT_PARTS * SUBLANES), COUNT_PARTS,
                                                SUBLANES, tq), axis=0)
            return acc
        acc = lax.fori_loop(0, nkt, body, jnp.zeros((COUNT_PARTS, SUBLANES, tq), F32))
        return jnp.sum(jnp.sum(acc, axis=0), axis=0, keepdims=True)

    def below(cur):
        shape = (COUNT_ROWS // (COUNT_PARTS * SUBLANES), COUNT_PARTS, SUBLANES, tq)

        def body(j, carry):
            top, num = carry
            for c in range(kb // COUNT_ROWS):
                k0 = pl.multiple_of(j * kb + c * COUNT_ROWS, COUNT_ROWS)
                s = score_ref[pl.ds(k0, COUNT_ROWS), :]
                under = s < cur
                top = jnp.maximum(top, jnp.max(jnp.where(under, s, -jnp.inf).reshape(shape), axis=0))
                num = num + jnp.sum(jnp.where(under, 0.0, 1.0).reshape(shape), axis=0)
            return top, num

        top, num = lax.fori_loop(0, nkt, body, (jnp.full(shape[1:], -jnp.inf, F32),
                                                jnp.zeros(shape[1:], F32)))
        return (jnp.max(jnp.max(top, axis=0), axis=0, keepdims=True),
                jnp.sum(jnp.sum(num, axis=0), axis=0, keepdims=True))

    n_adm = jnp.minimum((q_chunk + 1) << CHUNK_SHIFT, s_real).astype(F32)

    def open_rows(cnt_lo):
        return cnt_lo > kf

    def wide_rows(st):
        return jnp.logical_and(open_rows(st[2]), st[2] - st[3] > FINISH_MAX)

    def flag(pred):
        return jnp.max(jnp.where(pred, 1.0, 0.0))

    def bisect(st):
        lo, hi, cnt_lo, cnt_hi = st
        active = open_rows(cnt_lo)
        mid = 0.5 * jnp.maximum(lo, rmin) + 0.5 * jnp.minimum(hi, rmax)
        cm = count(lambda s, k0: s >= mid)
        up = jnp.logical_and(active, cm >= kf)
        dn = jnp.logical_and(active, cm < kf)
        return (jnp.where(up, mid, lo), jnp.where(dn, mid, hi),
                jnp.where(up, cm, cnt_lo), jnp.where(dn, cm, cnt_hi))

    st = (jnp.full((1, tq), -F32_MAX, F32), jnp.full((1, tq), F32_MAX, F32), n_adm,
          jnp.zeros((1, tq), F32))
    st = lax.fori_loop(0, SELECT_MIN_ITERS, lambda _, s: bisect(s), st)

    def narrow(c):
        nxt = bisect(c[2])
        return c[0] + 1, flag(wide_rows(nxt)), nxt

    _, _, st = lax.while_loop(
        lambda c: jnp.logical_and(c[0] < SELECT_MAX_ITERS, c[1] > 0.0), narrow,
        (jnp.int32(SELECT_MIN_ITERS), flag(wide_rows(st)), st))

    lo, hi, cnt_lo, cnt_hi = st
    cand, _ = below(hi)

    def walk(c):
        _, walking, lo, hi, cnt_lo, cnt_hi, cand = c
        nxt, cge = below(cand)
        settle = jnp.logical_and(walking > 0.0, cge >= kf)
        move = jnp.logical_and(walking > 0.0, cge < kf)
        walking = jnp.where(move, 1.0, 0.0)
        return (jnp.max(walking), walking, jnp.where(settle, cand, lo), jnp.where(move, cand, hi),
                jnp.where(settle, cge, cnt_lo), jnp.where(move, cge, cnt_hi), jnp.where(move, nxt, cand))

    walking = jnp.where(open_rows(cnt_lo), 1.0, 0.0)
    _, _, lo, hi, cnt_lo, cnt_hi, _ = lax.while_loop(
        lambda c: c[0] > 0.0, walk, (jnp.max(walking), walking, lo, hi, cnt_lo, cnt_hi, cand))
    n_open = flag(open_rows(cnt_lo))

    def clear_tile(j, carry):
        k0 = pl.multiple_of(j * kb, kb)
        mask_ref[0, 0, pl.ds(k0, kb), :] = jnp.zeros((kb, tq), jnp.int8)
        return carry

    lax.fori_loop(nkt, s_pad // kb, clear_tile, 0)

    @pl.when(n_open == 0.0)
    def _():
        def write_tile(j, carry):
            k0 = pl.multiple_of(j * kb, kb)
            keep = score_ref[pl.ds(k0, kb), :] >= lo
            mask_ref[0, 0, pl.ds(k0, kb), :] = jnp.where(keep, 1, 0).astype(jnp.int8)
            return carry

        lax.fori_loop(0, nkt, write_tile, 0)

    @pl.when(n_open > 0.0)
    def _():
        tied = open_rows(cnt_lo)
        need = jnp.where(tied, kf - cnt_hi, 0.0)
        key_iota = lax.broadcasted_iota(jnp.int32, (kb, tq), 0)
        strip_iota = lax.broadcasted_iota(jnp.int32, (COUNT_ROWS, tq), 0)
        shape = (COUNT_ROWS // (COUNT_PARTS * SUBLANES), COUNT_PARTS, SUBLANES, tq)

        def next_tied(prev):
            def body(j, acc):
                for c in range(kb // COUNT_ROWS):
                    k0 = pl.multiple_of(j * kb + c * COUNT_ROWS, COUNT_ROWS)
                    idx = k0 + strip_iota
                    hit = jnp.logical_and(score_ref[pl.ds(k0, COUNT_ROWS), :] == lo, idx > prev)
                    acc = jnp.minimum(acc, jnp.min(jnp.where(hit, idx, s_pad).reshape(shape), axis=0))
                return acc
            acc = lax.fori_loop(0, nkt, body, jnp.full(shape[1:], s_pad, jnp.int32))
            return jnp.min(jnp.min(acc, axis=0), axis=0, keepdims=True)

        def cut_by_walk(_):
            def step(c):
                _, left, last = c
                take = left > 0.0
                last = jnp.where(take, next_tied(last), last)
                left = jnp.where(take, left - 1.0, left)
                return jnp.max(left), left, last
            _, _, last = lax.while_loop(lambda c: c[0] > 0.0, step,
                                        (jnp.max(need), need, jnp.full((1, tq), -1, jnp.int32)))
            return last + 1

        def cut_by_bisection(_):
            def step(_, st):
                jlo, jhi = st
                jmid = (jlo + jhi) >> 1
                cm = count(lambda s, k0: jnp.logical_and(s == lo, k0 + strip_iota < jmid))
                ok = cm >= need
                return jnp.where(ok, jlo, jmid), jnp.where(ok, jmid, jhi)
            steps = int(math.ceil(math.log2(s_pad))) + 1
            return lax.fori_loop(0, steps, step, (jnp.zeros((1, tq), jnp.int32),
                                                  jnp.full((1, tq), s_pad, jnp.int32)))[1]

        jcut = lax.cond(jnp.max(need) <= TIE_WALK_MAX, cut_by_walk, cut_by_bisection, 0)
        jcut = jnp.where(tied, jcut, s_pad)

        def write_tile(j, carry):
            k0 = pl.multiple_of(j * kb, kb)
            s = score_ref[pl.ds(k0, kb), :]
            keep = jnp.logical_or(s >= hi, jnp.logical_and(s >= lo, k0 + key_iota < jcut))
            mask_ref[0, 0, pl.ds(k0, kb), :] = jnp.where(keep, 1, 0).astype(jnp.int8)
            return carry

        lax.fori_loop(0, nkt, write_tile, 0)


def _select(z3, ki_past, tq, kb, s_pad, topk):
    b, t, _ = z3.shape
    past = ki_past.shape[1]
    iq_w = IDX_HEADS * IDX_DIM
    kern = functools.partial(_select_kernel, tq=tq, kb=kb, s_pad=s_pad, s_real=past + t,
                             q_pos0=past, topk=topk)
    cached = ([pl.BlockSpec((1, past, IDX_DIM), lambda i, j: (i, 0, 0))], [ki_past]) if past else ([], [])
    return pl.pallas_call(
        kern,
        grid=(b, t // tq),
        in_specs=[pl.BlockSpec((1, tq, iq_w), lambda i, j: (i, j, COL_IQ // iq_w)),
                  pl.BlockSpec((1, tq, LANES), lambda i, j: (i, j, COL_SMALL // LANES)),
                  pl.BlockSpec((1, t, LANES), lambda i, j: (i, 0, COL_SMALL // LANES))] + cached[0],
        out_specs=pl.BlockSpec((1, 1, s_pad, tq), lambda i, j: (i, j, 0, 0)),
        out_shape=jax.ShapeDtypeStruct((b, t // tq, s_pad, tq), jnp.int8),
        scratch_shapes=[pltpu.VMEM((s_pad, tq), F32),
                        pltpu.VMEM((IDX_HEADS // 2, 3 * IDX_DIM, 2 * tq), BF16),
                        pltpu.VMEM((s_pad, 3 * IDX_DIM), BF16)],
        compiler_params=_cparams(("parallel", "arbitrary")),
        name="index_select",
    )(z3, z3, z3, *cached[1])


def _attn_kernel(q_of_ref, k_of_ref, qt_ref, k_ref, vt_ref, *refs, n_mask):
    mask_refs = refs[:n_mask]
    o_ref, acc_ref, m_ref, l_ref, s_ref, bias_ref = refs[n_mask:]
    step = pl.program_id(1)
    hd = D // A_HEADS
    tq = qt_ref.shape[2]
    tk = k_ref.shape[1]
    tq_mask = tq // n_mask
    pv_rows = math.gcd(tk, PV_ROWS)
    first = k_of_ref[step] == 0
    nxt = jnp.minimum(step + 1, pl.num_programs(1) - 1)
    last = jnp.logical_or(step == pl.num_programs(1) - 1, q_of_ref[nxt] != q_of_ref[step])

    @pl.when(first)
    def _():
        acc_ref[...] = jnp.zeros(acc_ref.shape, F32)
        m_ref[...] = jnp.full(m_ref.shape, MASK_NEG, F32)
        l_ref[...] = jnp.zeros(l_ref.shape, F32)

    for n, mask_ref in enumerate(mask_refs):
        bias_ref[:, n * tq_mask:(n + 1) * tq_mask] = jnp.where(
            mask_ref[0, 0].astype(jnp.int32) != 0, 0.0, MASK_NEG)
    def logits(h):
        hs = slice(h * hd, (h + 1) * hd)
        top = None
        for r in range(0, tk, QK_ROWS):
            s = jnp.dot(k_ref[0, r:r + QK_ROWS, hs], qt_ref[0, hs, :],
                        preferred_element_type=F32) + bias_ref[r:r + QK_ROWS, :]
            s_ref[h, r:r + QK_ROWS, :] = s
            part = jnp.max(s.reshape(QK_ROWS // SUBLANES, SUBLANES, tq), axis=0)
            top = part if top is None else jnp.maximum(top, part)
        return jnp.max(top, axis=0, keepdims=True)

    def weigh(h, tile_max):
        hs = slice(h * hd, (h + 1) * hd)
        m_prev = m_ref[h:h + 1, :]
        m_new = jnp.maximum(m_prev, tile_max)
        alpha = jnp.exp2(m_prev - m_new)
        pv = None
        for r in range(0, tk, pv_rows):
            p = jnp.exp2(s_ref[h, r:r + pv_rows, :] - m_new)
            prod = jnp.dot(vt_ref[0, 0, h * VT_HEAD_ROWS:(h + 1) * VT_HEAD_ROWS, r:r + pv_rows],
                           p.astype(BF16), preferred_element_type=F32)
            pv = prod if pv is None else pv + prod
        l_ref[h:h + 1, :] = alpha * l_ref[h:h + 1, :] + pv[hd:hd + 1, :]
        acc_ref[hs, :] = alpha * acc_ref[hs, :] + pv[0:hd, :]
        m_ref[h:h + 1, :] = m_new

    tile_max = [logits(h) for h in range(A_HEADS)]

    @pl.when(k_of_ref[step] >= 0)
    def _():
        for h in range(A_HEADS):
            weigh(h, tile_max[h])

    @pl.when(last)
    def _():
        eye = _eye_bf16(tq)
        for h in range(A_HEADS):
            hs = slice(h * hd, (h + 1) * hd)
            o_ref[0, :, hs] = _lanes_to_frames(eye, acc_ref[hs, :] / l_ref[h:h + 1, :])


def _attention(qt, kb, vt, mask_t, tq, tk, s_real, q_pos0):
    b, _, t = qt.shape
    n_mask = tq // mask_t.shape[3]
    steps = [(q, j) for q in range(t // tq) for j in range(_num_key_tiles_static(q, tq, tk, q_pos0, s_real))]
    q_of = jnp.array([q for q, _ in steps], jnp.int32)
    k_of = jnp.array([j for _, j in steps], jnp.int32)

    def mask_spec(n):
        return pl.BlockSpec((1, 1, tk, tq // n_mask),
                            lambda i, s, q_of, k_of: (i, q_of[s] * n_mask + n, k_of[s], 0))

    return pl.pallas_call(
        functools.partial(_attn_kernel, n_mask=n_mask),
        grid_spec=pltpu.PrefetchScalarGridSpec(
            num_scalar_prefetch=2,
            grid=(b, len(steps)),
            in_specs=[pl.BlockSpec((1, D, tq), lambda i, s, q_of, k_of: (i, 0, q_of[s])),
                      pl.BlockSpec((1, tk, D), lambda i, s, q_of, k_of: (i, k_of[s], 0)),
                      pl.BlockSpec((1, 1, A_HEADS * VT_HEAD_ROWS, tk), lambda i, s, q_of, k_of: (i, k_of[s], 0, 0))]
            + [mask_spec(n) for n in range(n_mask)],
            out_specs=pl.BlockSpec((1, tq, D), lambda i, s, q_of, k_of: (i, q_of[s], 0)),
            scratch_shapes=[pltpu.VMEM((D, tq), F32),
                            pltpu.VMEM((A_HEADS, tq), F32),
                            pltpu.VMEM((A_HEADS, tq), F32),
                            pltpu.VMEM((A_HEADS, tk, tq), F32),
                            pltpu.VMEM((tk, tq), F32)]),
        out_shape=jax.ShapeDtypeStruct((b, t, D), BF16),
        compiler_params=_cparams(("parallel", "arbitrary")),
        name="masked_attention",
    )(q_of, k_of, qt, kb, vt, *([mask_t] * n_mask))


def _attn_past_kernel(qt_ref, kn_ref, vtn_ref, kp_ref, vp_ref, mask_ref, o_ref):
    hd = D // A_HEADS
    t = qt_ref.shape[2]
    past = kp_ref.shape[1] // A_HEADS

    def cached(ref, h):
        return ref[0, pl.ds(h, past, stride=A_HEADS), :].astype(BF16)

    bias = jnp.where(mask_ref[0, 0].astype(jnp.int32) != 0, 0.0, MASK_NEG)
    bias_p, bias_n = bias[0:past], bias[past:past + t]
    eye_hd, eye_t = _eye_bf16(hd), _eye_bf16(t)
    for h in range(A_HEADS):
        hs = slice(h * hd, (h + 1) * hd)
        q = qt_ref[0, hs, :]
        s_p = jnp.dot(cached(kp_ref, h), q, preferred_element_type=F32) + bias_p
        s_n = jnp.dot(kn_ref[0, :, hs], q, preferred_element_type=F32) + bias_n
        m = jnp.maximum(jnp.max(s_p, axis=0, keepdims=True), jnp.max(s_n, axis=0, keepdims=True))
        p_p = jnp.exp2(s_p - m)
        p_n = jnp.exp2(s_n - m)
        l = jnp.sum(p_p, axis=0, keepdims=True) + jnp.sum(p_n, axis=0, keepdims=True)
        vt_p = _transpose_bf16(eye_hd, cached(vp_ref, h)).astype(BF16)
        out_t = (jnp.dot(vt_p, p_p.astype(BF16), preferred_element_type=F32)
                 + jnp.dot(vtn_ref[0, 0, h * VT_HEAD_ROWS:h * VT_HEAD_ROWS + hd, :], p_n.astype(BF16),
                           preferred_element_type=F32))
        o_ref[0, :, hs] = _transpose_bf16(eye_t, (out_t / l).astype(BF16)).astype(BF16)


def _attention_with_past(qt, kb, vt, k_past, v_past, mask_t):
    b, _, t = qt.shape
    past, heads, hd = k_past.shape[1:]
    s_pad = mask_t.shape[2]
    k_past = k_past.reshape(b, past * heads, hd)
    v_past = v_past.reshape(b, past * heads, hd)
    cache = pl.BlockSpec((1, past * heads, hd), lambda i: (i, 0, 0))
    return pl.pallas_call(
        _attn_past_kernel,
        grid=(b,),
        in_specs=[pl.BlockSpec((1, D, t), lambda i: (i, 0, 0)),
                  pl.BlockSpec((1, t, D), lambda i: (i, 0, 0)),
                  pl.BlockSpec((1, 1, A_HEADS * VT_HEAD_ROWS, t), lambda i: (i, 0, 0, 0)),
                  cache, cache,
                  pl.BlockSpec((1, 1, s_pad, t), lambda i: (i, 0, 0, 0))],
        out_specs=pl.BlockSpec((1, t, D), lambda i: (i, 0, 0)),
        out_shape=jax.ShapeDtypeStruct((b, t, D), BF16),
        compiler_params=_cparams(("parallel",)),
        name="masked_attention_past",
    )(qt, kb, vt, k_past, v_past, mask_t)


def _merge_kernel(x_ref, ya_ref, yb_ref, ga_ref, gb_ref, wa_ref, wb_ref, wo_ref, o_ref):
    a = jnp.dot(ya_ref[...], wa_ref[...], preferred_element_type=F32)
    bb = jnp.dot(yb_ref[...], wb_ref[...], preferred_element_type=F32)
    mix = jax.nn.sigmoid(ga_ref[...]) * a + jax.nn.sigmoid(gb_ref[...]) * bb
    o_ref[...] = x_ref[...] + jnp.dot(mix.astype(BF16), wo_ref[...], preferred_element_type=F32)


def _merge(x, ya, yb, z, wa, wb, wo, tm):
    n = x.shape[0]
    row = pl.BlockSpec((tm, D), lambda i: (i, 0))
    wspec = pl.BlockSpec((D, D), lambda i: (0, 0))
    return pl.pallas_call(
        _merge_kernel,
        grid=(n // tm,),
        in_specs=[row, row, row,
                  pl.BlockSpec((tm, D), lambda i: (i, COL_GA // D)),
                  pl.BlockSpec((tm, D), lambda i: (i, COL_GB // D)),
                  wspec, wspec, wspec],
        out_specs=row,
        out_shape=jax.ShapeDtypeStruct((n, D), F32),
        compiler_params=_cparams(("parallel",)),
        name="merge_out_proj",
    )(x, ya, yb, z, z, wa, wb, wo)


def _ffn_kernel(x_ref, g_ref, wg_ref, wu_ref, wd_ref, o_ref, xn_ref, acc_ref):
    c = pl.program_id(1)

    @pl.when(c == 0)
    def _():
        x = x_ref[...]
        ms = jnp.mean(x * x, axis=-1, keepdims=True)
        xn_ref[...] = ((x * lax.rsqrt(ms + EPS)) * g_ref[...]).astype(BF16)
        acc_ref[...] = x

    xn = xn_ref[...]
    gate = jnp.dot(xn, wg_ref[...], preferred_element_type=F32)
    up = jnp.dot(xn, wu_ref[...], preferred_element_type=F32)
    act = (gate * jax.nn.sigmoid(gate)) * up
    acc_ref[...] += jnp.dot(act.astype(BF16), wd_ref[...], preferred_element_type=F32)

    @pl.when(c == pl.num_programs(1) - 1)
    def _():
        o_ref[...] = acc_ref[...]


def _ffn(x, g, w_in, w_out, tm, tf):
    n = x.shape[0]
    dff = w_out.shape[0]
    nf = dff // tf
    row = pl.BlockSpec((tm, D), lambda i, c: (i, 0))
    once = {"pipeline_mode": pl.Buffered(1)} if nf == 1 else {}
    return pl.pallas_call(
        _ffn_kernel,
        grid=(n // tm, nf),
        in_specs=[row, pl.BlockSpec((1, D), lambda i, c: (0, 0)),
                  pl.BlockSpec((D, tf), lambda i, c: (0, c), **once),
                  pl.BlockSpec((D, tf), lambda i, c: (0, nf + c), **once),
                  pl.BlockSpec((tf, D), lambda i, c: (c, 0), **once)],
        out_specs=row,
        out_shape=jax.ShapeDtypeStruct((n, D), F32),
        scratch_shapes=[pltpu.VMEM((tm, D), BF16), pltpu.VMEM((tm, D), F32)],
        compiler_params=_cparams(("parallel", "arbitrary")),
        name="swiglu_ffn",
    )(x, g, w_in, w_in, w_out)


def _layer(x, past, p, cfg):
    k_past, v_past, ki_past, c0, n0, m0, conv0 = past
    b, t, _ = x.shape
    n_tok = b * t
    past_len = k_past.shape[1]
    s_real = past_len + t
    topk = min(TOPK_MAX, s_real // 4)
    tq_sel, kb_sel, tq_att, tk_att = cfg["tq_sel"], cfg["kb_sel"], cfg["tq_att"], cfg["tk_att"]
    s_pad = -(-s_real // tk_att) * tk_att

    x2 = x.reshape(n_tok, D)
    z = _norm_matmul(x2, p["g_norm1"], p["w_in"], cfg["tm_proj"], cfg["tn_proj"])
    z3 = z.reshape(b, t, D_Z)

    y_a, c_new, n_new, m_new, conv_new = _mlstm(
        z3, p["gate_bias"], p["w_conv"], p["b_conv"], p["g_mnorm"], c0, n0,
        m0.reshape(b, 1, M_HEADS), conv0, cfg["mlstm_chunk"])

    k32, v32, kb, qt, vt = _qkv(x, p["g_norm1"], p["w_att"], p["g_q"], p["g_k"], cfg["tm_qkv"])
    ik = z3[:, :, COL_SMALL:COL_SMALL + IDX_DIM]
    mask_t = _select(z3, ki_past.astype(F32), tq_sel, kb_sel, s_pad, topk)
    if past_len == 0:
        assert cfg["tm_qkv"] == tk_att and s_pad == s_real
        y_b = _attention(qt, kb, vt, mask_t, tq_att, tk_att, s_real, past_len)
    else:
        assert cfg["tm_qkv"] == t == tq_sel
        y_b = _attention_with_past(qt, kb, vt, k_past, v_past, mask_t)

    x1 = _merge(x2, y_a.reshape(n_tok, D), y_b.reshape(n_tok, D), z,
                p["w_a_out"], p["w_b_out"], p["w_o"], cfg["tm_rows"])
    y = _ffn(x1, p["g_norm2"], p["w_ffn_in"], p["w_ffn_out"], cfg["tm_ffn"], p["w_ffn_out"].shape[0])

    hd = D // A_HEADS
    return y.reshape(b, t, D), (k32.reshape(b, t, A_HEADS, hd), v32.reshape(b, t, A_HEADS, hd), ik,
                                c_new, n_new, m_new.reshape(b, M_HEADS), conv_new)


O_MI = 4 * D
O_AQ = O_MI + 2 * M_HEADS
O_IQ = O_AQ + 3 * D
O_IK = O_IQ + IDX_HEADS * IDX_DIM
O_IW = O_IK + IDX_DIM
O_GA = O_IW + IDX_HEADS
D_IN = O_GA + 2 * D


def _split_w_kernel(w_ref, main_ref, att_ref):
    rows = w_ref.shape[0]

    def columns(first, width):
        start = first // LANES * LANES
        win = width + LANES
        if start + win <= D_IN:
            x = w_ref[:, start:start + win]
        else:
            tail = D_IN - (start + width)
            x = jnp.concatenate([w_ref[:, start:start + width],
                                 w_ref[:, start + width:D_IN],
                                 jnp.zeros((rows, LANES - tail), F32)], axis=1)
        return pltpu.roll(x, win - (first - start), axis=1)[:, 0:width]

    att_ref[...] = columns(O_AQ, 3 * D).astype(BF16)
    main_ref[:, 0:O_MI] = w_ref[:, 0:O_MI].astype(BF16)
    main_ref[:, COL_GA:COL_GA + 2 * D] = columns(O_GA, 2 * D).astype(BF16)
    main_ref[:, COL_IQ:COL_IQ + IDX_HEADS * IDX_DIM] = columns(O_IQ, IDX_HEADS * IDX_DIM).astype(BF16)
    lane = lax.broadcasted_iota(jnp.int32, (rows, LANES), 1)
    ik_iw = columns(O_IK, LANES)
    gates = pltpu.roll(w_ref[:, O_MI:O_MI + LANES], SM_MI, axis=1)
    small = jnp.where(lane < SM_MI, ik_iw, jnp.where(lane < SM_MF + M_HEADS, gates, 0.0))
    main_ref[:, COL_SMALL:COL_SMALL + LANES] = small.astype(BF16)
    main_ref[:, COL_SMALL + LANES:D_Z] = jnp.zeros((rows, D_Z - COL_SMALL - LANES), BF16)


def _split_w(w_in, layer, tr):
    return pl.pallas_call(
        _split_w_kernel,
        grid=(D // tr,),
        in_specs=[pl.BlockSpec((None, tr, D_IN), lambda i: (layer, i, 0))],
        out_specs=[pl.BlockSpec((tr, D_Z), lambda i: (i, 0)), pl.BlockSpec((tr, 3 * D), lambda i: (i, 0))],
        out_shape=[jax.ShapeDtypeStruct((D, D_Z), BF16), jax.ShapeDtypeStruct((D, 3 * D), BF16)],
        compiler_params=_cparams(("parallel",)),
        name="split_in_proj_weight",
    )(w_in)


def _prep_params(g_norm1, w_in, layer, b_if, w_conv, b_conv, g_mnorm, g_q, g_k, w_a_out, w_b_out, w_o,
                 g_norm2, w_ffn_in, w_ffn_out):
    assert w_in.shape[1:] == (D, D_IN)
    w_perm, w_att = _split_w(w_in, layer, WEIGHT_SPLIT_ROWS)
    gate_bias = jnp.zeros((1, LANES), F32).at[0, SM_MI:SM_MI + 2 * M_HEADS].set(b_if.astype(F32))
    return {
        "g_norm1": g_norm1.reshape(1, D), "w_in": w_perm, "w_att": w_att,
        "gate_bias": gate_bias,
        "w_conv": w_conv, "b_conv": b_conv.reshape(1, 2 * D), "g_mnorm": g_mnorm.reshape(1, D),
        "g_q": g_q.reshape(1, -1), "g_k": g_k.reshape(1, -1),
        "w_a_out": w_a_out.astype(BF16), "w_b_out": w_b_out.astype(BF16), "w_o": w_o.astype(BF16),
        "g_norm2": g_norm2.reshape(1, D), "w_ffn_in": w_ffn_in.astype(BF16),
        "w_ffn_out": w_ffn_out.astype(BF16),
    }


def _config(b, t, past_len):
    n_tok = b * t
    s_real = past_len + t
    tm = min(2048, n_tok)
    tq_sel = min(256, t)
    tq_att = min(512, t)
    if s_real % 512 == 0:
        kb_sel = tk_att = 512
    else:
        kb_sel = tk_att = -(-s_real // LANES) * LANES
    return {"tm_proj": tm, "tn_proj": 1024,"mlstm_chunk": min(128, t), "tm_rows": min(512, n_tok),
            "tm_qkv": min(512, t), "tq_sel": tq_sel, "kb_sel": kb_sel, "tq_att": tq_att,
            "tk_att": tk_att, "tm_ffn": min(512, n_tok)}


def kernel(x_prompt, x_sample, cache_k, cache_v, cache_kidx, state_C, state_n, state_m, state_conv,
           g_norm1, w_in, b_if, w_conv, b_conv, g_mnorm, g_q, g_k, w_a_out, w_b_out, w_o,
           g_norm2, w_ffn_in, w_ffn_out):
    depth = w_in.shape[0]
    bp = x_prompt.shape[0]
    hd_a = D // A_HEADS
    hd_m = D // M_HEADS
    yp, ys = x_prompt, x_sample
    new_p, new_s = [], []
    for l in range(depth):
        p = _prep_params(g_norm1[l], w_in, l, b_if[l], w_conv[l], b_conv[l], g_mnorm[l], g_q[l],
                         g_k[l], w_a_out[l], w_b_out[l], w_o[l], g_norm2[l], w_ffn_in[l],
                         w_ffn_out[l])
        empty = (jnp.zeros((bp, 0, A_HEADS, hd_a), F32), jnp.zeros((bp, 0, A_HEADS, hd_a), F32),
                 jnp.zeros((bp, 0, IDX_DIM), F32), jnp.zeros((bp, M_HEADS, hd_m, hd_m), F32),
                 jnp.zeros((bp, M_HEADS, hd_m), F32), jnp.zeros((bp, M_HEADS), F32),
                 jnp.zeros((bp, CONV_W - 1, 2 * D), F32))
        yp, sp = _layer(yp, empty, p, _config(bp, yp.shape[1], 0))
        ys, ss = _layer(ys, (cache_k[l], cache_v[l], cache_kidx[l], state_C[l], state_n[l],
                             state_m[l], state_conv[l]), p,
                        _config(ys.shape[0], ys.shape[1], cache_k.shape[2]))
        new_p.append(sp)
        new_s.append(ss)

    def stk(lst, i):
        return jnp.stack([s[i] for s in lst])

    return (yp, ys) + tuple(stk(new_p, i) for i in range(7)) + tuple(stk(new_s, i) for i in range(7))
```
